```python
import math
import jax, jax.numpy as jnp
from jax import lax
import numpy as np

D_MODEL = 2048
BATCH = 4
SEQ = 2048
DEPTH = 1
DEC_BATCH = 128
DEC_SEQ = 8
PAST_LEN = 16384
PAGE_SIZE = 128

MIX_A = D_MODEL // 2
MIX_B = D_MODEL - MIX_A
DN_HEAD_DIM = 128
DN_HEADS = MIX_A // DN_HEAD_DIM
CONV_WIDTH = 4
CHUNK = 64
S5_GROUP = 16
S5_GROUPS = MIX_B // S5_GROUP
S5_STATE = 64
QKV_DIM = 3 * MIX_A
PROJ_DIM = QKV_DIM + MIX_A + 2 * DN_HEADS + MIX_B
D_FF = ((8 * D_MODEL + 767) // 768) * 256
EPS = 1e-6

kernel_name = 'hybrid_gdn_s5_decoder_step'


def rmsnorm(x, w):
    x32 = x.astype(jnp.float32)
    y = x32 * lax.rsqrt(jnp.mean(x32 * x32, axis=-1, keepdims=True) + EPS)
    return (y * w.astype(jnp.float32)).astype(x.dtype)


def l2norm(x):
    return x * lax.rsqrt(jnp.sum(x * x, axis=-1, keepdims=True) + 1e-6)


def gated_delta_rule(q, k, v, g, beta, s0):
    bsz, t, h, _ = q.shape
    dv = v.shape[-1]
    c = CHUNK if t >= CHUNK else t
    pad = (-t) % c
    if pad:
        pw = ((0, 0), (0, pad), (0, 0), (0, 0))
        q, k, v = jnp.pad(q, pw), jnp.pad(k, pw), jnp.pad(v, pw)
        g, beta = jnp.pad(g, pw[:3]), jnp.pad(beta, pw[:3])
    n = (t + pad) // c

    def chunks(a):
        a = a.reshape((bsz, n, c, h) + a.shape[3:])
        return jnp.moveaxis(a, 3, 2).swapaxes(0, 1)

    qc, kc, vc, gc, bc = [chunks(a) for a in (q, k, v, g, beta)]
    gcum = jnp.cumsum(gc, axis=-1)
    idx = jnp.arange(c)
    causal = idx[:, None] >= idx[None, :]
    strict = idx[:, None] > idx[None, :]
    decay = jnp.exp(jnp.where(causal, gcum[..., :, None] - gcum[..., None, :], -jnp.inf))
    kb = kc * bc[..., None]
    lmat = jnp.where(strict, jnp.einsum('nbhik,nbhjk->nbhij', kb, kc) * decay, 0.0)
    eye = jnp.broadcast_to(jnp.eye(c, dtype=jnp.float32), lmat.shape)
    tinv = lax.linalg.triangular_solve(eye + lmat, eye, left_side=True, lower=True)
    u = jnp.einsum('nbhij,nbhjv->nbhiv', tinv, vc * bc[..., None])
    w = jnp.einsum('nbhij,nbhjk->nbhik', tinv, kb * jnp.exp(gcum)[..., None])
    attn = jnp.einsum('nbhik,nbhjk->nbhij', qc, kc) * decay

    def step(s, xs):
        q_i, k_i, u_i, w_i, a_i, g_i = xs
        v_new = u_i - jnp.einsum('bhck,bhkv->bhcv', w_i, s)
        o = (jnp.einsum('bhck,bhkv->bhcv', q_i * jnp.exp(g_i)[..., None], s)
             + jnp.einsum('bhij,bhjv->bhiv', a_i, v_new))
        g_last = g_i[..., -1]
        s = (s * jnp.exp(g_last)[..., None, None]
             + jnp.einsum('bhck,bhcv->bhkv', k_i * jnp.exp(g_last[..., None] - g_i)[..., None], v_new))
        return s, o

    s_fin, o = lax.scan(step, s0, (qc, kc, u, w, attn, gcum))
    o = jnp.moveaxis(o.swapaxes(0, 1), 2, 3).reshape(bsz, n * c, h, dv)[:, :t]
    return o, s_fin


def s5_ssm(u, lam_re, lam_im, log_step, b_re, b_im, c_re, c_im, d_skip, h0_re, h0_im):
    f32 = jnp.float32
    u = u.astype(f32)
    lre = jnp.minimum(lam_re.astype(f32), -1e-4)
    lim = lam_im.astype(f32)
    dt = jnp.exp(log_step.astype(f32))[:, None]
    mag = jnp.exp(lre * dt)
    ang = lim * dt
    ab_re, ab_im = mag * jnp.cos(ang), mag * jnp.sin(ang)
    nr, ni = ab_re - 1.0, ab_im
    den = lre * lre + lim * lim
    co_re = (nr * lre + ni * lim) / den
    co_im = (ni * lre - nr * lim) / den
    br, bi = b_re.astype(f32), b_im.astype(f32)
    bb_re = co_re[..., None] * br - co_im[..., None] * bi
    bb_im = co_re[..., None] * bi + co_im[..., None] * br
    bu_re = jnp.einsum('btgh,gph->btgp', u, bb_re)
    bu_im = jnp.einsum('btgh,gph->btgp', u, bb_im)
    h0r, h0i = h0_re.astype(f32), h0_im.astype(f32)
    bu_re = bu_re.at[:, 0].add(ab_re * h0r - ab_im * h0i)
    bu_im = bu_im.at[:, 0].add(ab_re * h0i + ab_im * h0r)
    a_re = jnp.broadcast_to(ab_re, bu_re.shape)
    a_im = jnp.broadcast_to(ab_im, bu_im.shape)

    def combine(e1, e2):
        a1r, a1i, b1r, b1i = e1
        a2r, a2i, b2r, b2i = e2
        return (a2r * a1r - a2i * a1i, a2r * a1i + a2i * a1r,
                a2r * b1r - a2i * b1i + b2r, a2r * b1i + a2i * b1r + b2i)

    _, _, h_re, h_im = lax.associative_scan(combine, (a_re, a_im, bu_re, bu_im), axis=1)
    y = (jnp.einsum('ghp,btgp->btgh', c_re.astype(f32), h_re)
         - jnp.einsum('ghp,btgp->btgh', c_im.astype(f32), h_im)
         + d_skip.astype(f32) * u)
    return y, h_re[:, -1], h_im[:, -1]


def decoder_layer(x, c, conv_buf, s0, h0_re, h0_im,
                  w_ada, b_ada, g_pre_mix, g_post_mix, g_pre_ffn, g_post_ffn,
                  w_in, w_conv, a_log, dt_bias, g_dn_out,
                  lam_re, lam_im, log_step, b_re, b_im, c_re, c_im, d_skip,
                  w_glu, g_s5_out, w_out, w_gate, w_up, w_down):
    f32 = jnp.float32
    bsz, t, _ = x.shape
    mod = jax.nn.silu(c) @ w_ada + b_ada
    sh1, sc1, gt1, sh2, sc2, gt2 = [m[:, None, :] for m in jnp.split(mod, 6, axis=-1)]

    h = rmsnorm(x, g_pre_mix) * (1 + sc1) + sh1
    proj = h @ w_in
    cuts = [QKV_DIM, QKV_DIM + MIX_A, QKV_DIM + MIX_A + DN_HEADS, QKV_DIM + MIX_A + 2 * DN_HEADS]
    qkv, z, a_raw, b_raw, u = jnp.split(proj, cuts, axis=-1)

    xcat = jnp.concatenate([conv_buf.astype(qkv.dtype), qkv], axis=1)
    conv = sum(xcat[:, j:j + t] * w_conv[j] for j in range(CONV_WIDTH))
    new_conv = xcat[:, t:]
    qkv = jax.nn.silu(conv.astype(f32))
    q, k, v = [a.reshape(bsz, t, DN_HEADS, DN_HEAD_DIM) for a in jnp.split(qkv, 3, axis=-1)]
    q = l2norm(q) * (DN_HEAD_DIM ** -0.5)
    k = l2norm(k)
    beta = jax.nn.sigmoid(b_raw.astype(f32))
    g = -jnp.exp(a_log.astype(f32)) * jax.nn.softplus(a_raw.astype(f32) + dt_bias.astype(f32))
    o, s_new = gated_delta_rule(q, k, v, g, beta, s0.astype(f32))
    o = rmsnorm(o, g_dn_out) * jax.nn.silu(z.astype(f32).reshape(bsz, t, DN_HEADS, DN_HEAD_DIM))
    o = o.reshape(bsz, t, MIX_A).astype(x.dtype)

    y5, hr, hi = s5_ssm(u.reshape(bsz, t, S5_GROUPS, S5_GROUP), lam_re, lam_im, log_step,
                        b_re, b_im, c_re, c_im, d_skip, h0_re, h0_im)
    y5 = jax.nn.gelu(y5.reshape(bsz, t, MIX_B)).astype(x.dtype)
    ab = y5 @ w_glu
    y5 = ab[..., :MIX_B] * jax.nn.sigmoid(ab[..., MIX_B:])
    y5 = rmsnorm(y5, g_s5_out)

    mix = jnp.concatenate([o, y5], axis=-1) @ w_out
    x = x + gt1 * rmsnorm(mix, g_post_mix)

    h = rmsnorm(x, g_pre_ffn) * (1 + sc2) + sh2
    f = (jax.nn.silu(h @ w_gate) * (h @ w_up)) @ w_down
    x = x + gt2 * rmsnorm(f, g_post_ffn)
    return x, new_conv.astype(x.dtype), s_new.astype(x.dtype), hr.astype(x.dtype), hi.astype(x.dtype)


def setup_inputs(seed: int = 0) -> dict:
    key = jax.random.key(seed)
    ks = iter(jax.random.split(key, 48))
    nrm = lambda shape, s: jax.random.normal(next(ks), shape, jnp.float32) * s
    L = DEPTH
    gain = lambda n: 1.0 + nrm((L, n), 0.01)
    dt = jnp.exp(jax.random.uniform(next(ks), (L, DN_HEADS), jnp.float32, math.log(1e-3), math.log(1e-1)))
    lam_im0 = jnp.broadcast_to(math.pi * jnp.arange(S5_STATE, dtype=jnp.float32), (L, S5_GROUPS, S5_STATE))
    return {
        'x_prompt': nrm((BATCH, SEQ, D_MODEL), 1.0),
        'x_sample': nrm((DEC_BATCH, DEC_SEQ, D_MODEL), 1.0),
        'c_prompt': nrm((BATCH, D_MODEL), 1.0),
        'c_sample': nrm((DEC_BATCH, D_MODEL), 1.0),
        'state_conv': nrm((L, DEC_BATCH, CONV_WIDTH - 1, QKV_DIM), 1.0),
        'state_delta': nrm((L, DEC_BATCH, DN_HEADS, DN_HEAD_DIM, DN_HEAD_DIM), 0.1),
        'state_ssm_re': nrm((L, DEC_BATCH, S5_GROUPS, S5_STATE), 0.5),
        'state_ssm_im': nrm((L, DEC_BATCH, S5_GROUPS, S5_STATE), 0.5),
        'w_ada': nrm((L, D_MODEL, 6 * D_MODEL), D_MODEL ** -0.5),
        'b_ada': nrm((L, 6 * D_MODEL), 0.01),
        'g_pre_mix': gain(D_MODEL),
        'g_post_mix': gain(D_MODEL),
        'g_pre_ffn': gain(D_MODEL),
        'g_post_ffn': gain(D_MODEL),
        'w_in': nrm((L, D_MODEL, PROJ_DIM), D_MODEL ** -0.5),
        'w_conv': nrm((L, CONV_WIDTH, QKV_DIM), CONV_WIDTH ** -0.5),
        'a_log': jnp.log(jax.random.uniform(next(ks), (L, DN_HEADS), jnp.float32, 1.0, 16.0)),
        'dt_bias': dt + jnp.log(-jnp.expm1(-dt)),
        'g_dn_out': gain(DN_HEAD_DIM),
        'lam_re': -0.5 + nrm((L, S5_GROUPS, S5_STATE), 0.01),
        'lam_im': lam_im0 + nrm((L, S5_GROUPS, S5_STATE), 0.01),
        'log_step': jax.random.uniform(next(ks), (L, S5_GROUPS), jnp.float32, math.log(1e-3), math.log(1e-1)),
        'b_re': nrm((L, S5_GROUPS, S5_STATE, S5_GROUP), (2 * S5_GROUP) ** -0.5),
        'b_im': nrm((L, S5_GROUPS, S5_STATE, S5_GROUP), (2 * S5_GROUP) ** -0.5),
        'c_re': nrm((L, S5_GROUPS, S5_GROUP, S5_STATE), (2 * S5_STATE) ** -0.5),
        'c_im': nrm((L, S5_GROUPS, S5_GROUP, S5_STATE), (2 * S5_STATE) ** -0.5),
        'd_skip': nrm((L, S5_GROUPS, S5_GROUP), 1.0),
        'w_glu': nrm((L, MIX_B, 2 * MIX_B), MIX_B ** -0.5),
        'g_s5_out': gain(MIX_B),
        'w_out': nrm((L, D_MODEL, D_MODEL), D_MODEL ** -0.5),
        'w_gate': nrm((L, D_MODEL, D_FF), D_MODEL ** -0.5),
        'w_up': nrm((L, D_MODEL, D_FF), D_MODEL ** -0.5),
        'w_down': nrm((L, D_FF, D_MODEL), D_FF ** -0.5),
    }


def reference(x_prompt, x_sample, c_prompt, c_sample, state_conv, state_delta, state_ssm_re, state_ssm_im,
              w_ada, b_ada, g_pre_mix, g_post_mix, g_pre_ffn, g_post_ffn,
              w_in, w_conv, a_log, dt_bias, g_dn_out,
              lam_re, lam_im, log_step, b_re, b_im, c_re, c_im, d_skip,
              w_glu, g_s5_out, w_out, w_gate, w_up, w_down):
    weights = (w_ada, b_ada, g_pre_mix, g_post_mix, g_pre_ffn, g_post_ffn,
               w_in, w_conv, a_log, dt_bias, g_dn_out,
               lam_re, lam_im, log_step, b_re, b_im, c_re, c_im, d_skip,
               w_glu, g_s5_out, w_out, w_gate, w_up, w_down)
    bp = x_prompt.shape[0]
    dt_ = x_prompt.dtype
    yp, ys = x_prompt, x_sample
    pc, pd, pr, pim, sc, sd, sr, sim = [], [], [], [], [], [], [], []
    for l in range(DEPTH):
        lw = [w[l] for w in weights]
        zc = jnp.zeros((bp, CONV_WIDTH - 1, QKV_DIM), dt_)
        zd = jnp.zeros((bp, DN_HEADS, DN_HEAD_DIM, DN_HEAD_DIM), dt_)
        zs = jnp.zeros((bp, S5_GROUPS, S5_STATE), dt_)
        yp, c1, d1, r1, i1 = decoder_layer(yp, c_prompt, zc, zd, zs, zs, *lw)
        ys, c2, d2, r2, i2 = decoder_layer(ys, c_sample, state_conv[l], state_delta[l],
                                           state_ssm_re[l], state_ssm_im[l], *lw)
        pc.append(c1); pd.append(d1); pr.append(r1); pim.append(i1)
        sc.append(c2); sd.append(d2); sr.append(r2); sim.append(i2)
    conv_prompt, delta_prompt = jnp.stack(pc), jnp.stack(pd)
    ssm_re_prompt, ssm_im_prompt = jnp.stack(pr), jnp.stack(pim)
    conv_sample, delta_sample = jnp.stack(sc), jnp.stack(sd)
    ssm_re_sample, ssm_im_sample = jnp.stack(sr), jnp.stack(sim)
    return (yp, ys, conv_prompt, delta_prompt, ssm_re_prompt, ssm_im_prompt,
            conv_sample, delta_sample, ssm_re_sample, ssm_im_sample)
```

```python
import functools
import math

import jax
import jax.numpy as jnp
from jax import lax
from jax.experimental import pallas as pl
from jax.experimental.pallas import tpu as pltpu

F32 = jnp.float32
BF16 = jnp.bfloat16
EPS = 1e-6
LANE = 128
SUBLANE = 8
V7X_VMEM_BYTES = 64 << 20
VMEM_LIMIT = V7X_VMEM_BYTES - (8 << 20)

DN_HEAD_DIM = 128
CONV_WIDTH = 4
S5_GROUP = 16
S5_STATE = 64
S5_SLAB_GROUPS = LANE // S5_GROUP
S5_SLAB_STATES = S5_SLAB_GROUPS * S5_STATE


def _params(*sem):
    return pltpu.CompilerParams(dimension_semantics=sem, vmem_limit_bytes=VMEM_LIMIT)


def _silu(x):
    return x * jax.nn.sigmoid(x)


def _mm(a, b):
    return jnp.dot(a.astype(BF16), b.astype(BF16), preferred_element_type=F32)


def _mm_nt(a, b):
    return lax.dot_general(a.astype(BF16), b.astype(BF16), (((1,), (1,)), ((), ())),
                           preferred_element_type=F32)


def _mm_tn(a, b):
    return lax.dot_general(a.astype(BF16), b.astype(BF16), (((0,), (0,)), ((), ())),
                           preferred_element_type=F32)


def _split3(x):
    x1 = x.astype(BF16)
    r1 = x - x1.astype(F32)
    x2 = r1.astype(BF16)
    x3 = (r1 - x2.astype(F32)).astype(BF16)
    return x1, x2, x3


def _ada_body(c_ref, w_ref, b_ref, o_ref):
    s = _silu(c_ref[...])
    o_ref[...] = _mm(s, w_ref[...]) + b_ref[...]


def _ada(c, w, b):
    rows, d = c.shape
    n = w.shape[1]
    tn = 1024
    return pl.pallas_call(
        _ada_body,
        grid=(n // tn,),
        in_specs=[pl.BlockSpec((rows, d), lambda j: (0, 0)),
                  pl.BlockSpec((d, tn), lambda j: (0, j)),
                  pl.BlockSpec((1, tn), lambda j: (0, j))],
        out_specs=pl.BlockSpec((rows, tn), lambda j: (0, j)),
        out_shape=jax.ShapeDtypeStruct((rows, n), F32),
        compiler_params=_params("arbitrary"),
        name="ada",
    )(c, w, b)


def _token_blocks(nb, t, rows):
    if t >= rows:
        assert t % rows == 0
        return 1, rows
    assert rows % t == 0 and nb % (rows // t) == 0
    return rows // t, t


def _modulated_norm(x, g, scale, shift):
    y = x * lax.rsqrt(jnp.mean(x * x, axis=-1, keepdims=True) + EPS) * g
    return y * (1.0 + scale) + shift


def _nmm_body(x_ref, mod_ref, g_ref, w_ref, o_ref, h_ref, *, shift, scale):
    @pl.when(pl.program_id(1) == 0)
    def _():
        x = x_ref[...]
        bb, tt, d = x.shape
        h = _modulated_norm(x, g_ref[...], mod_ref[:, scale:scale + 1, :], mod_ref[:, shift:shift + 1, :])
        h_ref[...] = h.reshape(bb * tt, d).astype(BF16)

    o_ref[...] = jnp.dot(h_ref[...], w_ref[...], preferred_element_type=F32)


def _nmm(x3, mod, g, w, *, shift, scale, rows, tn):
    nb, t, d = x3.shape
    n = w.shape[1]
    bb, tt = _token_blocks(nb, t, rows)
    nt = t // tt
    return pl.pallas_call(
        functools.partial(_nmm_body, shift=shift, scale=scale),
        grid=(nb * t // rows, n // tn),
        in_specs=[pl.BlockSpec((bb, tt, d), lambda i, j: (i // nt, i % nt, 0)),
                  pl.BlockSpec((bb, 6, d), lambda i, j: (i // nt, 0, 0)),
                  pl.BlockSpec((1, d), lambda i, j: (0, 0)),
                  pl.BlockSpec((d, tn), lambda i, j: (0, j))],
        out_specs=pl.BlockSpec((rows, tn), lambda i, j: (i, j)),
        out_shape=jax.ShapeDtypeStruct((nb * t, n), F32),
        scratch_shapes=[pltpu.VMEM((rows, d), BF16)],
        compiler_params=_params("parallel", "arbitrary"),
        name="nmm",
    )(x3, mod, g, w)


def _seg_masks(c, seg):
    row = lax.broadcasted_iota(jnp.int32, (c, c), 0)
    col = lax.broadcasted_iota(jnp.int32, (c, c), 1)
    sh = int(math.log2(seg))
    same = (row >> sh) == (col >> sh)
    return row, col, same, same & (row >= col), same & (row > col)


def _tri_inverse_offdiag(lmat, row, col, base, seg):
    bsh = int(math.log2(base))
    l0 = jnp.where((row >> bsh) == (col >> bsh), lmat, 0.0)
    pt = -l0
    q = _mm(l0, l0)
    k = 2
    while k < base:
        pt = pt + q + _mm(pt, q)
        k *= 2
        if k < base:
            q = _mm(q, q)
    m = base
    while m < seg:
        msh = int(math.log2(m))
        e = jnp.where(((row >> (msh + 1)) == (col >> (msh + 1))) & ((row >> msh) != (col >> msh)), lmat, 0.0)
        x = e + _mm(pt, e)
        pt = pt - (x + _mm(x, pt))
        m *= 2
    return pt


def _delta_gates(ab, alog, dtb, causal_bf, same_bf, nheads):
    lane = lax.broadcasted_iota(jnp.int32, ab.shape, 1)
    x = ab + dtb
    softplus = jnp.maximum(x, 0.0) + jnp.log1p(jnp.exp(-jnp.abs(x)))
    g = jnp.where(lane < nheads, -jnp.exp(alog) * softplus, 0.0)
    beta = jax.nn.sigmoid(ab)
    pieces = _split3(g)
    gcum = sum(jnp.dot(causal_bf, p, preferred_element_type=F32) for p in pieces)
    glast = sum(jnp.dot(same_bf, p, preferred_element_type=F32) for p in pieces)
    return gcum, beta, glast


def _decay_logits(gcum, nheads):
    c = gcum.shape[0]
    lane = lax.broadcasted_iota(jnp.int32, (c, LANE), 1)
    ones = jnp.ones((c, LANE), BF16)
    pad = jnp.zeros((LANE - c, 2 * LANE), BF16) if c < LANE else None
    out = None
    for p in _split3(gcum):
        lhs = jnp.concatenate([p, ones], axis=1)
        blocks = []
        for h in range(nheads):
            onehot = lane == h
            blk = jnp.concatenate([jnp.where(onehot, 1.0, 0.0).astype(BF16),
                                   jnp.where(onehot, -p, jnp.zeros_like(p))], axis=1)
            blocks.append(blk)
            if pad is not None:
                blocks.append(pad)
        rhs = jnp.concatenate(blocks, axis=0)
        d = lax.dot_general(lhs, rhs, (((1,), (1,)), ((), ())), preferred_element_type=F32)
        out = d if out is None else out + d
    return out


def _delta_head_intra(act, h, nheads, gates, dlog, masks, base, seg):
    gcum, beta, glast = gates
    row, col, _, causal, strict = masks
    c = act.shape[0]
    dh = DN_HEAD_DIM
    q = act[:, h * dh:(h + 1) * dh]
    k = act[:, (nheads + h) * dh:(nheads + h + 1) * dh]
    v = act[:, (2 * nheads + h) * dh:(2 * nheads + h + 1) * dh]
    q = q * lax.rsqrt(jnp.sum(q * q, axis=-1, keepdims=True) + 1e-6) * (dh ** -0.5)
    k = k * lax.rsqrt(jnp.sum(k * k, axis=-1, keepdims=True) + 1e-6)
    b = beta[:, nheads + h:nheads + h + 1]
    gc = gcum[:, h:h + 1]
    eg = jnp.exp(gc)
    ed = jnp.exp(glast[:, h:h + 1] - gc)
    kb = k * b
    vb = v * b
    kbg = kb * eg
    qg = q * eg
    kdec = k * ed
    dl = dlog[:, h * LANE:h * LANE + c]
    dec = jnp.where(causal, jnp.exp(jnp.where(causal, dl, 0.0)), 0.0)
    kq = _mm_nt(jnp.concatenate([kb, q], axis=0), k)
    lmat = jnp.where(strict, kq[:c] * dec, 0.0)
    attn = kq[c:] * dec
    tt = _tri_inverse_offdiag(lmat, row, col, base, seg)
    rhs = jnp.concatenate([vb, kbg], axis=1)
    uw = rhs + _mm(tt, rhs)
    return uw[:, :dh], uw[:, dh:], qg, kdec, attn


def _delta_out(o, z, gdn):
    on = o * lax.rsqrt(jnp.mean(o * o, axis=-1, keepdims=True) + EPS) * gdn
    return on * _silu(z)


def _delta_prompt_body(qkv_ref, z_ref, ab_ref, cst_ref, s0_ref, wc_ref, alog_ref, dtb_ref, gdn_ref,
                       o_ref, cnew_ref, snew_ref, xc_ref, s_ref, *, c, nheads, base):
    dh = DN_HEAD_DIM
    halo = SUBLANE
    taps = CONV_WIDTH - 1

    @pl.when(pl.program_id(1) == 0)
    def _():
        xc_ref[0:halo, :] = jnp.zeros((halo, xc_ref.shape[1]), F32)
        xc_ref[halo - taps:halo, :] = cst_ref[0]
        s_ref[...] = s0_ref[0]

    xc_ref[halo:halo + c, :] = qkv_ref[...]
    conv = xc_ref[halo - taps:halo - taps + c, :] * wc_ref[0:1, :]
    for j in range(1, CONV_WIDTH):
        conv = conv + xc_ref[halo - taps + j:halo - taps + j + c, :] * wc_ref[j:j + 1, :]
    cnew_ref[0] = xc_ref[halo + c - taps:halo + c, :]
    xc_ref[0:halo, :] = xc_ref[c:c + halo, :]
    act = _silu(conv)

    masks = _seg_masks(c, c)
    causal_bf = jnp.where(masks[3], 1.0, 0.0).astype(BF16)
    same_bf = jnp.ones((c, c), BF16)
    gates = _delta_gates(ab_ref[...], alog_ref[...], dtb_ref[...], causal_bf, same_bf, nheads)
    dlog = _decay_logits(gates[0], nheads)
    eglast = jnp.exp(gates[2][0:1, :])
    gdn = gdn_ref[...]
    for h in range(nheads):
        u, w, qg, kdec, attn = _delta_head_intra(act, h, nheads, gates, dlog, masks, base, c)
        s = s_ref[h]
        ws = _mm(jnp.concatenate([w, qg], axis=0), s)
        vnew = u - ws[:c]
        o = ws[c:] + _mm(attn, vnew)
        s_ref[h] = s * eglast[:, h:h + 1] + _mm_tn(kdec, vnew)
        o_ref[:, h * dh:(h + 1) * dh] = _delta_out(o, z_ref[:, h * dh:(h + 1) * dh], gdn).astype(o_ref.dtype)

    @pl.when(pl.program_id(1) == pl.num_programs(1) - 1)
    def _():
        snew_ref[0] = s_ref[...]


def _delta_prompt(qkvz, uab, conv_state, s0, w_conv, alog, dtb, gdn, *, nb, t, c, base):
    nheads = s0.shape[1]
    dh = DN_HEAD_DIM
    mix = nheads * dh
    nc = t // c
    return pl.pallas_call(
        functools.partial(_delta_prompt_body, c=c, nheads=nheads, base=base),
        grid=(nb, nc),
        in_specs=[pl.BlockSpec((c, 3 * mix), lambda b, i: (b * nc + i, 0)),
                  pl.BlockSpec((c, mix), lambda b, i: (b * nc + i, 3)),
                  pl.BlockSpec((c, LANE), lambda b, i: (b * nc + i, uab.shape[1] // LANE - 1)),
                  pl.BlockSpec((1, CONV_WIDTH - 1, 3 * mix), lambda b, i: (b, 0, 0)),
                  pl.BlockSpec((1, nheads, dh, dh), lambda b, i: (b, 0, 0, 0)),
                  pl.BlockSpec((CONV_WIDTH, 3 * mix), lambda b, i: (0, 0)),
                  pl.BlockSpec((1, LANE), lambda b, i: (0, 0)),
                  pl.BlockSpec((1, LANE), lambda b, i: (0, 0)),
                  pl.BlockSpec((1, dh), lambda b, i: (0, 0))],
        out_specs=[pl.BlockSpec((c, mix), lambda b, i: (b * nc + i, 0)),
                   pl.BlockSpec((1, CONV_WIDTH - 1, 3 * mix), lambda b, i: (b, 0, 0)),
                   pl.BlockSpec((1, nheads, dh, dh), lambda b, i: (b, 0, 0, 0))],
        out_shape=[jax.ShapeDtypeStruct((nb * t, mix), BF16),
                   jax.ShapeDtypeStruct((nb, CONV_WIDTH - 1, 3 * mix), F32),
                   jax.ShapeDtypeStruct((nb, nheads, dh, dh), F32)],
        scratch_shapes=[pltpu.VMEM((c + SUBLANE, 3 * mix), F32),
                        pltpu.VMEM((nheads, dh, dh), F32)],
        compiler_params=_params("parallel", "arbitrary"),
        name="delta_prompt",
    )(qkvz, qkvz, uab, conv_state, s0, w_conv, alog, dtb, gdn)


def _delta_sample_body(qkv_ref, z_ref, ab_ref, cst_ref, s0_ref, wc_ref, alog_ref, dtb_ref, gdn_ref,
                       o_ref, cnew_ref, snew_ref,
                       xs_ref, w_s, qg_s, u_s, kd_s, vn_s, qs_s, attn_s, egl_s, *, bb, t, nheads):
    dh = DN_HEAD_DIM
    c = bb * t
    halo = SUBLANE
    taps = CONV_WIDTH - 1

    xs_ref[:, 0:halo, :] = jnp.zeros((bb, halo, xs_ref.shape[2]), F32)
    xs_ref[:, halo - taps:halo, :] = cst_ref[...]
    xs_ref[:, halo:halo + t, :] = qkv_ref[...]
    conv = xs_ref[:, halo - taps:halo - taps + t, :] * wc_ref[0:1, :]
    for j in range(1, CONV_WIDTH):
        conv = conv + xs_ref[:, halo - taps + j:halo - taps + j + t, :] * wc_ref[j:j + 1, :]
    cnew_ref[...] = xs_ref[:, halo + t - taps:halo + t, :]
    act = _silu(conv).reshape(c, conv.shape[2])

    masks = _seg_masks(c, t)
    causal_bf = jnp.where(masks[3], 1.0, 0.0).astype(BF16)
    same_bf = jnp.where(masks[2], 1.0, 0.0).astype(BF16)
    gates = _delta_gates(ab_ref[...], alog_ref[...], dtb_ref[...], causal_bf, same_bf, nheads)
    dlog = _decay_logits(gates[0], nheads)
    egl_s[...] = jnp.exp(gates[2])
    for h in range(nheads):
        u, w, qg, kdec, attn = _delta_head_intra(act, h, nheads, gates, dlog, masks, t, t)
        hs = slice(h * dh, (h + 1) * dh)
        u_s[:, hs] = u
        w_s[:, hs] = w
        qg_s[:, hs] = qg
        kd_s[:, hs] = kdec
        attn_s[h] = attn

    def per_batch(b, carry):
        r = pl.multiple_of(b * t, t)
        rows = pl.ds(r, t)
        for h in range(nheads):
            hs = slice(h * dh, (h + 1) * dh)
            s = s0_ref[b, h]
            ws = _mm(jnp.concatenate([w_s[rows, hs], qg_s[rows, hs]], axis=0), s)
            vnew = u_s[rows, hs] - ws[:t]
            vn_s[rows, hs] = vnew
            qs_s[rows, hs] = ws[t:]
            egl = egl_s[pl.ds(r, 1), h:h + 1]
            snew_ref[b, h] = s * egl + _mm_tn(kd_s[rows, hs], vnew)
        return carry

    lax.fori_loop(0, bb, per_batch, 0)

    gdn = gdn_ref[...]
    for h in range(nheads):
        hs = slice(h * dh, (h + 1) * dh)
        o = qs_s[:, hs] + _mm(attn_s[h], vn_s[:, hs])
        o_ref[:, hs] = _delta_out(o, z_ref[:, hs], gdn).astype(o_ref.dtype)


def _delta_sample(qkvz, uab, conv_state, s0, w_conv, alog, dtb, gdn, *, nb, t, bb):
    nheads = s0.shape[1]
    dh = DN_HEAD_DIM
    mix = nheads * dh
    c = bb * t
    qkvz3 = qkvz.reshape(nb, t, qkvz.shape[1])
    return pl.pallas_call(
        functools.partial(_delta_sample_body, bb=bb, t=t, nheads=nheads),
        grid=(nb // bb,),
        in_specs=[pl.BlockSpec((bb, t, 3 * mix), lambda i: (i, 0, 0)),
                  pl.BlockSpec((c, mix), lambda i: (i, 3)),
                  pl.BlockSpec((c, LANE), lambda i: (i, uab.shape[1] // LANE - 1)),
                  pl.BlockSpec((bb, CONV_WIDTH - 1, 3 * mix), lambda i: (i, 0, 0)),
                  pl.BlockSpec((bb, nheads, dh, dh), lambda i: (i, 0, 0, 0)),
                  pl.BlockSpec((CONV_WIDTH, 3 * mix), lambda i: (0, 0)),
                  pl.BlockSpec((1, LANE), lambda i: (0, 0)),
                  pl.BlockSpec((1, LANE), lambda i: (0, 0)),
                  pl.BlockSpec((1, dh), lambda i: (0, 0))],
        out_specs=[pl.BlockSpec((c, mix), lambda i: (i, 0)),
                   pl.BlockSpec((bb, CONV_WIDTH - 1, 3 * mix), lambda i: (i, 0, 0)),
                   pl.BlockSpec((bb, nheads, dh, dh), lambda i: (i, 0, 0, 0))],
        out_shape=[jax.ShapeDtypeStruct((nb * t, mix), BF16),
                   jax.ShapeDtypeStruct((nb, CONV_WIDTH - 1, 3 * mix), F32),
                   jax.ShapeDtypeStruct((nb, nheads, dh, dh), F32)],
        scratch_shapes=[pltpu.VMEM((bb, SUBLANE + t, 3 * mix), F32)]
                       + [pltpu.VMEM((c, mix), F32) for _ in range(6)]
                       + [pltpu.VMEM((nheads, c, c), F32), pltpu.VMEM((c, LANE), F32)],
        compiler_params=_params("parallel"),
        name="delta_sample",
    )(qkvz3, qkvz, uab, conv_state, s0, w_conv, alog, dtb, gdn)


def _s5_prep_body(lre_ref, lim_ref, ls_ref, bre_ref, bim_ref, wa_ref, abre_ref, abim_ref):
    lre = jnp.minimum(lre_ref[...], -1e-4)
    lim = lim_ref[...]
    dt = jnp.exp(ls_ref[...])
    mag = jnp.exp(lre * dt)
    ang = lim * dt
    abr = mag * jnp.cos(ang)
    abi = mag * jnp.sin(ang)
    nr = abr - 1.0
    ni = abi
    den = lre * lre + lim * lim
    cor = (nr * lre + ni * lim) / den
    coi = (ni * lre - nr * lim) / den
    bre = bre_ref[...]
    bim = bim_ref[...]
    shape2 = bre.shape[1:]
    row = lax.broadcasted_iota(jnp.int32, shape2, 0)
    col = lax.broadcasted_iota(jnp.int32, shape2, 1)
    diag = ((row // S5_GROUP) == (col // S5_STATE))[None]
    half = shape2[1]
    wa_ref[:, :, 0:half] = jnp.where(diag, cor * bre - coi * bim, 0.0).astype(BF16)
    wa_ref[:, :, half:2 * half] = jnp.where(diag, cor * bim + coi * bre, 0.0).astype(BF16)
    abre_ref[...] = abr
    abim_ref[...] = abi


def _s5_prep(lam_re, lam_im, log_step, b_re, b_im):
    groups, states, gch = b_re.shape
    slabs = groups // S5_SLAB_GROUPS
    cols = S5_SLAB_STATES

    def row_param(p):
        return p.reshape(slabs, 1, cols)

    def tiled_b(b):
        bt = jnp.transpose(b, (0, 2, 1)).reshape(slabs, LANE, states)
        return jnp.tile(bt, (1, 1, S5_SLAB_GROUPS))

    ls = jnp.broadcast_to(log_step[:, None], (groups, states))
    wa, abre, abim = pl.pallas_call(
        _s5_prep_body,
        out_shape=[jax.ShapeDtypeStruct((slabs, LANE, 2 * cols), BF16),
                   jax.ShapeDtypeStruct((slabs, 1, cols), F32),
                   jax.ShapeDtypeStruct((slabs, 1, cols), F32)],
        compiler_params=pltpu.CompilerParams(vmem_limit_bytes=VMEM_LIMIT),
        name="s5prep",
    )(row_param(lam_re), row_param(lam_im), row_param(ls), tiled_b(b_re), tiled_b(b_im))
    nblk = groups * states // LANE
    return wa, abre.reshape(nblk, 1, LANE), abim.reshape(nblk, 1, LANE)


def _s5_out_weights(c):
    groups, gch, states = c.shape
    slabs = groups // S5_SLAB_GROUPS
    ct = jnp.transpose(c, (0, 2, 1)).reshape(slabs, S5_SLAB_STATES, gch)
    ct = jnp.tile(ct, (1, 1, S5_SLAB_GROUPS))
    row = jnp.arange(S5_SLAB_STATES)[:, None] // states
    col = jnp.arange(LANE)[None, :] // gch
    return jnp.where((row == col)[None], ct, 0.0).astype(BF16)


def _gelu_tanh(x):
    return x * (0.5 * (1.0 + jnp.tanh(math.sqrt(2.0 / math.pi) * (x + 0.044715 * (x * x * x)))))


S5_SCAN_BLOCKS = 8


def _s5_body(u_ref, h0re_ref, h0im_ref, wa_ref, abre_ref, abim_ref, wcre_ref, wcim_ref, dsk_ref,
             wglu_ref, gs5_ref, y_ref, hre_ref, him_ref, xre, xim, cre, cim, *, nb, lc):
    rows = nb * lc
    nblk = xre.shape[0]
    slabs = wa_ref.shape[0]
    per_slab = nblk // slabs
    half = per_slab * LANE

    @pl.when(pl.program_id(1) == 0)
    def _():
        for j in range(nblk):
            cre[j] = h0re_ref[:, j * LANE:(j + 1) * LANE]
            cim[j] = h0im_ref[:, j * LANE:(j + 1) * LANE]

    u = u_ref[...].reshape(rows, u_ref.shape[2])
    ub = u.astype(BF16)
    for s in range(slabs):
        x = jnp.dot(ub[:, s * LANE:(s + 1) * LANE], wa_ref[s], preferred_element_type=F32)
        for k in range(per_slab):
            xre[s * per_slab + k] = x[:, k * LANE:(k + 1) * LANE]
            xim[s * per_slab + k] = x[:, half + k * LANE:half + (k + 1) * LANE]

    for j0 in range(0, nblk, S5_SCAN_BLOCKS):
        js = list(range(j0, j0 + S5_SCAN_BLOCKS))
        ar = [jnp.broadcast_to(abre_ref[j], (nb, LANE)) for j in js]
        ai = [jnp.broadcast_to(abim_ref[j], (nb, LANE)) for j in js]

        def step(tt, carry, js=js, ar=ar, ai=ai):
            out = []
            for n, j in enumerate(js):
                hr, hi = carry[2 * n], carry[2 * n + 1]
                at = pl.ds(tt, nb, stride=lc)
                nr = ar[n] * hr - ai[n] * hi + xre[j, at, :]
                ni = ar[n] * hi + ai[n] * hr + xim[j, at, :]
                xre[j, at, :] = nr
                xim[j, at, :] = ni
                out += [nr, ni]
            return tuple(out)

        init = []
        for j in js:
            init += [cre[j], cim[j]]
        fin = lax.fori_loop(0, lc, step, tuple(init))
        for n, j in enumerate(js):
            cre[j] = fin[2 * n]
            cim[j] = fin[2 * n + 1]

    ys = []
    for s in range(slabs):
        hre = jnp.concatenate([xre[s * per_slab + k] for k in range(per_slab)], axis=1)
        him = jnp.concatenate([xim[s * per_slab + k] for k in range(per_slab)], axis=1)
        ys.append(_mm(hre, wcre_ref[s]) - _mm(him, wcim_ref[s]))
    y = jnp.concatenate(ys, axis=1) + dsk_ref[...] * u
    ab = _mm(_gelu_tanh(y), wglu_ref[...])
    n = y.shape[1]
    glu = ab[:, :n] * jax.nn.sigmoid(ab[:, n:])
    out = glu * lax.rsqrt(jnp.mean(glu * glu, axis=-1, keepdims=True) + EPS) * gs5_ref[...]
    y_ref[...] = out.reshape(y_ref.shape).astype(y_ref.dtype)

    @pl.when(pl.program_id(1) == pl.num_programs(1) - 1)
    def _():
        for j in range(nblk):
            hre_ref[:, j * LANE:(j + 1) * LANE] = cre[j]
            him_ref[:, j * LANE:(j + 1) * LANE] = cim[j]


def _s5(uab, h0re, h0im, wa, abre, abim, wcre, wcim, dskip, wglu, gs5, *, nb_total, t, nb, lc):
    mixb = dskip.shape[1]
    nstate = h0re.shape[1]
    nblk = nstate // LANE
    uab3 = uab.reshape(nb_total, t, uab.shape[1])
    full = lambda a: pl.BlockSpec(a.shape, lambda i, j: (0,) * a.ndim)
    y, hre, him = pl.pallas_call(
        functools.partial(_s5_body, nb=nb, lc=lc),
        grid=(nb_total // nb, t // lc),
        in_specs=[pl.BlockSpec((nb, lc, mixb), lambda i, j: (i, j, 0)),
                  pl.BlockSpec((nb, nstate), lambda i, j: (i, 0)),
                  pl.BlockSpec((nb, nstate), lambda i, j: (i, 0)),
                  full(wa), full(abre), full(abim), full(wcre), full(wcim), full(dskip), full(wglu), full(gs5)],
        out_specs=[pl.BlockSpec((nb, lc, mixb), lambda i, j: (i, j, 0)),
                   pl.BlockSpec((nb, nstate), lambda i, j: (i, 0)),
                   pl.BlockSpec((nb, nstate), lambda i, j: (i, 0))],
        out_shape=[jax.ShapeDtypeStruct((nb_total, t, mixb), F32),
                   jax.ShapeDtypeStruct((nb_total, nstate), F32),
                   jax.ShapeDtypeStruct((nb_total, nstate), F32)],
        scratch_shapes=[pltpu.VMEM((nblk, nb * lc, LANE), F32),
                        pltpu.VMEM((nblk, nb * lc, LANE), F32),
                        pltpu.VMEM((nblk, nb, LANE), F32),
                        pltpu.VMEM((nblk, nb, LANE), F32)],
        compiler_params=_params("parallel", "arbitrary"),
        name="s5",
    )(uab3, h0re, h0im, wa, abre, abim, wcre, wcim, dskip, wglu, gs5)
    return y.reshape(nb_total * t, mixb), hre, him


def _outproj_body(o_ref, y_ref, wt_ref, wb_ref, x_ref, mod_ref, g_ref, out_ref, *, gate):
    mix = (jnp.dot(o_ref[...], wt_ref[...], preferred_element_type=F32)
           + _mm(y_ref[...], wb_ref[...]))
    n = mix * lax.rsqrt(jnp.mean(mix * mix, axis=-1, keepdims=True) + EPS) * g_ref[...]
    x = x_ref[...]
    out_ref[...] = x + mod_ref[:, gate:gate + 1, :] * n.reshape(x.shape)


def _outproj(o, y5, wtop, wbot, x3, mod, g, *, gate, rows):
    nb, t, d = x3.shape
    bb, tt = _token_blocks(nb, t, rows)
    nt = t // tt
    ka, kb = o.shape[1], y5.shape[1]
    return pl.pallas_call(
        functools.partial(_outproj_body, gate=gate),
        grid=(nb * t // rows,),
        in_specs=[pl.BlockSpec((rows, ka), lambda i: (i, 0)),
                  pl.BlockSpec((rows, kb), lambda i: (i, 0)),
                  pl.BlockSpec((ka, d), lambda i: (0, 0)),
                  pl.BlockSpec((kb, d), lambda i: (0, 0)),
                  pl.BlockSpec((bb, tt, d), lambda i: (i // nt, i % nt, 0)),
                  pl.BlockSpec((bb, 6, d), lambda i: (i // nt, 0, 0)),
                  pl.BlockSpec((1, d), lambda i: (0, 0))],
        out_specs=pl.BlockSpec((bb, tt, d), lambda i: (i // nt, i % nt, 0)),
        out_shape=jax.ShapeDtypeStruct((nb, t, d), F32),
        compiler_params=_params("parallel"),
        name="outproj",
    )(o, y5, wtop, wbot, x3, mod, g)


def _ffn_body(x_ref, mod_ref, gpre_ref, wg_ref, wu_ref, wd_ref, gpost_ref, out_ref, h_ref, acc_ref,
              *, shift, scale, gate):
    j = pl.program_id(1)

    @pl.when(j == 0)
    def _():
        x = x_ref[...]
        bb, tt, d = x.shape
        h = _modulated_norm(x, gpre_ref[...], mod_ref[:, scale:scale + 1, :], mod_ref[:, shift:shift + 1, :])
        h_ref[...] = h.reshape(bb * tt, d).astype(BF16)
        acc_ref[...] = jnp.zeros(acc_ref.shape, F32)

    h = h_ref[...]
    a = (_silu(jnp.dot(h, wg_ref[...], preferred_element_type=F32))
         * jnp.dot(h, wu_ref[...], preferred_element_type=F32))
    acc_ref[...] += _mm(a, wd_ref[...])

    @pl.when(j == pl.num_programs(1) - 1)
    def _():
        f = acc_ref[...]
        n = f * lax.rsqrt(jnp.mean(f * f, axis=-1, keepdims=True) + EPS) * gpost_ref[...]
        x = x_ref[...]
        out_ref[...] = x + mod_ref[:, gate:gate + 1, :] * n.reshape(x.shape)


def _ffn(x3, mod, gpre, wg, wu, wd, gpost, *, shift, scale, gate, rows, tf):
    nb, t, d = x3.shape
    dff = wg.shape[1]
    bb, tt = _token_blocks(nb, t, rows)
    nt = t // tt
    return pl.pallas_call(
        functools.partial(_ffn_body, shift=shift, scale=scale, gate=gate),
        grid=(nb * t // rows, dff // tf),
        in_specs=[pl.BlockSpec((bb, tt, d), lambda i, j: (i // nt, i % nt, 0)),
                  pl.BlockSpec((bb, 6, d), lambda i, j: (i // nt, 0, 0)),
                  pl.BlockSpec((1, d), lambda i, j: (0, 0)),
                  pl.BlockSpec((d, tf), lambda i, j: (0, j)),
                  pl.BlockSpec((d, tf), lambda i, j: (0, j)),
                  pl.BlockSpec((tf, d), lambda i, j: (j, 0)),
                  pl.BlockSpec((1, d), lambda i, j: (0, 0))],
        out_specs=pl.BlockSpec((bb, tt, d), lambda i, j: (i // nt, i % nt, 0)),
        out_shape=jax.ShapeDtypeStruct((nb, t, d), F32),
        scratch_shapes=[pltpu.VMEM((rows, d), BF16), pltpu.VMEM((rows, d), F32)],
        compiler_params=_params("parallel", "arbitrary"),
        name="ffn",
    )(x3, mod, gpre, wg, wu, wd, gpost)


ROWS = 512
PROMPT_CHUNK = 128
PROMPT_BASE = 16
SAMPLE_BATCH = 8
S5_PROMPT_STEPS = 128
S5_SAMPLE_BATCH = 8


def _layer_weights(w_ada, b_ada, g_pre_mix, g_post_mix, g_pre_ffn, g_post_ffn,
                   w_in, w_conv, a_log, dt_bias, g_dn_out,
                   lam_re, lam_im, log_step, b_re, b_im, c_re, c_im, d_skip,
                   w_glu, g_s5_out, w_out, w_gate, w_up, w_down):
    d = w_in.shape[0]
    nheads = a_log.shape[0]
    mix_a = nheads * DN_HEAD_DIM
    qkvz = 4 * mix_a
    gates = 2 * nheads
    wq = w_in[:, :qkvz].astype(BF16)
    wuab = jnp.concatenate([w_in[:, qkvz + gates:], w_in[:, qkvz:qkvz + gates],
                            jnp.zeros((d, LANE - gates), w_in.dtype)], axis=1).astype(BF16)
    pad = lambda v: jnp.pad(v, (0, LANE - v.shape[0]))[None, :]
    wa, abre, abim = _s5_prep(lam_re, lam_im, log_step, b_re, b_im)
    return dict(
        w_ada=w_ada, b_ada=b_ada[None, :],
        g_pre_mix=g_pre_mix[None, :], g_post_mix=g_post_mix[None, :],
        g_pre_ffn=g_pre_ffn[None, :], g_post_ffn=g_post_ffn[None, :],
        wq=wq, wuab=wuab, w_conv=w_conv, alog=pad(a_log), dtb=pad(dt_bias), gdn=g_dn_out[None, :],
        wa=wa, abre=abre, abim=abim, wcre=_s5_out_weights(c_re), wcim=_s5_out_weights(c_im),
        dskip=d_skip.reshape(1, -1), wglu=w_glu.astype(BF16), gs5=g_s5_out[None, :],
        wtop=w_out[:mix_a].astype(BF16), wbot=w_out[mix_a:].astype(BF16),
        wg=w_gate.astype(BF16), wu=w_up.astype(BF16), wd=w_down.astype(BF16),
    )


def _layer_group(x3, mod, conv_state, s0, h0re, h0im, lw, *, prompt):
    nb, t, d = x3.shape
    rows = min(ROWS, nb * t)
    qkvz = _nmm(x3, mod, lw["g_pre_mix"], lw["wq"], shift=0, scale=1, rows=rows, tn=1024)
    uab = _nmm(x3, mod, lw["g_pre_mix"], lw["wuab"], shift=0, scale=1, rows=rows, tn=lw["wuab"].shape[1])
    if prompt:
        o, cnew, snew = _delta_prompt(qkvz, uab, conv_state, s0, lw["w_conv"], lw["alog"], lw["dtb"], lw["gdn"],
                                      nb=nb, t=t, c=PROMPT_CHUNK, base=PROMPT_BASE)
        y5, hre, him = _s5(uab, h0re, h0im, lw["wa"], lw["abre"], lw["abim"], lw["wcre"], lw["wcim"],
                           lw["dskip"], lw["wglu"], lw["gs5"], nb_total=nb, t=t, nb=nb, lc=S5_PROMPT_STEPS)
    else:
        o, cnew, snew = _delta_sample(qkvz, uab, conv_state, s0, lw["w_conv"], lw["alog"], lw["dtb"], lw["gdn"],
                                      nb=nb, t=t, bb=SAMPLE_BATCH)
        y5, hre, him = _s5(uab, h0re, h0im, lw["wa"], lw["abre"], lw["abim"], lw["wcre"], lw["wcim"],
                           lw["dskip"], lw["wglu"], lw["gs5"], nb_total=nb, t=t, nb=S5_SAMPLE_BATCH, lc=t)
    x1 = _outproj(o, y5, lw["wtop"], lw["wbot"], x3, mod, lw["g_post_mix"], gate=2, rows=rows)
    x2 = _ffn(x1, mod, lw["g_pre_ffn"], lw["wg"], lw["wu"], lw["wd"], lw["g_post_ffn"],
              shift=3, scale=4, gate=5, rows=rows, tf=512)
    ngroups = h0re.shape[1] // S5_STATE
    return x2, cnew, snew, hre.reshape(nb, ngroups, S5_STATE), him.reshape(nb, ngroups, S5_STATE)


def kernel(x_prompt, x_sample, c_prompt, c_sample, state_conv, state_delta, state_ssm_re, state_ssm_im, w_ada, b_ada, g_pre_mix, g_post_mix, g_pre_ffn, g_post_ffn, w_in, w_conv, a_log, dt_bias, g_dn_out, lam_re, lam_im, log_step, b_re, b_im, c_re, c_im, d_skip, w_glu, g_s5_out, w_out, w_gate, w_up, w_down):
    weights = (w_ada, b_ada, g_pre_mix, g_post_mix, g_pre_ffn, g_post_ffn,
               w_in, w_conv, a_log, dt_bias, g_dn_out,
               lam_re, lam_im, log_step, b_re, b_im, c_re, c_im, d_skip,
               w_glu, g_s5_out, w_out, w_gate, w_up, w_down)
    depth = w_ada.shape[0]
    bp, d = c_prompt.shape
    bs = c_sample.shape[0]
    dt_ = x_prompt.dtype
    nheads = state_delta.shape[2]
    nstate = state_ssm_re.shape[2] * state_ssm_re.shape[3]
    crows = -(-(bp + bs) // SUBLANE) * SUBLANE
    c_all = jnp.concatenate([c_prompt, c_sample, jnp.zeros((crows - bp - bs, d), dt_)], axis=0)
    yp, ys = x_prompt, x_sample
    outs = [[] for _ in range(8)]
    for l in range(depth):
        lw = _layer_weights(*[w[l] for w in weights])
        mod = _ada(c_all, lw["w_ada"], lw["b_ada"]).reshape(crows, 6, d)
        zc = jnp.zeros((bp, CONV_WIDTH - 1, state_conv.shape[3]), dt_)
        zd = jnp.zeros((bp,) + state_delta.shape[2:], dt_)
        zs = jnp.zeros((bp, nstate), dt_)
        yp, c1, d1, r1, i1 = _layer_group(yp, mod[:bp], zc, zd, zs, zs, lw, prompt=True)
        ys, c2, d2, r2, i2 = _layer_group(ys, mod[bp:bp + bs], state_conv[l], state_delta[l],
                                          state_ssm_re[l].reshape(bs, nstate), state_ssm_im[l].reshape(bs, nstate),
                                          lw, prompt=False)
        for lst, v in zip(outs, (c1, d1, r1, i1, c2, d2, r2, i2)):
            lst.append(v)
    stacked = [jnp.stack(v) for v in outs]
    return (yp, ys, *stacked)
```

```python
import functools
import math

import jax
import jax.numpy as jnp
from jax import lax
from jax.experimental import pallas as pl
from jax.experimental.pallas import tpu as pltpu

F32 = jnp.float32
BF16 = jnp.bfloat16
EPS = 1e-6
LANE = 128
SUBLANE = 8
V7X_VMEM_BYTES = 64 << 20
VMEM_LIMIT = V7X_VMEM_BYTES - (8 << 20)

DN_HEAD_DIM = 128
CONV_WIDTH = 4
S5_GROUP = 16
S5_STATE = 64
S5_SLAB_GROUPS = LANE // S5_GROUP
S5_SLAB_STATES = S5_SLAB_GROUPS * S5_STATE


def _params(*sem):
    return pltpu.CompilerParams(dimension_semantics=sem, vmem_limit_bytes=VMEM_LIMIT)


def _silu(x):
    return x * jax.nn.sigmoid(x)


def _mm(a, b):
    return jnp.dot(a.astype(BF16), b.astype(BF16), preferred_element_type=F32)


def _mm_nt(a, b):
    return lax.dot_general(a.astype(BF16), b.astype(BF16), (((1,), (1,)), ((), ())),
                           preferred_element_type=F32)


def _mm_tn(a, b):
    return lax.dot_general(a.astype(BF16), b.astype(BF16), (((0,), (0,)), ((), ())),
                           preferred_element_type=F32)


def _split3(x):
    x1 = x.astype(BF16)
    r1 = x - x1.astype(F32)
    x2 = r1.astype(BF16)
    x3 = (r1 - x2.astype(F32)).astype(BF16)
    return x1, x2, x3


def _ada_body(c_ref, w_ref, b_ref, o_ref):
    s = _silu(c_ref[...])
    o_ref[...] = _mm(s, w_ref[...]) + b_ref[...]


def _ada(c, w, b):
    rows, d = c.shape
    n = w.shape[1]
    tn = 1024
    return pl.pallas_call(
        _ada_body,
        grid=(n // tn,),
        in_specs=[pl.BlockSpec((rows, d), lambda j: (0, 0)),
                  pl.BlockSpec((d, tn), lambda j: (0, j)),
                  pl.BlockSpec((1, tn), lambda j: (0, j))],
        out_specs=pl.BlockSpec((rows, tn), lambda j: (0, j)),
        out_shape=jax.ShapeDtypeStruct((rows, n), F32),
        compiler_params=_params("arbitrary"),
        name="ada",
    )(c, w, b)


def _token_blocks(nb, t, rows):
    if t >= rows:
        assert t % rows == 0
        return 1, rows
    assert rows % t == 0 and nb % (rows // t) == 0
    return rows // t, t


def _modulated_norm(x, g, scale, shift):
    y = x * lax.rsqrt(jnp.mean(x * x, axis=-1, keepdims=True) + EPS) * g
    return y * (1.0 + scale) + shift


def _nmm_body(x_ref, mod_ref, g_ref, w_ref, o_ref, h_ref, *, shift, scale):
    @pl.when(pl.program_id(1) == 0)
    def _():
        x = x_ref[...]
        bb, tt, d = x.shape
        h = _modulated_norm(x, g_ref[...], mod_ref[:, scale:scale + 1, :], mod_ref[:, shift:shift + 1, :])
        h_ref[...] = h.reshape(bb * tt, d).astype(BF16)

    o_ref[...] = jnp.dot(h_ref[...], w_ref[...], preferred_element_type=F32)


def _nmm(x3, mod, g, w, *, shift, scale, rows, tn):
    nb, t, d = x3.shape
    n = w.shape[1]
    bb, tt = _token_blocks(nb, t, rows)
    nt = t // tt
    return pl.pallas_call(
        functools.partial(_nmm_body, shift=shift, scale=scale),
        grid=(nb * t // rows, n // tn),
        in_specs=[pl.BlockSpec((bb, tt, d), lambda i, j: (i // nt, i % nt, 0)),
                  pl.BlockSpec((bb, 6, d), lambda i, j: (i // nt, 0, 0)),
                  pl.BlockSpec((1, d), lambda i, j: (0, 0)),
                  pl.BlockSpec((d, tn), lambda i, j: (0, j))],
        out_specs=pl.BlockSpec((rows, tn), lambda i, j: (i, j)),
        out_shape=jax.ShapeDtypeStruct((nb * t, n), F32),
        scratch_shapes=[pltpu.VMEM((rows, d), BF16)],
        compiler_params=_params("parallel", "arbitrary"),
        name="nmm",
    )(x3, mod, g, w)


def _seg_masks(c, seg):
    row = lax.broadcasted_iota(jnp.int32, (c, c), 0)
    col = lax.broadcasted_iota(jnp.int32, (c, c), 1)
    sh = int(math.log2(seg))
    same = (row >> sh) == (col >> sh)
    return row, col, same, same & (row >= col), same & (row > col)


def _tri_inverse_offdiag(lmats, row, col, base, seg):
    bsh = int(math.log2(base))
    blk = (row >> bsh) == (col >> bsh)
    l0s = [jnp.where(blk, l, 0.0) for l in lmats]
    pts = [-l for l in l0s]
    qs = [_mm(l, l) for l in l0s]
    k = 2
    while k < base:
        pts = [p + q + _mm(p, q) for p, q in zip(pts, qs)]
        k *= 2
        if k < base:
            qs = [_mm(q, q) for q in qs]
    m = base
    while m < seg:
        msh = int(math.log2(m))
        sel = ((row >> (msh + 1)) == (col >> (msh + 1))) & ((row >> msh) != (col >> msh))
        es = [jnp.where(sel, l, 0.0) for l in lmats]
        xs = [e + _mm(p, e) for p, e in zip(pts, es)]
        pts = [p - (x + _mm(x, p)) for p, x in zip(pts, xs)]
        m *= 2
    return pts


def _delta_gates(ab, alog, dtb, causal_bf, same_bf, nheads):
    lane = lax.broadcasted_iota(jnp.int32, ab.shape, 1)
    x = ab + dtb
    softplus = jnp.maximum(x, 0.0) + jnp.log1p(jnp.exp(-jnp.abs(x)))
    g = jnp.where(lane < nheads, -jnp.exp(alog) * softplus, 0.0)
    beta = jax.nn.sigmoid(ab)
    pieces = _split3(g)
    gcum = sum(jnp.dot(causal_bf, p, preferred_element_type=F32) for p in pieces)
    glast = sum(jnp.dot(same_bf, p, preferred_element_type=F32) for p in pieces)
    return gcum, beta, glast


def _decay_logits(gcum, nheads):
    c = gcum.shape[0]
    lane = lax.broadcasted_iota(jnp.int32, (c, LANE), 1)
    ones = jnp.ones((c, LANE), BF16)
    pad = jnp.zeros((LANE - c, 2 * LANE), BF16) if c < LANE else None
    out = None
    for p in _split3(gcum):
        lhs = jnp.concatenate([p, ones], axis=1)
        blocks = []
        for h in range(nheads):
            onehot = lane == h
            blk = jnp.concatenate([jnp.where(onehot, 1.0, 0.0).astype(BF16),
                                   jnp.where(onehot, -p, jnp.zeros_like(p))], axis=1)
            blocks.append(blk)
            if pad is not None:
                blocks.append(pad)
        rhs = jnp.concatenate(blocks, axis=0)
        d = lax.dot_general(lhs, rhs, (((1,), (1,)), ((), ())), preferred_element_type=F32)
        out = d if out is None else out + d
    return out


def _delta_intra(act, nheads, gates, dlog, masks, base, seg):
    gcum, beta, glast = gates
    row, col, _, causal, strict = masks
    c = act.shape[0]
    dh = DN_HEAD_DIM
    eg_all = jnp.exp(gcum)
    ed_all = jnp.exp(glast - gcum)
    ks, kqs_lhs, rhss, qgs, kdecs = [], [], [], [], []
    for h in range(nheads):
        q = act[:, h * dh:(h + 1) * dh]
        k = act[:, (nheads + h) * dh:(nheads + h + 1) * dh]
        v = act[:, (2 * nheads + h) * dh:(2 * nheads + h + 1) * dh]
        q = q * lax.rsqrt(jnp.sum(q * q, axis=-1, keepdims=True) + 1e-6) * (dh ** -0.5)
        k = k * lax.rsqrt(jnp.sum(k * k, axis=-1, keepdims=True) + 1e-6)
        b = beta[:, nheads + h:nheads + h + 1]
        eg = eg_all[:, h:h + 1]
        kb = k * b
        ks.append(k)
        kqs_lhs.append(jnp.concatenate([kb, q], axis=0))
        rhss.append(jnp.concatenate([v * b, kb * eg], axis=1))
        qgs.append(q * eg)
        kdecs.append(k * ed_all[:, h:h + 1])
    kqs = [_mm_nt(a, k) for a, k in zip(kqs_lhs, ks)]
    lmats, attns = [], []
    for h in range(nheads):
        dl = dlog[:, h * LANE:h * LANE + c]
        dec = jnp.where(causal, jnp.exp(jnp.where(causal, dl, 0.0)), 0.0)
        lmats.append(jnp.where(strict, kqs[h][:c] * dec, 0.0))
        attns.append(kqs[h][c:] * dec)
    tts = _tri_inverse_offdiag(lmats, row, col, base, seg)
    uws = [r + _mm(t, r) for t, r in zip(tts, rhss)]
    return [uw[:, :dh] for uw in uws], [uw[:, dh:] for uw in uws], qgs, kdecs, attns


def _delta_out(o, z, gdn):
    on = o * lax.rsqrt(jnp.mean(o * o, axis=-1, keepdims=True) + EPS) * gdn
    return on * _silu(z)


def _delta_prompt_body(qkv_ref, z_ref, ab_ref, cst_ref, s0_ref, wc_ref, alog_ref, dtb_ref, gdn_ref,
                       o_ref, cnew_ref, snew_ref, xc_ref, s_ref, *, c, nheads, base):
    dh = DN_HEAD_DIM
    halo = SUBLANE
    taps = CONV_WIDTH - 1

    @pl.when(pl.program_id(1) == 0)
    def _():
        xc_ref[0:halo, :] = jnp.zeros((halo, xc_ref.shape[1]), F32)
        xc_ref[halo - taps:halo, :] = cst_ref[0]
        s_ref[...] = s0_ref[0]

    xc_ref[halo:halo + c, :] = qkv_ref[...]
    conv = xc_ref[halo - taps:halo - taps + c, :] * wc_ref[0:1, :]
    for j in range(1, CONV_WIDTH):
        conv = conv + xc_ref[halo - taps + j:halo - taps + j + c, :] * wc_ref[j:j + 1, :]
    cnew_ref[0] = xc_ref[halo + c - taps:halo + c, :]
    xc_ref[0:halo, :] = xc_ref[c:c + halo, :]
    act = _silu(conv)

    masks = _seg_masks(c, c)
    causal_bf = jnp.where(masks[3], 1.0, 0.0).astype(BF16)
    same_bf = jnp.ones((c, c), BF16)
    gates = _delta_gates(ab_ref[...], alog_ref[...], dtb_ref[...], causal_bf, same_bf, nheads)
    dlog = _decay_logits(gates[0], nheads)
    eglast = jnp.exp(gates[2][0:1, :])
    gdn = gdn_ref[...]
    us, ws_, qgs, kdecs, attns = _delta_intra(act, nheads, gates, dlog, masks, base, c)
    heads = range(nheads)
    ss = [s_ref[h] for h in heads]
    wss = [_mm(jnp.concatenate([ws_[h], qgs[h]], axis=0), ss[h]) for h in heads]
    vnews = [us[h] - wss[h][:c] for h in heads]
    os_ = [wss[h][c:] + _mm(attns[h], vnews[h]) for h in heads]
    snews = [ss[h] * eglast[:, h:h + 1] + _mm_tn(kdecs[h], vnews[h]) for h in heads]
    for h in heads:
        s_ref[h] = snews[h]
        o_ref[:, h * dh:(h + 1) * dh] = _delta_out(os_[h], z_ref[:, h * dh:(h + 1) * dh], gdn).astype(o_ref.dtype)

    @pl.when(pl.program_id(1) == pl.num_programs(1) - 1)
    def _():
        snew_ref[0] = s_ref[...]


def _delta_prompt(qkvz, uab, conv_state, s0, w_conv, alog, dtb, gdn, *, nb, t, c, base):
    nheads = s0.shape[1]
    dh = DN_HEAD_DIM
    mix = nheads * dh
    nc = t // c
    return pl.pallas_call(
        functools.partial(_delta_prompt_body, c=c, nheads=nheads, base=base),
        grid=(nb, nc),
        in_specs=[pl.BlockSpec((c, 3 * mix), lambda b, i: (b * nc + i, 0)),
                  pl.BlockSpec((c, mix), lambda b, i: (b * nc + i, 3)),
                  pl.BlockSpec((c, LANE), lambda b, i: (b * nc + i, uab.shape[1] // LANE - 1)),
                  pl.BlockSpec((1, CONV_WIDTH - 1, 3 * mix), lambda b, i: (b, 0, 0)),
                  pl.BlockSpec((1, nheads, dh, dh), lambda b, i: (b, 0, 0, 0)),
                  pl.BlockSpec((CONV_WIDTH, 3 * mix), lambda b, i: (0, 0)),
                  pl.BlockSpec((1, LANE), lambda b, i: (0, 0)),
                  pl.BlockSpec((1, LANE), lambda b, i: (0, 0)),
                  pl.BlockSpec((1, dh), lambda b, i: (0, 0))],
        out_specs=[pl.BlockSpec((c, mix), lambda b, i: (b * nc + i, 0)),
                   pl.BlockSpec((1, CONV_WIDTH - 1, 3 * mix), lambda b, i: (b, 0, 0)),
                   pl.BlockSpec((1, nheads, dh, dh), lambda b, i: (b, 0, 0, 0))],
        out_shape=[jax.ShapeDtypeStruct((nb * t, mix), BF16),
                   jax.ShapeDtypeStruct((nb, CONV_WIDTH - 1, 3 * mix), F32),
                   jax.ShapeDtypeStruct((nb, nheads, dh, dh), F32)],
        scratch_shapes=[pltpu.VMEM((c + SUBLANE, 3 * mix), F32),
                        pltpu.VMEM((nheads, dh, dh), F32)],
        compiler_params=_params("parallel", "arbitrary"),
        name="delta_prompt",
    )(qkvz, qkvz, uab, conv_state, s0, w_conv, alog, dtb, gdn)


def _delta_sample_body(qkv_ref, z_ref, ab_ref, cst_ref, s0_ref, wc_ref, alog_ref, dtb_ref, gdn_ref,
                       o_ref, cnew_ref, snew_ref,
                       xs_ref, w_s, qg_s, u_s, kd_s, vn_s, qs_s, attn_s, egl_s, *, bb, t, nheads):
    dh = DN_HEAD_DIM
    c = bb * t
    halo = SUBLANE
    taps = CONV_WIDTH - 1

    xs_ref[:, 0:halo, :] = jnp.zeros((bb, halo, xs_ref.shape[2]), F32)
    xs_ref[:, halo - taps:halo, :] = cst_ref[...]
    xs_ref[:, halo:halo + t, :] = qkv_ref[...]
    conv = xs_ref[:, halo - taps:halo - taps + t, :] * wc_ref[0:1, :]
    for j in range(1, CONV_WIDTH):
        conv = conv + xs_ref[:, halo - taps + j:halo - taps + j + t, :] * wc_ref[j:j + 1, :]
    cnew_ref[...] = xs_ref[:, halo + t - taps:halo + t, :]
    act = _silu(conv).reshape(c, conv.shape[2])

    masks = _seg_masks(c, t)
    causal_bf = jnp.where(masks[3], 1.0, 0.0).astype(BF16)
    same_bf = jnp.where(masks[2], 1.0, 0.0).astype(BF16)
    gates = _delta_gates(ab_ref[...], alog_ref[...], dtb_ref[...], causal_bf, same_bf, nheads)
    dlog = _decay_logits(gates[0], nheads)
    egl_s[...] = jnp.exp(gates[2])
    us, ws_, qgs, kdecs, attns = _delta_intra(act, nheads, gates, dlog, masks, t, t)
    for h in range(nheads):
        hs = slice(h * dh, (h + 1) * dh)
        u_s[:, hs] = us[h]
        w_s[:, hs] = ws_[h]
        qg_s[:, hs] = qgs[h]
        kd_s[:, hs] = kdecs[h]
        attn_s[h] = attns[h]

    def per_batch(b, carry):
        r = pl.multiple_of(b * t, t)
        rows = pl.ds(r, t)
        heads = range(nheads)
        hsl = [slice(h * dh, (h + 1) * dh) for h in heads]
        ss = [s0_ref[b, h] for h in heads]
        wss = [_mm(jnp.concatenate([w_s[rows, hsl[h]], qg_s[rows, hsl[h]]], axis=0), ss[h]) for h in heads]
        vnews = [u_s[rows, hsl[h]] - wss[h][:t] for h in heads]
        upds = [_mm_tn(kd_s[rows, hsl[h]], vnews[h]) for h in heads]
        egl = egl_s[pl.ds(r, 1), :]
        for h in heads:
            vn_s[rows, hsl[h]] = vnews[h]
            qs_s[rows, hsl[h]] = wss[h][t:]
            snew_ref[b, h] = ss[h] * egl[:, h:h + 1] + upds[h]
        return carry

    lax.fori_loop(0, bb, per_batch, 0)

    gdn = gdn_ref[...]
    for h in range(nheads):
        hs = slice(h * dh, (h + 1) * dh)
        o = qs_s[:, hs] + _mm(attn_s[h], vn_s[:, hs])
        o_ref[:, hs] = _delta_out(o, z_ref[:, hs], gdn).astype(o_ref.dtype)


def _delta_sample(qkvz, uab, conv_state, s0, w_conv, alog, dtb, gdn, *, nb, t, bb):
    nheads = s0.shape[1]
    dh = DN_HEAD_DIM
    mix = nheads * dh
    c = bb * t
    qkvz3 = qkvz.reshape(nb, t, qkvz.shape[1])
    return pl.pallas_call(
        functools.partial(_delta_sample_body, bb=bb, t=t, nheads=nheads),
        grid=(nb // bb,),
        in_specs=[pl.BlockSpec((bb, t, 3 * mix), lambda i: (i, 0, 0)),
                  pl.BlockSpec((c, mix), lambda i: (i, 3)),
                  pl.BlockSpec((c, LANE), lambda i: (i, uab.shape[1] // LANE - 1)),
                  pl.BlockSpec((bb, CONV_WIDTH - 1, 3 * mix), lambda i: (i, 0, 0)),
                  pl.BlockSpec((bb, nheads, dh, dh), lambda i: (i, 0, 0, 0)),
                  pl.BlockSpec((CONV_WIDTH, 3 * mix), lambda i: (0, 0)),
                  pl.BlockSpec((1, LANE), lambda i: (0, 0)),
                  pl.BlockSpec((1, LANE), lambda i: (0, 0)),
                  pl.BlockSpec((1, dh), lambda i: (0, 0))],
        out_specs=[pl.BlockSpec((c, mix), lambda i: (i, 0)),
                   pl.BlockSpec((bb, CONV_WIDTH - 1, 3 * mix), lambda i: (i, 0, 0)),
                   pl.BlockSpec((bb, nheads, dh, dh), lambda i: (i, 0, 0, 0))],
        out_shape=[jax.ShapeDtypeStruct((nb * t, mix), BF16),
                   jax.ShapeDtypeStruct((nb, CONV_WIDTH - 1, 3 * mix), F32),
                   jax.ShapeDtypeStruct((nb, nheads, dh, dh), F32)],
        scratch_shapes=[pltpu.VMEM((bb, SUBLANE + t, 3 * mix), F32)]
                       + [pltpu.VMEM((c, mix), F32) for _ in range(6)]
                       + [pltpu.VMEM((nheads, c, c), F32), pltpu.VMEM((c, LANE), F32)],
        compiler_params=_params("parallel"),
        name="delta_sample",
    )(qkvz3, qkvz, uab, conv_state, s0, w_conv, alog, dtb, gdn)


def _s5_prep_body(lre_ref, lim_ref, ls_ref, bre_ref, bim_ref, wa_ref, abre_ref, abim_ref):
    lre = jnp.minimum(lre_ref[...], -1e-4)
    lim = lim_ref[...]
    dt = jnp.exp(ls_ref[...])
    mag = jnp.exp(lre * dt)
    ang = lim * dt
    abr = mag * jnp.cos(ang)
    abi = mag * jnp.sin(ang)
    nr = abr - 1.0
    ni = abi
    den = lre * lre + lim * lim
    cor = (nr * lre + ni * lim) / den
    coi = (ni * lre - nr * lim) / den
    bre = bre_ref[...]
    bim = bim_ref[...]
    shape2 = bre.shape[1:]
    row = lax.broadcasted_iota(jnp.int32, shape2, 0)
    col = lax.broadcasted_iota(jnp.int32, shape2, 1)
    diag = ((row // S5_GROUP) == (col // S5_STATE))[None]
    half = shape2[1]
    wa_ref[:, :, 0:half] = jnp.where(diag, cor * bre - coi * bim, 0.0).astype(BF16)
    wa_ref[:, :, half:2 * half] = jnp.where(diag, cor * bim + coi * bre, 0.0).astype(BF16)
    abre_ref[...] = abr
    abim_ref[...] = abi


def _s5_prep(lam_re, lam_im, log_step, b_re, b_im):
    groups, states, gch = b_re.shape
    slabs = groups // S5_SLAB_GROUPS
    cols = S5_SLAB_STATES

    def row_param(p):
        return p.reshape(slabs, 1, cols)

    def tiled_b(b):
        bt = jnp.transpose(b, (0, 2, 1)).reshape(slabs, LANE, states)
        return jnp.tile(bt, (1, 1, S5_SLAB_GROUPS))

    ls = jnp.broadcast_to(log_step[:, None], (groups, states))
    wa, abre, abim = pl.pallas_call(
        _s5_prep_body,
        out_shape=[jax.ShapeDtypeStruct((slabs, LANE, 2 * cols), BF16),
                   jax.ShapeDtypeStruct((slabs, 1, cols), F32),
                   jax.ShapeDtypeStruct((slabs, 1, cols), F32)],
        compiler_params=pltpu.CompilerParams(vmem_limit_bytes=VMEM_LIMIT),
        name="s5prep",
    )(row_param(lam_re), row_param(lam_im), row_param(ls), tiled_b(b_re), tiled_b(b_im))
    nblk = groups * states // LANE
    return wa, abre.reshape(nblk, 1, LANE), abim.reshape(nblk, 1, LANE)


def _s5_out_weights(c):
    groups, gch, states = c.shape
    slabs = groups // S5_SLAB_GROUPS
    ct = jnp.transpose(c, (0, 2, 1)).reshape(slabs, S5_SLAB_STATES, gch)
    ct = jnp.tile(ct, (1, 1, S5_SLAB_GROUPS))
    row = jnp.arange(S5_SLAB_STATES)[:, None] // states
    col = jnp.arange(LANE)[None, :] // gch
    return jnp.where((row == col)[None], ct, 0.0).astype(BF16)


def _gelu_tanh(x):
    return x * (0.5 * (1.0 + jnp.tanh(math.sqrt(2.0 / math.pi) * (x + 0.044715 * (x * x * x)))))


S5_SCAN_BLOCKS = 4


def _time_major_perms(rows, nb, lc):
    r = lax.broadcasted_iota(jnp.int32, (rows, rows), 0)
    c = lax.broadcasted_iota(jnp.int32, (rows, rows), 1)
    nsh, lsh = int(math.log2(nb)), int(math.log2(lc))
    to_tm = c == (((r & (nb - 1)) << lsh) | (r >> nsh))
    to_bm = c == (((r & (lc - 1)) << nsh) | (r >> lsh))
    as_bf = lambda m: jnp.where(m, 1.0, 0.0).astype(BF16)
    return as_bf(to_tm), as_bf(to_bm)


def _s5_body(u_ref, h0re_ref, h0im_ref, wa_ref, abre_ref, abim_ref, wcre_ref, wcim_ref, dsk_ref,
             wglu_ref, gs5_ref, y_ref, hre_ref, him_ref, xre, xim, cre, cim, *, nb, lc):
    rows = nb * lc
    nblk = xre.shape[0]
    slabs = wa_ref.shape[0]
    per_slab = nblk // slabs
    half = per_slab * LANE
    steps_per_tile = SUBLANE // nb
    assert steps_per_tile in (1, 2) and lc % steps_per_tile == 0

    @pl.when(pl.program_id(1) == 0)
    def _():
        for j in range(nblk):
            if nb < SUBLANE:
                cre[j] = jnp.zeros((SUBLANE, LANE), F32)
                cim[j] = jnp.zeros((SUBLANE, LANE), F32)
            cre[j, SUBLANE - nb:SUBLANE, :] = h0re_ref[:, j * LANE:(j + 1) * LANE]
            cim[j, SUBLANE - nb:SUBLANE, :] = h0im_ref[:, j * LANE:(j + 1) * LANE]

    to_tm, to_bm = _time_major_perms(rows, nb, lc)
    u = sum(jnp.dot(to_tm, p, preferred_element_type=F32)
            for p in _split3(u_ref[...].reshape(rows, u_ref.shape[2])))
    ub = u.astype(BF16)
    for s in range(slabs):
        x = jnp.dot(ub[:, s * LANE:(s + 1) * LANE], wa_ref[s], preferred_element_type=F32)
        for k in range(per_slab):
            xre[s * per_slab + k] = x[:, k * LANE:(k + 1) * LANE]
            xim[s * per_slab + k] = x[:, half + k * LANE:half + (k + 1) * LANE]

    upper = lax.broadcasted_iota(jnp.int32, (SUBLANE, LANE), 0) >= nb
    for j0 in range(0, nblk, S5_SCAN_BLOCKS):
        js = list(range(j0, j0 + S5_SCAN_BLOCKS))
        coef = []
        for j in js:
            ar = jnp.broadcast_to(abre_ref[j], (SUBLANE, LANE))
            ai = jnp.broadcast_to(abim_ref[j], (SUBLANE, LANE))
            if steps_per_tile == 2:
                coef.append((jnp.where(upper, ar, 0.0), jnp.where(upper, ai, 0.0),
                             jnp.where(upper, ar * ar - ai * ai, ar), jnp.where(upper, 2.0 * ar * ai, ai)))
            else:
                coef.append((ar, ai))

        def step(v, carry, js=js, coef=coef):
            at = pl.ds(pl.multiple_of(v * SUBLANE, SUBLANE), SUBLANE)
            out = []
            for n, j in enumerate(js):
                hr, hi = carry[2 * n], carry[2 * n + 1]
                xr, xi = xre[j, at, :], xim[j, at, :]
                if steps_per_tile == 2:
                    m1r, m1i, pr, pi = coef[n]
                    sr, si = pltpu.roll(xr, nb, 0), pltpu.roll(xi, nb, 0)
                    yr = xr + m1r * sr - m1i * si
                    yi = xi + m1r * si + m1i * sr
                    cr = jnp.where(upper, hr, pltpu.roll(hr, nb, 0))
                    ci = jnp.where(upper, hi, pltpu.roll(hi, nb, 0))
                    nr = yr + pr * cr - pi * ci
                    ni = yi + pr * ci + pi * cr
                else:
                    ar, ai = coef[n]
                    nr = ar * hr - ai * hi + xr
                    ni = ar * hi + ai * hr + xi
                xre[j, at, :] = nr
                xim[j, at, :] = ni
                out += [nr, ni]
            return tuple(out)

        init = []
        for j in js:
            init += [cre[j], cim[j]]
        fin = lax.fori_loop(0, rows // SUBLANE, step, tuple(init))
        for n, j in enumerate(js):
            cre[j] = fin[2 * n]
            cim[j] = fin[2 * n + 1]

    ys = []
    for s in range(slabs):
        hre = jnp.concatenate([xre[s * per_slab + k] for k in range(per_slab)], axis=1)
        him = jnp.concatenate([xim[s * per_slab + k] for k in range(per_slab)], axis=1)
        ys.append(_mm(hre, wcre_ref[s]) - _mm(him, wcim_ref[s]))
    y = jnp.concatenate(ys, axis=1) + dsk_ref[...] * u
    ab = _mm(_gelu_tanh(y), wglu_ref[...])
    n = y.shape[1]
    glu = ab[:, :n] * jax.nn.sigmoid(ab[:, n:])
    out = glu * lax.rsqrt(jnp.mean(glu * glu, axis=-1, keepdims=True) + EPS) * gs5_ref[...]
    out = jnp.dot(to_bm, out.astype(BF16), preferred_element_type=F32)
    y_ref[...] = out.reshape(y_ref.shape).astype(y_ref.dtype)

    @pl.when(pl.program_id(1) == pl.num_programs(1) - 1)
    def _():
        for j in range(nblk):
            hre_ref[:, j * LANE:(j + 1) * LANE] = cre[j, SUBLANE - nb:SUBLANE, :]
            him_ref[:, j * LANE:(j + 1) * LANE] = cim[j, SUBLANE - nb:SUBLANE, :]


def _s5(uab, h0re, h0im, wa, abre, abim, wcre, wcim, dskip, wglu, gs5, *, nb_total, t, nb, lc):
    mixb = dskip.shape[1]
    nstate = h0re.shape[1]
    nblk = nstate // LANE
    uab3 = uab.reshape(nb_total, t, uab.shape[1])
    full = lambda a: pl.BlockSpec(a.shape, lambda i, j: (0,) * a.ndim)
    y, hre, him = pl.pallas_call(
        functools.partial(_s5_body, nb=nb, lc=lc),
        grid=(nb_total // nb, t // lc),
        in_specs=[pl.BlockSpec((nb, lc, mixb), lambda i, j: (i, j, 0)),
                  pl.BlockSpec((nb, nstate), lambda i, j: (i, 0)),
                  pl.BlockSpec((nb, nstate), lambda i, j: (i, 0)),
                  full(wa), full(abre), full(abim), full(wcre), full(wcim), full(dskip), full(wglu), full(gs5)],
        out_specs=[pl.BlockSpec((nb, lc, mixb), lambda i, j: (i, j, 0)),
                   pl.BlockSpec((nb, nstate), lambda i, j: (i, 0)),
                   pl.BlockSpec((nb, nstate), lambda i, j: (i, 0))],
        out_shape=[jax.ShapeDtypeStruct((nb_total, t, mixb), F32),
                   jax.ShapeDtypeStruct((nb_total, nstate), F32),
                   jax.ShapeDtypeStruct((nb_total, nstate), F32)],
        scratch_shapes=[pltpu.VMEM((nblk, nb * lc, LANE), F32),
                        pltpu.VMEM((nblk, nb * lc, LANE), F32),
                        pltpu.VMEM((nblk, SUBLANE, LANE), F32),
                        pltpu.VMEM((nblk, SUBLANE, LANE), F32)],
        compiler_params=_params("parallel", "arbitrary"),
        name="s5",
    )(uab3, h0re, h0im, wa, abre, abim, wcre, wcim, dskip, wglu, gs5)
    return y.reshape(nb_total * t, mixb), hre, him


def _outproj_body(o_ref, y_ref, wt_ref, wb_ref, x_ref, mod_ref, g_ref, out_ref, *, gate):
    mix = (jnp.dot(o_ref[...], wt_ref[...], preferred_element_type=F32)
           + _mm(y_ref[...], wb_ref[...]))
    n = mix * lax.rsqrt(jnp.mean(mix * mix, axis=-1, keepdims=True) + EPS) * g_ref[...]
    x = x_ref[...]
    out_ref[...] = x + mod_ref[:, gate:gate + 1, :] * n.reshape(x.shape)


def _outproj(o, y5, wtop, wbot, x3, mod, g, *, gate, rows):
    nb, t, d = x3.shape
    bb, tt = _token_blocks(nb, t, rows)
    nt = t // tt
    ka, kb = o.shape[1], y5.shape[1]
    return pl.pallas_call(
        functools.partial(_outproj_body, gate=gate),
        grid=(nb * t // rows,),
        in_specs=[pl.BlockSpec((rows, ka), lambda i: (i, 0)),
                  pl.BlockSpec((rows, kb), lambda i: (i, 0)),
                  pl.BlockSpec((ka, d), lambda i: (0, 0)),
                  pl.BlockSpec((kb, d), lambda i: (0, 0)),
                  pl.BlockSpec((bb, tt, d), lambda i: (i // nt, i % nt, 0)),
                  pl.BlockSpec((bb, 6, d), lambda i: (i // nt, 0, 0)),
                  pl.BlockSpec((1, d), lambda i: (0, 0))],
        out_specs=pl.BlockSpec((bb, tt, d), lambda i: (i // nt, i % nt, 0)),
        out_shape=jax.ShapeDtypeStruct((nb, t, d), F32),
        compiler_params=_params("parallel"),
        name="outproj",
    )(o, y5, wtop, wbot, x3, mod, g)


def _ffn_body(x_ref, mod_ref, gpre_ref, wg_ref, wu_ref, wd_ref, gpost_ref, out_ref, h_ref, acc_ref,
              *, shift, scale, gate):
    j = pl.program_id(1)

    @pl.when(j == 0)
    def _():
        x = x_ref[...]
        bb, tt, d = x.shape
        h = _modulated_norm(x, gpre_ref[...], mod_ref[:, scale:scale + 1, :], mod_ref[:, shift:shift + 1, :])
        h_ref[...] = h.reshape(bb * tt, d).astype(BF16)
        acc_ref[...] = jnp.zeros(acc_ref.shape, F32)

    h = h_ref[...]
    a = (_silu(jnp.dot(h, wg_ref[...], preferred_element_type=F32))
         * jnp.dot(h, wu_ref[...], preferred_element_type=F32))
    acc_ref[...] += _mm(a, wd_ref[...])

    @pl.when(j == pl.num_programs(1) - 1)
    def _():
        f = acc_ref[...]
        n = f * lax.rsqrt(jnp.mean(f * f, axis=-1, keepdims=True) + EPS) * gpost_ref[...]
        x = x_ref[...]
        out_ref[...] = x + mod_ref[:, gate:gate + 1, :] * n.reshape(x.shape)


def _ffn(x3, mod, gpre, wg, wu, wd, gpost, *, shift, scale, gate, rows, tf):
    nb, t, d = x3.shape
    dff = wg.shape[1]
    bb, tt = _token_blocks(nb, t, rows)
    nt = t // tt
    return pl.pallas_call(
        functools.partial(_ffn_body, shift=shift, scale=scale, gate=gate),
        grid=(nb * t // rows, dff // tf),
        in_specs=[pl.BlockSpec((bb, tt, d), lambda i, j: (i // nt, i % nt, 0)),
                  pl.BlockSpec((bb, 6, d), lambda i, j: (i // nt, 0, 0)),
                  pl.BlockSpec((1, d), lambda i, j: (0, 0)),
                  pl.BlockSpec((d, tf), lambda i, j: (0, j)),
                  pl.BlockSpec((d, tf), lambda i, j: (0, j)),
                  pl.BlockSpec((tf, d), lambda i, j: (j, 0)),
                  pl.BlockSpec((1, d), lambda i, j: (0, 0))],
        out_specs=pl.BlockSpec((bb, tt, d), lambda i, j: (i // nt, i % nt, 0)),
        out_shape=jax.ShapeDtypeStruct((nb, t, d), F32),
        scratch_shapes=[pltpu.VMEM((rows, d), BF16), pltpu.VMEM((rows, d), F32)],
        compiler_params=_params("parallel", "arbitrary"),
        name="ffn",
    )(x3, mod, gpre, wg, wu, wd, gpost)


ROWS = 512
PROMPT_CHUNK = 128
PROMPT_BASE = 16
SAMPLE_BATCH = 8
S5_PROMPT_STEPS = 128
S5_SAMPLE_BATCH = 8


def _layer_weights(w_ada, b_ada, g_pre_mix, g_post_mix, g_pre_ffn, g_post_ffn,
                   w_in, w_conv, a_log, dt_bias, g_dn_out,
                   lam_re, lam_im, log_step, b_re, b_im, c_re, c_im, d_skip,
                   w_glu, g_s5_out, w_out, w_gate, w_up, w_down):
    d = w_in.shape[0]
    nheads = a_log.shape[0]
    mix_a = nheads * DN_HEAD_DIM
    qkvz = 4 * mix_a
    gates = 2 * nheads
    wq = w_in[:, :qkvz].astype(BF16)
    wuab = jnp.concatenate([w_in[:, qkvz + gates:], w_in[:, qkvz:qkvz + gates],
                            jnp.zeros((d, LANE - gates), w_in.dtype)], axis=1).astype(BF16)
    pad = lambda v: jnp.pad(v, (0, LANE - v.shape[0]))[None, :]
    wa, abre, abim = _s5_prep(lam_re, lam_im, log_step, b_re, b_im)
    return dict(
        w_ada=w_ada, b_ada=b_ada[None, :],
        g_pre_mix=g_pre_mix[None, :], g_post_mix=g_post_mix[None, :],
        g_pre_ffn=g_pre_ffn[None, :], g_post_ffn=g_post_ffn[None, :],
        wq=wq, wuab=wuab, w_conv=w_conv, alog=pad(a_log), dtb=pad(dt_bias), gdn=g_dn_out[None, :],
        wa=wa, abre=abre, abim=abim, wcre=_s5_out_weights(c_re), wcim=_s5_out_weights(c_im),
        dskip=d_skip.reshape(1, -1), wglu=w_glu.astype(BF16), gs5=g_s5_out[None, :],
        wtop=w_out[:mix_a].astype(BF16), wbot=w_out[mix_a:].astype(BF16),
        wg=w_gate.astype(BF16), wu=w_up.astype(BF16), wd=w_down.astype(BF16),
    )


def _layer_group(x3, mod, conv_state, s0, h0re, h0im, lw, *, prompt):
    nb, t, d = x3.shape
    rows = min(ROWS, nb * t)
    qkvz = _nmm(x3, mod, lw["g_pre_mix"], lw["wq"], shift=0, scale=1, rows=rows, tn=1024)
    uab = _nmm(x3, mod, lw["g_pre_mix"], lw["wuab"], shift=0, scale=1, rows=rows, tn=lw["wuab"].shape[1])
    if prompt:
        o, cnew, snew = _delta_prompt(qkvz, uab, conv_state, s0, lw["w_conv"], lw["alog"], lw["dtb"], lw["gdn"],
                                      nb=nb, t=t, c=PROMPT_CHUNK, base=PROMPT_BASE)
        y5, hre, him = _s5(uab, h0re, h0im, lw["wa"], lw["abre"], lw["abim"], lw["wcre"], lw["wcim"],
                           lw["dskip"], lw["wglu"], lw["gs5"], nb_total=nb, t=t, nb=nb, lc=S5_PROMPT_STEPS)
    else:
        o, cnew, snew = _delta_sample(qkvz, uab, conv_state, s0, lw["w_conv"], lw["alog"], lw["dtb"], lw["gdn"],
                                      nb=nb, t=t, bb=SAMPLE_BATCH)
        y5, hre, him = _s5(uab, h0re, h0im, lw["wa"], lw["abre"], lw["abim"], lw["wcre"], lw["wcim"],
                           lw["dskip"], lw["wglu"], lw["gs5"], nb_total=nb, t=t, nb=S5_SAMPLE_BATCH, lc=t)
    x1 = _outproj(o, y5, lw["wtop"], lw["wbot"], x3, mod, lw["g_post_mix"], gate=2, rows=rows)
    x2 = _ffn(x1, mod, lw["g_pre_ffn"], lw["wg"], lw["wu"], lw["wd"], lw["g_post_ffn"],
              shift=3, scale=4, gate=5, rows=rows, tf=512)
    ngroups = h0re.shape[1] // S5_STATE
    return x2, cnew, snew, hre.reshape(nb, ngroups, S5_STATE), him.reshape(nb, ngroups, S5_STATE)


def kernel(x_prompt, x_sample, c_prompt, c_sample, state_conv, state_delta, state_ssm_re, state_ssm_im, w_ada, b_ada, g_pre_mix, g_post_mix, g_pre_ffn, g_post_ffn, w_in, w_conv, a_log, dt_bias, g_dn_out, lam_re, lam_im, log_step, b_re, b_im, c_re, c_im, d_skip, w_glu, g_s5_out, w_out, w_gate, w_up, w_down):
    weights = (w_ada, b_ada, g_pre_mix, g_post_mix, g_pre_ffn, g_post_ffn,
               w_in, w_conv, a_log, dt_bias, g_dn_out,
               lam_re, lam_im, log_step, b_re, b_im, c_re, c_im, d_skip,
               w_glu, g_s5_out, w_out, w_gate, w_up, w_down)
    depth = w_ada.shape[0]
    bp, d = c_prompt.shape
    bs = c_sample.shape[0]
    dt_ = x_prompt.dtype
    nheads = state_delta.shape[2]
    nstate = state_ssm_re.shape[2] * state_ssm_re.shape[3]
    crows = -(-(bp + bs) // SUBLANE) * SUBLANE
    c_all = jnp.concatenate([c_prompt, c_sample, jnp.zeros((crows - bp - bs, d), dt_)], axis=0)
    yp, ys = x_prompt, x_sample
    outs = [[] for _ in range(8)]
    for l in range(depth):
        lw = _layer_weights(*[w[l] for w in weights])
        mod = _ada(c_all, lw["w_ada"], lw["b_ada"]).reshape(crows, 6, d)
        zc = jnp.zeros((bp, CONV_WIDTH - 1, state_conv.shape[3]), dt_)
        zd = jnp.zeros((bp,) + state_delta.shape[2:], dt_)
        zs = jnp.zeros((bp, nstate), dt_)
        yp, c1, d1, r1, i1 = _layer_group(yp, mod[:bp], zc, zd, zs, zs, lw, prompt=True)
        ys, c2, d2, r2, i2 = _layer_group(ys, mod[bp:bp + bs], state_conv[l], state_delta[l],
                                          state_ssm_re[l].reshape(bs, nstate), state_ssm_im[l].reshape(bs, nstate),
                                          lw, prompt=False)
        for lst, v in zip(outs, (c1, d1, r1, i1, c2, d2, r2, i2)):
            lst.append(v)
    stacked = [jnp.stack(v) for v in outs]
    return (yp, ys, *stacked)
```

```python
import functools
import math

import jax
import jax.numpy as jnp
from jax import lax
from jax.experimental import pallas as pl
from jax.experimental.pallas import tpu as pltpu

F32 = jnp.float32
BF16 = jnp.bfloat16
EPS = 1e-6
LANE = 128
SUBLANE = 8
V7X_VMEM_BYTES = 64 << 20
VMEM_LIMIT = V7X_VMEM_BYTES - (8 << 20)

DN_HEAD_DIM = 128
CONV_WIDTH = 4
S5_GROUP = 16
S5_STATE = 64
S5_SLAB_GROUPS = LANE // S5_GROUP
S5_SLAB_STATES = S5_SLAB_GROUPS * S5_STATE


def _params(*sem):
    return pltpu.CompilerParams(dimension_semantics=sem, vmem_limit_bytes=VMEM_LIMIT)


def _silu(x):
    return x * jax.nn.sigmoid(x)


def _mm(a, b):
    return jnp.dot(a.astype(BF16), b.astype(BF16), preferred_element_type=F32)


def _mm_nt(a, b):
    return lax.dot_general(a.astype(BF16), b.astype(BF16), (((1,), (1,)), ((), ())),
                           preferred_element_type=F32)


def _mm_tn(a, b):
    return lax.dot_general(a.astype(BF16), b.astype(BF16), (((0,), (0,)), ((), ())),
                           preferred_element_type=F32)


def _split3(x):
    x1 = x.astype(BF16)
    r1 = x - x1.astype(F32)
    x2 = r1.astype(BF16)
    x3 = (r1 - x2.astype(F32)).astype(BF16)
    return x1, x2, x3


def _ada_body(c_ref, w_ref, b_ref, o_ref):
    s = _silu(c_ref[...])
    o_ref[...] = _mm(s, w_ref[...]) + b_ref[...]


def _ada(c, w, b):
    rows, d = c.shape
    n = w.shape[1]
    tn = 1024
    return pl.pallas_call(
        _ada_body,
        grid=(n // tn,),
        in_specs=[pl.BlockSpec((rows, d), lambda j: (0, 0)),
                  pl.BlockSpec((d, tn), lambda j: (0, j)),
                  pl.BlockSpec((1, tn), lambda j: (0, j))],
        out_specs=pl.BlockSpec((rows, tn), lambda j: (0, j)),
        out_shape=jax.ShapeDtypeStruct((rows, n), F32),
        compiler_params=_params("arbitrary"),
        name="ada",
    )(c, w, b)


def _token_blocks(nb, t, rows):
    if t >= rows:
        assert t % rows == 0
        return 1, rows
    assert rows % t == 0 and nb % (rows // t) == 0
    return rows // t, t


def _modulated_norm(x, g, scale, shift):
    y = x * lax.rsqrt(jnp.mean(x * x, axis=-1, keepdims=True) + EPS) * g
    return y * (1.0 + scale) + shift


def _nmm_body(x_ref, mod_ref, g_ref, w_ref, o_ref, *, shift, scale):
    x = x_ref[...]
    bb, tt, d = x.shape
    h = _modulated_norm(x, g_ref[...], mod_ref[:, scale:scale + 1, :], mod_ref[:, shift:shift + 1, :])
    o_ref[...] = jnp.dot(h.reshape(bb * tt, d).astype(BF16), w_ref[...], preferred_element_type=F32)


def _nmm(x3, mod, g, w, *, shift, scale, rows):
    nb, t, d = x3.shape
    n = w.shape[1]
    bb, tt = _token_blocks(nb, t, rows)
    nt = t // tt
    return pl.pallas_call(
        functools.partial(_nmm_body, shift=shift, scale=scale),
        grid=(nb * t // rows,),
        in_specs=[pl.BlockSpec((bb, tt, d), lambda i: (i // nt, i % nt, 0)),
                  pl.BlockSpec((bb, 6, d), lambda i: (i // nt, 0, 0)),
                  pl.BlockSpec((1, d), lambda i: (0, 0)),
                  pl.BlockSpec((d, n), lambda i: (0, 0), pipeline_mode=pl.Buffered(1))],
        out_specs=pl.BlockSpec((rows, n), lambda i: (i, 0)),
        out_shape=jax.ShapeDtypeStruct((nb * t, n), F32),
        compiler_params=_params("parallel"),
        name="nmm",
    )(x3, mod, g, w)


def _seg_masks(c, seg):
    row = lax.broadcasted_iota(jnp.int32, (c, c), 0)
    col = lax.broadcasted_iota(jnp.int32, (c, c), 1)
    sh = int(math.log2(seg))
    same = (row >> sh) == (col >> sh)
    return row, col, same, same & (row >= col), same & (row > col)


def _tri_inverse_offdiag(lmats, row, col, base, seg):
    bsh = int(math.log2(base))
    blk = (row >> bsh) == (col >> bsh)
    l0s = [jnp.where(blk, l, 0.0) for l in lmats]
    pts = [-l for l in l0s]
    qs = [_mm(l, l) for l in l0s]
    k = 2
    while k < base:
        pts = [p + q + _mm(p, q) for p, q in zip(pts, qs)]
        k *= 2
        if k < base:
            qs = [_mm(q, q) for q in qs]
    m = base
    while m < seg:
        msh = int(math.log2(m))
        sel = ((row >> (msh + 1)) == (col >> (msh + 1))) & ((row >> msh) != (col >> msh))
        es = [jnp.where(sel, l, 0.0) for l in lmats]
        xs = [e + _mm(p, e) for p, e in zip(pts, es)]
        pts = [p - (x + _mm(x, p)) for p, x in zip(pts, xs)]
        m *= 2
    return pts


def _delta_gates(ab, alog, dtb, causal_bf, same_bf, nheads):
    lane = lax.broadcasted_iota(jnp.int32, ab.shape, 1)
    x = ab + dtb
    softplus = jnp.maximum(x, 0.0) + jnp.log1p(jnp.exp(-jnp.abs(x)))
    g = jnp.where(lane < nheads, -jnp.exp(alog) * softplus, 0.0)
    beta = jax.nn.sigmoid(ab)
    pieces = _split3(g)
    gcum = sum(jnp.dot(causal_bf, p, preferred_element_type=F32) for p in pieces)
    glast = sum(jnp.dot(same_bf, p, preferred_element_type=F32) for p in pieces)
    return gcum, beta, glast


def _decay_logits(gcum, nheads):
    c = gcum.shape[0]
    lane = lax.broadcasted_iota(jnp.int32, (c, LANE), 1)
    ones = jnp.ones((c, LANE), BF16)
    pad = jnp.zeros((LANE - c, 2 * LANE), BF16) if c < LANE else None
    out = None
    for p in _split3(gcum):
        lhs = jnp.concatenate([p, ones], axis=1)
        blocks = []
        for h in range(nheads):
            onehot = lane == h
            blk = jnp.concatenate([jnp.where(onehot, 1.0, 0.0).astype(BF16),
                                   jnp.where(onehot, -p, jnp.zeros_like(p))], axis=1)
            blocks.append(blk)
            if pad is not None:
                blocks.append(pad)
        rhs = jnp.concatenate(blocks, axis=0)
        d = lax.dot_general(lhs, rhs, (((1,), (1,)), ((), ())), preferred_element_type=F32)
        out = d if out is None else out + d
    return out


def _delta_intra(act, nheads, gates, dlog, masks, base, seg):
    gcum, beta, glast = gates
    row, col, _, causal, strict = masks
    c = act.shape[0]
    dh = DN_HEAD_DIM
    eg_all = jnp.exp(gcum)
    ed_all = jnp.exp(glast - gcum)
    ks, kqs_lhs, rhss, qgs, kdecs = [], [], [], [], []
    for h in range(nheads):
        q = act[:, h * dh:(h + 1) * dh]
        k = act[:, (nheads + h) * dh:(nheads + h + 1) * dh]
        v = act[:, (2 * nheads + h) * dh:(2 * nheads + h + 1) * dh]
        q = q * lax.rsqrt(jnp.sum(q * q, axis=-1, keepdims=True) + 1e-6) * (dh ** -0.5)
        k = k * lax.rsqrt(jnp.sum(k * k, axis=-1, keepdims=True) + 1e-6)
        b = beta[:, nheads + h:nheads + h + 1]
        eg = eg_all[:, h:h + 1]
        kb = k * b
        ks.append(k)
        kqs_lhs.append(jnp.concatenate([kb, q], axis=0))
        rhss.append(jnp.concatenate([v * b, kb * eg], axis=1))
        qgs.append(q * eg)
        kdecs.append(k * ed_all[:, h:h + 1])
    kqs = [_mm_nt(a, k) for a, k in zip(kqs_lhs, ks)]
    lmats, attns = [], []
    for h in range(nheads):
        dl = dlog[:, h * LANE:h * LANE + c]
        dec = jnp.where(causal, jnp.exp(jnp.where(causal, dl, 0.0)), 0.0)
        lmats.append(jnp.where(strict, kqs[h][:c] * dec, 0.0))
        attns.append(kqs[h][c:] * dec)
    tts = _tri_inverse_offdiag(lmats, row, col, base, seg)
    uws = [r + _mm(t, r) for t, r in zip(tts, rhss)]
    return [uw[:, :dh] for uw in uws], [uw[:, dh:] for uw in uws], qgs, kdecs, attns


def _delta_out(o, z, gdn):
    on = o * lax.rsqrt(jnp.mean(o * o, axis=-1, keepdims=True) + EPS) * gdn
    return on * _silu(z)


def _delta_prompt_body(qkv_ref, z_ref, ab_ref, cst_ref, s0_ref, wc_ref, alog_ref, dtb_ref, gdn_ref,
                       o_ref, cnew_ref, snew_ref, xc_ref, s_ref, *, c, nheads, base):
    dh = DN_HEAD_DIM
    halo = SUBLANE
    taps = CONV_WIDTH - 1

    @pl.when(pl.program_id(1) == 0)
    def _():
        xc_ref[0:halo, :] = jnp.zeros((halo, xc_ref.shape[1]), F32)
        xc_ref[halo - taps:halo, :] = cst_ref[0]
        s_ref[...] = s0_ref[0]

    xc_ref[halo:halo + c, :] = qkv_ref[...]
    conv = xc_ref[halo - taps:halo - taps + c, :] * wc_ref[0:1, :]
    for j in range(1, CONV_WIDTH):
        conv = conv + xc_ref[halo - taps + j:halo - taps + j + c, :] * wc_ref[j:j + 1, :]
    cnew_ref[0] = xc_ref[halo + c - taps:halo + c, :]
    xc_ref[0:halo, :] = xc_ref[c:c + halo, :]
    act = _silu(conv)

    masks = _seg_masks(c, c)
    causal_bf = jnp.where(masks[3], 1.0, 0.0).astype(BF16)
    same_bf = jnp.ones((c, c), BF16)
    gates = _delta_gates(ab_ref[...], alog_ref[...], dtb_ref[...], causal_bf, same_bf, nheads)
    dlog = _decay_logits(gates[0], nheads)
    eglast = jnp.exp(gates[2][0:1, :])
    gdn = gdn_ref[...]
    us, ws_, qgs, kdecs, attns = _delta_intra(act, nheads, gates, dlog, masks, base, c)
    heads = range(nheads)
    ss = [s_ref[h] for h in heads]
    wss = [_mm(jnp.concatenate([ws_[h], qgs[h]], axis=0), ss[h]) for h in heads]
    vnews = [us[h] - wss[h][:c] for h in heads]
    os_ = [wss[h][c:] + _mm(attns[h], vnews[h]) for h in heads]
    snews = [ss[h] * eglast[:, h:h + 1] + _mm_tn(kdecs[h], vnews[h]) for h in heads]
    for h in heads:
        s_ref[h] = snews[h]
        o_ref[:, h * dh:(h + 1) * dh] = _delta_out(os_[h], z_ref[:, h * dh:(h + 1) * dh], gdn).astype(o_ref.dtype)

    @pl.when(pl.program_id(1) == pl.num_programs(1) - 1)
    def _():
        snew_ref[0] = s_ref[...]


def _delta_prompt(qkvz, uab, conv_state, s0, w_conv, alog, dtb, gdn, *, nb, t, c, base):
    nheads = s0.shape[1]
    dh = DN_HEAD_DIM
    mix = nheads * dh
    nc = t // c
    return pl.pallas_call(
        functools.partial(_delta_prompt_body, c=c, nheads=nheads, base=base),
        grid=(nb, nc),
        in_specs=[pl.BlockSpec((c, 3 * mix), lambda b, i: (b * nc + i, 0)),
                  pl.BlockSpec((c, mix), lambda b, i: (b * nc + i, 3)),
                  pl.BlockSpec((c, LANE), lambda b, i: (b * nc + i, uab.shape[1] // LANE - 1)),
                  pl.BlockSpec((1, CONV_WIDTH - 1, 3 * mix), lambda b, i: (b, 0, 0)),
                  pl.BlockSpec((1, nheads, dh, dh), lambda b, i: (b, 0, 0, 0)),
                  pl.BlockSpec((CONV_WIDTH, 3 * mix), lambda b, i: (0, 0)),
                  pl.BlockSpec((1, LANE), lambda b, i: (0, 0)),
                  pl.BlockSpec((1, LANE), lambda b, i: (0, 0)),
                  pl.BlockSpec((1, dh), lambda b, i: (0, 0))],
        out_specs=[pl.BlockSpec((c, mix), lambda b, i: (b * nc + i, 0)),
                   pl.BlockSpec((1, CONV_WIDTH - 1, 3 * mix), lambda b, i: (b, 0, 0)),
                   pl.BlockSpec((1, nheads, dh, dh), lambda b, i: (b, 0, 0, 0))],
        out_shape=[jax.ShapeDtypeStruct((nb * t, mix), BF16),
                   jax.ShapeDtypeStruct((nb, CONV_WIDTH - 1, 3 * mix), F32),
                   jax.ShapeDtypeStruct((nb, nheads, dh, dh), F32)],
        scratch_shapes=[pltpu.VMEM((c + SUBLANE, 3 * mix), F32),
                        pltpu.VMEM((nheads, dh, dh), F32)],
        compiler_params=_params("parallel", "arbitrary"),
        name="delta_prompt",
    )(qkvz, qkvz, uab, conv_state, s0, w_conv, alog, dtb, gdn)


def _delta_sample_body(qkv_ref, z_ref, ab_ref, cst_ref, s0_ref, wc_ref, alog_ref, dtb_ref, gdn_ref,
                       o_ref, cnew_ref, snew_ref,
                       xs_ref, w_s, qg_s, u_s, kd_s, vn_s, qs_s, attn_s, egl_s, *, bb, t, nheads):
    dh = DN_HEAD_DIM
    c = bb * t
    halo = SUBLANE
    taps = CONV_WIDTH - 1

    xs_ref[:, 0:halo, :] = jnp.zeros((bb, halo, xs_ref.shape[2]), F32)
    xs_ref[:, halo - taps:halo, :] = cst_ref[...]
    xs_ref[:, halo:halo + t, :] = qkv_ref[...]
    conv = xs_ref[:, halo - taps:halo - taps + t, :] * wc_ref[0:1, :]
    for j in range(1, CONV_WIDTH):
        conv = conv + xs_ref[:, halo - taps + j:halo - taps + j + t, :] * wc_ref[j:j + 1, :]
    cnew_ref[...] = xs_ref[:, halo + t - taps:halo + t, :]
    act = _silu(conv).reshape(c, conv.shape[2])

    masks = _seg_masks(c, t)
    causal_bf = jnp.where(masks[3], 1.0, 0.0).astype(BF16)
    same_bf = jnp.where(masks[2], 1.0, 0.0).astype(BF16)
    gates = _delta_gates(ab_ref[...], alog_ref[...], dtb_ref[...], causal_bf, same_bf, nheads)
    dlog = _decay_logits(gates[0], nheads)
    egl_s[...] = jnp.exp(gates[2])
    us, ws_, qgs, kdecs, attns = _delta_intra(act, nheads, gates, dlog, masks, t, t)
    for h in range(nheads):
        hs = slice(h * dh, (h + 1) * dh)
        u_s[:, hs] = us[h]
        w_s[:, hs] = ws_[h]
        qg_s[:, hs] = qgs[h]
        kd_s[:, hs] = kdecs[h]
        attn_s[h] = attns[h]

    def per_batch(b, carry):
        r = pl.multiple_of(b * t, t)
        rows = pl.ds(r, t)
        heads = range(nheads)
        hsl = [slice(h * dh, (h + 1) * dh) for h in heads]
        ss = [s0_ref[b, h] for h in heads]
        wss = [_mm(jnp.concatenate([w_s[rows, hsl[h]], qg_s[rows, hsl[h]]], axis=0), ss[h]) for h in heads]
        vnews = [u_s[rows, hsl[h]] - wss[h][:t] for h in heads]
        upds = [_mm_tn(kd_s[rows, hsl[h]], vnews[h]) for h in heads]
        egl = egl_s[pl.ds(r, 1), :]
        for h in heads:
            vn_s[rows, hsl[h]] = vnews[h]
            qs_s[rows, hsl[h]] = wss[h][t:]
            snew_ref[b, h] = ss[h] * egl[:, h:h + 1] + upds[h]
        return carry

    lax.fori_loop(0, bb, per_batch, 0)

    gdn = gdn_ref[...]
    for h in range(nheads):
        hs = slice(h * dh, (h + 1) * dh)
        o = qs_s[:, hs] + _mm(attn_s[h], vn_s[:, hs])
        o_ref[:, hs] = _delta_out(o, z_ref[:, hs], gdn).astype(o_ref.dtype)


def _delta_sample(qkvz, uab, conv_state, s0, w_conv, alog, dtb, gdn, *, nb, t, bb):
    nheads = s0.shape[1]
    dh = DN_HEAD_DIM
    mix = nheads * dh
    c = bb * t
    qkvz3 = qkvz.reshape(nb, t, qkvz.shape[1])
    return pl.pallas_call(
        functools.partial(_delta_sample_body, bb=bb, t=t, nheads=nheads),
        grid=(nb // bb,),
        in_specs=[pl.BlockSpec((bb, t, 3 * mix), lambda i: (i, 0, 0)),
                  pl.BlockSpec((c, mix), lambda i: (i, 3)),
                  pl.BlockSpec((c, LANE), lambda i: (i, uab.shape[1] // LANE - 1)),
                  pl.BlockSpec((bb, CONV_WIDTH - 1, 3 * mix), lambda i: (i, 0, 0)),
                  pl.BlockSpec((bb, nheads, dh, dh), lambda i: (i, 0, 0, 0)),
                  pl.BlockSpec((CONV_WIDTH, 3 * mix), lambda i: (0, 0)),
                  pl.BlockSpec((1, LANE), lambda i: (0, 0)),
                  pl.BlockSpec((1, LANE), lambda i: (0, 0)),
                  pl.BlockSpec((1, dh), lambda i: (0, 0))],
        out_specs=[pl.BlockSpec((c, mix), lambda i: (i, 0)),
                   pl.BlockSpec((bb, CONV_WIDTH - 1, 3 * mix), lambda i: (i, 0, 0)),
                   pl.BlockSpec((bb, nheads, dh, dh), lambda i: (i, 0, 0, 0))],
        out_shape=[jax.ShapeDtypeStruct((nb * t, mix), BF16),
                   jax.ShapeDtypeStruct((nb, CONV_WIDTH - 1, 3 * mix), F32),
                   jax.ShapeDtypeStruct((nb, nheads, dh, dh), F32)],
        scratch_shapes=[pltpu.VMEM((bb, SUBLANE + t, 3 * mix), F32)]
                       + [pltpu.VMEM((c, mix), F32) for _ in range(6)]
                       + [pltpu.VMEM((nheads, c, c), F32), pltpu.VMEM((c, LANE), F32)],
        compiler_params=_params("parallel"),
        name="delta_sample",
    )(qkvz3, qkvz, uab, conv_state, s0, w_conv, alog, dtb, gdn)


def _s5_prep_body(lre_ref, lim_ref, ls_ref, bre_ref, bim_ref, wa_ref, abre_ref, abim_ref):
    lre = jnp.minimum(lre_ref[...], -1e-4)
    lim = lim_ref[...]
    dt = jnp.exp(ls_ref[...])
    mag = jnp.exp(lre * dt)
    ang = lim * dt
    abr = mag * jnp.cos(ang)
    abi = mag * jnp.sin(ang)
    nr = abr - 1.0
    ni = abi
    den = lre * lre + lim * lim
    cor = (nr * lre + ni * lim) / den
    coi = (ni * lre - nr * lim) / den
    bre = bre_ref[...]
    bim = bim_ref[...]
    shape2 = bre.shape[1:]
    row = lax.broadcasted_iota(jnp.int32, shape2, 0)
    col = lax.broadcasted_iota(jnp.int32, shape2, 1)
    diag = ((row // S5_GROUP) == (col // S5_STATE))[None]
    half = shape2[1]
    wa_ref[:, :, 0:half] = jnp.where(diag, cor * bre - coi * bim, 0.0).astype(BF16)
    wa_ref[:, :, half:2 * half] = jnp.where(diag, cor * bim + coi * bre, 0.0).astype(BF16)
    abre_ref[...] = abr
    abim_ref[...] = abi


def _s5_prep(lam_re, lam_im, log_step, b_re, b_im):
    groups, states, gch = b_re.shape
    slabs = groups // S5_SLAB_GROUPS
    cols = S5_SLAB_STATES

    def row_param(p):
        return p.reshape(slabs, 1, cols)

    def tiled_b(b):
        bt = jnp.transpose(b, (0, 2, 1)).reshape(slabs, LANE, states)
        return jnp.tile(bt, (1, 1, S5_SLAB_GROUPS))

    ls = jnp.broadcast_to(log_step[:, None], (groups, states))
    wa, abre, abim = pl.pallas_call(
        _s5_prep_body,
        out_shape=[jax.ShapeDtypeStruct((slabs, LANE, 2 * cols), BF16),
                   jax.ShapeDtypeStruct((slabs, 1, cols), F32),
                   jax.ShapeDtypeStruct((slabs, 1, cols), F32)],
        compiler_params=pltpu.CompilerParams(vmem_limit_bytes=VMEM_LIMIT),
        name="s5prep",
    )(row_param(lam_re), row_param(lam_im), row_param(ls), tiled_b(b_re), tiled_b(b_im))
    nblk = groups * states // LANE
    return wa, abre.reshape(nblk, 1, LANE), abim.reshape(nblk, 1, LANE)


def _s5_out_weights(c):
    groups, gch, states = c.shape
    slabs = groups // S5_SLAB_GROUPS
    ct = jnp.transpose(c, (0, 2, 1)).reshape(slabs, S5_SLAB_STATES, gch)
    ct = jnp.tile(ct, (1, 1, S5_SLAB_GROUPS))
    row = jnp.arange(S5_SLAB_STATES)[:, None] // states
    col = jnp.arange(LANE)[None, :] // gch
    return jnp.where((row == col)[None], ct, 0.0).astype(BF16)


def _gelu_tanh(x):
    return x * (0.5 * (1.0 + jnp.tanh(math.sqrt(2.0 / math.pi) * (x + 0.044715 * (x * x * x)))))


S5_SCAN_BLOCKS = 4


def _time_major_perms(rows, nb, lc):
    r = lax.broadcasted_iota(jnp.int32, (rows, rows), 0)
    c = lax.broadcasted_iota(jnp.int32, (rows, rows), 1)
    nsh, lsh = int(math.log2(nb)), int(math.log2(lc))
    to_tm = c == (((r & (nb - 1)) << lsh) | (r >> nsh))
    to_bm = c == (((r & (lc - 1)) << nsh) | (r >> lsh))
    as_bf = lambda m: jnp.where(m, 1.0, 0.0).astype(BF16)
    return as_bf(to_tm), as_bf(to_bm)


def _s5_body(u_ref, h0re_ref, h0im_ref, wa_ref, abre_ref, abim_ref, wcre_ref, wcim_ref, dsk_ref,
             wglu_ref, gs5_ref, y_ref, hre_ref, him_ref, xre, xim, cre, cim, *, nb, lc):
    rows = nb * lc
    nblk = xre.shape[0]
    slabs = wa_ref.shape[0]
    per_slab = nblk // slabs
    half = per_slab * LANE
    steps_per_tile = SUBLANE // nb
    assert steps_per_tile in (1, 2) and lc % steps_per_tile == 0

    @pl.when(pl.program_id(1) == 0)
    def _():
        for j in range(nblk):
            if nb < SUBLANE:
                cre[j] = jnp.zeros((SUBLANE, LANE), F32)
                cim[j] = jnp.zeros((SUBLANE, LANE), F32)
            cre[j, SUBLANE - nb:SUBLANE, :] = h0re_ref[:, j * LANE:(j + 1) * LANE]
            cim[j, SUBLANE - nb:SUBLANE, :] = h0im_ref[:, j * LANE:(j + 1) * LANE]

    to_tm, to_bm = _time_major_perms(rows, nb, lc)
    u = sum(jnp.dot(to_tm, p, preferred_element_type=F32)
            for p in _split3(u_ref[...].reshape(rows, u_ref.shape[2])))
    ub = u.astype(BF16)
    for s in range(slabs):
        x = jnp.dot(ub[:, s * LANE:(s + 1) * LANE], wa_ref[s], preferred_element_type=F32)
        for k in range(per_slab):
            xre[s * per_slab + k] = x[:, k * LANE:(k + 1) * LANE]
            xim[s * per_slab + k] = x[:, half + k * LANE:half + (k + 1) * LANE]

    upper = lax.broadcasted_iota(jnp.int32, (SUBLANE, LANE), 0) >= nb
    for j0 in range(0, nblk, S5_SCAN_BLOCKS):
        js = list(range(j0, j0 + S5_SCAN_BLOCKS))
        coef = []
        for j in js:
            ar = jnp.broadcast_to(abre_ref[j], (SUBLANE, LANE))
            ai = jnp.broadcast_to(abim_ref[j], (SUBLANE, LANE))
            if steps_per_tile == 2:
                coef.append((jnp.where(upper, ar, 0.0), jnp.where(upper, ai, 0.0),
                             jnp.where(upper, ar * ar - ai * ai, ar), jnp.where(upper, 2.0 * ar * ai, ai)))
            else:
                coef.append((ar, ai))

        def step(v, carry, js=js, coef=coef):
            at = pl.ds(pl.multiple_of(v * SUBLANE, SUBLANE), SUBLANE)
            out = []
            for n, j in enumerate(js):
                hr, hi = carry[2 * n], carry[2 * n + 1]
                xr, xi = xre[j, at, :], xim[j, at, :]
                if steps_per_tile == 2:
                    m1r, m1i, pr, pi = coef[n]
                    sr, si = pltpu.roll(xr, nb, 0), pltpu.roll(xi, nb, 0)
                    yr = xr + m1r * sr - m1i * si
                    yi = xi + m1r * si + m1i * sr
                    cr = jnp.where(upper, hr, pltpu.roll(hr, nb, 0))
                    ci = jnp.where(upper, hi, pltpu.roll(hi, nb, 0))
                    nr = yr + pr * cr - pi * ci
                    ni = yi + pr * ci + pi * cr
                else:
                    ar, ai = coef[n]
                    nr = ar * hr - ai * hi + xr
                    ni = ar * hi + ai * hr + xi
                xre[j, at, :] = nr
                xim[j, at, :] = ni
                out += [nr, ni]
            return tuple(out)

        init = []
        for j in js:
            init += [cre[j], cim[j]]
        fin = lax.fori_loop(0, rows // SUBLANE, step, tuple(init))
        for n, j in enumerate(js):
            cre[j] = fin[2 * n]
            cim[j] = fin[2 * n + 1]

    ys = []
    for s in range(slabs):
        hre = jnp.concatenate([xre[s * per_slab + k] for k in range(per_slab)], axis=1)
        him = jnp.concatenate([xim[s * per_slab + k] for k in range(per_slab)], axis=1)
        ys.append(_mm(hre, wcre_ref[s]) - _mm(him, wcim_ref[s]))
    y = jnp.concatenate(ys, axis=1) + dsk_ref[...] * u
    ab = _mm(_gelu_tanh(y), wglu_ref[...])
    n = y.shape[1]
    glu = ab[:, :n] * jax.nn.sigmoid(ab[:, n:])
    out = glu * lax.rsqrt(jnp.mean(glu * glu, axis=-1, keepdims=True) + EPS) * gs5_ref[...]
    out = jnp.dot(to_bm, out.astype(BF16), preferred_element_type=F32)
    y_ref[...] = out.reshape(y_ref.shape).astype(y_ref.dtype)

    @pl.when(pl.program_id(1) == pl.num_programs(1) - 1)
    def _():
        for j in range(nblk):
            hre_ref[:, j * LANE:(j + 1) * LANE] = cre[j, SUBLANE - nb:SUBLANE, :]
            him_ref[:, j * LANE:(j + 1) * LANE] = cim[j, SUBLANE - nb:SUBLANE, :]


def _s5(uab, h0re, h0im, wa, abre, abim, wcre, wcim, dskip, wglu, gs5, *, nb_total, t, nb, lc):
    mixb = dskip.shape[1]
    nstate = h0re.shape[1]
    nblk = nstate // LANE
    uab3 = uab.reshape(nb_total, t, uab.shape[1])
    full = lambda a: pl.BlockSpec(a.shape, lambda i, j: (0,) * a.ndim)
    y, hre, him = pl.pallas_call(
        functools.partial(_s5_body, nb=nb, lc=lc),
        grid=(nb_total // nb, t // lc),
        in_specs=[pl.BlockSpec((nb, lc, mixb), lambda i, j: (i, j, 0)),
                  pl.BlockSpec((nb, nstate), lambda i, j: (i, 0)),
                  pl.BlockSpec((nb, nstate), lambda i, j: (i, 0)),
                  full(wa), full(abre), full(abim), full(wcre), full(wcim), full(dskip), full(wglu), full(gs5)],
        out_specs=[pl.BlockSpec((nb, lc, mixb), lambda i, j: (i, j, 0)),
                   pl.BlockSpec((nb, nstate), lambda i, j: (i, 0)),
                   pl.BlockSpec((nb, nstate), lambda i, j: (i, 0))],
        out_shape=[jax.ShapeDtypeStruct((nb_total, t, mixb), F32),
                   jax.ShapeDtypeStruct((nb_total, nstate), F32),
                   jax.ShapeDtypeStruct((nb_total, nstate), F32)],
        scratch_shapes=[pltpu.VMEM((nblk, nb * lc, LANE), F32),
                        pltpu.VMEM((nblk, nb * lc, LANE), F32),
                        pltpu.VMEM((nblk, SUBLANE, LANE), F32),
                        pltpu.VMEM((nblk, SUBLANE, LANE), F32)],
        compiler_params=_params("parallel", "arbitrary"),
        name="s5",
    )(uab3, h0re, h0im, wa, abre, abim, wcre, wcim, dskip, wglu, gs5)
    return y.reshape(nb_total * t, mixb), hre, him


def _outproj_body(o_ref, y_ref, wt_ref, wb_ref, x_ref, mod_ref, g_ref, out_ref, *, gate):
    mix = (jnp.dot(o_ref[...], wt_ref[...], preferred_element_type=F32)
           + _mm(y_ref[...], wb_ref[...]))
    n = mix * lax.rsqrt(jnp.mean(mix * mix, axis=-1, keepdims=True) + EPS) * g_ref[...]
    x = x_ref[...]
    out_ref[...] = x + mod_ref[:, gate:gate + 1, :] * n.reshape(x.shape)


def _outproj(o, y5, wtop, wbot, x3, mod, g, *, gate, rows):
    nb, t, d = x3.shape
    bb, tt = _token_blocks(nb, t, rows)
    nt = t // tt
    ka, kb = o.shape[1], y5.shape[1]
    return pl.pallas_call(
        functools.partial(_outproj_body, gate=gate),
        grid=(nb * t // rows,),
        in_specs=[pl.BlockSpec((rows, ka), lambda i: (i, 0)),
                  pl.BlockSpec((rows, kb), lambda i: (i, 0)),
                  pl.BlockSpec((ka, d), lambda i: (0, 0)),
                  pl.BlockSpec((kb, d), lambda i: (0, 0)),
                  pl.BlockSpec((bb, tt, d), lambda i: (i // nt, i % nt, 0)),
                  pl.BlockSpec((bb, 6, d), lambda i: (i // nt, 0, 0)),
                  pl.BlockSpec((1, d), lambda i: (0, 0))],
        out_specs=pl.BlockSpec((bb, tt, d), lambda i: (i // nt, i % nt, 0)),
        out_shape=jax.ShapeDtypeStruct((nb, t, d), F32),
        compiler_params=_params("parallel"),
        name="outproj",
    )(o, y5, wtop, wbot, x3, mod, g)


def _ffn_body(x_ref, mod_ref, gpre_ref, wg_ref, wu_ref, wd_ref, gpost_ref, out_ref, h_ref,
              *, shift, scale, gate):
    j = pl.program_id(1)

    @pl.when(j == 0)
    def _():
        x = x_ref[...]
        bb, tt, d = x.shape
        h = _modulated_norm(x, gpre_ref[...], mod_ref[:, scale:scale + 1, :], mod_ref[:, shift:shift + 1, :])
        h_ref[...] = h.reshape(bb * tt, d).astype(BF16)
        out_ref[...] = jnp.zeros(out_ref.shape, F32)

    h = h_ref[...]
    a = (_silu(jnp.dot(h, wg_ref[...], preferred_element_type=F32))
         * jnp.dot(h, wu_ref[...], preferred_element_type=F32))
    out_ref[...] += _mm(a, wd_ref[...]).reshape(out_ref.shape)

    @pl.when(j == pl.num_programs(1) - 1)
    def _():
        f = out_ref[...]
        n = f * lax.rsqrt(jnp.mean(f * f, axis=-1, keepdims=True) + EPS) * gpost_ref[...]
        out_ref[...] = x_ref[...] + mod_ref[:, gate:gate + 1, :] * n


def _ffn(x3, mod, gpre, wg, wu, wd, gpost, *, shift, scale, gate, rows, tf):
    nb, t, d = x3.shape
    dff = wg.shape[1]
    bb, tt = _token_blocks(nb, t, rows)
    nt = t // tt
    return pl.pallas_call(
        functools.partial(_ffn_body, shift=shift, scale=scale, gate=gate),
        grid=(nb * t // rows, dff // tf),
        in_specs=[pl.BlockSpec((bb, tt, d), lambda i, j: (i // nt, i % nt, 0), pipeline_mode=pl.Buffered(1)),
                  pl.BlockSpec((bb, 6, d), lambda i, j: (i // nt, 0, 0), pipeline_mode=pl.Buffered(1)),
                  pl.BlockSpec((1, d), lambda i, j: (0, 0)),
                  pl.BlockSpec((d, tf), lambda i, j: (0, j)),
                  pl.BlockSpec((d, tf), lambda i, j: (0, j)),
                  pl.BlockSpec((tf, d), lambda i, j: (j, 0)),
                  pl.BlockSpec((1, d), lambda i, j: (0, 0))],
        out_specs=pl.BlockSpec((bb, tt, d), lambda i, j: (i // nt, i % nt, 0)),
        out_shape=jax.ShapeDtypeStruct((nb, t, d), F32),
        scratch_shapes=[pltpu.VMEM((rows, d), BF16)],
        compiler_params=_params("parallel", "arbitrary"),
        name="ffn",
    )(x3, mod, gpre, wg, wu, wd, gpost)


ROWS = 512
FFN_ROWS = 1024
FFN_COLS = 512
PROMPT_CHUNK = 128
PROMPT_BASE = 16
SAMPLE_BATCH = 8
S5_PROMPT_STEPS = 128
S5_SAMPLE_BATCH = 8


def _layer_weights(w_ada, b_ada, g_pre_mix, g_post_mix, g_pre_ffn, g_post_ffn,
                   w_in, w_conv, a_log, dt_bias, g_dn_out,
                   lam_re, lam_im, log_step, b_re, b_im, c_re, c_im, d_skip,
                   w_glu, g_s5_out, w_out, w_gate, w_up, w_down):
    d = w_in.shape[0]
    nheads = a_log.shape[0]
    mix_a = nheads * DN_HEAD_DIM
    qkvz = 4 * mix_a
    gates = 2 * nheads
    wq = w_in[:, :qkvz].astype(BF16)
    wuab = jnp.concatenate([w_in[:, qkvz + gates:], w_in[:, qkvz:qkvz + gates],
                            jnp.zeros((d, LANE - gates), w_in.dtype)], axis=1).astype(BF16)
    pad = lambda v: jnp.pad(v, (0, LANE - v.shape[0]))[None, :]
    wa, abre, abim = _s5_prep(lam_re, lam_im, log_step, b_re, b_im)
    return dict(
        w_ada=w_ada, b_ada=b_ada[None, :],
        g_pre_mix=g_pre_mix[None, :], g_post_mix=g_post_mix[None, :],
        g_pre_ffn=g_pre_ffn[None, :], g_post_ffn=g_post_ffn[None, :],
        wq=wq, wuab=wuab, w_conv=w_conv, alog=pad(a_log), dtb=pad(dt_bias), gdn=g_dn_out[None, :],
        wa=wa, abre=abre, abim=abim, wcre=_s5_out_weights(c_re), wcim=_s5_out_weights(c_im),
        dskip=d_skip.reshape(1, -1), wglu=w_glu.astype(BF16), gs5=g_s5_out[None, :],
        wtop=w_out[:mix_a].astype(BF16), wbot=w_out[mix_a:].astype(BF16),
        wg=w_gate.astype(BF16), wu=w_up.astype(BF16), wd=w_down.astype(BF16),
    )


def _layer_group(x3, mod, conv_state, s0, h0re, h0im, lw, *, prompt):
    nb, t, d = x3.shape
    rows = min(ROWS, nb * t)
    qkvz = _nmm(x3, mod, lw["g_pre_mix"], lw["wq"], shift=0, scale=1, rows=rows)
    uab = _nmm(x3, mod, lw["g_pre_mix"], lw["wuab"], shift=0, scale=1, rows=rows)
    if prompt:
        o, cnew, snew = _delta_prompt(qkvz, uab, conv_state, s0, lw["w_conv"], lw["alog"], lw["dtb"], lw["gdn"],
                                      nb=nb, t=t, c=PROMPT_CHUNK, base=PROMPT_BASE)
        y5, hre, him = _s5(uab, h0re, h0im, lw["wa"], lw["abre"], lw["abim"], lw["wcre"], lw["wcim"],
                           lw["dskip"], lw["wglu"], lw["gs5"], nb_total=nb, t=t, nb=nb, lc=S5_PROMPT_STEPS)
    else:
        o, cnew, snew = _delta_sample(qkvz, uab, conv_state, s0, lw["w_conv"], lw["alog"], lw["dtb"], lw["gdn"],
                                      nb=nb, t=t, bb=SAMPLE_BATCH)
        y5, hre, him = _s5(uab, h0re, h0im, lw["wa"], lw["abre"], lw["abim"], lw["wcre"], lw["wcim"],
                           lw["dskip"], lw["wglu"], lw["gs5"], nb_total=nb, t=t, nb=S5_SAMPLE_BATCH, lc=t)
    x1 = _outproj(o, y5, lw["wtop"], lw["wbot"], x3, mod, lw["g_post_mix"], gate=2, rows=rows)
    x2 = _ffn(x1, mod, lw["g_pre_ffn"], lw["wg"], lw["wu"], lw["wd"], lw["g_post_ffn"],
              shift=3, scale=4, gate=5, rows=min(FFN_ROWS, t) if t >= rows else rows, tf=FFN_COLS)
    ngroups = h0re.shape[1] // S5_STATE
    return x2, cnew, snew, hre.reshape(nb, ngroups, S5_STATE), him.reshape(nb, ngroups, S5_STATE)


def kernel(x_prompt, x_sample, c_prompt, c_sample, state_conv, state_delta, state_ssm_re, state_ssm_im, w_ada, b_ada, g_pre_mix, g_post_mix, g_pre_ffn, g_post_ffn, w_in, w_conv, a_log, dt_bias, g_dn_out, lam_re, lam_im, log_step, b_re, b_im, c_re, c_im, d_skip, w_glu, g_s5_out, w_out, w_gate, w_up, w_down):
    weights = (w_ada, b_ada, g_pre_mix, g_post_mix, g_pre_ffn, g_post_ffn,
               w_in, w_conv, a_log, dt_bias, g_dn_out,
               lam_re, lam_im, log_step, b_re, b_im, c_re, c_im, d_skip,
               w_glu, g_s5_out, w_out, w_gate, w_up, w_down)
    depth = w_ada.shape[0]
    bp, d = c_prompt.shape
    bs = c_sample.shape[0]
    dt_ = x_prompt.dtype
    nheads = state_delta.shape[2]
    nstate = state_ssm_re.shape[2] * state_ssm_re.shape[3]
    crows = -(-(bp + bs) // SUBLANE) * SUBLANE
    c_all = jnp.concatenate([c_prompt, c_sample, jnp.zeros((crows - bp - bs, d), dt_)], axis=0)
    yp, ys = x_prompt, x_sample
    outs = [[] for _ in range(8)]
    for l in range(depth):
        lw = _layer_weights(*[w[l] for w in weights])
        mod = _ada(c_all, lw["w_ada"], lw["b_ada"]).reshape(crows, 6, d)
        zc = jnp.zeros((bp, CONV_WIDTH - 1, state_conv.shape[3]), dt_)
        zd = jnp.zeros((bp,) + state_delta.shape[2:], dt_)
        zs = jnp.zeros((bp, nstate), dt_)
        yp, c1, d1, r1, i1 = _layer_group(yp, mod[:bp], zc, zd, zs, zs, lw, prompt=True)
        ys, c2, d2, r2, i2 = _layer_group(ys, mod[bp:bp + bs], state_conv[l], state_delta[l],
                                          state_ssm_re[l].reshape(bs, nstate), state_ssm_im[l].reshape(bs, nstate),
                                          lw, prompt=False)
        for lst, v in zip(outs, (c1, d1, r1, i1, c2, d2, r2, i2)):
            lst.append(v)
    stacked = [jnp.stack(v) for v in outs]
    return (yp, ys, *stacked)
```

```python
import functools
import math

import jax
import jax.numpy as jnp
import numpy as np
from jax import lax
from jax.experimental import pallas as pl
from jax.experimental.pallas import tpu as pltpu

F32 = jnp.float32
BF16 = jnp.bfloat16
EPS = 1e-6
LANE = 128
SUBLANE = 8
V7X_VMEM_BYTES = 64 << 20
VMEM_LIMIT = V7X_VMEM_BYTES - (8 << 20)

DN_HEAD_DIM = 128
CONV_WIDTH = 4
S5_GROUP = 16
S5_STATE = 64
S5_SLAB_GROUPS = LANE // S5_GROUP
S5_SLAB_STATES = S5_SLAB_GROUPS * S5_STATE


def _params(*sem):
    return pltpu.CompilerParams(dimension_semantics=sem, vmem_limit_bytes=VMEM_LIMIT)


def _silu(x):
    return x * jax.nn.sigmoid(x)


def _mm(a, b):
    return jnp.dot(a.astype(BF16), b.astype(BF16), preferred_element_type=F32)


def _mm_nt(a, b):
    return lax.dot_general(a.astype(BF16), b.astype(BF16), (((1,), (1,)), ((), ())),
                           preferred_element_type=F32)


def _mm_tn(a, b):
    return lax.dot_general(a.astype(BF16), b.astype(BF16), (((0,), (0,)), ((), ())),
                           preferred_element_type=F32)


def _split3(x):
    x1 = x.astype(BF16)
    r1 = x - x1.astype(F32)
    x2 = r1.astype(BF16)
    x3 = (r1 - x2.astype(F32)).astype(BF16)
    return x1, x2, x3


def _ada_body(c_ref, w_ref, b_ref, o_ref):
    s = _silu(c_ref[...])
    o_ref[...] = _mm(s, w_ref[...]) + b_ref[...]


def _ada(c, w, b):
    rows, d = c.shape
    n = w.shape[1]
    tn = 1024
    return pl.pallas_call(
        _ada_body,
        grid=(n // tn,),
        in_specs=[pl.BlockSpec((rows, d), lambda j: (0, 0)),
                  pl.BlockSpec((d, tn), lambda j: (0, j)),
                  pl.BlockSpec((1, tn), lambda j: (0, j))],
        out_specs=pl.BlockSpec((rows, tn), lambda j: (0, j)),
        out_shape=jax.ShapeDtypeStruct((rows, n), F32),
        compiler_params=_params("arbitrary"),
        name="ada",
    )(c, w, b)


def _token_blocks(nb, t, rows):
    if t >= rows:
        assert t % rows == 0
        return 1, rows
    assert rows % t == 0 and nb % (rows // t) == 0
    return rows // t, t


def _modulated_norm(x, g, scale, shift):
    y = x * lax.rsqrt(jnp.mean(x * x, axis=-1, keepdims=True) + EPS) * g
    return y * (1.0 + scale) + shift


def _nmm_body(x_ref, mod_ref, g_ref, w_ref, o_ref, *, shift, scale):
    x = x_ref[...]
    bb, tt, d = x.shape
    h = _modulated_norm(x, g_ref[...], mod_ref[:, scale:scale + 1, :], mod_ref[:, shift:shift + 1, :])
    o_ref[...] = jnp.dot(h.reshape(bb * tt, d).astype(BF16), w_ref[...], preferred_element_type=F32)


def _nmm(x3, mod, g, w, *, shift, scale, rows):
    nb, t, d = x3.shape
    n = w.shape[1]
    bb, tt = _token_blocks(nb, t, rows)
    nt = t // tt
    return pl.pallas_call(
        functools.partial(_nmm_body, shift=shift, scale=scale),
        grid=(nb * t // rows,),
        in_specs=[pl.BlockSpec((bb, tt, d), lambda i: (i // nt, i % nt, 0)),
                  pl.BlockSpec((bb, 6, d), lambda i: (i // nt, 0, 0)),
                  pl.BlockSpec((1, d), lambda i: (0, 0)),
                  pl.BlockSpec((d, n), lambda i: (0, 0), pipeline_mode=pl.Buffered(1))],
        out_specs=pl.BlockSpec((rows, n), lambda i: (i, 0)),
        out_shape=jax.ShapeDtypeStruct((nb * t, n), F32),
        compiler_params=_params("parallel"),
        name="nmm",
    )(x3, mod, g, w)


def _seg_masks(c, seg):
    row = lax.broadcasted_iota(jnp.int32, (c, c), 0)
    col = lax.broadcasted_iota(jnp.int32, (c, c), 1)
    sh = int(math.log2(seg))
    same = (row >> sh) == (col >> sh)
    return row, col, same, same & (row >= col), same & (row > col)


def _tri_inverse_offdiag(lmats, row, col, base, seg):
    bsh = int(math.log2(base))
    blk = (row >> bsh) == (col >> bsh)
    l0s = [jnp.where(blk, l, 0.0) for l in lmats]
    pts = [-l for l in l0s]
    qs = [_mm(l, l) for l in l0s]
    k = 2
    while k < base:
        pts = [p + q + _mm(p, q) for p, q in zip(pts, qs)]
        k *= 2
        if k < base:
            qs = [_mm(q, q) for q in qs]
    m = base
    while m < seg:
        msh = int(math.log2(m))
        sel = ((row >> (msh + 1)) == (col >> (msh + 1))) & ((row >> msh) != (col >> msh))
        es = [jnp.where(sel, l, 0.0) for l in lmats]
        xs = [e + _mm(p, e) for p, e in zip(pts, es)]
        pts = [p - (x + _mm(x, p)) for p, x in zip(pts, xs)]
        m *= 2
    return pts


def _delta_gates(ab, alog, dtb, causal_bf, same_bf, nheads):
    lane = lax.broadcasted_iota(jnp.int32, ab.shape, 1)
    x = ab + dtb
    softplus = jnp.maximum(x, 0.0) + jnp.log1p(jnp.exp(-jnp.abs(x)))
    g = jnp.where(lane < nheads, -jnp.exp(alog) * softplus, 0.0)
    beta = jax.nn.sigmoid(ab)
    pieces = _split3(g)
    gcum = sum(jnp.dot(causal_bf, p, preferred_element_type=F32) for p in pieces)
    glast = sum(jnp.dot(same_bf, p, preferred_element_type=F32) for p in pieces)
    return gcum, beta, glast


def _decay_logits(gcum, nheads):
    c = gcum.shape[0]
    lane = lax.broadcasted_iota(jnp.int32, (c, LANE), 1)
    ones = jnp.ones((c, LANE), BF16)
    pad = jnp.zeros((LANE - c, 2 * LANE), BF16) if c < LANE else None
    out = None
    for p in _split3(gcum):
        lhs = jnp.concatenate([p, ones], axis=1)
        blocks = []
        for h in range(nheads):
            onehot = lane == h
            blk = jnp.concatenate([jnp.where(onehot, 1.0, 0.0).astype(BF16),
                                   jnp.where(onehot, -p, jnp.zeros_like(p))], axis=1)
            blocks.append(blk)
            if pad is not None:
                blocks.append(pad)
        rhs = jnp.concatenate(blocks, axis=0)
        d = lax.dot_general(lhs, rhs, (((1,), (1,)), ((), ())), preferred_element_type=F32)
        out = d if out is None else out + d
    return out


def _delta_intra(act, nheads, gates, dlog, masks, base, seg):
    gcum, beta, glast = gates
    row, col, _, causal, strict = masks
    c = act.shape[0]
    dh = DN_HEAD_DIM
    eg_all = jnp.exp(gcum)
    ed_all = jnp.exp(glast - gcum)
    ks, kqs_lhs, rhss, qgs, kdecs = [], [], [], [], []
    for h in range(nheads):
        q = act[:, h * dh:(h + 1) * dh]
        k = act[:, (nheads + h) * dh:(nheads + h + 1) * dh]
        v = act[:, (2 * nheads + h) * dh:(2 * nheads + h + 1) * dh]
        q = q * lax.rsqrt(jnp.sum(q * q, axis=-1, keepdims=True) + 1e-6) * (dh ** -0.5)
        k = k * lax.rsqrt(jnp.sum(k * k, axis=-1, keepdims=True) + 1e-6)
        b = beta[:, nheads + h:nheads + h + 1]
        eg = eg_all[:, h:h + 1]
        kb = k * b
        ks.append(k)
        kqs_lhs.append(jnp.concatenate([kb, q], axis=0))
        rhss.append(jnp.concatenate([v * b, kb * eg], axis=1))
        qgs.append(q * eg)
        kdecs.append(k * ed_all[:, h:h + 1])
    kqs = [_mm_nt(a, k) for a, k in zip(kqs_lhs, ks)]
    lmats, attns = [], []
    for h in range(nheads):
        dl = dlog[:, h * LANE:h * LANE + c]
        dec = jnp.where(causal, jnp.exp(jnp.where(causal, dl, 0.0)), 0.0)
        lmats.append(jnp.where(strict, kqs[h][:c] * dec, 0.0))
        attns.append(kqs[h][c:] * dec)
    tts = _tri_inverse_offdiag(lmats, row, col, base, seg)
    uws = [r + _mm(t, r) for t, r in zip(tts, rhss)]
    return [uw[:, :dh] for uw in uws], [uw[:, dh:] for uw in uws], qgs, kdecs, attns


def _delta_out(o, z, gdn):
    on = o * lax.rsqrt(jnp.mean(o * o, axis=-1, keepdims=True) + EPS) * gdn
    return on * _silu(z)


def _delta_prompt_body(qkv_ref, z_ref, ab_ref, cst_ref, s0_ref, wc_ref, alog_ref, dtb_ref, gdn_ref,
                       o_ref, cnew_ref, snew_ref, xc_ref, s_ref, *, c, nheads, base):
    dh = DN_HEAD_DIM
    halo = SUBLANE
    taps = CONV_WIDTH - 1

    @pl.when(pl.program_id(1) == 0)
    def _():
        xc_ref[0:halo, :] = jnp.zeros((halo, xc_ref.shape[1]), F32)
        xc_ref[halo - taps:halo, :] = cst_ref[0]
        s_ref[...] = s0_ref[0]

    xc_ref[halo:halo + c, :] = qkv_ref[...]
    conv = xc_ref[halo - taps:halo - taps + c, :] * wc_ref[0:1, :]
    for j in range(1, CONV_WIDTH):
        conv = conv + xc_ref[halo - taps + j:halo - taps + j + c, :] * wc_ref[j:j + 1, :]
    cnew_ref[0] = xc_ref[halo + c - taps:halo + c, :]
    xc_ref[0:halo, :] = xc_ref[c:c + halo, :]
    act = _silu(conv)

    masks = _seg_masks(c, c)
    causal_bf = jnp.where(masks[3], 1.0, 0.0).astype(BF16)
    same_bf = jnp.ones((c, c), BF16)
    gates = _delta_gates(ab_ref[...], alog_ref[...], dtb_ref[...], causal_bf, same_bf, nheads)
    dlog = _decay_logits(gates[0], nheads)
    eglast = jnp.exp(gates[2][0:1, :])
    gdn = gdn_ref[...]
    us, ws_, qgs, kdecs, attns = _delta_intra(act, nheads, gates, dlog, masks, base, c)
    heads = range(nheads)
    ss = [s_ref[h] for h in heads]
    wss = [_mm(jnp.concatenate([ws_[h], qgs[h]], axis=0), ss[h]) for h in heads]
    vnews = [us[h] - wss[h][:c] for h in heads]
    os_ = [wss[h][c:] + _mm(attns[h], vnews[h]) for h in heads]
    snews = [ss[h] * eglast[:, h:h + 1] + _mm_tn(kdecs[h], vnews[h]) for h in heads]
    for h in heads:
        s_ref[h] = snews[h]
        o_ref[:, h * dh:(h + 1) * dh] = _delta_out(os_[h], z_ref[:, h * dh:(h + 1) * dh], gdn).astype(o_ref.dtype)

    @pl.when(pl.program_id(1) == pl.num_programs(1) - 1)
    def _():
        snew_ref[0] = s_ref[...]


def _delta_prompt(qkvz, uab, conv_state, s0, w_conv, alog, dtb, gdn, *, nb, t, c, base):
    nheads = s0.shape[1]
    dh = DN_HEAD_DIM
    mix = nheads * dh
    nc = t // c
    return pl.pallas_call(
        functools.partial(_delta_prompt_body, c=c, nheads=nheads, base=base),
        grid=(nb, nc),
        in_specs=[pl.BlockSpec((c, 3 * mix), lambda b, i: (b * nc + i, 0)),
                  pl.BlockSpec((c, mix), lambda b, i: (b * nc + i, 3)),
                  pl.BlockSpec((c, LANE), lambda b, i: (b * nc + i, uab.shape[1] // LANE - 1)),
                  pl.BlockSpec((1, CONV_WIDTH - 1, 3 * mix), lambda b, i: (b, 0, 0)),
                  pl.BlockSpec((1, nheads, dh, dh), lambda b, i: (b, 0, 0, 0)),
                  pl.BlockSpec((CONV_WIDTH, 3 * mix), lambda b, i: (0, 0)),
                  pl.BlockSpec((1, LANE), lambda b, i: (0, 0)),
                  pl.BlockSpec((1, LANE), lambda b, i: (0, 0)),
                  pl.BlockSpec((1, dh), lambda b, i: (0, 0))],
        out_specs=[pl.BlockSpec((c, mix), lambda b, i: (b * nc + i, 0)),
                   pl.BlockSpec((1, CONV_WIDTH - 1, 3 * mix), lambda b, i: (b, 0, 0)),
                   pl.BlockSpec((1, nheads, dh, dh), lambda b, i: (b, 0, 0, 0))],
        out_shape=[jax.ShapeDtypeStruct((nb * t, mix), BF16),
                   jax.ShapeDtypeStruct((nb, CONV_WIDTH - 1, 3 * mix), F32),
                   jax.ShapeDtypeStruct((nb, nheads, dh, dh), F32)],
        scratch_shapes=[pltpu.VMEM((c + SUBLANE, 3 * mix), F32),
                        pltpu.VMEM((nheads, dh, dh), F32)],
        compiler_params=_params("parallel", "arbitrary"),
        name="delta_prompt",
    )(qkvz, qkvz, uab, conv_state, s0, w_conv, alog, dtb, gdn)


def _delta_sample_body(qkv_ref, z_ref, ab_ref, cst_ref, s0_ref, wc_ref, alog_ref, dtb_ref, gdn_ref,
                       o_ref, cnew_ref, snew_ref,
                       xs_ref, w_s, qg_s, u_s, kd_s, vn_s, qs_s, attn_s, egl_s, *, bb, t, nheads):
    dh = DN_HEAD_DIM
    c = bb * t
    halo = SUBLANE
    taps = CONV_WIDTH - 1

    xs_ref[:, 0:halo, :] = jnp.zeros((bb, halo, xs_ref.shape[2]), F32)
    xs_ref[:, halo - taps:halo, :] = cst_ref[...]
    xs_ref[:, halo:halo + t, :] = qkv_ref[...]
    conv = xs_ref[:, halo - taps:halo - taps + t, :] * wc_ref[0:1, :]
    for j in range(1, CONV_WIDTH):
        conv = conv + xs_ref[:, halo - taps + j:halo - taps + j + t, :] * wc_ref[j:j + 1, :]
    cnew_ref[...] = xs_ref[:, halo + t - taps:halo + t, :]
    act = _silu(conv).reshape(c, conv.shape[2])

    masks = _seg_masks(c, t)
    causal_bf = jnp.where(masks[3], 1.0, 0.0).astype(BF16)
    same_bf = jnp.where(masks[2], 1.0, 0.0).astype(BF16)
    gates = _delta_gates(ab_ref[...], alog_ref[...], dtb_ref[...], causal_bf, same_bf, nheads)
    dlog = _decay_logits(gates[0], nheads)
    egl_s[...] = jnp.exp(gates[2])
    us, ws_, qgs, kdecs, attns = _delta_intra(act, nheads, gates, dlog, masks, t, t)
    for h in range(nheads):
        hs = slice(h * dh, (h + 1) * dh)
        u_s[:, hs] = us[h]
        w_s[:, hs] = ws_[h]
        qg_s[:, hs] = qgs[h]
        kd_s[:, hs] = kdecs[h]
        attn_s[h] = attns[h]

    def per_batch(b, carry):
        r = pl.multiple_of(b * t, t)
        rows = pl.ds(r, t)
        heads = range(nheads)
        hsl = [slice(h * dh, (h + 1) * dh) for h in heads]
        ss = [s0_ref[b, h] for h in heads]
        wss = [_mm(jnp.concatenate([w_s[rows, hsl[h]], qg_s[rows, hsl[h]]], axis=0), ss[h]) for h in heads]
        vnews = [u_s[rows, hsl[h]] - wss[h][:t] for h in heads]
        upds = [_mm_tn(kd_s[rows, hsl[h]], vnews[h]) for h in heads]
        egl = egl_s[pl.ds(r, 1), :]
        for h in heads:
            vn_s[rows, hsl[h]] = vnews[h]
            qs_s[rows, hsl[h]] = wss[h][t:]
            snew_ref[b, h] = ss[h] * egl[:, h:h + 1] + upds[h]
        return carry

    lax.fori_loop(0, bb, per_batch, 0)

    gdn = gdn_ref[...]
    for h in range(nheads):
        hs = slice(h * dh, (h + 1) * dh)
        o = qs_s[:, hs] + _mm(attn_s[h], vn_s[:, hs])
        o_ref[:, hs] = _delta_out(o, z_ref[:, hs], gdn).astype(o_ref.dtype)


def _delta_sample(qkvz, uab, conv_state, s0, w_conv, alog, dtb, gdn, *, nb, t, bb):
    nheads = s0.shape[1]
    dh = DN_HEAD_DIM
    mix = nheads * dh
    c = bb * t
    qkvz3 = qkvz.reshape(nb, t, qkvz.shape[1])
    return pl.pallas_call(
        functools.partial(_delta_sample_body, bb=bb, t=t, nheads=nheads),
        grid=(nb // bb,),
        in_specs=[pl.BlockSpec((bb, t, 3 * mix), lambda i: (i, 0, 0)),
                  pl.BlockSpec((c, mix), lambda i: (i, 3)),
                  pl.BlockSpec((c, LANE), lambda i: (i, uab.shape[1] // LANE - 1)),
                  pl.BlockSpec((bb, CONV_WIDTH - 1, 3 * mix), lambda i: (i, 0, 0)),
                  pl.BlockSpec((bb, nheads, dh, dh), lambda i: (i, 0, 0, 0)),
                  pl.BlockSpec((CONV_WIDTH, 3 * mix), lambda i: (0, 0)),
                  pl.BlockSpec((1, LANE), lambda i: (0, 0)),
                  pl.BlockSpec((1, LANE), lambda i: (0, 0)),
                  pl.BlockSpec((1, dh), lambda i: (0, 0))],
        out_specs=[pl.BlockSpec((c, mix), lambda i: (i, 0)),
                   pl.BlockSpec((bb, CONV_WIDTH - 1, 3 * mix), lambda i: (i, 0, 0)),
                   pl.BlockSpec((bb, nheads, dh, dh), lambda i: (i, 0, 0, 0))],
        out_shape=[jax.ShapeDtypeStruct((nb * t, mix), BF16),
                   jax.ShapeDtypeStruct((nb, CONV_WIDTH - 1, 3 * mix), F32),
                   jax.ShapeDtypeStruct((nb, nheads, dh, dh), F32)],
        scratch_shapes=[pltpu.VMEM((bb, SUBLANE + t, 3 * mix), F32)]
                       + [pltpu.VMEM((c, mix), F32) for _ in range(6)]
                       + [pltpu.VMEM((nheads, c, c), F32), pltpu.VMEM((c, LANE), F32)],
        compiler_params=_params("parallel"),
        name="delta_sample",
    )(qkvz3, qkvz, uab, conv_state, s0, w_conv, alog, dtb, gdn)


def _s5_prep_body(lre_ref, lim_ref, ls_ref, bre_ref, bim_ref, wa_ref, abre_ref, abim_ref):
    lre = jnp.minimum(lre_ref[...], -1e-4)
    lim = lim_ref[...]
    dt = jnp.exp(ls_ref[...])
    mag = jnp.exp(lre * dt)
    ang = lim * dt
    abr = mag * jnp.cos(ang)
    abi = mag * jnp.sin(ang)
    nr = abr - 1.0
    ni = abi
    den = lre * lre + lim * lim
    cor = (nr * lre + ni * lim) / den
    coi = (ni * lre - nr * lim) / den
    bre = bre_ref[...]
    bim = bim_ref[...]
    shape2 = bre.shape[1:]
    row = lax.broadcasted_iota(jnp.int32, shape2, 0)
    col = lax.broadcasted_iota(jnp.int32, shape2, 1)
    diag = ((row // S5_GROUP) == (col // S5_STATE))[None]
    half = shape2[1]
    rows = shape2[0]
    wre = jnp.where(diag, cor * bre - coi * bim, 0.0)
    wim = jnp.where(diag, cor * bim + coi * bre, 0.0)
    wa_ref[:, 0:rows, 0:half] = wre.astype(BF16)
    wa_ref[:, 0:rows, half:2 * half] = wim.astype(BF16)
    wa_ref[:, rows:2 * rows, 0:half] = (abr * wre - abi * wim).astype(BF16)
    wa_ref[:, rows:2 * rows, half:2 * half] = (abr * wim + abi * wre).astype(BF16)
    abre_ref[...] = abr
    abim_ref[...] = abi


def _s5_prep(lam_re, lam_im, log_step, b_re, b_im):
    groups, states, gch = b_re.shape
    slabs = groups // S5_SLAB_GROUPS
    cols = S5_SLAB_STATES

    def row_param(p):
        return p.reshape(slabs, 1, cols)

    def tiled_b(b):
        bt = jnp.transpose(b, (0, 2, 1)).reshape(slabs, LANE, states)
        return jnp.tile(bt, (1, 1, S5_SLAB_GROUPS))

    ls = jnp.broadcast_to(log_step[:, None], (groups, states))
    wa, abre, abim = pl.pallas_call(
        _s5_prep_body,
        out_shape=[jax.ShapeDtypeStruct((slabs, 2 * LANE, 2 * cols), BF16),
                   jax.ShapeDtypeStruct((slabs, 1, cols), F32),
                   jax.ShapeDtypeStruct((slabs, 1, cols), F32)],
        compiler_params=pltpu.CompilerParams(vmem_limit_bytes=VMEM_LIMIT),
        name="s5prep",
    )(row_param(lam_re), row_param(lam_im), row_param(ls), tiled_b(b_re), tiled_b(b_im))
    nblk = groups * states // LANE
    return wa, abre.reshape(nblk, 1, LANE), abim.reshape(nblk, 1, LANE)


def _s5_out_weights(c):
    groups, gch, states = c.shape
    slabs = groups // S5_SLAB_GROUPS
    ct = jnp.transpose(c, (0, 2, 1)).reshape(slabs, S5_SLAB_STATES, gch)
    ct = jnp.tile(ct, (1, 1, S5_SLAB_GROUPS))
    row = jnp.arange(S5_SLAB_STATES)[:, None] // states
    col = jnp.arange(LANE)[None, :] // gch
    return jnp.where((row == col)[None], ct, 0.0).astype(BF16)


def _gelu_tanh(x):
    return x * (0.5 * (1.0 + jnp.tanh(math.sqrt(2.0 / math.pi) * (x + 0.044715 * (x * x * x)))))


S5_PIPELINE_PARTS = 4


def _time_major_perms(nb, lc):
    rows = nb * lc
    r = np.arange(rows)[:, None]
    c = np.arange(rows)[None, :]
    t_tm, b_tm = r // nb, r % nb
    to_tm = c == b_tm * lc + t_tm
    to_tm_prev = (c == b_tm * lc + t_tm - 1) & (t_tm % 2 == 1)
    to_bm = c == (r % lc) * nb + r // lc
    return tuple(jnp.asarray(m, BF16) for m in (to_tm, to_tm_prev, to_bm))


def _s5_body(u_ref, h0re_ref, h0im_ref, tm_ref, tmp_ref, bm_ref, wa_ref, abre_ref, abim_ref, wcre_ref, wcim_ref,
             dsk_ref, wglu_ref, gs5_ref, y_ref, hre_ref, him_ref, cre, cim, *xbufs, nb, lc):
    rows = nb * lc
    nblk = cre.shape[0]
    slabs = wa_ref.shape[0]
    per_slab = nblk // slabs
    half = per_slab * LANE
    steps_per_tile = SUBLANE // nb
    assert steps_per_tile in (1, 2) and lc % steps_per_tile == 0

    @pl.when(pl.program_id(1) == 0)
    def _():
        for j in range(nblk):
            if nb < SUBLANE:
                cre[j] = jnp.zeros((SUBLANE, LANE), F32)
                cim[j] = jnp.zeros((SUBLANE, LANE), F32)
            cre[j, SUBLANE - nb:SUBLANE, :] = h0re_ref[:, j * LANE:(j + 1) * LANE]
            cim[j, SUBLANE - nb:SUBLANE, :] = h0im_ref[:, j * LANE:(j + 1) * LANE]

    to_tm, to_bm = tm_ref[...], bm_ref[...]
    pieces = _split3(u_ref[...].reshape(rows, u_ref.shape[2]))
    u = sum(jnp.dot(to_tm, p, preferred_element_type=F32) for p in pieces)
    ub = u.astype(BF16)
    if steps_per_tile == 2:
        ub_prev = jnp.dot(tmp_ref[...], pieces[0], preferred_element_type=F32).astype(BF16)

    upper = lax.broadcasted_iota(jnp.int32, (SUBLANE, LANE), 0) >= nb
    parts = S5_PIPELINE_PARTS if rows % (S5_PIPELINE_PARTS * LANE) == 0 else 1
    prow = rows // parts

    def project_in(s, r):
        rs = slice(r * prow, (r + 1) * prow)
        sl = slice(s * LANE, (s + 1) * LANE)
        if steps_per_tile == 2:
            x = jnp.dot(jnp.concatenate([ub[rs, sl], ub_prev[rs, sl]], axis=1), wa_ref[s],
                        preferred_element_type=F32)
        else:
            x = jnp.dot(ub[rs, sl], wa_ref[s, 0:LANE, :], preferred_element_type=F32)
        for k in range(per_slab):
            xbufs[2 * s][k, rs, :] = x[:, k * LANE:(k + 1) * LANE]
            xbufs[2 * s + 1][k, rs, :] = x[:, half + k * LANE:half + (k + 1) * LANE]

    def recur(s, r, coef, carry):
        xre, xim = xbufs[2 * s], xbufs[2 * s + 1]
        for v in range(r * prow // SUBLANE, (r + 1) * prow // SUBLANE):
            at = slice(v * SUBLANE, (v + 1) * SUBLANE)
            for n in range(per_slab):
                hr, hi = carry[n]
                ar, ai = coef[n]
                if steps_per_tile == 2:
                    hr = jnp.where(upper, hr, pltpu.roll(hr, nb, 0))
                    hi = jnp.where(upper, hi, pltpu.roll(hi, nb, 0))
                nr = ar * hr - ai * hi + xre[n, at, :]
                ni = ar * hi + ai * hr + xim[n, at, :]
                xre[n, at, :] = nr
                xim[n, at, :] = ni
                carry[n] = (nr, ni)

    def project_out(s, r):
        rs = slice(r * prow, (r + 1) * prow)
        hre = jnp.concatenate([xbufs[2 * s][k, rs, :] for k in range(per_slab)], axis=1)
        him = jnp.concatenate([xbufs[2 * s + 1][k, rs, :] for k in range(per_slab)], axis=1)
        return _mm(hre, wcre_ref[s]) - _mm(him, wcim_ref[s])

    ychunks = [[None] * parts for _ in range(slabs)]
    for r in range(parts):
        project_in(0, r)
    for s in range(slabs):
        js = list(range(s * per_slab, (s + 1) * per_slab))
        coef, carry = [], []
        for j in js:
            ar = jnp.broadcast_to(abre_ref[j], (SUBLANE, LANE))
            ai = jnp.broadcast_to(abim_ref[j], (SUBLANE, LANE))
            if steps_per_tile == 2:
                coef.append((jnp.where(upper, ar * ar - ai * ai, ar), jnp.where(upper, 2.0 * ar * ai, ai)))
            else:
                coef.append((ar, ai))
            carry.append((cre[j], cim[j]))
        for r in range(parts):
            recur(s, r, coef, carry)
            if s + 1 < slabs:
                project_in(s + 1, r)
            if s >= 1:
                ychunks[s - 1][r] = project_out(s - 1, r)
        for n, j in enumerate(js):
            cre[j], cim[j] = carry[n]
    for r in range(parts):
        ychunks[slabs - 1][r] = project_out(slabs - 1, r)
    y = jnp.concatenate([jnp.concatenate(yc, axis=0) for yc in ychunks], axis=1) + dsk_ref[...] * u
    ab = _mm(_gelu_tanh(y), wglu_ref[...])
    n = y.shape[1]
    glu = ab[:, :n] * jax.nn.sigmoid(ab[:, n:])
    out = glu * lax.rsqrt(jnp.mean(glu * glu, axis=-1, keepdims=True) + EPS) * gs5_ref[...]
    out = jnp.dot(to_bm, out.astype(BF16), preferred_element_type=F32)
    y_ref[...] = out.reshape(y_ref.shape).astype(y_ref.dtype)

    @pl.when(pl.program_id(1) == pl.num_programs(1) - 1)
    def _():
        for j in range(nblk):
            hre_ref[:, j * LANE:(j + 1) * LANE] = cre[j, SUBLANE - nb:SUBLANE, :]
            him_ref[:, j * LANE:(j + 1) * LANE] = cim[j, SUBLANE - nb:SUBLANE, :]


def _s5(uab, h0re, h0im, wa, abre, abim, wcre, wcim, dskip, wglu, gs5, *, nb_total, t, nb, lc):
    mixb = dskip.shape[1]
    nstate = h0re.shape[1]
    nblk = nstate // LANE
    slabs = wa.shape[0]
    uab3 = uab.reshape(nb_total, t, uab.shape[1])
    perms = _time_major_perms(nb, lc)
    full = lambda a: pl.BlockSpec(a.shape, lambda i, j: (0,) * a.ndim)
    y, hre, him = pl.pallas_call(
        functools.partial(_s5_body, nb=nb, lc=lc),
        grid=(nb_total // nb, t // lc),
        in_specs=[pl.BlockSpec((nb, lc, mixb), lambda i, j: (i, j, 0)),
                  pl.BlockSpec((nb, nstate), lambda i, j: (i, 0)),
                  pl.BlockSpec((nb, nstate), lambda i, j: (i, 0))]
                 + [full(a) for a in perms]
                 + [full(wa), full(abre), full(abim), full(wcre), full(wcim), full(dskip), full(wglu), full(gs5)],
        out_specs=[pl.BlockSpec((nb, lc, mixb), lambda i, j: (i, j, 0)),
                   pl.BlockSpec((nb, nstate), lambda i, j: (i, 0)),
                   pl.BlockSpec((nb, nstate), lambda i, j: (i, 0))],
        out_shape=[jax.ShapeDtypeStruct((nb_total, t, mixb), F32),
                   jax.ShapeDtypeStruct((nb_total, nstate), F32),
                   jax.ShapeDtypeStruct((nb_total, nstate), F32)],
        scratch_shapes=[pltpu.VMEM((nblk, SUBLANE, LANE), F32),
                        pltpu.VMEM((nblk, SUBLANE, LANE), F32)]
                       + [pltpu.VMEM((nblk // slabs, nb * lc, LANE), F32) for _ in range(2 * slabs)],
        compiler_params=_params("parallel", "arbitrary"),
        name="s5",
    )(uab3, h0re, h0im, *perms, wa, abre, abim, wcre, wcim, dskip, wglu, gs5)
    return y.reshape(nb_total * t, mixb), hre, him


def _outproj_body(o_ref, y_ref, wt_ref, wb_ref, x_ref, mod_ref, g_ref, out_ref, *, gate):
    mix = (jnp.dot(o_ref[...], wt_ref[...], preferred_element_type=F32)
           + _mm(y_ref[...], wb_ref[...]))
    n = mix * lax.rsqrt(jnp.mean(mix * mix, axis=-1, keepdims=True) + EPS) * g_ref[...]
    x = x_ref[...]
    out_ref[...] = x + mod_ref[:, gate:gate + 1, :] * n.reshape(x.shape)


def _outproj(o, y5, wtop, wbot, x3, mod, g, *, gate, rows):
    nb, t, d = x3.shape
    bb, tt = _token_blocks(nb, t, rows)
    nt = t // tt
    ka, kb = o.shape[1], y5.shape[1]
    return pl.pallas_call(
        functools.partial(_outproj_body, gate=gate),
        grid=(nb * t // rows,),
        in_specs=[pl.BlockSpec((rows, ka), lambda i: (i, 0)),
                  pl.BlockSpec((rows, kb), lambda i: (i, 0)),
                  pl.BlockSpec((ka, d), lambda i: (0, 0)),
                  pl.BlockSpec((kb, d), lambda i: (0, 0)),
                  pl.BlockSpec((bb, tt, d), lambda i: (i // nt, i % nt, 0)),
                  pl.BlockSpec((bb, 6, d), lambda i: (i // nt, 0, 0)),
                  pl.BlockSpec((1, d), lambda i: (0, 0))],
        out_specs=pl.BlockSpec((bb, tt, d), lambda i: (i // nt, i % nt, 0)),
        out_shape=jax.ShapeDtypeStruct((nb, t, d), F32),
        compiler_params=_params("parallel"),
        name="outproj",
    )(o, y5, wtop, wbot, x3, mod, g)


def _ffn_body(x_ref, mod_ref, gpre_ref, wg_ref, wu_ref, wd_ref, gpost_ref, out_ref, h_ref,
              *, shift, scale, gate):
    j = pl.program_id(1)

    @pl.when(j == 0)
    def _():
        x = x_ref[...]
        bb, tt, d = x.shape
        h = _modulated_norm(x, gpre_ref[...], mod_ref[:, scale:scale + 1, :], mod_ref[:, shift:shift + 1, :])
        h_ref[...] = h.reshape(bb * tt, d).astype(BF16)
        out_ref[...] = jnp.zeros(out_ref.shape, F32)

    h = h_ref[...]
    a = (_silu(jnp.dot(h, wg_ref[...], preferred_element_type=F32))
         * jnp.dot(h, wu_ref[...], preferred_element_type=F32))
    out_ref[...] += _mm(a, wd_ref[...]).reshape(out_ref.shape)

    @pl.when(j == pl.num_programs(1) - 1)
    def _():
        f = out_ref[...]
        n = f * lax.rsqrt(jnp.mean(f * f, axis=-1, keepdims=True) + EPS) * gpost_ref[...]
        out_ref[...] = x_ref[...] + mod_ref[:, gate:gate + 1, :] * n


def _ffn(x3, mod, gpre, wg, wu, wd, gpost, *, shift, scale, gate, rows, tf):
    nb, t, d = x3.shape
    dff = wg.shape[1]
    bb, tt = _token_blocks(nb, t, rows)
    nt = t // tt
    return pl.pallas_call(
        functools.partial(_ffn_body, shift=shift, scale=scale, gate=gate),
        grid=(nb * t // rows, dff // tf),
        in_specs=[pl.BlockSpec((bb, tt, d), lambda i, j: (i // nt, i % nt, 0)),
                  pl.BlockSpec((bb, 6, d), lambda i, j: (i // nt, 0, 0)),
                  pl.BlockSpec((1, d), lambda i, j: (0, 0)),
                  pl.BlockSpec((d, tf), lambda i, j: (0, j)),
                  pl.BlockSpec((d, tf), lambda i, j: (0, j)),
                  pl.BlockSpec((tf, d), lambda i, j: (j, 0)),
                  pl.BlockSpec((1, d), lambda i, j: (0, 0))],
        out_specs=pl.BlockSpec((bb, tt, d), lambda i, j: (i // nt, i % nt, 0)),
        out_shape=jax.ShapeDtypeStruct((nb, t, d), F32),
        scratch_shapes=[pltpu.VMEM((rows, d), BF16)],
        compiler_params=_params("parallel", "arbitrary"),
        name="ffn",
    )(x3, mod, gpre, wg, wu, wd, gpost)


ROWS = 512
FFN_ROWS = 512
FFN_COLS = 512
PROMPT_CHUNK = 128
PROMPT_BASE = 16
SAMPLE_BATCH = 8
S5_PROMPT_STEPS = 128
S5_SAMPLE_BATCH = 8


def _layer_weights(w_ada, b_ada, g_pre_mix, g_post_mix, g_pre_ffn, g_post_ffn,
                   w_in, w_conv, a_log, dt_bias, g_dn_out,
                   lam_re, lam_im, log_step, b_re, b_im, c_re, c_im, d_skip,
                   w_glu, g_s5_out, w_out, w_gate, w_up, w_down):
    d = w_in.shape[0]
    nheads = a_log.shape[0]
    mix_a = nheads * DN_HEAD_DIM
    qkvz = 4 * mix_a
    gates = 2 * nheads
    wq = w_in[:, :qkvz].astype(BF16)
    wuab = jnp.concatenate([w_in[:, qkvz + gates:], w_in[:, qkvz:qkvz + gates],
                            jnp.zeros((d, LANE - gates), w_in.dtype)], axis=1).astype(BF16)
    pad = lambda v: jnp.pad(v, (0, LANE - v.shape[0]))[None, :]
    wa, abre, abim = _s5_prep(lam_re, lam_im, log_step, b_re, b_im)
    return dict(
        w_ada=w_ada, b_ada=b_ada[None, :],
        g_pre_mix=g_pre_mix[None, :], g_post_mix=g_post_mix[None, :],
        g_pre_ffn=g_pre_ffn[None, :], g_post_ffn=g_post_ffn[None, :],
        wq=wq, wuab=wuab, w_conv=w_conv, alog=pad(a_log), dtb=pad(dt_bias), gdn=g_dn_out[None, :],
        wa=wa, abre=abre, abim=abim, wcre=_s5_out_weights(c_re), wcim=_s5_out_weights(c_im),
        dskip=d_skip.reshape(1, -1), wglu=w_glu.astype(BF16), gs5=g_s5_out[None, :],
        wtop=w_out[:mix_a].astype(BF16), wbot=w_out[mix_a:].astype(BF16),
        wg=w_gate.astype(BF16), wu=w_up.astype(BF16), wd=w_down.astype(BF16),
    )


def _layer_group(x3, mod, conv_state, s0, h0re, h0im, lw, *, prompt):
    nb, t, d = x3.shape
    rows = min(ROWS, nb * t)
    qkvz = _nmm(x3, mod, lw["g_pre_mix"], lw["wq"], shift=0, scale=1, rows=rows)
    uab = _nmm(x3, mod, lw["g_pre_mix"], lw["wuab"], shift=0, scale=1, rows=rows)
    if prompt:
        o, cnew, snew = _delta_prompt(qkvz, uab, conv_state, s0, lw["w_conv"], lw["alog"], lw["dtb"], lw["gdn"],
                                      nb=nb, t=t, c=PROMPT_CHUNK, base=PROMPT_BASE)
        y5, hre, him = _s5(uab, h0re, h0im, lw["wa"], lw["abre"], lw["abim"], lw["wcre"], lw["wcim"],
                           lw["dskip"], lw["wglu"], lw["gs5"], nb_total=nb, t=t, nb=nb, lc=S5_PROMPT_STEPS)
    else:
        o, cnew, snew = _delta_sample(qkvz, uab, conv_state, s0, lw["w_conv"], lw["alog"], lw["dtb"], lw["gdn"],
                                      nb=nb, t=t, bb=SAMPLE_BATCH)
        y5, hre, him = _s5(uab, h0re, h0im, lw["wa"], lw["abre"], lw["abim"], lw["wcre"], lw["wcim"],
                           lw["dskip"], lw["wglu"], lw["gs5"], nb_total=nb, t=t, nb=S5_SAMPLE_BATCH, lc=t)
    x1 = _outproj(o, y5, lw["wtop"], lw["wbot"], x3, mod, lw["g_post_mix"], gate=2, rows=rows)
    x2 = _ffn(x1, mod, lw["g_pre_ffn"], lw["wg"], lw["wu"], lw["wd"], lw["g_post_ffn"],
              shift=3, scale=4, gate=5, rows=min(FFN_ROWS, t) if t >= rows else rows, tf=FFN_COLS)
    ngroups = h0re.shape[1] // S5_STATE
    return x2, cnew, snew, hre.reshape(nb, ngroups, S5_STATE), him.reshape(nb, ngroups, S5_STATE)


def kernel(x_prompt, x_sample, c_prompt, c_sample, state_conv, state_delta, state_ssm_re, state_ssm_im, w_ada, b_ada, g_pre_mix, g_post_mix, g_pre_ffn, g_post_ffn, w_in, w_conv, a_log, dt_bias, g_dn_out, lam_re, lam_im, log_step, b_re, b_im, c_re, c_im, d_skip, w_glu, g_s5_out, w_out, w_gate, w_up, w_down):
    weights = (w_ada, b_ada, g_pre_mix, g_post_mix, g_pre_ffn, g_post_ffn,
               w_in, w_conv, a_log, dt_bias, g_dn_out,
               lam_re, lam_im, log_step, b_re, b_im, c_re, c_im, d_skip,
               w_glu, g_s5_out, w_out, w_gate, w_up, w_down)
    depth = w_ada.shape[0]
    bp, d = c_prompt.shape
    bs = c_sample.shape[0]
    dt_ = x_prompt.dtype
    nheads = state_delta.shape[2]
    nstate = state_ssm_re.shape[2] * state_ssm_re.shape[3]
    crows = -(-(bp + bs) // SUBLANE) * SUBLANE
    c_all = jnp.concatenate([c_prompt, c_sample, jnp.zeros((crows - bp - bs, d), dt_)], axis=0)
    yp, ys = x_prompt, x_sample
    outs = [[] for _ in range(8)]
    for l in range(depth):
        lw = _layer_weights(*[w[l] for w in weights])
        mod = _ada(c_all, lw["w_ada"], lw["b_ada"]).reshape(crows, 6, d)
        zc = jnp.zeros((bp, CONV_WIDTH - 1, state_conv.shape[3]), dt_)
        zd = jnp.zeros((bp,) + state_delta.shape[2:], dt_)
        zs = jnp.zeros((bp, nstate), dt_)
        yp, c1, d1, r1, i1 = _layer_group(yp, mod[:bp], zc, zd, zs, zs, lw, prompt=True)
        ys, c2, d2, r2, i2 = _layer_group(ys, mod[bp:bp + bs], state_conv[l], state_delta[l],
                                          state_ssm_re[l].reshape(bs, nstate), state_ssm_im[l].reshape(bs, nstate),
                                          lw, prompt=False)
        for lst, v in zip(outs, (c1, d1, r1, i1, c2, d2, r2, i2)):
            lst.append(v)
    stacked = [jnp.stack(v) for v in outs]
    return (yp, ys, *stacked)
```

```python
import functools
import math

import jax
import jax.numpy as jnp
import numpy as np
from jax import lax
from jax.experimental import pallas as pl
from jax.experimental.pallas import tpu as pltpu

F32 = jnp.float32
BF16 = jnp.bfloat16
EPS = 1e-6
LANE = 128
SUBLANE = 8
V7X_VMEM_BYTES = 64 << 20
VMEM_LIMIT = V7X_VMEM_BYTES - (8 << 20)

DN_HEAD_DIM = 128
CONV_WIDTH = 4
S5_GROUP = 16
S5_STATE = 64
S5_SLAB_GROUPS = LANE // S5_GROUP
S5_SLAB_STATES = S5_SLAB_GROUPS * S5_STATE


def _params(*sem):
    return pltpu.CompilerParams(dimension_semantics=sem, vmem_limit_bytes=VMEM_LIMIT)


def _silu(x):
    return x * jax.nn.sigmoid(x)


def _mm(a, b):
    return jnp.dot(a.astype(BF16), b.astype(BF16), preferred_element_type=F32)


def _mm_nt(a, b):
    return lax.dot_general(a.astype(BF16), b.astype(BF16), (((1,), (1,)), ((), ())),
                           preferred_element_type=F32)


def _mm_tn(a, b):
    return lax.dot_general(a.astype(BF16), b.astype(BF16), (((0,), (0,)), ((), ())),
                           preferred_element_type=F32)


def _split3(x):
    x1 = x.astype(BF16)
    r1 = x - x1.astype(F32)
    x2 = r1.astype(BF16)
    x3 = (r1 - x2.astype(F32)).astype(BF16)
    return x1, x2, x3


def _ada_body(c_ref, w_ref, b_ref, o_ref):
    s = _silu(c_ref[...])
    o_ref[...] = _mm(s, w_ref[...]) + b_ref[...]


def _ada(c, w, b):
    rows, d = c.shape
    n = w.shape[1]
    tn = 1024
    return pl.pallas_call(
        _ada_body,
        grid=(n // tn,),
        in_specs=[pl.BlockSpec((rows, d), lambda j: (0, 0)),
                  pl.BlockSpec((d, tn), lambda j: (0, j)),
                  pl.BlockSpec((1, tn), lambda j: (0, j))],
        out_specs=pl.BlockSpec((rows, tn), lambda j: (0, j)),
        out_shape=jax.ShapeDtypeStruct((rows, n), F32),
        compiler_params=_params("arbitrary"),
        name="ada",
    )(c, w, b)


def _token_blocks(nb, t, rows):
    if t >= rows:
        assert t % rows == 0
        return 1, rows
    assert rows % t == 0 and nb % (rows // t) == 0
    return rows // t, t


def _modulated_norm(x, g, scale, shift):
    y = x * lax.rsqrt(jnp.mean(x * x, axis=-1, keepdims=True) + EPS) * g
    return y * (1.0 + scale) + shift


def _win_prep_body(w_ref, o_ref, *, main, gates):
    w = w_ref[...]
    rows, n = w.shape
    rest = n - main - gates
    o_ref[:, 0:main] = w[:, 0:main].astype(BF16)
    o_ref[:, main:main + rest] = w[:, main + gates:n].astype(BF16)
    tail = jnp.concatenate([w[:, main:main + gates], jnp.zeros((rows, LANE - gates), w.dtype)], axis=1)
    o_ref[:, main + rest:main + rest + LANE] = tail.astype(BF16)


def _win_prep(w_in, main, gates):
    d, n = w_in.shape
    rb = 256
    nout = n - gates + LANE
    return pl.pallas_call(
        functools.partial(_win_prep_body, main=main, gates=gates),
        grid=(d // rb,),
        in_specs=[pl.BlockSpec((rb, n), lambda i: (i, 0))],
        out_specs=pl.BlockSpec((rb, nout), lambda i: (i, 0)),
        out_shape=jax.ShapeDtypeStruct((d, nout), BF16),
        compiler_params=_params("parallel"),
        name="winprep",
    )(w_in)


def _nmm_body(x_ref, mod_ref, g_ref, w_ref, o_ref, *, shift, scale):
    x = x_ref[...]
    bb, tt, d = x.shape
    h = _modulated_norm(x, g_ref[...], mod_ref[:, scale:scale + 1, :], mod_ref[:, shift:shift + 1, :])
    o_ref[...] = jnp.dot(h.reshape(bb * tt, d).astype(BF16), w_ref[...], preferred_element_type=F32)


def _nmm(x3, mod, g, w, *, shift, scale, rows):
    nb, t, d = x3.shape
    n = w.shape[1]
    bb, tt = _token_blocks(nb, t, rows)
    nt = t // tt
    return pl.pallas_call(
        functools.partial(_nmm_body, shift=shift, scale=scale),
        grid=(nb * t // rows,),
        in_specs=[pl.BlockSpec((bb, tt, d), lambda i: (i // nt, i % nt, 0)),
                  pl.BlockSpec((bb, 6, d), lambda i: (i // nt, 0, 0)),
                  pl.BlockSpec((1, d), lambda i: (0, 0)),
                  pl.BlockSpec((d, n), lambda i: (0, 0), pipeline_mode=pl.Buffered(1))],
        out_specs=pl.BlockSpec((rows, n), lambda i: (i, 0)),
        out_shape=jax.ShapeDtypeStruct((nb * t, n), F32),
        compiler_params=_params("parallel"),
        name="nmm",
    )(x3, mod, g, w)


def _seg_masks(c, seg):
    row = lax.broadcasted_iota(jnp.int32, (c, c), 0)
    col = lax.broadcasted_iota(jnp.int32, (c, c), 1)
    sh = int(math.log2(seg))
    same = (row >> sh) == (col >> sh)
    return row, col, same, same & (row >= col), same & (row > col)


def _tri_inverse_offdiag(lmats, row, col, base, seg):
    bsh = int(math.log2(base))
    blk = (row >> bsh) == (col >> bsh)
    l0s = [jnp.where(blk, l, 0.0) for l in lmats]
    pts = [-l for l in l0s]
    qs = [_mm(l, l) for l in l0s]
    k = 2
    while k < base:
        pts = [p + q + _mm(p, q) for p, q in zip(pts, qs)]
        k *= 2
        if k < base:
            qs = [_mm(q, q) for q in qs]
    m = base
    while m < seg:
        msh = int(math.log2(m))
        sel = ((row >> (msh + 1)) == (col >> (msh + 1))) & ((row >> msh) != (col >> msh))
        es = [jnp.where(sel, l, 0.0) for l in lmats]
        xs = [e + _mm(p, e) for p, e in zip(pts, es)]
        pts = [p - (x + _mm(x, p)) for p, x in zip(pts, xs)]
        m *= 2
    return pts


def _delta_gates(ab, alog, dtb, causal_bf, same_bf, nheads):
    lane = lax.broadcasted_iota(jnp.int32, ab.shape, 1)
    x = ab + dtb
    softplus = jnp.maximum(x, 0.0) + jnp.log1p(jnp.exp(-jnp.abs(x)))
    g = jnp.where(lane < nheads, -jnp.exp(alog) * softplus, 0.0)
    beta = jax.nn.sigmoid(ab)
    pieces = _split3(g)
    gcum = sum(jnp.dot(causal_bf, p, preferred_element_type=F32) for p in pieces)
    glast = sum(jnp.dot(same_bf, p, preferred_element_type=F32) for p in pieces)
    return gcum, beta, glast


def _decay_logits(gcum, nheads):
    c = gcum.shape[0]
    lane = lax.broadcasted_iota(jnp.int32, (c, LANE), 1)
    ones = jnp.ones((c, LANE), BF16)
    pad = jnp.zeros((LANE - c, 2 * LANE), BF16) if c < LANE else None
    out = None
    for p in _split3(gcum):
        lhs = jnp.concatenate([p, ones], axis=1)
        blocks = []
        for h in range(nheads):
            onehot = lane == h
            blk = jnp.concatenate([jnp.where(onehot, 1.0, 0.0).astype(BF16),
                                   jnp.where(onehot, -p, jnp.zeros_like(p))], axis=1)
            blocks.append(blk)
            if pad is not None:
                blocks.append(pad)
        rhs = jnp.concatenate(blocks, axis=0)
        d = lax.dot_general(lhs, rhs, (((1,), (1,)), ((), ())), preferred_element_type=F32)
        out = d if out is None else out + d
    return out


def _delta_intra(act, nheads, gates, dlog, masks, base, seg):
    gcum, beta, glast = gates
    row, col, _, causal, strict = masks
    c = act.shape[0]
    dh = DN_HEAD_DIM
    eg_all = jnp.exp(gcum)
    ed_all = jnp.exp(glast - gcum)
    ks, kqs_lhs, rhss, qgs, kdecs = [], [], [], [], []
    for h in range(nheads):
        q = act[:, h * dh:(h + 1) * dh]
        k = act[:, (nheads + h) * dh:(nheads + h + 1) * dh]
        v = act[:, (2 * nheads + h) * dh:(2 * nheads + h + 1) * dh]
        q = q * lax.rsqrt(jnp.sum(q * q, axis=-1, keepdims=True) + 1e-6) * (dh ** -0.5)
        k = k * lax.rsqrt(jnp.sum(k * k, axis=-1, keepdims=True) + 1e-6)
        b = beta[:, nheads + h:nheads + h + 1]
        eg = eg_all[:, h:h + 1]
        kb = k * b
        ks.append(k)
        kqs_lhs.append(jnp.concatenate([kb, q], axis=0))
        rhss.append(jnp.concatenate([v * b, kb * eg], axis=1))
        qgs.append(q * eg)
        kdecs.append(k * ed_all[:, h:h + 1])
    kqs = [_mm_nt(a, k) for a, k in zip(kqs_lhs, ks)]
    lmats, attns = [], []
    for h in range(nheads):
        dl = dlog[:, h * LANE:h * LANE + c]
        dec = jnp.where(causal, jnp.exp(jnp.where(causal, dl, 0.0)), 0.0)
        lmats.append(jnp.where(strict, kqs[h][:c] * dec, 0.0))
        attns.append(kqs[h][c:] * dec)
    tts = _tri_inverse_offdiag(lmats, row, col, base, seg)
    uws = [r + _mm(t, r) for t, r in zip(tts, rhss)]
    return [uw[:, :dh] for uw in uws], [uw[:, dh:] for uw in uws], qgs, kdecs, attns


def _delta_out(o, z, gdn):
    on = o * lax.rsqrt(jnp.mean(o * o, axis=-1, keepdims=True) + EPS) * gdn
    return on * _silu(z)


def _delta_prompt_body(qkv_ref, z_ref, ab_ref, cst_ref, s0_ref, wc_ref, alog_ref, dtb_ref, gdn_ref,
                       o_ref, cnew_ref, snew_ref, xc_ref, s_ref, *, c, nheads, base):
    dh = DN_HEAD_DIM
    halo = SUBLANE
    taps = CONV_WIDTH - 1

    @pl.when(pl.program_id(1) == 0)
    def _():
        xc_ref[0:halo, :] = jnp.zeros((halo, xc_ref.shape[1]), F32)
        xc_ref[halo - taps:halo, :] = cst_ref[0]
        s_ref[...] = s0_ref[0]

    xc_ref[halo:halo + c, :] = qkv_ref[...]
    conv = xc_ref[halo - taps:halo - taps + c, :] * wc_ref[0:1, :]
    for j in range(1, CONV_WIDTH):
        conv = conv + xc_ref[halo - taps + j:halo - taps + j + c, :] * wc_ref[j:j + 1, :]
    cnew_ref[0] = xc_ref[halo + c - taps:halo + c, :]
    xc_ref[0:halo, :] = xc_ref[c:c + halo, :]
    act = _silu(conv)

    masks = _seg_masks(c, c)
    causal_bf = jnp.where(masks[3], 1.0, 0.0).astype(BF16)
    same_bf = jnp.ones((c, c), BF16)
    gates = _delta_gates(ab_ref[...], alog_ref[...], dtb_ref[...], causal_bf, same_bf, nheads)
    dlog = _decay_logits(gates[0], nheads)
    eglast = jnp.exp(gates[2][0:1, :])
    gdn = gdn_ref[...]
    us, ws_, qgs, kdecs, attns = _delta_intra(act, nheads, gates, dlog, masks, base, c)
    heads = range(nheads)
    ss = [s_ref[h] for h in heads]
    wss = [_mm(jnp.concatenate([ws_[h], qgs[h]], axis=0), ss[h]) for h in heads]
    vnews = [us[h] - wss[h][:c] for h in heads]
    os_ = [wss[h][c:] + _mm(attns[h], vnews[h]) for h in heads]
    snews = [ss[h] * eglast[:, h:h + 1] + _mm_tn(kdecs[h], vnews[h]) for h in heads]
    for h in heads:
        s_ref[h] = snews[h]
        o_ref[:, h * dh:(h + 1) * dh] = _delta_out(os_[h], z_ref[:, h * dh:(h + 1) * dh], gdn).astype(o_ref.dtype)

    @pl.when(pl.program_id(1) == pl.num_programs(1) - 1)
    def _():
        snew_ref[0] = s_ref[...]


def _delta_prompt(qkvz, uab, conv_state, s0, w_conv, alog, dtb, gdn, *, nb, t, c, base):
    nheads = s0.shape[1]
    dh = DN_HEAD_DIM
    mix = nheads * dh
    nc = t // c
    return pl.pallas_call(
        functools.partial(_delta_prompt_body, c=c, nheads=nheads, base=base),
        grid=(nb, nc),
        in_specs=[pl.BlockSpec((c, 3 * mix), lambda b, i: (b * nc + i, 0)),
                  pl.BlockSpec((c, mix), lambda b, i: (b * nc + i, 3)),
                  pl.BlockSpec((c, LANE), lambda b, i: (b * nc + i, uab.shape[1] // LANE - 1)),
                  pl.BlockSpec((1, CONV_WIDTH - 1, 3 * mix), lambda b, i: (b, 0, 0)),
                  pl.BlockSpec((1, nheads, dh, dh), lambda b, i: (b, 0, 0, 0)),
                  pl.BlockSpec((CONV_WIDTH, 3 * mix), lambda b, i: (0, 0)),
                  pl.BlockSpec((1, LANE), lambda b, i: (0, 0)),
                  pl.BlockSpec((1, LANE), lambda b, i: (0, 0)),
                  pl.BlockSpec((1, dh), lambda b, i: (0, 0))],
        out_specs=[pl.BlockSpec((c, mix), lambda b, i: (b * nc + i, 0)),
                   pl.BlockSpec((1, CONV_WIDTH - 1, 3 * mix), lambda b, i: (b, 0, 0)),
                   pl.BlockSpec((1, nheads, dh, dh), lambda b, i: (b, 0, 0, 0))],
        out_shape=[jax.ShapeDtypeStruct((nb * t, mix), BF16),
                   jax.ShapeDtypeStruct((nb, CONV_WIDTH - 1, 3 * mix), F32),
                   jax.ShapeDtypeStruct((nb, nheads, dh, dh), F32)],
        scratch_shapes=[pltpu.VMEM((c + SUBLANE, 3 * mix), F32),
                        pltpu.VMEM((nheads, dh, dh), F32)],
        compiler_params=_params("parallel", "arbitrary"),
        name="delta_prompt",
    )(qkvz, qkvz, uab, conv_state, s0, w_conv, alog, dtb, gdn)


def _delta_sample_body(qkv_ref, z_ref, ab_ref, cst_ref, s0_ref, wc_ref, alog_ref, dtb_ref, gdn_ref,
                       o_ref, cnew_ref, snew_ref,
                       xs_ref, w_s, qg_s, u_s, kd_s, vn_s, qs_s, attn_s, egl_s, *, bb, t, nheads):
    dh = DN_HEAD_DIM
    c = bb * t
    halo = SUBLANE
    taps = CONV_WIDTH - 1

    xs_ref[:, 0:halo, :] = jnp.zeros((bb, halo, xs_ref.shape[2]), F32)
    xs_ref[:, halo - taps:halo, :] = cst_ref[...]
    xs_ref[:, halo:halo + t, :] = qkv_ref[...]
    conv = xs_ref[:, halo - taps:halo - taps + t, :] * wc_ref[0:1, :]
    for j in range(1, CONV_WIDTH):
        conv = conv + xs_ref[:, halo - taps + j:halo - taps + j + t, :] * wc_ref[j:j + 1, :]
    cnew_ref[...] = xs_ref[:, halo + t - taps:halo + t, :]
    act = _silu(conv).reshape(c, conv.shape[2])

    masks = _seg_masks(c, t)
    causal_bf = jnp.where(masks[3], 1.0, 0.0).astype(BF16)
    same_bf = jnp.where(masks[2], 1.0, 0.0).astype(BF16)
    gates = _delta_gates(ab_ref[...], alog_ref[...], dtb_ref[...], causal_bf, same_bf, nheads)
    dlog = _decay_logits(gates[0], nheads)
    egl_s[...] = jnp.exp(gates[2])
    us, ws_, qgs, kdecs, attns = _delta_intra(act, nheads, gates, dlog, masks, t, t)
    for h in range(nheads):
        hs = slice(h * dh, (h + 1) * dh)
        u_s[:, hs] = us[h]
        w_s[:, hs] = ws_[h]
        qg_s[:, hs] = qgs[h]
        kd_s[:, hs] = kdecs[h]
        attn_s[h] = attns[h]

    def per_batch(b, carry):
        r = pl.multiple_of(b * t, t)
        rows = pl.ds(r, t)
        heads = range(nheads)
        hsl = [slice(h * dh, (h + 1) * dh) for h in heads]
        ss = [s0_ref[b, h] for h in heads]
        wss = [_mm(jnp.concatenate([w_s[rows, hsl[h]], qg_s[rows, hsl[h]]], axis=0), ss[h]) for h in heads]
        vnews = [u_s[rows, hsl[h]] - wss[h][:t] for h in heads]
        upds = [_mm_tn(kd_s[rows, hsl[h]], vnews[h]) for h in heads]
        egl = egl_s[pl.ds(r, 1), :]
        for h in heads:
            vn_s[rows, hsl[h]] = vnews[h]
            qs_s[rows, hsl[h]] = wss[h][t:]
            snew_ref[b, h] = ss[h] * egl[:, h:h + 1] + upds[h]
        return carry

    lax.fori_loop(0, bb, per_batch, 0)

    gdn = gdn_ref[...]
    for h in range(nheads):
        hs = slice(h * dh, (h + 1) * dh)
        o = qs_s[:, hs] + _mm(attn_s[h], vn_s[:, hs])
        o_ref[:, hs] = _delta_out(o, z_ref[:, hs], gdn).astype(o_ref.dtype)


def _delta_sample(qkvz, uab, conv_state, s0, w_conv, alog, dtb, gdn, *, nb, t, bb):
    nheads = s0.shape[1]
    dh = DN_HEAD_DIM
    mix = nheads * dh
    c = bb * t
    qkvz3 = qkvz.reshape(nb, t, qkvz.shape[1])
    return pl.pallas_call(
        functools.partial(_delta_sample_body, bb=bb, t=t, nheads=nheads),
        grid=(nb // bb,),
        in_specs=[pl.BlockSpec((bb, t, 3 * mix), lambda i: (i, 0, 0)),
                  pl.BlockSpec((c, mix), lambda i: (i, 3)),
                  pl.BlockSpec((c, LANE), lambda i: (i, uab.shape[1] // LANE - 1)),
                  pl.BlockSpec((bb, CONV_WIDTH - 1, 3 * mix), lambda i: (i, 0, 0)),
                  pl.BlockSpec((bb, nheads, dh, dh), lambda i: (i, 0, 0, 0)),
                  pl.BlockSpec((CONV_WIDTH, 3 * mix), lambda i: (0, 0)),
                  pl.BlockSpec((1, LANE), lambda i: (0, 0)),
                  pl.BlockSpec((1, LANE), lambda i: (0, 0)),
                  pl.BlockSpec((1, dh), lambda i: (0, 0))],
        out_specs=[pl.BlockSpec((c, mix), lambda i: (i, 0)),
                   pl.BlockSpec((bb, CONV_WIDTH - 1, 3 * mix), lambda i: (i, 0, 0)),
                   pl.BlockSpec((bb, nheads, dh, dh), lambda i: (i, 0, 0, 0))],
        out_shape=[jax.ShapeDtypeStruct((nb * t, mix), BF16),
                   jax.ShapeDtypeStruct((nb, CONV_WIDTH - 1, 3 * mix), F32),
                   jax.ShapeDtypeStruct((nb, nheads, dh, dh), F32)],
        scratch_shapes=[pltpu.VMEM((bb, SUBLANE + t, 3 * mix), F32)]
                       + [pltpu.VMEM((c, mix), F32) for _ in range(6)]
                       + [pltpu.VMEM((nheads, c, c), F32), pltpu.VMEM((c, LANE), F32)],
        compiler_params=_params("parallel"),
        name="delta_sample",
    )(qkvz3, qkvz, uab, conv_state, s0, w_conv, alog, dtb, gdn)


def _s5_prep_body(lre_ref, lim_ref, ls_ref, bre_ref, bim_ref, wa_ref, abre_ref, abim_ref):
    lre = jnp.minimum(lre_ref[...], -1e-4)
    lim = lim_ref[...]
    dt = jnp.exp(ls_ref[...])
    mag = jnp.exp(lre * dt)
    ang = lim * dt
    abr = mag * jnp.cos(ang)
    abi = mag * jnp.sin(ang)
    nr = abr - 1.0
    ni = abi
    den = lre * lre + lim * lim
    cor = (nr * lre + ni * lim) / den
    coi = (ni * lre - nr * lim) / den
    bre = bre_ref[...]
    bim = bim_ref[...]
    shape2 = bre.shape[1:]
    row = lax.broadcasted_iota(jnp.int32, shape2, 0)
    col = lax.broadcasted_iota(jnp.int32, shape2, 1)
    diag = ((row // S5_GROUP) == (col // S5_STATE))[None]
    half = shape2[1]
    rows = shape2[0]
    wre = jnp.where(diag, cor * bre - coi * bim, 0.0)
    wim = jnp.where(diag, cor * bim + coi * bre, 0.0)
    wa_ref[:, 0:rows, 0:half] = wre.astype(BF16)
    wa_ref[:, 0:rows, half:2 * half] = wim.astype(BF16)
    wa_ref[:, rows:2 * rows, 0:half] = (abr * wre - abi * wim).astype(BF16)
    wa_ref[:, rows:2 * rows, half:2 * half] = (abr * wim + abi * wre).astype(BF16)
    abre_ref[...] = abr
    abim_ref[...] = abi


def _s5_prep(lam_re, lam_im, log_step, b_re, b_im):
    groups, states, gch = b_re.shape
    slabs = groups // S5_SLAB_GROUPS
    cols = S5_SLAB_STATES

    def row_param(p):
        return p.reshape(slabs, 1, cols)

    def tiled_b(b):
        bt = jnp.transpose(b, (0, 2, 1)).reshape(slabs, LANE, states)
        return jnp.tile(bt, (1, 1, S5_SLAB_GROUPS))

    ls = jnp.broadcast_to(log_step[:, None], (groups, states))
    wa, abre, abim = pl.pallas_call(
        _s5_prep_body,
        out_shape=[jax.ShapeDtypeStruct((slabs, 2 * LANE, 2 * cols), BF16),
                   jax.ShapeDtypeStruct((slabs, 1, cols), F32),
                   jax.ShapeDtypeStruct((slabs, 1, cols), F32)],
        compiler_params=pltpu.CompilerParams(vmem_limit_bytes=VMEM_LIMIT),
        name="s5prep",
    )(row_param(lam_re), row_param(lam_im), row_param(ls), tiled_b(b_re), tiled_b(b_im))
    nblk = groups * states // LANE
    return wa, abre.reshape(nblk, 1, LANE), abim.reshape(nblk, 1, LANE)


def _s5_out_weights(c):
    groups, gch, states = c.shape
    slabs = groups // S5_SLAB_GROUPS
    ct = jnp.transpose(c, (0, 2, 1)).reshape(slabs, S5_SLAB_STATES, gch)
    ct = jnp.tile(ct, (1, 1, S5_SLAB_GROUPS))
    row = jnp.arange(S5_SLAB_STATES)[:, None] // states
    col = jnp.arange(LANE)[None, :] // gch
    return jnp.where((row == col)[None], ct, 0.0).astype(BF16)


def _gelu_tanh(x):
    return x * (0.5 * (1.0 + jnp.tanh(math.sqrt(2.0 / math.pi) * (x + 0.044715 * (x * x * x)))))


S5_PIPELINE_PARTS = 4


def _time_major_perms(nb, lc):
    rows = nb * lc
    r = np.arange(rows)[:, None]
    c = np.arange(rows)[None, :]
    t_tm, b_tm = r // nb, r % nb
    to_tm = c == b_tm * lc + t_tm
    to_tm_prev = (c == b_tm * lc + t_tm - 1) & (t_tm % 2 == 1)
    to_bm = c == (r % lc) * nb + r // lc
    return tuple(jnp.asarray(m, BF16) for m in (to_tm, to_tm_prev, to_bm))


def _s5_body(u_ref, h0re_ref, h0im_ref, tm_ref, tmp_ref, bm_ref, wa_ref, abre_ref, abim_ref, wcre_ref, wcim_ref,
             dsk_ref, wglu_ref, gs5_ref, y_ref, hre_ref, him_ref, cre, cim, *xbufs, nb, lc):
    rows = nb * lc
    nblk = cre.shape[0]
    slabs = wa_ref.shape[0]
    per_slab = nblk // slabs
    half = per_slab * LANE
    steps_per_tile = SUBLANE // nb
    assert steps_per_tile in (1, 2) and lc % steps_per_tile == 0

    @pl.when(pl.program_id(1) == 0)
    def _():
        for j in range(nblk):
            if nb < SUBLANE:
                cre[j] = jnp.zeros((SUBLANE, LANE), F32)
                cim[j] = jnp.zeros((SUBLANE, LANE), F32)
            cre[j, SUBLANE - nb:SUBLANE, :] = h0re_ref[:, j * LANE:(j + 1) * LANE]
            cim[j, SUBLANE - nb:SUBLANE, :] = h0im_ref[:, j * LANE:(j + 1) * LANE]

    to_tm, to_bm = tm_ref[...], bm_ref[...]
    pieces = _split3(u_ref[...].reshape(rows, u_ref.shape[2]))
    u = sum(jnp.dot(to_tm, p, preferred_element_type=F32) for p in pieces)
    ub = u.astype(BF16)
    if steps_per_tile == 2:
        ub_prev = jnp.dot(tmp_ref[...], pieces[0], preferred_element_type=F32).astype(BF16)

    upper = lax.broadcasted_iota(jnp.int32, (SUBLANE, LANE), 0) >= nb
    parts = S5_PIPELINE_PARTS if rows % (S5_PIPELINE_PARTS * LANE) == 0 else 1
    prow = rows // parts

    def project_in(s, r):
        rs = slice(r * prow, (r + 1) * prow)
        sl = slice(s * LANE, (s + 1) * LANE)
        if steps_per_tile == 2:
            x = jnp.dot(jnp.concatenate([ub[rs, sl], ub_prev[rs, sl]], axis=1), wa_ref[s],
                        preferred_element_type=F32)
        else:
            x = jnp.dot(ub[rs, sl], wa_ref[s, 0:LANE, :], preferred_element_type=F32)
        for k in range(per_slab):
            xbufs[2 * s][k, rs, :] = x[:, k * LANE:(k + 1) * LANE]
            xbufs[2 * s + 1][k, rs, :] = x[:, half + k * LANE:half + (k + 1) * LANE]

    def recur(s, r, coef, carry):
        xre, xim = xbufs[2 * s], xbufs[2 * s + 1]
        for v in range(r * prow // SUBLANE, (r + 1) * prow // SUBLANE):
            at = slice(v * SUBLANE, (v + 1) * SUBLANE)
            for n in range(per_slab):
                hr, hi = carry[n]
                ar, ai = coef[n]
                if steps_per_tile == 2:
                    hr = jnp.where(upper, hr, pltpu.roll(hr, nb, 0))
                    hi = jnp.where(upper, hi, pltpu.roll(hi, nb, 0))
                nr = ar * hr - ai * hi + xre[n, at, :]
                ni = ar * hi + ai * hr + xim[n, at, :]
                xre[n, at, :] = nr
                xim[n, at, :] = ni
                carry[n] = (nr, ni)

    def project_out(s, r):
        rs = slice(r * prow, (r + 1) * prow)
        hre = jnp.concatenate([xbufs[2 * s][k, rs, :] for k in range(per_slab)], axis=1)
        him = jnp.concatenate([xbufs[2 * s + 1][k, rs, :] for k in range(per_slab)], axis=1)
        return _mm(hre, wcre_ref[s]) - _mm(him, wcim_ref[s])

    ychunks = [[None] * parts for _ in range(slabs)]
    for r in range(parts):
        project_in(0, r)
    for s in range(slabs):
        js = list(range(s * per_slab, (s + 1) * per_slab))
        coef, carry = [], []
        for j in js:
            ar = jnp.broadcast_to(abre_ref[j], (SUBLANE, LANE))
            ai = jnp.broadcast_to(abim_ref[j], (SUBLANE, LANE))
            if steps_per_tile == 2:
                coef.append((jnp.where(upper, ar * ar - ai * ai, ar), jnp.where(upper, 2.0 * ar * ai, ai)))
            else:
                coef.append((ar, ai))
            carry.append((cre[j], cim[j]))
        for r in range(parts):
            recur(s, r, coef, carry)
            if s + 1 < slabs:
                project_in(s + 1, r)
            if s >= 1:
                ychunks[s - 1][r] = project_out(s - 1, r)
        for n, j in enumerate(js):
            cre[j], cim[j] = carry[n]
    for r in range(parts):
        ychunks[slabs - 1][r] = project_out(slabs - 1, r)
    y = jnp.concatenate([jnp.concatenate(yc, axis=0) for yc in ychunks], axis=1) + dsk_ref[...] * u
    ab = _mm(_gelu_tanh(y), wglu_ref[...])
    n = y.shape[1]
    glu = ab[:, :n] * jax.nn.sigmoid(ab[:, n:])
    out = glu * lax.rsqrt(jnp.mean(glu * glu, axis=-1, keepdims=True) + EPS) * gs5_ref[...]
    out = jnp.dot(to_bm, out.astype(BF16), preferred_element_type=F32)
    y_ref[...] = out.reshape(y_ref.shape).astype(y_ref.dtype)

    @pl.when(pl.program_id(1) == pl.num_programs(1) - 1)
    def _():
        for j in range(nblk):
            hre_ref[:, j * LANE:(j + 1) * LANE] = cre[j, SUBLANE - nb:SUBLANE, :]
            him_ref[:, j * LANE:(j + 1) * LANE] = cim[j, SUBLANE - nb:SUBLANE, :]


def _s5(proj, u_col, h0re, h0im, wa, abre, abim, wcre, wcim, dskip, wglu, gs5, *, nb_total, t, nb, lc):
    mixb = dskip.shape[1]
    nstate = h0re.shape[1]
    nblk = nstate // LANE
    slabs = wa.shape[0]
    assert u_col % mixb == 0
    ublk = u_col // mixb
    uab3 = proj.reshape(nb_total, t, proj.shape[1])
    perms = _time_major_perms(nb, lc)
    full = lambda a: pl.BlockSpec(a.shape, lambda i, j: (0,) * a.ndim)
    y, hre, him = pl.pallas_call(
        functools.partial(_s5_body, nb=nb, lc=lc),
        grid=(nb_total // nb, t // lc),
        in_specs=[pl.BlockSpec((nb, lc, mixb), lambda i, j: (i, j, ublk)),
                  pl.BlockSpec((nb, nstate), lambda i, j: (i, 0)),
                  pl.BlockSpec((nb, nstate), lambda i, j: (i, 0))]
                 + [full(a) for a in perms]
                 + [full(wa), full(abre), full(abim), full(wcre), full(wcim), full(dskip), full(wglu), full(gs5)],
        out_specs=[pl.BlockSpec((nb, lc, mixb), lambda i, j: (i, j, 0)),
                   pl.BlockSpec((nb, nstate), lambda i, j: (i, 0)),
                   pl.BlockSpec((nb, nstate), lambda i, j: (i, 0))],
        out_shape=[jax.ShapeDtypeStruct((nb_total, t, mixb), F32),
                   jax.ShapeDtypeStruct((nb_total, nstate), F32),
                   jax.ShapeDtypeStruct((nb_total, nstate), F32)],
        scratch_shapes=[pltpu.VMEM((nblk, SUBLANE, LANE), F32),
                        pltpu.VMEM((nblk, SUBLANE, LANE), F32)]
                       + [pltpu.VMEM((nblk // slabs, nb * lc, LANE), F32) for _ in range(2 * slabs)],
        compiler_params=_params("parallel", "arbitrary"),
        name="s5",
    )(uab3, h0re, h0im, *perms, wa, abre, abim, wcre, wcim, dskip, wglu, gs5)
    return y.reshape(nb_total * t, mixb), hre, him


def _outproj_body(o_ref, y_ref, wt_ref, wb_ref, x_ref, mod_ref, g_ref, out_ref, *, gate):
    mix = (jnp.dot(o_ref[...], wt_ref[...], preferred_element_type=F32)
           + _mm(y_ref[...], wb_ref[...]))
    n = mix * lax.rsqrt(jnp.mean(mix * mix, axis=-1, keepdims=True) + EPS) * g_ref[...]
    x = x_ref[...]
    out_ref[...] = x + mod_ref[:, gate:gate + 1, :] * n.reshape(x.shape)


def _outproj(o, y5, wtop, wbot, x3, mod, g, *, gate, rows):
    nb, t, d = x3.shape
    bb, tt = _token_blocks(nb, t, rows)
    nt = t // tt
    ka, kb = o.shape[1], y5.shape[1]
    return pl.pallas_call(
        functools.partial(_outproj_body, gate=gate),
        grid=(nb * t // rows,),
        in_specs=[pl.BlockSpec((rows, ka), lambda i: (i, 0)),
                  pl.BlockSpec((rows, kb), lambda i: (i, 0)),
                  pl.BlockSpec((ka, d), lambda i: (0, 0)),
                  pl.BlockSpec((kb, d), lambda i: (0, 0)),
                  pl.BlockSpec((bb, tt, d), lambda i: (i // nt, i % nt, 0)),
                  pl.BlockSpec((bb, 6, d), lambda i: (i // nt, 0, 0)),
                  pl.BlockSpec((1, d), lambda i: (0, 0))],
        out_specs=pl.BlockSpec((bb, tt, d), lambda i: (i // nt, i % nt, 0)),
        out_shape=jax.ShapeDtypeStruct((nb, t, d), F32),
        compiler_params=_params("parallel"),
        name="outproj",
    )(o, y5, wtop, wbot, x3, mod, g)


FFN_PART_ROWS = 256

def _ffn_body(x_ref, mod_ref, gpre_ref, wg_ref, wu_ref, wd_ref, gpost_ref, out_ref, h_ref,
              *, shift, scale, gate):
    j = pl.program_id(1)
    last = pl.num_programs(1) - 1
    bb, tt, d = x_ref.shape
    rows = bb * tt
    nparts = max(1, rows // FFN_PART_ROWS)
    if bb > 1:
        nparts = min(nparts, bb)
    prow = rows // nparts

    def part(r):
        if bb == 1:
            return (slice(None), slice(r * prow, (r + 1) * prow)), slice(None), slice(r * prow, (r + 1) * prow)
        bs = slice(r * bb // nparts, (r + 1) * bb // nparts)
        return (bs, slice(None)), bs, slice(r * prow, (r + 1) * prow)

    def swiglu(h):
        a = (_silu(jnp.dot(h, wg_ref[...], preferred_element_type=F32))
             * jnp.dot(h, wu_ref[...], preferred_element_type=F32))
        return _mm(a, wd_ref[...])

    @pl.when(j == 0)
    def _():
        for r in range(nparts):
            xi, mi, hr = part(r)
            x = x_ref[xi]
            h = _modulated_norm(x, gpre_ref[...], mod_ref[mi, scale:scale + 1, :], mod_ref[mi, shift:shift + 1, :])
            h = h.reshape(prow, d).astype(BF16)
            h_ref[hr, :] = h
            out_ref[xi] = swiglu(h).reshape(x.shape)

    @pl.when((j > 0) & (j < last))
    def _():
        out_ref[...] += swiglu(h_ref[...]).reshape(out_ref.shape)

    @pl.when((j == last) & (j > 0))
    def _():
        for r in range(nparts):
            xi, mi, hr = part(r)
            x = x_ref[xi]
            f = out_ref[xi] + swiglu(h_ref[hr, :]).reshape(x.shape)
            n = f * lax.rsqrt(jnp.mean(f * f, axis=-1, keepdims=True) + EPS) * gpost_ref[...]
            out_ref[xi] = x + mod_ref[mi, gate:gate + 1, :] * n


def _ffn(x3, mod, gpre, wg, wu, wd, gpost, *, shift, scale, gate, rows, tf):
    nb, t, d = x3.shape
    dff = wg.shape[1]
    assert dff // tf >= 2
    bb, tt = _token_blocks(nb, t, rows)
    nt = t // tt
    return pl.pallas_call(
        functools.partial(_ffn_body, shift=shift, scale=scale, gate=gate),
        grid=(nb * t // rows, dff // tf),
        in_specs=[pl.BlockSpec((bb, tt, d), lambda i, j: (i // nt, i % nt, 0)),
                  pl.BlockSpec((bb, 6, d), lambda i, j: (i // nt, 0, 0)),
                  pl.BlockSpec((1, d), lambda i, j: (0, 0)),
                  pl.BlockSpec((d, tf), lambda i, j: (0, j)),
                  pl.BlockSpec((d, tf), lambda i, j: (0, j)),
                  pl.BlockSpec((tf, d), lambda i, j: (j, 0)),
                  pl.BlockSpec((1, d), lambda i, j: (0, 0))],
        out_specs=pl.BlockSpec((bb, tt, d), lambda i, j: (i // nt, i % nt, 0)),
        out_shape=jax.ShapeDtypeStruct((nb, t, d), F32),
        scratch_shapes=[pltpu.VMEM((rows, d), BF16)],
        compiler_params=_params("parallel", "arbitrary"),
        name="ffn",
    )(x3, mod, gpre, wg, wu, wd, gpost)


ROWS = 512
FFN_ROWS = 512
FFN_COLS = 512
PROMPT_CHUNK = 128
PROMPT_BASE = 16
SAMPLE_BATCH = 8
S5_PROMPT_STEPS = 128
S5_SAMPLE_BATCH = 8


def _layer_weights(w_ada, b_ada, g_pre_mix, g_post_mix, g_pre_ffn, g_post_ffn,
                   w_in, w_conv, a_log, dt_bias, g_dn_out,
                   lam_re, lam_im, log_step, b_re, b_im, c_re, c_im, d_skip,
                   w_glu, g_s5_out, w_out, w_gate, w_up, w_down):
    d = w_in.shape[0]
    nheads = a_log.shape[0]
    mix_a = nheads * DN_HEAD_DIM
    pad = lambda v: jnp.pad(v, (0, LANE - v.shape[0]))[None, :]
    wa, abre, abim = _s5_prep(lam_re, lam_im, log_step, b_re, b_im)
    return dict(
        w_ada=w_ada, b_ada=b_ada[None, :],
        g_pre_mix=g_pre_mix[None, :], g_post_mix=g_post_mix[None, :],
        g_pre_ffn=g_pre_ffn[None, :], g_post_ffn=g_post_ffn[None, :],
        w_proj=_win_prep(w_in, 4 * mix_a, 2 * nheads), u_col=4 * mix_a,
        w_conv=w_conv, alog=pad(a_log), dtb=pad(dt_bias), gdn=g_dn_out[None, :],
        wa=wa, abre=abre, abim=abim, wcre=_s5_out_weights(c_re), wcim=_s5_out_weights(c_im),
        dskip=d_skip.reshape(1, -1), wglu=w_glu.astype(BF16), gs5=g_s5_out[None, :],
        wtop=w_out[:mix_a].astype(BF16), wbot=w_out[mix_a:].astype(BF16),
        wg=w_gate.astype(BF16), wu=w_up.astype(BF16), wd=w_down.astype(BF16),
    )


def _layer_group(x3, mod, conv_state, s0, h0re, h0im, lw, *, prompt):
    nb, t, d = x3.shape
    rows = min(ROWS, nb * t)
    proj = _nmm(x3, mod, lw["g_pre_mix"], lw["w_proj"], shift=0, scale=1, rows=rows if t >= rows else rows // 2)
    if prompt:
        o, cnew, snew = _delta_prompt(proj, proj, conv_state, s0, lw["w_conv"], lw["alog"], lw["dtb"], lw["gdn"],
                                      nb=nb, t=t, c=PROMPT_CHUNK, base=PROMPT_BASE)
        y5, hre, him = _s5(proj, lw["u_col"], h0re, h0im, lw["wa"], lw["abre"], lw["abim"], lw["wcre"], lw["wcim"],
                           lw["dskip"], lw["wglu"], lw["gs5"], nb_total=nb, t=t, nb=nb, lc=S5_PROMPT_STEPS)
    else:
        o, cnew, snew = _delta_sample(proj, proj, conv_state, s0, lw["w_conv"], lw["alog"], lw["dtb"], lw["gdn"],
                                      nb=nb, t=t, bb=SAMPLE_BATCH)
        y5, hre, him = _s5(proj, lw["u_col"], h0re, h0im, lw["wa"], lw["abre"], lw["abim"], lw["wcre"], lw["wcim"],
                           lw["dskip"], lw["wglu"], lw["gs5"], nb_total=nb, t=t, nb=S5_SAMPLE_BATCH, lc=t)
    x1 = _outproj(o, y5, lw["wtop"], lw["wbot"], x3, mod, lw["g_post_mix"], gate=2, rows=rows)
    x2 = _ffn(x1, mod, lw["g_pre_ffn"], lw["wg"], lw["wu"], lw["wd"], lw["g_post_ffn"],
              shift=3, scale=4, gate=5, rows=min(FFN_ROWS, t) if t >= rows else rows, tf=FFN_COLS)
    ngroups = h0re.shape[1] // S5_STATE
    return x2, cnew, snew, hre.reshape(nb, ngroups, S5_STATE), him.reshape(nb, ngroups, S5_STATE)


def kernel(x_prompt, x_sample, c_prompt, c_sample, state_conv, state_delta, state_ssm_re, state_ssm_im, w_ada, b_ada, g_pre_mix, g_post_mix, g_pre_ffn, g_post_ffn, w_in, w_conv, a_log, dt_bias, g_dn_out, lam_re, lam_im, log_step, b_re, b_im, c_re, c_im, d_skip, w_glu, g_s5_out, w_out, w_gate, w_up, w_down):
    weights = (w_ada, b_ada, g_pre_mix, g_post_mix, g_pre_ffn, g_post_ffn,
               w_in, w_conv, a_log, dt_bias, g_dn_out,
               lam_re, lam_im, log_step, b_re, b_im, c_re, c_im, d_skip,
               w_glu, g_s5_out, w_out, w_gate, w_up, w_down)
    depth = w_ada.shape[0]
    bp, d = c_prompt.shape
    bs = c_sample.shape[0]
    dt_ = x_prompt.dtype
    nheads = state_delta.shape[2]
    nstate = state_ssm_re.shape[2] * state_ssm_re.shape[3]
    crows = -(-(bp + bs) // SUBLANE) * SUBLANE
    c_all = jnp.concatenate([c_prompt, c_sample, jnp.zeros((crows - bp - bs, d), dt_)], axis=0)
    yp, ys = x_prompt, x_sample
    outs = [[] for _ in range(8)]
    for l in range(depth):
        lw = _layer_weights(*[w[l] for w in weights])
        mod = _ada(c_all, lw["w_ada"], lw["b_ada"]).reshape(crows, 6, d)
        zc = jnp.zeros((bp, CONV_WIDTH - 1, state_conv.shape[3]), dt_)
        zd = jnp.zeros((bp,) + state_delta.shape[2:], dt_)
        zs = jnp.zeros((bp, nstate), dt_)
        yp, c1, d1, r1, i1 = _layer_group(yp, mod[:bp], zc, zd, zs, zs, lw, prompt=True)
        ys, c2, d2, r2, i2 = _layer_group(ys, mod[bp:bp + bs], state_conv[l], state_delta[l],
                                          state_ssm_re[l].reshape(bs, nstate), state_ssm_im[l].reshape(bs, nstate),
                                          lw, prompt=False)
        for lst, v in zip(outs, (c1, d1, r1, i1, c2, d2, r2, i2)):
            lst.append(v)
    stacked = [jnp.stack(v) for v in outs]
    return (yp, ys, *stacked)
```

```python
import functools
import math

import jax
import jax.numpy as jnp
import numpy as np
from jax import lax
from jax.experimental import pallas as pl
from jax.experimental.pallas import tpu as pltpu

F32 = jnp.float32
BF16 = jnp.bfloat16
EPS = 1e-6
LANE = 128
SUBLANE = 8
V7X_VMEM_BYTES = 64 << 20
VMEM_LIMIT = V7X_VMEM_BYTES - (8 << 20)

DN_HEAD_DIM = 128
CONV_WIDTH = 4
S5_GROUP = 16
S5_STATE = 64
S5_SLAB_GROUPS = LANE // S5_GROUP
S5_SLAB_STATES = S5_SLAB_GROUPS * S5_STATE


def _params(*sem):
    return pltpu.CompilerParams(dimension_semantics=sem, vmem_limit_bytes=VMEM_LIMIT)


def _silu(x):
    return x * jax.nn.sigmoid(x)


def _mm(a, b):
    return jnp.dot(a.astype(BF16), b.astype(BF16), preferred_element_type=F32)


def _mm_nt(a, b):
    return lax.dot_general(a.astype(BF16), b.astype(BF16), (((1,), (1,)), ((), ())),
                           preferred_element_type=F32)


def _mm_tn(a, b):
    return lax.dot_general(a.astype(BF16), b.astype(BF16), (((0,), (0,)), ((), ())),
                           preferred_element_type=F32)


def _split3(x):
    x1 = x.astype(BF16)
    r1 = x - x1.astype(F32)
    x2 = r1.astype(BF16)
    x3 = (r1 - x2.astype(F32)).astype(BF16)
    return x1, x2, x3


def _ada_body(c_ref, w_ref, b_ref, o_ref):
    s = _silu(c_ref[...])
    o_ref[...] = _mm(s, w_ref[...]) + b_ref[...]


def _ada(c, w, b):
    rows, d = c.shape
    n = w.shape[1]
    tn = 1024
    return pl.pallas_call(
        _ada_body,
        grid=(n // tn,),
        in_specs=[pl.BlockSpec((rows, d), lambda j: (0, 0)),
                  pl.BlockSpec((d, tn), lambda j: (0, j)),
                  pl.BlockSpec((1, tn), lambda j: (0, j))],
        out_specs=pl.BlockSpec((rows, tn), lambda j: (0, j)),
        out_shape=jax.ShapeDtypeStruct((rows, n), F32),
        compiler_params=_params("arbitrary"),
        name="ada",
    )(c, w, b)


def _token_blocks(nb, t, rows):
    if t >= rows:
        assert t % rows == 0
        return 1, rows
    assert rows % t == 0 and nb % (rows // t) == 0
    return rows // t, t


def _modulated_norm(x, g, scale, shift):
    y = x * lax.rsqrt(jnp.mean(x * x, axis=-1, keepdims=True) + EPS) * g
    return y * (1.0 + scale) + shift


def _win_prep_body(w_ref, o_ref, *, main, gates):
    n = w_ref.shape[0]
    rest = n - main - gates
    o_ref[0:main, :] = w_ref[0:main, :].astype(BF16)
    o_ref[main:main + rest, :] = w_ref[main + gates:n, :].astype(BF16)
    o_ref[main + rest:main + rest + gates, :] = w_ref[main:main + gates, :].astype(BF16)
    o_ref[main + rest + gates:, :] = jnp.zeros((LANE - gates, o_ref.shape[1]), BF16)


def _win_prep(w_in_t, main, gates):
    n, d = w_in_t.shape
    cb = min(512, d)
    nout = n - gates + LANE
    return pl.pallas_call(
        functools.partial(_win_prep_body, main=main, gates=gates),
        grid=(d // cb,),
        in_specs=[pl.BlockSpec((n, cb), lambda i: (0, i))],
        out_specs=pl.BlockSpec((nout, cb), lambda i: (0, i)),
        out_shape=jax.ShapeDtypeStruct((nout, d), BF16),
        compiler_params=_params("parallel"),
        name="winprep",
    )(w_in_t)


def _nmm_body(x_ref, mod_ref, g_ref, w_ref, o_ref, *, shift, scale):
    x = x_ref[...]
    bb, tt, d = x.shape
    h = _modulated_norm(x, g_ref[...], mod_ref[:, scale:scale + 1, :], mod_ref[:, shift:shift + 1, :])
    o_ref[...] = _mm_nt(h.reshape(bb * tt, d), w_ref[...])


def _nmm(x3, mod, g, w_t, *, shift, scale, rows):
    nb, t, d = x3.shape
    n = w_t.shape[0]
    bb, tt = _token_blocks(nb, t, rows)
    nt = t // tt
    return pl.pallas_call(
        functools.partial(_nmm_body, shift=shift, scale=scale),
        grid=(nb * t // rows,),
        in_specs=[pl.BlockSpec((bb, tt, d), lambda i: (i // nt, i % nt, 0)),
                  pl.BlockSpec((bb, 6, d), lambda i: (i // nt, 0, 0)),
                  pl.BlockSpec((1, d), lambda i: (0, 0)),
                  pl.BlockSpec((n, d), lambda i: (0, 0), pipeline_mode=pl.Buffered(1))],
        out_specs=pl.BlockSpec((rows, n), lambda i: (i, 0)),
        out_shape=jax.ShapeDtypeStruct((nb * t, n), F32),
        compiler_params=_params("parallel"),
        name="nmm",
    )(x3, mod, g, w_t)


def _seg_masks(c, seg):
    row = lax.broadcasted_iota(jnp.int32, (c, c), 0)
    col = lax.broadcasted_iota(jnp.int32, (c, c), 1)
    sh = int(math.log2(seg))
    same = (row >> sh) == (col >> sh)
    return row, col, same, same & (row >= col), same & (row > col)


def _tri_inverse_offdiag(lmats, row, col, base, seg):
    bsh = int(math.log2(base))
    blk = (row >> bsh) == (col >> bsh)
    l0s = [jnp.where(blk, l, 0.0) for l in lmats]
    pts = [-l for l in l0s]
    qs = [_mm(l, l) for l in l0s]
    k = 2
    while k < base:
        pts = [p + q + _mm(p, q) for p, q in zip(pts, qs)]
        k *= 2
        if k < base:
            qs = [_mm(q, q) for q in qs]
    m = base
    while m < seg:
        msh = int(math.log2(m))
        sel = ((row >> (msh + 1)) == (col >> (msh + 1))) & ((row >> msh) != (col >> msh))
        es = [jnp.where(sel, l, 0.0) for l in lmats]
        xs = [e + _mm(p, e) for p, e in zip(pts, es)]
        pts = [p - (x + _mm(x, p)) for p, x in zip(pts, xs)]
        m *= 2
    return pts


def _delta_gates(ab, alog, dtb, causal_bf, same_bf, nheads):
    lane = lax.broadcasted_iota(jnp.int32, ab.shape, 1)
    x = ab + dtb
    softplus = jnp.maximum(x, 0.0) + jnp.log1p(jnp.exp(-jnp.abs(x)))
    g = jnp.where(lane < nheads, -jnp.exp(alog) * softplus, 0.0)
    beta = jax.nn.sigmoid(ab)
    pieces = _split3(g)
    gcum = sum(jnp.dot(causal_bf, p, preferred_element_type=F32) for p in pieces)
    glast = sum(jnp.dot(same_bf, p, preferred_element_type=F32) for p in pieces)
    return gcum, beta, glast


def _decay_logits(gcum, nheads):
    c = gcum.shape[0]
    lane = lax.broadcasted_iota(jnp.int32, (c, LANE), 1)
    ones = jnp.ones((c, LANE), BF16)
    pad = jnp.zeros((LANE - c, 2 * LANE), BF16) if c < LANE else None
    out = None
    for p in _split3(gcum):
        lhs = jnp.concatenate([p, ones], axis=1)
        blocks = []
        for h in range(nheads):
            onehot = lane == h
            blk = jnp.concatenate([jnp.where(onehot, 1.0, 0.0).astype(BF16),
                                   jnp.where(onehot, -p, jnp.zeros_like(p))], axis=1)
            blocks.append(blk)
            if pad is not None:
                blocks.append(pad)
        rhs = jnp.concatenate(blocks, axis=0)
        d = lax.dot_general(lhs, rhs, (((1,), (1,)), ((), ())), preferred_element_type=F32)
        out = d if out is None else out + d
    return out


def _delta_intra(act, nheads, gates, dlog, masks, base, seg):
    gcum, beta, glast = gates
    row, col, _, causal, strict = masks
    c = act.shape[0]
    dh = DN_HEAD_DIM
    eg_all = jnp.exp(gcum)
    ed_all = jnp.exp(glast - gcum)
    ks, kqs_lhs, rhss, qgs, kdecs = [], [], [], [], []
    for h in range(nheads):
        q = act[:, h * dh:(h + 1) * dh]
        k = act[:, (nheads + h) * dh:(nheads + h + 1) * dh]
        v = act[:, (2 * nheads + h) * dh:(2 * nheads + h + 1) * dh]
        q = q * lax.rsqrt(jnp.sum(q * q, axis=-1, keepdims=True) + 1e-6) * (dh ** -0.5)
        k = k * lax.rsqrt(jnp.sum(k * k, axis=-1, keepdims=True) + 1e-6)
        b = beta[:, nheads + h:nheads + h + 1]
        eg = eg_all[:, h:h + 1]
        kb = k * b
        ks.append(k)
        kqs_lhs.append(jnp.concatenate([kb, q], axis=0))
        rhss.append(jnp.concatenate([v * b, kb * eg], axis=1))
        qgs.append(q * eg)
        kdecs.append(k * ed_all[:, h:h + 1])
    kqs = [_mm_nt(a, k) for a, k in zip(kqs_lhs, ks)]
    lmats, attns = [], []
    for h in range(nheads):
        dl = dlog[:, h * LANE:h * LANE + c]
        dec = jnp.where(causal, jnp.exp(jnp.where(causal, dl, 0.0)), 0.0)
        lmats.append(jnp.where(strict, kqs[h][:c] * dec, 0.0))
        attns.append(kqs[h][c:] * dec)
    tts = _tri_inverse_offdiag(lmats, row, col, base, seg)
    uws = [r + _mm(t, r) for t, r in zip(tts, rhss)]
    return [uw[:, :dh] for uw in uws], [uw[:, dh:] for uw in uws], qgs, kdecs, attns


def _delta_out(o, z, gdn):
    on = o * lax.rsqrt(jnp.mean(o * o, axis=-1, keepdims=True) + EPS) * gdn
    return on * _silu(z)


def _delta_prompt_body(qkv_ref, z_ref, ab_ref, cst_ref, s0_ref, wc_ref, alog_ref, dtb_ref, gdn_ref,
                       o_ref, cnew_ref, snew_ref, xc_ref, s_ref, *, c, nheads, base):
    dh = DN_HEAD_DIM
    halo = SUBLANE
    taps = CONV_WIDTH - 1

    @pl.when(pl.program_id(1) == 0)
    def _():
        xc_ref[0:halo, :] = jnp.zeros((halo, xc_ref.shape[1]), F32)
        xc_ref[halo - taps:halo, :] = cst_ref[0]
        s_ref[...] = s0_ref[0]

    xc_ref[halo:halo + c, :] = qkv_ref[...]
    conv = xc_ref[halo - taps:halo - taps + c, :] * wc_ref[0:1, :]
    for j in range(1, CONV_WIDTH):
        conv = conv + xc_ref[halo - taps + j:halo - taps + j + c, :] * wc_ref[j:j + 1, :]
    cnew_ref[0] = xc_ref[halo + c - taps:halo + c, :]
    xc_ref[0:halo, :] = xc_ref[c:c + halo, :]
    act = _silu(conv)

    masks = _seg_masks(c, c)
    causal_bf = jnp.where(masks[3], 1.0, 0.0).astype(BF16)
    same_bf = jnp.ones((c, c), BF16)
    gates = _delta_gates(ab_ref[...], alog_ref[...], dtb_ref[...], causal_bf, same_bf, nheads)
    dlog = _decay_logits(gates[0], nheads)
    eglast = jnp.exp(gates[2][0:1, :])
    gdn = gdn_ref[...]
    us, ws_, qgs, kdecs, attns = _delta_intra(act, nheads, gates, dlog, masks, base, c)
    heads = range(nheads)
    ss = [s_ref[h] for h in heads]
    wss = [_mm(jnp.concatenate([ws_[h], qgs[h]], axis=0), ss[h]) for h in heads]
    vnews = [us[h] - wss[h][:c] for h in heads]
    os_ = [wss[h][c:] + _mm(attns[h], vnews[h]) for h in heads]
    snews = [ss[h] * eglast[:, h:h + 1] + _mm_tn(kdecs[h], vnews[h]) for h in heads]
    for h in heads:
        s_ref[h] = snews[h]
        o_ref[:, h * dh:(h + 1) * dh] = _delta_out(os_[h], z_ref[:, h * dh:(h + 1) * dh], gdn).astype(o_ref.dtype)

    @pl.when(pl.program_id(1) == pl.num_programs(1) - 1)
    def _():
        snew_ref[0] = s_ref[...]


def _delta_prompt(qkvz, uab, conv_state, s0, w_conv, alog, dtb, gdn, *, nb, t, c, base):
    nheads = s0.shape[1]
    dh = DN_HEAD_DIM
    mix = nheads * dh
    nc = t // c
    return pl.pallas_call(
        functools.partial(_delta_prompt_body, c=c, nheads=nheads, base=base),
        grid=(nb, nc),
        in_specs=[pl.BlockSpec((c, 3 * mix), lambda b, i: (b * nc + i, 0)),
                  pl.BlockSpec((c, mix), lambda b, i: (b * nc + i, 3)),
                  pl.BlockSpec((c, LANE), lambda b, i: (b * nc + i, uab.shape[1] // LANE - 1)),
                  pl.BlockSpec((1, CONV_WIDTH - 1, 3 * mix), lambda b, i: (b, 0, 0)),
                  pl.BlockSpec((1, nheads, dh, dh), lambda b, i: (b, 0, 0, 0)),
                  pl.BlockSpec((CONV_WIDTH, 3 * mix), lambda b, i: (0, 0)),
                  pl.BlockSpec((1, LANE), lambda b, i: (0, 0)),
                  pl.BlockSpec((1, LANE), lambda b, i: (0, 0)),
                  pl.BlockSpec((1, dh), lambda b, i: (0, 0))],
        out_specs=[pl.BlockSpec((c, mix), lambda b, i: (b * nc + i, 0)),
                   pl.BlockSpec((1, CONV_WIDTH - 1, 3 * mix), lambda b, i: (b, 0, 0)),
                   pl.BlockSpec((1, nheads, dh, dh), lambda b, i: (b, 0, 0, 0))],
        out_shape=[jax.ShapeDtypeStruct((nb * t, mix), BF16),
                   jax.ShapeDtypeStruct((nb, CONV_WIDTH - 1, 3 * mix), F32),
                   jax.ShapeDtypeStruct((nb, nheads, dh, dh), F32)],
        scratch_shapes=[pltpu.VMEM((c + SUBLANE, 3 * mix), F32),
                        pltpu.VMEM((nheads, dh, dh), F32)],
        compiler_params=_params("parallel", "arbitrary"),
        name="delta_prompt",
    )(qkvz, qkvz, uab, conv_state, s0, w_conv, alog, dtb, gdn)


def _delta_sample_body(qkv_ref, z_ref, ab_ref, cst_ref, s0_ref, wc_ref, alog_ref, dtb_ref, gdn_ref,
                       o_ref, cnew_ref, snew_ref,
                       xs_ref, w_s, qg_s, u_s, kd_s, vn_s, qs_s, attn_s, egl_s, *, bb, t, nheads):
    dh = DN_HEAD_DIM
    c = bb * t
    halo = SUBLANE
    taps = CONV_WIDTH - 1

    xs_ref[:, 0:halo, :] = jnp.zeros((bb, halo, xs_ref.shape[2]), F32)
    xs_ref[:, halo - taps:halo, :] = cst_ref[...]
    xs_ref[:, halo:halo + t, :] = qkv_ref[...]
    conv = xs_ref[:, halo - taps:halo - taps + t, :] * wc_ref[0:1, :]
    for j in range(1, CONV_WIDTH):
        conv = conv + xs_ref[:, halo - taps + j:halo - taps + j + t, :] * wc_ref[j:j + 1, :]
    cnew_ref[...] = xs_ref[:, halo + t - taps:halo + t, :]
    act = _silu(conv).reshape(c, conv.shape[2])

    masks = _seg_masks(c, t)
    causal_bf = jnp.where(masks[3], 1.0, 0.0).astype(BF16)
    same_bf = jnp.where(masks[2], 1.0, 0.0).astype(BF16)
    gates = _delta_gates(ab_ref[...], alog_ref[...], dtb_ref[...], causal_bf, same_bf, nheads)
    dlog = _decay_logits(gates[0], nheads)
    egl_s[...] = jnp.exp(gates[2])
    us, ws_, qgs, kdecs, attns = _delta_intra(act, nheads, gates, dlog, masks, t, t)
    for h in range(nheads):
        hs = slice(h * dh, (h + 1) * dh)
        u_s[:, hs] = us[h]
        w_s[:, hs] = ws_[h]
        qg_s[:, hs] = qgs[h]
        kd_s[:, hs] = kdecs[h]
        attn_s[h] = attns[h]

    def per_batch(b, carry):
        r = pl.multiple_of(b * t, t)
        rows = pl.ds(r, t)
        heads = range(nheads)
        hsl = [slice(h * dh, (h + 1) * dh) for h in heads]
        ss = [s0_ref[b, h] for h in heads]
        wss = [_mm(jnp.concatenate([w_s[rows, hsl[h]], qg_s[rows, hsl[h]]], axis=0), ss[h]) for h in heads]
        vnews = [u_s[rows, hsl[h]] - wss[h][:t] for h in heads]
        upds = [_mm_tn(kd_s[rows, hsl[h]], vnews[h]) for h in heads]
        egl = egl_s[pl.ds(r, 1), :]
        for h in heads:
            vn_s[rows, hsl[h]] = vnews[h]
            qs_s[rows, hsl[h]] = wss[h][t:]
            snew_ref[b, h] = ss[h] * egl[:, h:h + 1] + upds[h]
        return carry

    lax.fori_loop(0, bb, per_batch, 0)

    gdn = gdn_ref[...]
    for h in range(nheads):
        hs = slice(h * dh, (h + 1) * dh)
        o = qs_s[:, hs] + _mm(attn_s[h], vn_s[:, hs])
        o_ref[:, hs] = _delta_out(o, z_ref[:, hs], gdn).astype(o_ref.dtype)


def _delta_sample(qkvz, uab, conv_state, s0, w_conv, alog, dtb, gdn, *, nb, t, bb):
    nheads = s0.shape[1]
    dh = DN_HEAD_DIM
    mix = nheads * dh
    c = bb * t
    qkvz3 = qkvz.reshape(nb, t, qkvz.shape[1])
    return pl.pallas_call(
        functools.partial(_delta_sample_body, bb=bb, t=t, nheads=nheads),
        grid=(nb // bb,),
        in_specs=[pl.BlockSpec((bb, t, 3 * mix), lambda i: (i, 0, 0)),
                  pl.BlockSpec((c, mix), lambda i: (i, 3)),
                  pl.BlockSpec((c, LANE), lambda i: (i, uab.shape[1] // LANE - 1)),
                  pl.BlockSpec((bb, CONV_WIDTH - 1, 3 * mix), lambda i: (i, 0, 0)),
                  pl.BlockSpec((bb, nheads, dh, dh), lambda i: (i, 0, 0, 0)),
                  pl.BlockSpec((CONV_WIDTH, 3 * mix), lambda i: (0, 0)),
                  pl.BlockSpec((1, LANE), lambda i: (0, 0)),
                  pl.BlockSpec((1, LANE), lambda i: (0, 0)),
                  pl.BlockSpec((1, dh), lambda i: (0, 0))],
        out_specs=[pl.BlockSpec((c, mix), lambda i: (i, 0)),
                   pl.BlockSpec((bb, CONV_WIDTH - 1, 3 * mix), lambda i: (i, 0, 0)),
                   pl.BlockSpec((bb, nheads, dh, dh), lambda i: (i, 0, 0, 0))],
        out_shape=[jax.ShapeDtypeStruct((nb * t, mix), BF16),
                   jax.ShapeDtypeStruct((nb, CONV_WIDTH - 1, 3 * mix), F32),
                   jax.ShapeDtypeStruct((nb, nheads, dh, dh), F32)],
        scratch_shapes=[pltpu.VMEM((bb, SUBLANE + t, 3 * mix), F32)]
                       + [pltpu.VMEM((c, mix), F32) for _ in range(6)]
                       + [pltpu.VMEM((nheads, c, c), F32), pltpu.VMEM((c, LANE), F32)],
        compiler_params=_params("parallel"),
        name="delta_sample",
    )(qkvz3, qkvz, uab, conv_state, s0, w_conv, alog, dtb, gdn)


def _s5_prep_body(lre_ref, lim_ref, ls_ref, bre_ref, bim_ref, wa_ref, abre_ref, abim_ref):
    lre = jnp.minimum(lre_ref[...], -1e-4)
    lim = lim_ref[...]
    dt = jnp.exp(ls_ref[...])
    mag = jnp.exp(lre * dt)
    ang = lim * dt
    abr = mag * jnp.cos(ang)
    abi = mag * jnp.sin(ang)
    nr = abr - 1.0
    ni = abi
    den = lre * lre + lim * lim
    cor = (nr * lre + ni * lim) / den
    coi = (ni * lre - nr * lim) / den
    bre = bre_ref[...]
    bim = bim_ref[...]
    shape2 = bre.shape[1:]
    row = lax.broadcasted_iota(jnp.int32, shape2, 0)
    col = lax.broadcasted_iota(jnp.int32, shape2, 1)
    diag = ((row // S5_GROUP) == (col // S5_STATE))[None]
    half = shape2[1]
    rows = shape2[0]
    wre = jnp.where(diag, cor * bre - coi * bim, 0.0)
    wim = jnp.where(diag, cor * bim + coi * bre, 0.0)
    wa_ref[:, 0:rows, 0:half] = wre.astype(BF16)
    wa_ref[:, 0:rows, half:2 * half] = wim.astype(BF16)
    wa_ref[:, rows:2 * rows, 0:half] = (abr * wre - abi * wim).astype(BF16)
    wa_ref[:, rows:2 * rows, half:2 * half] = (abr * wim + abi * wre).astype(BF16)
    abre_ref[...] = abr
    abim_ref[...] = abi


def _s5_prep(lam_re, lam_im, log_step, b_re, b_im):
    groups, states, gch = b_re.shape
    slabs = groups // S5_SLAB_GROUPS
    cols = S5_SLAB_STATES

    def row_param(p):
        return p.reshape(slabs, 1, cols)

    def tiled_b(b):
        bt = jnp.transpose(b, (0, 2, 1)).reshape(slabs, LANE, states)
        return jnp.tile(bt, (1, 1, S5_SLAB_GROUPS))

    ls = jnp.broadcast_to(log_step[:, None], (groups, states))
    wa, abre, abim = pl.pallas_call(
        _s5_prep_body,
        out_shape=[jax.ShapeDtypeStruct((slabs, 2 * LANE, 2 * cols), BF16),
                   jax.ShapeDtypeStruct((slabs, 1, cols), F32),
                   jax.ShapeDtypeStruct((slabs, 1, cols), F32)],
        compiler_params=pltpu.CompilerParams(vmem_limit_bytes=VMEM_LIMIT),
        name="s5prep",
    )(row_param(lam_re), row_param(lam_im), row_param(ls), tiled_b(b_re), tiled_b(b_im))
    nblk = groups * states // LANE
    return wa, abre.reshape(nblk, 1, LANE), abim.reshape(nblk, 1, LANE)


def _s5_out_weights(c):
    groups, gch, states = c.shape
    slabs = groups // S5_SLAB_GROUPS
    ct = jnp.transpose(c, (0, 2, 1)).reshape(slabs, S5_SLAB_STATES, gch)
    ct = jnp.tile(ct, (1, 1, S5_SLAB_GROUPS))
    row = jnp.arange(S5_SLAB_STATES)[:, None] // states
    col = jnp.arange(LANE)[None, :] // gch
    return jnp.where((row == col)[None], ct, 0.0).astype(BF16)


def _gelu_tanh(x):
    return x * (0.5 * (1.0 + jnp.tanh(math.sqrt(2.0 / math.pi) * (x + 0.044715 * (x * x * x)))))


S5_PIPELINE_PARTS = 4


def _time_major_perms(nb, lc):
    rows = nb * lc
    r = np.arange(rows)[:, None]
    c = np.arange(rows)[None, :]
    t_tm, b_tm = r // nb, r % nb
    to_tm = c == b_tm * lc + t_tm
    to_tm_prev = (c == b_tm * lc + t_tm - 1) & (t_tm % 2 == 1)
    to_bm = c == (r % lc) * nb + r // lc
    return tuple(jnp.asarray(m, BF16) for m in (to_tm, to_tm_prev, to_bm))


def _s5_body(u_ref, h0re_ref, h0im_ref, tm_ref, tmp_ref, bm_ref, wa_ref, abre_ref, abim_ref, wcre_ref, wcim_ref,
             dsk_ref, wglu_ref, gs5_ref, y_ref, hre_ref, him_ref, cre, cim, *xbufs, nb, lc):
    rows = nb * lc
    nblk = cre.shape[0]
    slabs = wa_ref.shape[0]
    per_slab = nblk // slabs
    half = per_slab * LANE
    steps_per_tile = SUBLANE // nb
    assert steps_per_tile in (1, 2) and lc % steps_per_tile == 0

    @pl.when(pl.program_id(1) == 0)
    def _():
        for j in range(nblk):
            if nb < SUBLANE:
                cre[j] = jnp.zeros((SUBLANE, LANE), F32)
                cim[j] = jnp.zeros((SUBLANE, LANE), F32)
            cre[j, SUBLANE - nb:SUBLANE, :] = h0re_ref[:, j * LANE:(j + 1) * LANE]
            cim[j, SUBLANE - nb:SUBLANE, :] = h0im_ref[:, j * LANE:(j + 1) * LANE]

    to_tm, to_bm = tm_ref[...], bm_ref[...]
    pieces = _split3(u_ref[...].reshape(rows, u_ref.shape[2]))
    u = sum(jnp.dot(to_tm, p, preferred_element_type=F32) for p in pieces)
    ub = u.astype(BF16)
    if steps_per_tile == 2:
        ub_prev = jnp.dot(tmp_ref[...], pieces[0], preferred_element_type=F32).astype(BF16)

    upper = lax.broadcasted_iota(jnp.int32, (SUBLANE, LANE), 0) >= nb
    parts = S5_PIPELINE_PARTS if rows % (S5_PIPELINE_PARTS * LANE) == 0 else 1
    prow = rows // parts

    def project_in(s, r):
        rs = slice(r * prow, (r + 1) * prow)
        sl = slice(s * LANE, (s + 1) * LANE)
        if steps_per_tile == 2:
            x = jnp.dot(jnp.concatenate([ub[rs, sl], ub_prev[rs, sl]], axis=1), wa_ref[s],
                        preferred_element_type=F32)
        else:
            x = jnp.dot(ub[rs, sl], wa_ref[s, 0:LANE, :], preferred_element_type=F32)
        for k in range(per_slab):
            xbufs[2 * s][k, rs, :] = x[:, k * LANE:(k + 1) * LANE]
            xbufs[2 * s + 1][k, rs, :] = x[:, half + k * LANE:half + (k + 1) * LANE]

    def recur(s, r, coef, carry):
        xre, xim = xbufs[2 * s], xbufs[2 * s + 1]
        for v in range(r * prow // SUBLANE, (r + 1) * prow // SUBLANE):
            at = slice(v * SUBLANE, (v + 1) * SUBLANE)
            for n in range(per_slab):
                hr, hi = carry[n]
                ar, ai = coef[n]
                if steps_per_tile == 2:
                    hr = jnp.where(upper, hr, pltpu.roll(hr, nb, 0))
                    hi = jnp.where(upper, hi, pltpu.roll(hi, nb, 0))
                nr = ar * hr - ai * hi + xre[n, at, :]
                ni = ar * hi + ai * hr + xim[n, at, :]
                xre[n, at, :] = nr
                xim[n, at, :] = ni
                carry[n] = (nr, ni)

    def project_out(s, r):
        rs = slice(r * prow, (r + 1) * prow)
        hre = jnp.concatenate([xbufs[2 * s][k, rs, :] for k in range(per_slab)], axis=1)
        him = jnp.concatenate([xbufs[2 * s + 1][k, rs, :] for k in range(per_slab)], axis=1)
        return _mm(hre, wcre_ref[s]) - _mm(him, wcim_ref[s])

    ychunks = [[None] * parts for _ in range(slabs)]
    for r in range(parts):
        project_in(0, r)
    for s in range(slabs):
        js = list(range(s * per_slab, (s + 1) * per_slab))
        coef, carry = [], []
        for j in js:
            ar = jnp.broadcast_to(abre_ref[j], (SUBLANE, LANE))
            ai = jnp.broadcast_to(abim_ref[j], (SUBLANE, LANE))
            if steps_per_tile == 2:
                coef.append((jnp.where(upper, ar * ar - ai * ai, ar), jnp.where(upper, 2.0 * ar * ai, ai)))
            else:
                coef.append((ar, ai))
            carry.append((cre[j], cim[j]))
        for r in range(parts):
            recur(s, r, coef, carry)
            if s + 1 < slabs:
                project_in(s + 1, r)
            if s >= 1:
                ychunks[s - 1][r] = project_out(s - 1, r)
        for n, j in enumerate(js):
            cre[j], cim[j] = carry[n]
    for r in range(parts):
        ychunks[slabs - 1][r] = project_out(slabs - 1, r)
    y = jnp.concatenate([jnp.concatenate(yc, axis=0) for yc in ychunks], axis=1) + dsk_ref[...] * u
    ab = _mm(_gelu_tanh(y), wglu_ref[...])
    n = y.shape[1]
    glu = ab[:, :n] * jax.nn.sigmoid(ab[:, n:])
    out = glu * lax.rsqrt(jnp.mean(glu * glu, axis=-1, keepdims=True) + EPS) * gs5_ref[...]
    out = jnp.dot(to_bm, out.astype(BF16), preferred_element_type=F32)
    y_ref[...] = out.reshape(y_ref.shape).astype(y_ref.dtype)

    @pl.when(pl.program_id(1) == pl.num_programs(1) - 1)
    def _():
        for j in range(nblk):
            hre_ref[:, j * LANE:(j + 1) * LANE] = cre[j, SUBLANE - nb:SUBLANE, :]
            him_ref[:, j * LANE:(j + 1) * LANE] = cim[j, SUBLANE - nb:SUBLANE, :]


def _s5(proj, u_col, h0re, h0im, wa, abre, abim, wcre, wcim, dskip, wglu, gs5, *, nb_total, t, nb, lc):
    mixb = dskip.shape[1]
    nstate = h0re.shape[1]
    nblk = nstate // LANE
    slabs = wa.shape[0]
    assert u_col % mixb == 0
    ublk = u_col // mixb
    uab3 = proj.reshape(nb_total, t, proj.shape[1])
    perms = _time_major_perms(nb, lc)
    full = lambda a: pl.BlockSpec(a.shape, lambda i, j: (0,) * a.ndim)
    y, hre, him = pl.pallas_call(
        functools.partial(_s5_body, nb=nb, lc=lc),
        grid=(nb_total // nb, t // lc),
        in_specs=[pl.BlockSpec((nb, lc, mixb), lambda i, j: (i, j, ublk)),
                  pl.BlockSpec((nb, nstate), lambda i, j: (i, 0)),
                  pl.BlockSpec((nb, nstate), lambda i, j: (i, 0))]
                 + [full(a) for a in perms]
                 + [full(wa), full(abre), full(abim), full(wcre), full(wcim), full(dskip), full(wglu), full(gs5)],
        out_specs=[pl.BlockSpec((nb, lc, mixb), lambda i, j: (i, j, 0)),
                   pl.BlockSpec((nb, nstate), lambda i, j: (i, 0)),
                   pl.BlockSpec((nb, nstate), lambda i, j: (i, 0))],
        out_shape=[jax.ShapeDtypeStruct((nb_total, t, mixb), F32),
                   jax.ShapeDtypeStruct((nb_total, nstate), F32),
                   jax.ShapeDtypeStruct((nb_total, nstate), F32)],
        scratch_shapes=[pltpu.VMEM((nblk, SUBLANE, LANE), F32),
                        pltpu.VMEM((nblk, SUBLANE, LANE), F32)]
                       + [pltpu.VMEM((nblk // slabs, nb * lc, LANE), F32) for _ in range(2 * slabs)],
        compiler_params=_params("parallel", "arbitrary"),
        name="s5",
    )(uab3, h0re, h0im, *perms, wa, abre, abim, wcre, wcim, dskip, wglu, gs5)
    return y.reshape(nb_total * t, mixb), hre, him


def _outproj_body(o_ref, y_ref, wt_ref, wb_ref, x_ref, mod_ref, g_ref, out_ref, *, gate):
    mix = (jnp.dot(o_ref[...], wt_ref[...], preferred_element_type=F32)
           + _mm(y_ref[...], wb_ref[...]))
    n = mix * lax.rsqrt(jnp.mean(mix * mix, axis=-1, keepdims=True) + EPS) * g_ref[...]
    x = x_ref[...]
    out_ref[...] = x + mod_ref[:, gate:gate + 1, :] * n.reshape(x.shape)


def _outproj(o, y5, wtop, wbot, x3, mod, g, *, gate, rows):
    nb, t, d = x3.shape
    bb, tt = _token_blocks(nb, t, rows)
    nt = t // tt
    ka, kb = o.shape[1], y5.shape[1]
    return pl.pallas_call(
        functools.partial(_outproj_body, gate=gate),
        grid=(nb * t // rows,),
        in_specs=[pl.BlockSpec((rows, ka), lambda i: (i, 0)),
                  pl.BlockSpec((rows, kb), lambda i: (i, 0)),
                  pl.BlockSpec((ka, d), lambda i: (0, 0)),
                  pl.BlockSpec((kb, d), lambda i: (0, 0)),
                  pl.BlockSpec((bb, tt, d), lambda i: (i // nt, i % nt, 0)),
                  pl.BlockSpec((bb, 6, d), lambda i: (i // nt, 0, 0)),
                  pl.BlockSpec((1, d), lambda i: (0, 0))],
        out_specs=pl.BlockSpec((bb, tt, d), lambda i: (i // nt, i % nt, 0)),
        out_shape=jax.ShapeDtypeStruct((nb, t, d), F32),
        compiler_params=_params("parallel"),
        name="outproj",
    )(o, y5, wtop, wbot, x3, mod, g)


FFN_PART_ROWS = 256

def _ffn_body(x_ref, mod_ref, gpre_ref, wg_ref, wu_ref, wd_ref, gpost_ref, out_ref, h_ref,
              *, shift, scale, gate):
    j = pl.program_id(1)
    last = pl.num_programs(1) - 1
    bb, tt, d = x_ref.shape
    rows = bb * tt
    nparts = max(1, rows // FFN_PART_ROWS)
    if bb > 1:
        nparts = min(nparts, bb)
    prow = rows // nparts

    def part(r):
        if bb == 1:
            return (slice(None), slice(r * prow, (r + 1) * prow)), slice(None), slice(r * prow, (r + 1) * prow)
        bs = slice(r * bb // nparts, (r + 1) * bb // nparts)
        return (bs, slice(None)), bs, slice(r * prow, (r + 1) * prow)

    def swiglu(h):
        a = _silu(_mm(h, wg_ref[...])) * _mm(h, wu_ref[...])
        return _mm(a, wd_ref[...])

    @pl.when(j == 0)
    def _():
        for r in range(nparts):
            xi, mi, hr = part(r)
            x = x_ref[xi]
            h = _modulated_norm(x, gpre_ref[...], mod_ref[mi, scale:scale + 1, :], mod_ref[mi, shift:shift + 1, :])
            h = h.reshape(prow, d).astype(BF16)
            h_ref[hr, :] = h
            out_ref[xi] = swiglu(h).reshape(x.shape)

    @pl.when((j > 0) & (j < last))
    def _():
        out_ref[...] += swiglu(h_ref[...]).reshape(out_ref.shape)

    @pl.when((j == last) & (j > 0))
    def _():
        for r in range(nparts):
            xi, mi, hr = part(r)
            x = x_ref[xi]
            f = out_ref[xi] + swiglu(h_ref[hr, :]).reshape(x.shape)
            n = f * lax.rsqrt(jnp.mean(f * f, axis=-1, keepdims=True) + EPS) * gpost_ref[...]
            out_ref[xi] = x + mod_ref[mi, gate:gate + 1, :] * n


def _ffn(x3, mod, gpre, wg, wu, wd, gpost, *, shift, scale, gate, rows, tf):
    nb, t, d = x3.shape
    dff = wg.shape[1]
    assert dff // tf >= 2
    bb, tt = _token_blocks(nb, t, rows)
    nt = t // tt
    once = dict(pipeline_mode=pl.Buffered(1)) if bb > 1 else {}
    return pl.pallas_call(
        functools.partial(_ffn_body, shift=shift, scale=scale, gate=gate),
        grid=(nb * t // rows, dff // tf),
        in_specs=[pl.BlockSpec((bb, tt, d), lambda i, j: (i // nt, i % nt, 0), **once),
                  pl.BlockSpec((bb, 6, d), lambda i, j: (i // nt, 0, 0), **once),
                  pl.BlockSpec((1, d), lambda i, j: (0, 0)),
                  pl.BlockSpec((d, tf), lambda i, j: (0, j)),
                  pl.BlockSpec((d, tf), lambda i, j: (0, j)),
                  pl.BlockSpec((tf, d), lambda i, j: (j, 0)),
                  pl.BlockSpec((1, d), lambda i, j: (0, 0))],
        out_specs=pl.BlockSpec((bb, tt, d), lambda i, j: (i // nt, i % nt, 0)),
        out_shape=jax.ShapeDtypeStruct((nb, t, d), F32),
        scratch_shapes=[pltpu.VMEM((rows, d), BF16)],
        compiler_params=_params("parallel", "arbitrary"),
        name="ffn",
    )(x3, mod, gpre, wg, wu, wd, gpost)


ROWS = 512
FFN_ROWS = 1024
FFN_COLS = 256
PROMPT_CHUNK = 128
PROMPT_BASE = 16
SAMPLE_BATCH = 8
S5_PROMPT_STEPS = 128
S5_SAMPLE_BATCH = 8


def _layer_weights(w_ada, b_ada, g_pre_mix, g_post_mix, g_pre_ffn, g_post_ffn,
                   w_in, w_conv, a_log, dt_bias, g_dn_out,
                   lam_re, lam_im, log_step, b_re, b_im, c_re, c_im, d_skip,
                   w_glu, g_s5_out, w_out, w_gate, w_up, w_down):
    d = w_in.shape[0]
    nheads = a_log.shape[0]
    mix_a = nheads * DN_HEAD_DIM
    pad = lambda v: jnp.pad(v, (0, LANE - v.shape[0]))[None, :]
    wa, abre, abim = _s5_prep(lam_re, lam_im, log_step, b_re, b_im)
    return dict(
        w_ada=w_ada, b_ada=b_ada[None, :],
        g_pre_mix=g_pre_mix[None, :], g_post_mix=g_post_mix[None, :],
        g_pre_ffn=g_pre_ffn[None, :], g_post_ffn=g_post_ffn[None, :],
        w_proj=_win_prep(jnp.swapaxes(w_in, 0, 1), 4 * mix_a, 2 * nheads), u_col=4 * mix_a,
        w_conv=w_conv, alog=pad(a_log), dtb=pad(dt_bias), gdn=g_dn_out[None, :],
        wa=wa, abre=abre, abim=abim, wcre=_s5_out_weights(c_re), wcim=_s5_out_weights(c_im),
        dskip=d_skip.reshape(1, -1), wglu=w_glu.astype(BF16), gs5=g_s5_out[None, :],
        wtop=w_out[:mix_a].astype(BF16), wbot=w_out[mix_a:].astype(BF16),
        wg=w_gate, wu=w_up, wd=w_down,
    )


def _layer_group(x3, mod, conv_state, s0, h0re, h0im, lw, *, prompt):
    nb, t, d = x3.shape
    rows = min(ROWS, nb * t)
    proj = _nmm(x3, mod, lw["g_pre_mix"], lw["w_proj"], shift=0, scale=1, rows=rows if t >= rows else rows // 2)
    if prompt:
        o, cnew, snew = _delta_prompt(proj, proj, conv_state, s0, lw["w_conv"], lw["alog"], lw["dtb"], lw["gdn"],
                                      nb=nb, t=t, c=PROMPT_CHUNK, base=PROMPT_BASE)
        y5, hre, him = _s5(proj, lw["u_col"], h0re, h0im, lw["wa"], lw["abre"], lw["abim"], lw["wcre"], lw["wcim"],
                           lw["dskip"], lw["wglu"], lw["gs5"], nb_total=nb, t=t, nb=nb, lc=S5_PROMPT_STEPS)
    else:
        o, cnew, snew = _delta_sample(proj, proj, conv_state, s0, lw["w_conv"], lw["alog"], lw["dtb"], lw["gdn"],
                                      nb=nb, t=t, bb=SAMPLE_BATCH)
        y5, hre, him = _s5(proj, lw["u_col"], h0re, h0im, lw["wa"], lw["abre"], lw["abim"], lw["wcre"], lw["wcim"],
                           lw["dskip"], lw["wglu"], lw["gs5"], nb_total=nb, t=t, nb=S5_SAMPLE_BATCH, lc=t)
    x1 = _outproj(o, y5, lw["wtop"], lw["wbot"], x3, mod, lw["g_post_mix"], gate=2, rows=rows)
    x2 = _ffn(x1, mod, lw["g_pre_ffn"], lw["wg"], lw["wu"], lw["wd"], lw["g_post_ffn"],
              shift=3, scale=4, gate=5, rows=min(FFN_ROWS, nb * t), tf=FFN_COLS)
    ngroups = h0re.shape[1] // S5_STATE
    return x2, cnew, snew, hre.reshape(nb, ngroups, S5_STATE), him.reshape(nb, ngroups, S5_STATE)


def kernel(x_prompt, x_sample, c_prompt, c_sample, state_conv, state_delta, state_ssm_re, state_ssm_im, w_ada, b_ada, g_pre_mix, g_post_mix, g_pre_ffn, g_post_ffn, w_in, w_conv, a_log, dt_bias, g_dn_out, lam_re, lam_im, log_step, b_re, b_im, c_re, c_im, d_skip, w_glu, g_s5_out, w_out, w_gate, w_up, w_down):
    weights = (w_ada, b_ada, g_pre_mix, g_post_mix, g_pre_ffn, g_post_ffn,
               w_in, w_conv, a_log, dt_bias, g_dn_out,
               lam_re, lam_im, log_step, b_re, b_im, c_re, c_im, d_skip,
               w_glu, g_s5_out, w_out, w_gate, w_up, w_down)
    depth = w_ada.shape[0]
    bp, d = c_prompt.shape
    bs = c_sample.shape[0]
    dt_ = x_prompt.dtype
    nheads = state_delta.shape[2]
    nstate = state_ssm_re.shape[2] * state_ssm_re.shape[3]
    crows = -(-(bp + bs) // SUBLANE) * SUBLANE
    c_all = jnp.concatenate([c_prompt, c_sample, jnp.zeros((crows - bp - bs, d), dt_)], axis=0)
    yp, ys = x_prompt, x_sample
    outs = [[] for _ in range(8)]
    for l in range(depth):
        lw = _layer_weights(*[w[l] for w in weights])
        mod = _ada(c_all, lw["w_ada"], lw["b_ada"]).reshape(crows, 6, d)
        zc = jnp.zeros((bp, CONV_WIDTH - 1, state_conv.shape[3]), dt_)
        zd = jnp.zeros((bp,) + state_delta.shape[2:], dt_)
        zs = jnp.zeros((bp, nstate), dt_)
        yp, c1, d1, r1, i1 = _layer_group(yp, mod[:bp], zc, zd, zs, zs, lw, prompt=True)
        ys, c2, d2, r2, i2 = _layer_group(ys, mod[bp:bp + bs], state_conv[l], state_delta[l],
                                          state_ssm_re[l].reshape(bs, nstate), state_ssm_im[l].reshape(bs, nstate),
                                          lw, prompt=False)
        for lst, v in zip(outs, (c1, d1, r1, i1, c2, d2, r2, i2)):
            lst.append(v)
    stacked = [jnp.stack(v) for v in outs]
    return (yp, ys, *stacked)
```

```python
import functools
import math

import jax
import jax.numpy as jnp
import numpy as np
from jax import lax
from jax.experimental import pallas as pl
from jax.experimental.pallas import tpu as pltpu

F32 = jnp.float32
BF16 = jnp.bfloat16
EPS = 1e-6
LANE = 128
SUBLANE = 8
V7X_VMEM_BYTES = 64 << 20
VMEM_LIMIT = V7X_VMEM_BYTES - (8 << 20)

DN_HEAD_DIM = 128
CONV_WIDTH = 4
S5_GROUP = 16
S5_STATE = 64
S5_SLAB_GROUPS = LANE // S5_GROUP
S5_SLAB_STATES = S5_SLAB_GROUPS * S5_STATE


def _params(*sem):
    return pltpu.CompilerParams(dimension_semantics=sem, vmem_limit_bytes=VMEM_LIMIT)


def _silu(x):
    return x * jax.nn.sigmoid(x)


def _mm(a, b):
    return jnp.dot(a.astype(BF16), b.astype(BF16), preferred_element_type=F32)


def _mm_nt(a, b):
    return lax.dot_general(a.astype(BF16), b.astype(BF16), (((1,), (1,)), ((), ())),
                           preferred_element_type=F32)


def _mm_tn(a, b):
    return lax.dot_general(a.astype(BF16), b.astype(BF16), (((0,), (0,)), ((), ())),
                           preferred_element_type=F32)


def _split3(x):
    x1 = x.astype(BF16)
    r1 = x - x1.astype(F32)
    x2 = r1.astype(BF16)
    x3 = (r1 - x2.astype(F32)).astype(BF16)
    return x1, x2, x3


def _ada_body(c_ref, w_ref, b_ref, o_ref):
    s = _silu(c_ref[...])
    o_ref[...] = _mm(s, w_ref[...]) + b_ref[...]


def _ada(c, w, b):
    rows, d = c.shape
    n = w.shape[1]
    tn = 1024
    return pl.pallas_call(
        _ada_body,
        grid=(n // tn,),
        in_specs=[pl.BlockSpec((rows, d), lambda j: (0, 0)),
                  pl.BlockSpec((d, tn), lambda j: (0, j)),
                  pl.BlockSpec((1, tn), lambda j: (0, j))],
        out_specs=pl.BlockSpec((rows, tn), lambda j: (0, j)),
        out_shape=jax.ShapeDtypeStruct((rows, n), F32),
        compiler_params=_params("arbitrary"),
        name="ada",
    )(c, w, b)


def _mod_block(mod, bb, nt):
    arr, first = mod
    assert first % bb == 0
    return arr, (bb, 6, arr.shape[2]), lambda i: (first // bb + i // nt, 0, 0)


def _token_blocks(nb, t, rows):
    if t >= rows:
        assert t % rows == 0
        return 1, rows
    assert rows % t == 0 and nb % (rows // t) == 0
    return rows // t, t


def _modulated_norm(x, g, scale, shift):
    y = x * lax.rsqrt(jnp.mean(x * x, axis=-1, keepdims=True) + EPS) * g
    return y * (1.0 + scale) + shift


def _win_prep_body(w_ref, o_ref, *, main, gates):
    n = w_ref.shape[0]
    rest = n - main - gates
    o_ref[0:main, :] = w_ref[0:main, :].astype(BF16)
    o_ref[main:main + rest, :] = w_ref[main + gates:n, :].astype(BF16)
    o_ref[main + rest:main + rest + gates, :] = w_ref[main:main + gates, :].astype(BF16)
    o_ref[main + rest + gates:, :] = jnp.zeros((LANE - gates, o_ref.shape[1]), BF16)


def _win_prep(w_in_t, main, gates):
    n, d = w_in_t.shape
    cb = min(512, d)
    nout = n - gates + LANE
    return pl.pallas_call(
        functools.partial(_win_prep_body, main=main, gates=gates),
        grid=(d // cb,),
        in_specs=[pl.BlockSpec((n, cb), lambda i: (0, i))],
        out_specs=pl.BlockSpec((nout, cb), lambda i: (0, i)),
        out_shape=jax.ShapeDtypeStruct((nout, d), BF16),
        compiler_params=_params("parallel"),
        name="winprep",
    )(w_in_t)


def _nmm_body(x_ref, mod_ref, g_ref, w_ref, o_ref, *, shift, scale):
    x = x_ref[...]
    bb, tt, d = x.shape
    h = _modulated_norm(x, g_ref[...], mod_ref[:, scale:scale + 1, :], mod_ref[:, shift:shift + 1, :])
    o_ref[...] = _mm_nt(h.reshape(bb * tt, d), w_ref[...])


def _nmm(x3, mod, g, w_t, *, shift, scale, rows):
    nb, t, d = x3.shape
    n = w_t.shape[0]
    bb, tt = _token_blocks(nb, t, rows)
    nt = t // tt
    mod, mod_shape, mod_index = _mod_block(mod, bb, nt)
    return pl.pallas_call(
        functools.partial(_nmm_body, shift=shift, scale=scale),
        grid=(nb * t // rows,),
        in_specs=[pl.BlockSpec((bb, tt, d), lambda i: (i // nt, i % nt, 0)),
                  pl.BlockSpec(mod_shape, mod_index),
                  pl.BlockSpec((1, d), lambda i: (0, 0)),
                  pl.BlockSpec((n, d), lambda i: (0, 0), pipeline_mode=pl.Buffered(1))],
        out_specs=pl.BlockSpec((rows, n), lambda i: (i, 0)),
        out_shape=jax.ShapeDtypeStruct((nb * t, n), F32),
        compiler_params=_params("parallel"),
        name="nmm",
    )(x3, mod, g, w_t)


def _seg_masks(c, seg):
    row = lax.broadcasted_iota(jnp.int32, (c, c), 0)
    col = lax.broadcasted_iota(jnp.int32, (c, c), 1)
    sh = int(math.log2(seg))
    same = (row >> sh) == (col >> sh)
    return row, col, same, same & (row >= col), same & (row > col)


def _tri_inverse_offdiag(lmats, row, col, base, seg):
    bsh = int(math.log2(base))
    blk = (row >> bsh) == (col >> bsh)
    l0s = [jnp.where(blk, l, 0.0) for l in lmats]
    pts = [-l for l in l0s]
    qs = [_mm(l, l) for l in l0s]
    k = 2
    while k < base:
        pts = [p + q + _mm(p, q) for p, q in zip(pts, qs)]
        k *= 2
        if k < base:
            qs = [_mm(q, q) for q in qs]
    m = base
    while m < seg:
        msh = int(math.log2(m))
        sel = ((row >> (msh + 1)) == (col >> (msh + 1))) & ((row >> msh) != (col >> msh))
        es = [jnp.where(sel, l, 0.0) for l in lmats]
        xs = [e + _mm(p, e) for p, e in zip(pts, es)]
        pts = [p - (x + _mm(x, p)) for p, x in zip(pts, xs)]
        m *= 2
    return pts


def _delta_gates(ab, alog, dtb, causal_bf, same_bf, nheads):
    lane = lax.broadcasted_iota(jnp.int32, ab.shape, 1)
    x = ab + dtb
    softplus = jnp.maximum(x, 0.0) + jnp.log1p(jnp.exp(-jnp.abs(x)))
    g = jnp.where(lane < nheads, -jnp.exp(alog) * softplus, 0.0)
    beta = jax.nn.sigmoid(ab)
    pieces = _split3(g)
    gcum = sum(jnp.dot(causal_bf, p, preferred_element_type=F32) for p in pieces)
    glast = sum(jnp.dot(same_bf, p, preferred_element_type=F32) for p in pieces)
    return gcum, beta, glast


def _decay_logits(gcum, nheads):
    c = gcum.shape[0]
    lane = lax.broadcasted_iota(jnp.int32, (c, LANE), 1)
    ones = jnp.ones((c, LANE), BF16)
    pad = jnp.zeros((LANE - c, 2 * LANE), BF16) if c < LANE else None
    out = None
    for p in _split3(gcum):
        lhs = jnp.concatenate([p, ones], axis=1)
        blocks = []
        for h in range(nheads):
            onehot = lane == h
            blk = jnp.concatenate([jnp.where(onehot, 1.0, 0.0).astype(BF16),
                                   jnp.where(onehot, -p, jnp.zeros_like(p))], axis=1)
            blocks.append(blk)
            if pad is not None:
                blocks.append(pad)
        rhs = jnp.concatenate(blocks, axis=0)
        d = lax.dot_general(lhs, rhs, (((1,), (1,)), ((), ())), preferred_element_type=F32)
        out = d if out is None else out + d
    return out


def _delta_intra(act, nheads, gates, dlog, masks, base, seg):
    gcum, beta, glast = gates
    row, col, _, causal, strict = masks
    c = act.shape[0]
    dh = DN_HEAD_DIM
    eg_all = jnp.exp(gcum)
    ed_all = jnp.exp(glast - gcum)
    ks, kqs_lhs, rhss, qgs, kdecs = [], [], [], [], []
    for h in range(nheads):
        q = act[:, h * dh:(h + 1) * dh]
        k = act[:, (nheads + h) * dh:(nheads + h + 1) * dh]
        v = act[:, (2 * nheads + h) * dh:(2 * nheads + h + 1) * dh]
        q = q * lax.rsqrt(jnp.sum(q * q, axis=-1, keepdims=True) + 1e-6) * (dh ** -0.5)
        k = k * lax.rsqrt(jnp.sum(k * k, axis=-1, keepdims=True) + 1e-6)
        b = beta[:, nheads + h:nheads + h + 1]
        eg = eg_all[:, h:h + 1]
        kb = k * b
        ks.append(k)
        kqs_lhs.append(jnp.concatenate([kb, q], axis=0))
        rhss.append(jnp.concatenate([v * b, kb * eg], axis=1))
        qgs.append(q * eg)
        kdecs.append(k * ed_all[:, h:h + 1])
    kqs = [_mm_nt(a, k) for a, k in zip(kqs_lhs, ks)]
    lmats, attns = [], []
    for h in range(nheads):
        dl = dlog[:, h * LANE:h * LANE + c]
        dec = jnp.where(causal, jnp.exp(jnp.where(causal, dl, 0.0)), 0.0)
        lmats.append(jnp.where(strict, kqs[h][:c] * dec, 0.0))
        attns.append(kqs[h][c:] * dec)
    tts = _tri_inverse_offdiag(lmats, row, col, base, seg)
    uws = [r + _mm(t, r) for t, r in zip(tts, rhss)]
    return [uw[:, :dh] for uw in uws], [uw[:, dh:] for uw in uws], qgs, kdecs, attns


def _delta_out(o, z, gdn):
    on = o * lax.rsqrt(jnp.mean(o * o, axis=-1, keepdims=True) + EPS) * gdn
    return on * _silu(z)


def _delta_prompt_body(qkv_ref, z_ref, ab_ref, cst_ref, s0_ref, wc_ref, alog_ref, dtb_ref, gdn_ref,
                       o_ref, cnew_ref, snew_ref, xc_ref, s_ref, *, c, nheads, base):
    dh = DN_HEAD_DIM
    halo = SUBLANE
    taps = CONV_WIDTH - 1

    @pl.when(pl.program_id(1) == 0)
    def _():
        xc_ref[0:halo, :] = jnp.zeros((halo, xc_ref.shape[1]), F32)
        xc_ref[halo - taps:halo, :] = cst_ref[0]
        s_ref[...] = s0_ref[0]

    xc_ref[halo:halo + c, :] = qkv_ref[...]
    conv = xc_ref[halo - taps:halo - taps + c, :] * wc_ref[0:1, :]
    for j in range(1, CONV_WIDTH):
        conv = conv + xc_ref[halo - taps + j:halo - taps + j + c, :] * wc_ref[j:j + 1, :]
    cnew_ref[0] = xc_ref[halo + c - taps:halo + c, :]
    xc_ref[0:halo, :] = xc_ref[c:c + halo, :]
    act = _silu(conv)

    masks = _seg_masks(c, c)
    causal_bf = jnp.where(masks[3], 1.0, 0.0).astype(BF16)
    same_bf = jnp.ones((c, c), BF16)
    gates = _delta_gates(ab_ref[...], alog_ref[...], dtb_ref[...], causal_bf, same_bf, nheads)
    dlog = _decay_logits(gates[0], nheads)
    eglast = jnp.exp(gates[2][0:1, :])
    gdn = gdn_ref[...]
    us, ws_, qgs, kdecs, attns = _delta_intra(act, nheads, gates, dlog, masks, base, c)
    heads = range(nheads)
    ss = [s_ref[h] for h in heads]
    wss = [_mm(jnp.concatenate([ws_[h], qgs[h]], axis=0), ss[h]) for h in heads]
    vnews = [us[h] - wss[h][:c] for h in heads]
    os_ = [wss[h][c:] + _mm(attns[h], vnews[h]) for h in heads]
    snews = [ss[h] * eglast[:, h:h + 1] + _mm_tn(kdecs[h], vnews[h]) for h in heads]
    for h in heads:
        s_ref[h] = snews[h]
        o_ref[:, h * dh:(h + 1) * dh] = _delta_out(os_[h], z_ref[:, h * dh:(h + 1) * dh], gdn).astype(o_ref.dtype)

    @pl.when(pl.program_id(1) == pl.num_programs(1) - 1)
    def _():
        snew_ref[0] = s_ref[...]


def _delta_prompt(qkvz, uab, conv_state, s0, w_conv, alog, dtb, gdn, *, nb, t, c, base):
    nheads = s0.shape[1]
    dh = DN_HEAD_DIM
    mix = nheads * dh
    nc = t // c
    return pl.pallas_call(
        functools.partial(_delta_prompt_body, c=c, nheads=nheads, base=base),
        grid=(nb, nc),
        in_specs=[pl.BlockSpec((c, 3 * mix), lambda b, i: (b * nc + i, 0)),
                  pl.BlockSpec((c, mix), lambda b, i: (b * nc + i, 3)),
                  pl.BlockSpec((c, LANE), lambda b, i: (b * nc + i, uab.shape[1] // LANE - 1)),
                  pl.BlockSpec((1, CONV_WIDTH - 1, 3 * mix), lambda b, i: (b, 0, 0)),
                  pl.BlockSpec((1, nheads, dh, dh), lambda b, i: (b, 0, 0, 0)),
                  pl.BlockSpec((CONV_WIDTH, 3 * mix), lambda b, i: (0, 0)),
                  pl.BlockSpec((1, LANE), lambda b, i: (0, 0)),
                  pl.BlockSpec((1, LANE), lambda b, i: (0, 0)),
                  pl.BlockSpec((1, dh), lambda b, i: (0, 0))],
        out_specs=[pl.BlockSpec((c, mix), lambda b, i: (b * nc + i, 0)),
                   pl.BlockSpec((1, CONV_WIDTH - 1, 3 * mix), lambda b, i: (b, 0, 0)),
                   pl.BlockSpec((1, nheads, dh, dh), lambda b, i: (b, 0, 0, 0))],
        out_shape=[jax.ShapeDtypeStruct((nb * t, mix), BF16),
                   jax.ShapeDtypeStruct((nb, CONV_WIDTH - 1, 3 * mix), F32),
                   jax.ShapeDtypeStruct((nb, nheads, dh, dh), F32)],
        scratch_shapes=[pltpu.VMEM((c + SUBLANE, 3 * mix), F32),
                        pltpu.VMEM((nheads, dh, dh), F32)],
        compiler_params=_params("parallel", "arbitrary"),
        name="delta_prompt",
    )(qkvz, qkvz, uab, conv_state, s0, w_conv, alog, dtb, gdn)


def _delta_sample_body(qkv_ref, z_ref, ab_ref, cst_ref, s0_ref, wc_ref, alog_ref, dtb_ref, gdn_ref,
                       o_ref, cnew_ref, snew_ref,
                       xs_ref, w_s, qg_s, u_s, kd_s, vn_s, qs_s, attn_s, egl_s, *, bb, t, nheads):
    dh = DN_HEAD_DIM
    c = bb * t
    halo = SUBLANE
    taps = CONV_WIDTH - 1

    xs_ref[:, 0:halo, :] = jnp.zeros((bb, halo, xs_ref.shape[2]), F32)
    xs_ref[:, halo - taps:halo, :] = cst_ref[...]
    xs_ref[:, halo:halo + t, :] = qkv_ref[...]
    conv = xs_ref[:, halo - taps:halo - taps + t, :] * wc_ref[0:1, :]
    for j in range(1, CONV_WIDTH):
        conv = conv + xs_ref[:, halo - taps + j:halo - taps + j + t, :] * wc_ref[j:j + 1, :]
    cnew_ref[...] = xs_ref[:, halo + t - taps:halo + t, :]
    act = _silu(conv).reshape(c, conv.shape[2])

    masks = _seg_masks(c, t)
    causal_bf = jnp.where(masks[3], 1.0, 0.0).astype(BF16)
    same_bf = jnp.where(masks[2], 1.0, 0.0).astype(BF16)
    gates = _delta_gates(ab_ref[...], alog_ref[...], dtb_ref[...], causal_bf, same_bf, nheads)
    dlog = _decay_logits(gates[0], nheads)
    egl_s[...] = jnp.exp(gates[2])
    us, ws_, qgs, kdecs, attns = _delta_intra(act, nheads, gates, dlog, masks, t, t)
    for h in range(nheads):
        hs = slice(h * dh, (h + 1) * dh)
        u_s[:, hs] = us[h]
        w_s[:, hs] = ws_[h]
        qg_s[:, hs] = qgs[h]
        kd_s[:, hs] = kdecs[h]
        attn_s[h] = attns[h]

    def per_batch(b, carry):
        r = pl.multiple_of(b * t, t)
        rows = pl.ds(r, t)
        heads = range(nheads)
        hsl = [slice(h * dh, (h + 1) * dh) for h in heads]
        ss = [s0_ref[b, h] for h in heads]
        wss = [_mm(jnp.concatenate([w_s[rows, hsl[h]], qg_s[rows, hsl[h]]], axis=0), ss[h]) for h in heads]
        vnews = [u_s[rows, hsl[h]] - wss[h][:t] for h in heads]
        upds = [_mm_tn(kd_s[rows, hsl[h]], vnews[h]) for h in heads]
        egl = egl_s[pl.ds(r, 1), :]
        for h in heads:
            vn_s[rows, hsl[h]] = vnews[h]
            qs_s[rows, hsl[h]] = wss[h][t:]
            snew_ref[b, h] = ss[h] * egl[:, h:h + 1] + upds[h]
        return carry

    lax.fori_loop(0, bb, per_batch, 0, unroll=2)

    gdn = gdn_ref[...]
    for h in range(nheads):
        hs = slice(h * dh, (h + 1) * dh)
        o = qs_s[:, hs] + _mm(attn_s[h], vn_s[:, hs])
        o_ref[:, hs] = _delta_out(o, z_ref[:, hs], gdn).astype(o_ref.dtype)


def _delta_sample(qkvz, uab, conv_state, s0, w_conv, alog, dtb, gdn, *, nb, t, bb):
    nheads = s0.shape[1]
    dh = DN_HEAD_DIM
    mix = nheads * dh
    c = bb * t
    qkvz3 = qkvz.reshape(nb, t, qkvz.shape[1])
    return pl.pallas_call(
        functools.partial(_delta_sample_body, bb=bb, t=t, nheads=nheads),
        grid=(nb // bb,),
        in_specs=[pl.BlockSpec((bb, t, 3 * mix), lambda i: (i, 0, 0)),
                  pl.BlockSpec((c, mix), lambda i: (i, 3)),
                  pl.BlockSpec((c, LANE), lambda i: (i, uab.shape[1] // LANE - 1)),
                  pl.BlockSpec((bb, CONV_WIDTH - 1, 3 * mix), lambda i: (i, 0, 0)),
                  pl.BlockSpec((bb, nheads, dh, dh), lambda i: (i, 0, 0, 0)),
                  pl.BlockSpec((CONV_WIDTH, 3 * mix), lambda i: (0, 0)),
                  pl.BlockSpec((1, LANE), lambda i: (0, 0)),
                  pl.BlockSpec((1, LANE), lambda i: (0, 0)),
                  pl.BlockSpec((1, dh), lambda i: (0, 0))],
        out_specs=[pl.BlockSpec((c, mix), lambda i: (i, 0)),
                   pl.BlockSpec((bb, CONV_WIDTH - 1, 3 * mix), lambda i: (i, 0, 0)),
                   pl.BlockSpec((bb, nheads, dh, dh), lambda i: (i, 0, 0, 0))],
        out_shape=[jax.ShapeDtypeStruct((nb * t, mix), BF16),
                   jax.ShapeDtypeStruct((nb, CONV_WIDTH - 1, 3 * mix), F32),
                   jax.ShapeDtypeStruct((nb, nheads, dh, dh), F32)],
        scratch_shapes=[pltpu.VMEM((bb, SUBLANE + t, 3 * mix), F32)]
                       + [pltpu.VMEM((c, mix), F32) for _ in range(6)]
                       + [pltpu.VMEM((nheads, c, c), F32), pltpu.VMEM((c, LANE), F32)],
        compiler_params=_params("parallel"),
        name="delta_sample",
    )(qkvz3, qkvz, uab, conv_state, s0, w_conv, alog, dtb, gdn)


def _s5_prep_body(lre_ref, lim_ref, ls_ref, bre_ref, bim_ref, wa_ref, abre_ref, abim_ref):
    lre = jnp.minimum(lre_ref[...], -1e-4)
    lim = lim_ref[...]
    dt = jnp.exp(ls_ref[...])
    mag = jnp.exp(lre * dt)
    ang = lim * dt
    abr = mag * jnp.cos(ang)
    abi = mag * jnp.sin(ang)
    nr = abr - 1.0
    ni = abi
    den = lre * lre + lim * lim
    cor = (nr * lre + ni * lim) / den
    coi = (ni * lre - nr * lim) / den
    bre = bre_ref[...]
    bim = bim_ref[...]
    shape2 = bre.shape[1:]
    row = lax.broadcasted_iota(jnp.int32, shape2, 0)
    col = lax.broadcasted_iota(jnp.int32, shape2, 1)
    diag = ((row // S5_GROUP) == (col // S5_STATE))[None]
    half = shape2[1]
    rows = shape2[0]
    wre = jnp.where(diag, cor * bre - coi * bim, 0.0)
    wim = jnp.where(diag, cor * bim + coi * bre, 0.0)
    wa_ref[:, 0:rows, 0:half] = wre.astype(BF16)
    wa_ref[:, 0:rows, half:2 * half] = wim.astype(BF16)
    wa_ref[:, rows:2 * rows, 0:half] = (abr * wre - abi * wim).astype(BF16)
    wa_ref[:, rows:2 * rows, half:2 * half] = (abr * wim + abi * wre).astype(BF16)
    abre_ref[...] = abr
    abim_ref[...] = abi


def _s5_prep(lam_re, lam_im, log_step, b_re, b_im):
    groups, states, gch = b_re.shape
    slabs = groups // S5_SLAB_GROUPS
    cols = S5_SLAB_STATES

    def row_param(p):
        return p.reshape(slabs, 1, cols)

    def tiled_b(b):
        bt = jnp.transpose(b, (0, 2, 1)).reshape(slabs, LANE, states)
        return jnp.tile(bt, (1, 1, S5_SLAB_GROUPS))

    ls = jnp.broadcast_to(log_step[:, None], (groups, states))
    wa, abre, abim = pl.pallas_call(
        _s5_prep_body,
        out_shape=[jax.ShapeDtypeStruct((slabs, 2 * LANE, 2 * cols), BF16),
                   jax.ShapeDtypeStruct((slabs, 1, cols), F32),
                   jax.ShapeDtypeStruct((slabs, 1, cols), F32)],
        compiler_params=pltpu.CompilerParams(vmem_limit_bytes=VMEM_LIMIT),
        name="s5prep",
    )(row_param(lam_re), row_param(lam_im), row_param(ls), tiled_b(b_re), tiled_b(b_im))
    nblk = groups * states // LANE
    return wa, abre.reshape(nblk, 1, LANE), abim.reshape(nblk, 1, LANE)


def _s5_out_weights(c):
    groups, gch, states = c.shape
    slabs = groups // S5_SLAB_GROUPS
    ct = jnp.transpose(c, (0, 2, 1)).reshape(slabs, S5_SLAB_STATES, gch)
    ct = jnp.tile(ct, (1, 1, S5_SLAB_GROUPS))
    row = jnp.arange(S5_SLAB_STATES)[:, None] // states
    col = jnp.arange(LANE)[None, :] // gch
    return jnp.where((row == col)[None], ct, 0.0).astype(BF16)


def _gelu_tanh(x):
    return x * (0.5 * (1.0 + jnp.tanh(math.sqrt(2.0 / math.pi) * (x + 0.044715 * (x * x * x)))))


S5_PIPELINE_PARTS = 4


def _time_major_perms(nb, lc):
    rows = nb * lc
    r = np.arange(rows)[:, None]
    c = np.arange(rows)[None, :]
    t_tm, b_tm = r // nb, r % nb
    to_tm = c == b_tm * lc + t_tm
    to_tm_prev = (c == b_tm * lc + t_tm - 1) & (t_tm % 2 == 1)
    to_bm = c == (r % lc) * nb + r // lc
    return tuple(jnp.asarray(m, BF16) for m in (to_tm, to_tm_prev, to_bm))


def _s5_body(u_ref, h0re_ref, h0im_ref, tm_ref, tmp_ref, bm_ref, wa_ref, abre_ref, abim_ref, wcre_ref, wcim_ref,
             dsk_ref, wglu_ref, gs5_ref, y_ref, hre_ref, him_ref, cre, cim, *xbufs, nb, lc):
    rows = nb * lc
    nblk = cre.shape[0]
    slabs = wa_ref.shape[0]
    per_slab = nblk // slabs
    half = per_slab * LANE
    steps_per_tile = SUBLANE // nb
    assert steps_per_tile in (1, 2) and lc % steps_per_tile == 0

    @pl.when(pl.program_id(1) == 0)
    def _():
        for j in range(nblk):
            if nb < SUBLANE:
                cre[j] = jnp.zeros((SUBLANE, LANE), F32)
                cim[j] = jnp.zeros((SUBLANE, LANE), F32)
            cre[j, SUBLANE - nb:SUBLANE, :] = h0re_ref[:, j * LANE:(j + 1) * LANE]
            cim[j, SUBLANE - nb:SUBLANE, :] = h0im_ref[:, j * LANE:(j + 1) * LANE]

    to_tm, to_bm = tm_ref[...], bm_ref[...]
    pieces = _split3(u_ref[...].reshape(rows, u_ref.shape[2]))
    u = sum(jnp.dot(to_tm, p, preferred_element_type=F32) for p in pieces)
    ub = u.astype(BF16)
    if steps_per_tile == 2:
        ub_prev = jnp.dot(tmp_ref[...], pieces[0], preferred_element_type=F32).astype(BF16)

    upper = lax.broadcasted_iota(jnp.int32, (SUBLANE, LANE), 0) >= nb
    parts = S5_PIPELINE_PARTS if rows % (S5_PIPELINE_PARTS * LANE) == 0 else 1
    prow = rows // parts

    def project_in(s, r):
        rs = slice(r * prow, (r + 1) * prow)
        sl = slice(s * LANE, (s + 1) * LANE)
        if steps_per_tile == 2:
            x = jnp.dot(jnp.concatenate([ub[rs, sl], ub_prev[rs, sl]], axis=1), wa_ref[s],
                        preferred_element_type=F32)
        else:
            x = jnp.dot(ub[rs, sl], wa_ref[s, 0:LANE, :], preferred_element_type=F32)
        for k in range(per_slab):
            xbufs[2 * s][k, rs, :] = x[:, k * LANE:(k + 1) * LANE]
            xbufs[2 * s + 1][k, rs, :] = x[:, half + k * LANE:half + (k + 1) * LANE]

    def recur(s, r, coef, carry):
        xre, xim = xbufs[2 * s], xbufs[2 * s + 1]
        for v in range(r * prow // SUBLANE, (r + 1) * prow // SUBLANE):
            at = slice(v * SUBLANE, (v + 1) * SUBLANE)
            for n in range(per_slab):
                hr, hi = carry[n]
                ar, ai = coef[n]
                if steps_per_tile == 2:
                    hr = jnp.where(upper, hr, pltpu.roll(hr, nb, 0))
                    hi = jnp.where(upper, hi, pltpu.roll(hi, nb, 0))
                nr = ar * hr - ai * hi + xre[n, at, :]
                ni = ar * hi + ai * hr + xim[n, at, :]
                xre[n, at, :] = nr
                xim[n, at, :] = ni
                carry[n] = (nr, ni)

    def project_out(s, r):
        rs = slice(r * prow, (r + 1) * prow)
        hre = jnp.concatenate([xbufs[2 * s][k, rs, :] for k in range(per_slab)], axis=1)
        him = jnp.concatenate([xbufs[2 * s + 1][k, rs, :] for k in range(per_slab)], axis=1)
        return _mm(hre, wcre_ref[s]) - _mm(him, wcim_ref[s])

    ychunks = [[None] * parts for _ in range(slabs)]
    for r in range(parts):
        project_in(0, r)
    for s in range(slabs):
        js = list(range(s * per_slab, (s + 1) * per_slab))
        coef, carry = [], []
        for j in js:
            ar = jnp.broadcast_to(abre_ref[j], (SUBLANE, LANE))
            ai = jnp.broadcast_to(abim_ref[j], (SUBLANE, LANE))
            if steps_per_tile == 2:
                coef.append((jnp.where(upper, ar * ar - ai * ai, ar), jnp.where(upper, 2.0 * ar * ai, ai)))
            else:
                coef.append((ar, ai))
            carry.append((cre[j], cim[j]))
        for r in range(parts):
            recur(s, r, coef, carry)
            if s + 1 < slabs:
                project_in(s + 1, r)
            if s >= 1:
                ychunks[s - 1][r] = project_out(s - 1, r)
        for n, j in enumerate(js):
            cre[j], cim[j] = carry[n]
    for r in range(parts):
        ychunks[slabs - 1][r] = project_out(slabs - 1, r)
    y = jnp.concatenate([jnp.concatenate(yc, axis=0) for yc in ychunks], axis=1) + dsk_ref[...] * u
    ab = _mm(_gelu_tanh(y), wglu_ref[...])
    n = y.shape[1]
    glu = ab[:, :n] * jax.nn.sigmoid(ab[:, n:])
    out = glu * lax.rsqrt(jnp.mean(glu * glu, axis=-1, keepdims=True) + EPS) * gs5_ref[...]
    out = jnp.dot(to_bm, out.astype(BF16), preferred_element_type=F32)
    y_ref[...] = out.reshape(y_ref.shape).astype(y_ref.dtype)

    @pl.when(pl.program_id(1) == pl.num_programs(1) - 1)
    def _():
        for j in range(nblk):
            hre_ref[:, j * LANE:(j + 1) * LANE] = cre[j, SUBLANE - nb:SUBLANE, :]
            him_ref[:, j * LANE:(j + 1) * LANE] = cim[j, SUBLANE - nb:SUBLANE, :]


def _s5(proj, u_col, h0re, h0im, wa, abre, abim, wcre, wcim, dskip, wglu, gs5, *, nb_total, t, nb, lc):
    mixb = dskip.shape[1]
    nstate = h0re.shape[1]
    nblk = nstate // LANE
    slabs = wa.shape[0]
    assert u_col % mixb == 0
    ublk = u_col // mixb
    uab3 = proj.reshape(nb_total, t, proj.shape[1])
    perms = _time_major_perms(nb, lc)
    full = lambda a: pl.BlockSpec(a.shape, lambda i, j: (0,) * a.ndim)
    y, hre, him = pl.pallas_call(
        functools.partial(_s5_body, nb=nb, lc=lc),
        grid=(nb_total // nb, t // lc),
        in_specs=[pl.BlockSpec((nb, lc, mixb), lambda i, j: (i, j, ublk)),
                  pl.BlockSpec((nb, nstate), lambda i, j: (i, 0)),
                  pl.BlockSpec((nb, nstate), lambda i, j: (i, 0))]
                 + [full(a) for a in perms]
                 + [full(wa), full(abre), full(abim), full(wcre), full(wcim), full(dskip), full(wglu), full(gs5)],
        out_specs=[pl.BlockSpec((nb, lc, mixb), lambda i, j: (i, j, 0)),
                   pl.BlockSpec((nb, nstate), lambda i, j: (i, 0)),
                   pl.BlockSpec((nb, nstate), lambda i, j: (i, 0))],
        out_shape=[jax.ShapeDtypeStruct((nb_total, t, mixb), F32),
                   jax.ShapeDtypeStruct((nb_total, nstate), F32),
                   jax.ShapeDtypeStruct((nb_total, nstate), F32)],
        scratch_shapes=[pltpu.VMEM((nblk, SUBLANE, LANE), F32),
                        pltpu.VMEM((nblk, SUBLANE, LANE), F32)]
                       + [pltpu.VMEM((nblk // slabs, nb * lc, LANE), F32) for _ in range(2 * slabs)],
        compiler_params=_params("parallel", "arbitrary"),
        name="s5",
    )(uab3, h0re, h0im, *perms, wa, abre, abim, wcre, wcim, dskip, wglu, gs5)
    return y.reshape(nb_total * t, mixb), hre, him


PART_ROWS = 256


def _row_parts(bb, tt):
    rows = bb * tt
    nparts = max(1, rows // PART_ROWS)
    if bb > 1:
        nparts = min(nparts, bb)
    prow = rows // nparts
    out = []
    for r in range(nparts):
        flat = slice(r * prow, (r + 1) * prow)
        if bb == 1:
            out.append(((slice(None), flat), slice(None), flat))
        else:
            bs = slice(r * bb // nparts, (r + 1) * bb // nparts)
            out.append(((bs, slice(None)), bs, flat))
    return out


def _outproj_body(o_ref, y_ref, wt_ref, wb_ref, x_ref, mod_ref, g_ref, out_ref, *, gate):
    bb, tt, _ = x_ref.shape
    parts = _row_parts(bb, tt)

    def project(r):
        flat = parts[r][2]
        return (jnp.dot(o_ref[flat, :], wt_ref[...], preferred_element_type=F32)
                + _mm(y_ref[flat, :], wb_ref[...]))

    def finish(r, mix):
        xi, mi, _ = parts[r]
        x = x_ref[xi]
        n = mix * lax.rsqrt(jnp.mean(mix * mix, axis=-1, keepdims=True) + EPS) * g_ref[...]
        out_ref[xi] = x + mod_ref[mi, gate:gate + 1, :] * n.reshape(x.shape)

    pending = project(0)
    for r in range(1, len(parts)):
        nxt = project(r)
        finish(r - 1, pending)
        pending = nxt
    finish(len(parts) - 1, pending)


def _outproj(o, y5, wtop, wbot, x3, mod, g, *, gate, rows):
    nb, t, d = x3.shape
    bb, tt = _token_blocks(nb, t, rows)
    nt = t // tt
    ka, kb = o.shape[1], y5.shape[1]
    mod, mod_shape, mod_index = _mod_block(mod, bb, nt)
    return pl.pallas_call(
        functools.partial(_outproj_body, gate=gate),
        grid=(nb * t // rows,),
        in_specs=[pl.BlockSpec((rows, ka), lambda i: (i, 0)),
                  pl.BlockSpec((rows, kb), lambda i: (i, 0)),
                  pl.BlockSpec((ka, d), lambda i: (0, 0)),
                  pl.BlockSpec((kb, d), lambda i: (0, 0)),
                  pl.BlockSpec((bb, tt, d), lambda i: (i // nt, i % nt, 0)),
                  pl.BlockSpec(mod_shape, mod_index),
                  pl.BlockSpec((1, d), lambda i: (0, 0))],
        out_specs=pl.BlockSpec((bb, tt, d), lambda i: (i // nt, i % nt, 0)),
        out_shape=jax.ShapeDtypeStruct((nb, t, d), F32),
        compiler_params=_params("parallel"),
        name="outproj",
    )(o, y5, wtop, wbot, x3, mod, g)


def _ffn_body(x_ref, mod_ref, gpre_ref, wg_ref, wu_ref, wd_ref, gpost_ref, out_ref, h_ref,
              *, shift, scale, gate):
    j = pl.program_id(1)
    last = pl.num_programs(1) - 1
    bb, tt, d = x_ref.shape
    parts = _row_parts(bb, tt)

    def swiglu(h):
        a = _silu(_mm(h, wg_ref[...])) * _mm(h, wu_ref[...])
        return _mm(a, wd_ref[...])

    @pl.when(j == 0)
    def _():
        for xi, mi, hr in parts:
            x = x_ref[xi]
            h = _modulated_norm(x, gpre_ref[...], mod_ref[mi, scale:scale + 1, :], mod_ref[mi, shift:shift + 1, :])
            h = h.reshape(hr.stop - hr.start, d).astype(BF16)
            h_ref[hr, :] = h
            out_ref[xi] = swiglu(h).reshape(x.shape)

    @pl.when((j > 0) & (j < last))
    def _():
        out_ref[...] += swiglu(h_ref[...]).reshape(out_ref.shape)

    @pl.when((j == last) & (j > 0))
    def _():
        for xi, mi, hr in parts:
            x = x_ref[xi]
            f = out_ref[xi] + swiglu(h_ref[hr, :]).reshape(x.shape)
            n = f * lax.rsqrt(jnp.mean(f * f, axis=-1, keepdims=True) + EPS) * gpost_ref[...]
            out_ref[xi] = x + mod_ref[mi, gate:gate + 1, :] * n


def _ffn(x3, mod, gpre, wg, wu, wd, gpost, *, shift, scale, gate, rows, tf):
    nb, t, d = x3.shape
    dff = wg.shape[1]
    assert dff // tf >= 2
    bb, tt = _token_blocks(nb, t, rows)
    nt = t // tt
    once = dict(pipeline_mode=pl.Buffered(1)) if bb > 1 else {}
    mod, mod_shape, mod_index = _mod_block(mod, bb, nt)
    return pl.pallas_call(
        functools.partial(_ffn_body, shift=shift, scale=scale, gate=gate),
        grid=(nb * t // rows, dff // tf),
        in_specs=[pl.BlockSpec((bb, tt, d), lambda i, j: (i // nt, i % nt, 0), **once),
                  pl.BlockSpec(mod_shape, lambda i, j: mod_index(i), **once),
                  pl.BlockSpec((1, d), lambda i, j: (0, 0)),
                  pl.BlockSpec((d, tf), lambda i, j: (0, j)),
                  pl.BlockSpec((d, tf), lambda i, j: (0, j)),
                  pl.BlockSpec((tf, d), lambda i, j: (j, 0)),
                  pl.BlockSpec((1, d), lambda i, j: (0, 0))],
        out_specs=pl.BlockSpec((bb, tt, d), lambda i, j: (i // nt, i % nt, 0)),
        out_shape=jax.ShapeDtypeStruct((nb, t, d), F32),
        scratch_shapes=[pltpu.VMEM((rows, d), BF16)],
        compiler_params=_params("parallel", "arbitrary"),
        name="ffn",
    )(x3, mod, gpre, wg, wu, wd, gpost)


ROWS = 512
FFN_ROWS = 1024
FFN_COLS = 256
PROMPT_CHUNK = 128
PROMPT_BASE = 16
SAMPLE_BATCH = 8
S5_PROMPT_STEPS = 128
S5_SAMPLE_BATCH = 8


def _layer_weights(w_ada, b_ada, g_pre_mix, g_post_mix, g_pre_ffn, g_post_ffn,
                   w_in, w_conv, a_log, dt_bias, g_dn_out,
                   lam_re, lam_im, log_step, b_re, b_im, c_re, c_im, d_skip,
                   w_glu, g_s5_out, w_out, w_gate, w_up, w_down):
    d = w_in.shape[0]
    nheads = a_log.shape[0]
    mix_a = nheads * DN_HEAD_DIM
    pad = lambda v: jnp.pad(v, (0, LANE - v.shape[0]))[None, :]
    wa, abre, abim = _s5_prep(lam_re, lam_im, log_step, b_re, b_im)
    return dict(
        w_ada=w_ada, b_ada=b_ada[None, :],
        g_pre_mix=g_pre_mix[None, :], g_post_mix=g_post_mix[None, :],
        g_pre_ffn=g_pre_ffn[None, :], g_post_ffn=g_post_ffn[None, :],
        w_proj=_win_prep(jnp.swapaxes(w_in, 0, 1), 4 * mix_a, 2 * nheads), u_col=4 * mix_a,
        w_conv=w_conv, alog=pad(a_log), dtb=pad(dt_bias), gdn=g_dn_out[None, :],
        wa=wa, abre=abre, abim=abim, wcre=_s5_out_weights(c_re), wcim=_s5_out_weights(c_im),
        dskip=d_skip.reshape(1, -1), wglu=w_glu.astype(BF16), gs5=g_s5_out[None, :],
        wtop=w_out[:mix_a].astype(BF16), wbot=w_out[mix_a:].astype(BF16),
        wg=w_gate, wu=w_up, wd=w_down,
    )


def _layer_group(x3, mod, conv_state, s0, h0re, h0im, lw, *, prompt):
    nb, t, d = x3.shape
    rows = min(ROWS, nb * t)
    proj = _nmm(x3, mod, lw["g_pre_mix"], lw["w_proj"], shift=0, scale=1, rows=rows if t >= rows else rows // 2)
    if prompt:
        o, cnew, snew = _delta_prompt(proj, proj, conv_state, s0, lw["w_conv"], lw["alog"], lw["dtb"], lw["gdn"],
                                      nb=nb, t=t, c=PROMPT_CHUNK, base=PROMPT_BASE)
        y5, hre, him = _s5(proj, lw["u_col"], h0re, h0im, lw["wa"], lw["abre"], lw["abim"], lw["wcre"], lw["wcim"],
                           lw["dskip"], lw["wglu"], lw["gs5"], nb_total=nb, t=t, nb=nb, lc=S5_PROMPT_STEPS)
    else:
        o, cnew, snew = _delta_sample(proj, proj, conv_state, s0, lw["w_conv"], lw["alog"], lw["dtb"], lw["gdn"],
                                      nb=nb, t=t, bb=SAMPLE_BATCH)
        y5, hre, him = _s5(proj, lw["u_col"], h0re, h0im, lw["wa"], lw["abre"], lw["abim"], lw["wcre"], lw["wcim"],
                           lw["dskip"], lw["wglu"], lw["gs5"], nb_total=nb, t=t, nb=S5_SAMPLE_BATCH, lc=t)
    x1 = _outproj(o, y5, lw["wtop"], lw["wbot"], x3, mod, lw["g_post_mix"], gate=2, rows=rows)
    x2 = _ffn(x1, mod, lw["g_pre_ffn"], lw["wg"], lw["wu"], lw["wd"], lw["g_post_ffn"],
              shift=3, scale=4, gate=5, rows=min(FFN_ROWS, nb * t), tf=FFN_COLS)
    ngroups = h0re.shape[1] // S5_STATE
    return x2, cnew, snew, hre.reshape(nb, ngroups, S5_STATE), him.reshape(nb, ngroups, S5_STATE)


def kernel(x_prompt, x_sample, c_prompt, c_sample, state_conv, state_delta, state_ssm_re, state_ssm_im, w_ada, b_ada, g_pre_mix, g_post_mix, g_pre_ffn, g_post_ffn, w_in, w_conv, a_log, dt_bias, g_dn_out, lam_re, lam_im, log_step, b_re, b_im, c_re, c_im, d_skip, w_glu, g_s5_out, w_out, w_gate, w_up, w_down):
    weights = (w_ada, b_ada, g_pre_mix, g_post_mix, g_pre_ffn, g_post_ffn,
               w_in, w_conv, a_log, dt_bias, g_dn_out,
               lam_re, lam_im, log_step, b_re, b_im, c_re, c_im, d_skip,
               w_glu, g_s5_out, w_out, w_gate, w_up, w_down)
    depth = w_ada.shape[0]
    bp, d = c_prompt.shape
    bs = c_sample.shape[0]
    dt_ = x_prompt.dtype
    nheads = state_delta.shape[2]
    nstate = state_ssm_re.shape[2] * state_ssm_re.shape[3]
    crows = -(-(bp + bs) // SUBLANE) * SUBLANE
    c_all = jnp.concatenate([c_sample, c_prompt, jnp.zeros((crows - bp - bs, d), dt_)], axis=0)
    yp, ys = x_prompt, x_sample
    outs = [[] for _ in range(8)]
    for l in range(depth):
        lw = _layer_weights(*[w[l] for w in weights])
        mod = _ada(c_all, lw["w_ada"], lw["b_ada"]).reshape(crows, 6, d)
        zc = jnp.zeros((bp, CONV_WIDTH - 1, state_conv.shape[3]), dt_)
        zd = jnp.zeros((bp,) + state_delta.shape[2:], dt_)
        zs = jnp.zeros((bp, nstate), dt_)
        yp, c1, d1, r1, i1 = _layer_group(yp, (mod, bs), zc, zd, zs, zs, lw, prompt=True)
        ys, c2, d2, r2, i2 = _layer_group(ys, (mod, 0), state_conv[l], state_delta[l],
                                          state_ssm_re[l].reshape(bs, nstate), state_ssm_im[l].reshape(bs, nstate),
                                          lw, prompt=False)
        for lst, v in zip(outs, (c1, d1, r1, i1, c2, d2, r2, i2)):
            lst.append(v)
    stacked = [jnp.stack(v) for v in outs]
    return (yp, ys, *stacked)
```

```python
import functools
import math

import jax
import jax.numpy as jnp
import numpy as np
from jax import lax
from jax.experimental import pallas as pl
from jax.experimental.pallas import tpu as pltpu

F32 = jnp.float32
BF16 = jnp.bfloat16
EPS = 1e-6
LANE = 128
SUBLANE = 8
V7X_VMEM_BYTES = 64 << 20
VMEM_LIMIT = V7X_VMEM_BYTES - (8 << 20)

DN_HEAD_DIM = 128
CONV_WIDTH = 4
S5_GROUP = 16
S5_STATE = 64
S5_SLAB_GROUPS = LANE // S5_GROUP
S5_SLAB_STATES = S5_SLAB_GROUPS * S5_STATE


def _params(*sem):
    return pltpu.CompilerParams(dimension_semantics=sem, vmem_limit_bytes=VMEM_LIMIT)


def _silu(x):
    return x * jax.nn.sigmoid(x)


def _mm(a, b):
    return jnp.dot(a.astype(BF16), b.astype(BF16), preferred_element_type=F32)


def _mm_nt(a, b):
    return lax.dot_general(a.astype(BF16), b.astype(BF16), (((1,), (1,)), ((), ())),
                           preferred_element_type=F32)


def _mm_tn(a, b):
    return lax.dot_general(a.astype(BF16), b.astype(BF16), (((0,), (0,)), ((), ())),
                           preferred_element_type=F32)


def _split3(x):
    x1 = x.astype(BF16)
    r1 = x - x1.astype(F32)
    x2 = r1.astype(BF16)
    x3 = (r1 - x2.astype(F32)).astype(BF16)
    return x1, x2, x3


def _ada_body(c_ref, w_ref, b_ref, o_ref):
    s = _silu(c_ref[...])
    o_ref[...] = _mm(s, w_ref[...]) + b_ref[...]


def _ada(c, w, b):
    rows, d = c.shape
    n = w.shape[1]
    tn = 1024
    return pl.pallas_call(
        _ada_body,
        grid=(n // tn,),
        in_specs=[pl.BlockSpec((rows, d), lambda j: (0, 0)),
                  pl.BlockSpec((d, tn), lambda j: (0, j)),
                  pl.BlockSpec((1, tn), lambda j: (0, j))],
        out_specs=pl.BlockSpec((rows, tn), lambda j: (0, j)),
        out_shape=jax.ShapeDtypeStruct((rows, n), F32),
        compiler_params=_params("arbitrary"),
        name="ada",
    )(c, w, b)


def _mod_block(mod, bb, nt):
    arr, first = mod
    assert first % bb == 0
    return arr, (bb, 6, arr.shape[2]), lambda i: (first // bb + i // nt, 0, 0)


def _token_blocks(nb, t, rows):
    if t >= rows:
        assert t % rows == 0
        return 1, rows
    assert rows % t == 0 and nb % (rows // t) == 0
    return rows // t, t


def _modulated_norm(x, g, scale, shift):
    y = x * lax.rsqrt(jnp.mean(x * x, axis=-1, keepdims=True) + EPS) * g
    return y * (1.0 + scale) + shift


def _win_prep_body(w_ref, o_ref, *, main, gates):
    n = w_ref.shape[0]
    rest = n - main - gates
    o_ref[0:main, :] = w_ref[0:main, :].astype(BF16)
    o_ref[main:main + rest, :] = w_ref[main + gates:n, :].astype(BF16)
    o_ref[main + rest:main + rest + gates, :] = w_ref[main:main + gates, :].astype(BF16)
    o_ref[main + rest + gates:, :] = jnp.zeros((LANE - gates, o_ref.shape[1]), BF16)


def _win_prep(w_in_t, main, gates):
    n, d = w_in_t.shape
    cb = min(512, d)
    nout = n - gates + LANE
    return pl.pallas_call(
        functools.partial(_win_prep_body, main=main, gates=gates),
        grid=(d // cb,),
        in_specs=[pl.BlockSpec((n, cb), lambda i: (0, i))],
        out_specs=pl.BlockSpec((nout, cb), lambda i: (0, i)),
        out_shape=jax.ShapeDtypeStruct((nout, d), BF16),
        compiler_params=_params("parallel"),
        name="winprep",
    )(w_in_t)


def _nmm_body(x_ref, mod_ref, g_ref, w_ref, o_ref, *, shift, scale):
    x = x_ref[...]
    bb, tt, d = x.shape
    h = _modulated_norm(x, g_ref[...], mod_ref[:, scale:scale + 1, :], mod_ref[:, shift:shift + 1, :])
    o_ref[...] = _mm_nt(h.reshape(bb * tt, d), w_ref[...])


def _nmm(x3, mod, g, w_t, *, shift, scale, rows):
    nb, t, d = x3.shape
    n = w_t.shape[0]
    bb, tt = _token_blocks(nb, t, rows)
    nt = t // tt
    mod, mod_shape, mod_index = _mod_block(mod, bb, nt)
    return pl.pallas_call(
        functools.partial(_nmm_body, shift=shift, scale=scale),
        grid=(nb * t // rows,),
        in_specs=[pl.BlockSpec((bb, tt, d), lambda i: (i // nt, i % nt, 0)),
                  pl.BlockSpec(mod_shape, mod_index),
                  pl.BlockSpec((1, d), lambda i: (0, 0)),
                  pl.BlockSpec((n, d), lambda i: (0, 0), pipeline_mode=pl.Buffered(1))],
        out_specs=pl.BlockSpec((rows, n), lambda i: (i, 0)),
        out_shape=jax.ShapeDtypeStruct((nb * t, n), F32),
        compiler_params=_params("parallel"),
        name="nmm",
    )(x3, mod, g, w_t)


def _seg_masks(c, seg):
    row = lax.broadcasted_iota(jnp.int32, (c, c), 0)
    col = lax.broadcasted_iota(jnp.int32, (c, c), 1)
    sh = int(math.log2(seg))
    same = (row >> sh) == (col >> sh)
    return row, col, same, same & (row >= col), same & (row > col)


def _tri_inverse_offdiag(lmats, row, col, base, seg):
    bsh = int(math.log2(base))
    blk = (row >> bsh) == (col >> bsh)
    l0s = [jnp.where(blk, l, 0.0) for l in lmats]
    pts = [-l for l in l0s]
    qs = [_mm(l, l) for l in l0s]
    k = 2
    while k < base:
        pts = [p + q + _mm(p, q) for p, q in zip(pts, qs)]
        k *= 2
        if k < base:
            qs = [_mm(q, q) for q in qs]
    m = base
    while m < seg:
        msh = int(math.log2(m))
        sel = ((row >> (msh + 1)) == (col >> (msh + 1))) & ((row >> msh) != (col >> msh))
        es = [jnp.where(sel, l, 0.0) for l in lmats]
        xs = [e + _mm(p, e) for p, e in zip(pts, es)]
        pts = [p - (x + _mm(x, p)) for p, x in zip(pts, xs)]
        m *= 2
    return pts


def _delta_gates(ab, alog, dtb, causal_bf, same_bf, nheads):
    lane = lax.broadcasted_iota(jnp.int32, ab.shape, 1)
    x = ab + dtb
    softplus = jnp.maximum(x, 0.0) + jnp.log1p(jnp.exp(-jnp.abs(x)))
    g = jnp.where(lane < nheads, -jnp.exp(alog) * softplus, 0.0)
    beta = jax.nn.sigmoid(ab)
    pieces = _split3(g)
    gcum = sum(jnp.dot(causal_bf, p, preferred_element_type=F32) for p in pieces)
    glast = sum(jnp.dot(same_bf, p, preferred_element_type=F32) for p in pieces)
    return gcum, beta, glast


def _decay_logits(gcum, nheads):
    c = gcum.shape[0]
    if c < LANE:
        gcum = jnp.concatenate([gcum, jnp.zeros((LANE - c, LANE), F32)], axis=0)
    return gcum.T


def _delta_intra(act, nheads, gates, dlog, masks, base, seg):
    gcum, beta, glast = gates
    row, col, _, causal, strict = masks
    c = act.shape[0]
    dh = DN_HEAD_DIM
    eg_all = jnp.exp(gcum)
    ed_all = jnp.exp(glast - gcum)
    ks, kqs_lhs, rhss, qgs, kdecs = [], [], [], [], []
    for h in range(nheads):
        q = act[:, h * dh:(h + 1) * dh]
        k = act[:, (nheads + h) * dh:(nheads + h + 1) * dh]
        v = act[:, (2 * nheads + h) * dh:(2 * nheads + h + 1) * dh]
        q = q * lax.rsqrt(jnp.sum(q * q, axis=-1, keepdims=True) + 1e-6) * (dh ** -0.5)
        k = k * lax.rsqrt(jnp.sum(k * k, axis=-1, keepdims=True) + 1e-6)
        b = beta[:, nheads + h:nheads + h + 1]
        eg = eg_all[:, h:h + 1]
        kb = k * b
        ks.append(k)
        kqs_lhs.append(jnp.concatenate([kb, q], axis=0))
        rhss.append(jnp.concatenate([v * b, kb * eg], axis=1))
        qgs.append(q * eg)
        kdecs.append(k * ed_all[:, h:h + 1])
    kqs = [_mm_nt(a, k) for a, k in zip(kqs_lhs, ks)]
    lmats, attns = [], []
    for h in range(nheads):
        dl = gcum[:, h:h + 1] - dlog[h:h + 1, 0:c]
        dec = jnp.where(causal, jnp.exp(jnp.where(causal, dl, 0.0)), 0.0)
        lmats.append(jnp.where(strict, kqs[h][:c] * dec, 0.0))
        attns.append(kqs[h][c:] * dec)
    tts = _tri_inverse_offdiag(lmats, row, col, base, seg)
    uws = [r + _mm(t, r) for t, r in zip(tts, rhss)]
    return [uw[:, :dh] for uw in uws], [uw[:, dh:] for uw in uws], qgs, kdecs, attns


def _delta_out(o, z, gdn):
    on = o * lax.rsqrt(jnp.mean(o * o, axis=-1, keepdims=True) + EPS) * gdn
    return on * _silu(z)


def _delta_prompt_body(qkv_ref, z_ref, ab_ref, cst_ref, s0_ref, wc_ref, alog_ref, dtb_ref, gdn_ref,
                       o_ref, cnew_ref, snew_ref, xc_ref, s_ref, *, c, nheads, base):
    dh = DN_HEAD_DIM
    halo = SUBLANE
    taps = CONV_WIDTH - 1

    @pl.when(pl.program_id(1) == 0)
    def _():
        xc_ref[0:halo, :] = jnp.zeros((halo, xc_ref.shape[1]), F32)
        xc_ref[halo - taps:halo, :] = cst_ref[0]
        s_ref[...] = s0_ref[0]

    xc_ref[halo:halo + c, :] = qkv_ref[...]
    conv = xc_ref[halo - taps:halo - taps + c, :] * wc_ref[0:1, :]
    for j in range(1, CONV_WIDTH):
        conv = conv + xc_ref[halo - taps + j:halo - taps + j + c, :] * wc_ref[j:j + 1, :]
    cnew_ref[0] = xc_ref[halo + c - taps:halo + c, :]
    xc_ref[0:halo, :] = xc_ref[c:c + halo, :]
    act = _silu(conv)

    masks = _seg_masks(c, c)
    causal_bf = jnp.where(masks[3], 1.0, 0.0).astype(BF16)
    same_bf = jnp.ones((c, c), BF16)
    gates = _delta_gates(ab_ref[...], alog_ref[...], dtb_ref[...], causal_bf, same_bf, nheads)
    dlog = _decay_logits(gates[0], nheads)
    eglast = jnp.exp(gates[2][0:1, :])
    gdn = gdn_ref[...]
    us, ws_, qgs, kdecs, attns = _delta_intra(act, nheads, gates, dlog, masks, base, c)
    heads = range(nheads)
    ss = [s_ref[h] for h in heads]
    wss = [_mm(jnp.concatenate([ws_[h], qgs[h]], axis=0), ss[h]) for h in heads]
    vnews = [us[h] - wss[h][:c] for h in heads]
    os_ = [wss[h][c:] + _mm(attns[h], vnews[h]) for h in heads]
    snews = [ss[h] * eglast[:, h:h + 1] + _mm_tn(kdecs[h], vnews[h]) for h in heads]
    for h in heads:
        s_ref[h] = snews[h]
        o_ref[:, h * dh:(h + 1) * dh] = _delta_out(os_[h], z_ref[:, h * dh:(h + 1) * dh], gdn).astype(o_ref.dtype)

    @pl.when(pl.program_id(1) == pl.num_programs(1) - 1)
    def _():
        snew_ref[0] = s_ref[...]


def _delta_prompt(qkvz, uab, conv_state, s0, w_conv, alog, dtb, gdn, *, nb, t, c, base):
    nheads = s0.shape[1]
    dh = DN_HEAD_DIM
    mix = nheads * dh
    nc = t // c
    return pl.pallas_call(
        functools.partial(_delta_prompt_body, c=c, nheads=nheads, base=base),
        grid=(nb, nc),
        in_specs=[pl.BlockSpec((c, 3 * mix), lambda b, i: (b * nc + i, 0)),
                  pl.BlockSpec((c, mix), lambda b, i: (b * nc + i, 3)),
                  pl.BlockSpec((c, LANE), lambda b, i: (b * nc + i, uab.shape[1] // LANE - 1)),
                  pl.BlockSpec((1, CONV_WIDTH - 1, 3 * mix), lambda b, i: (b, 0, 0)),
                  pl.BlockSpec((1, nheads, dh, dh), lambda b, i: (b, 0, 0, 0)),
                  pl.BlockSpec((CONV_WIDTH, 3 * mix), lambda b, i: (0, 0)),
                  pl.BlockSpec((1, LANE), lambda b, i: (0, 0)),
                  pl.BlockSpec((1, LANE), lambda b, i: (0, 0)),
                  pl.BlockSpec((1, dh), lambda b, i: (0, 0))],
        out_specs=[pl.BlockSpec((c, mix), lambda b, i: (b * nc + i, 0)),
                   pl.BlockSpec((1, CONV_WIDTH - 1, 3 * mix), lambda b, i: (b, 0, 0)),
                   pl.BlockSpec((1, nheads, dh, dh), lambda b, i: (b, 0, 0, 0))],
        out_shape=[jax.ShapeDtypeStruct((nb * t, mix), BF16),
                   jax.ShapeDtypeStruct((nb, CONV_WIDTH - 1, 3 * mix), F32),
                   jax.ShapeDtypeStruct((nb, nheads, dh, dh), F32)],
        scratch_shapes=[pltpu.VMEM((c + SUBLANE, 3 * mix), F32),
                        pltpu.VMEM((nheads, dh, dh), F32)],
        compiler_params=_params("parallel", "arbitrary"),
        name="delta_prompt",
    )(qkvz, qkvz, uab, conv_state, s0, w_conv, alog, dtb, gdn)


def _delta_sample_body(qkv_ref, z_ref, ab_ref, cst_ref, s0_ref, wc_ref, alog_ref, dtb_ref, gdn_ref,
                       o_ref, cnew_ref, snew_ref,
                       xs_ref, w_s, qg_s, u_s, kd_s, vn_s, qs_s, attn_s, egl_s, *, bb, t, nheads):
    dh = DN_HEAD_DIM
    c = bb * t
    halo = SUBLANE
    taps = CONV_WIDTH - 1

    xs_ref[:, 0:halo, :] = jnp.zeros((bb, halo, xs_ref.shape[2]), F32)
    xs_ref[:, halo - taps:halo, :] = cst_ref[...]
    xs_ref[:, halo:halo + t, :] = qkv_ref[...]
    conv = xs_ref[:, halo - taps:halo - taps + t, :] * wc_ref[0:1, :]
    for j in range(1, CONV_WIDTH):
        conv = conv + xs_ref[:, halo - taps + j:halo - taps + j + t, :] * wc_ref[j:j + 1, :]
    cnew_ref[...] = xs_ref[:, halo + t - taps:halo + t, :]
    act = _silu(conv).reshape(c, conv.shape[2])

    masks = _seg_masks(c, t)
    causal_bf = jnp.where(masks[3], 1.0, 0.0).astype(BF16)
    same_bf = jnp.where(masks[2], 1.0, 0.0).astype(BF16)
    gates = _delta_gates(ab_ref[...], alog_ref[...], dtb_ref[...], causal_bf, same_bf, nheads)
    dlog = _decay_logits(gates[0], nheads)
    egl_s[...] = jnp.exp(gates[2])
    us, ws_, qgs, kdecs, attns = _delta_intra(act, nheads, gates, dlog, masks, t, t)
    for h in range(nheads):
        hs = slice(h * dh, (h + 1) * dh)
        u_s[:, hs] = us[h]
        w_s[:, hs] = ws_[h]
        qg_s[:, hs] = qgs[h]
        kd_s[:, hs] = kdecs[h]
        attn_s[h] = attns[h]

    def per_batch(b, carry):
        r = pl.multiple_of(b * t, t)
        rows = pl.ds(r, t)
        heads = range(nheads)
        hsl = [slice(h * dh, (h + 1) * dh) for h in heads]
        ss = [s0_ref[b, h] for h in heads]
        wss = [_mm(jnp.concatenate([w_s[rows, hsl[h]], qg_s[rows, hsl[h]]], axis=0), ss[h]) for h in heads]
        vnews = [u_s[rows, hsl[h]] - wss[h][:t] for h in heads]
        upds = [_mm_tn(kd_s[rows, hsl[h]], vnews[h]) for h in heads]
        egl = egl_s[pl.ds(r, 1), :]
        for h in heads:
            vn_s[rows, hsl[h]] = vnews[h]
            qs_s[rows, hsl[h]] = wss[h][t:]
            snew_ref[b, h] = ss[h] * egl[:, h:h + 1] + upds[h]
        return carry

    lax.fori_loop(0, bb, per_batch, 0, unroll=2)

    gdn = gdn_ref[...]
    for h in range(nheads):
        hs = slice(h * dh, (h + 1) * dh)
        o = qs_s[:, hs] + _mm(attn_s[h], vn_s[:, hs])
        o_ref[:, hs] = _delta_out(o, z_ref[:, hs], gdn).astype(o_ref.dtype)


def _delta_sample(qkvz, uab, conv_state, s0, w_conv, alog, dtb, gdn, *, nb, t, bb):
    nheads = s0.shape[1]
    dh = DN_HEAD_DIM
    mix = nheads * dh
    c = bb * t
    qkvz3 = qkvz.reshape(nb, t, qkvz.shape[1])
    return pl.pallas_call(
        functools.partial(_delta_sample_body, bb=bb, t=t, nheads=nheads),
        grid=(nb // bb,),
        in_specs=[pl.BlockSpec((bb, t, 3 * mix), lambda i: (i, 0, 0)),
                  pl.BlockSpec((c, mix), lambda i: (i, 3)),
                  pl.BlockSpec((c, LANE), lambda i: (i, uab.shape[1] // LANE - 1)),
                  pl.BlockSpec((bb, CONV_WIDTH - 1, 3 * mix), lambda i: (i, 0, 0)),
                  pl.BlockSpec((bb, nheads, dh, dh), lambda i: (i, 0, 0, 0)),
                  pl.BlockSpec((CONV_WIDTH, 3 * mix), lambda i: (0, 0)),
                  pl.BlockSpec((1, LANE), lambda i: (0, 0)),
                  pl.BlockSpec((1, LANE), lambda i: (0, 0)),
                  pl.BlockSpec((1, dh), lambda i: (0, 0))],
        out_specs=[pl.BlockSpec((c, mix), lambda i: (i, 0)),
                   pl.BlockSpec((bb, CONV_WIDTH - 1, 3 * mix), lambda i: (i, 0, 0)),
                   pl.BlockSpec((bb, nheads, dh, dh), lambda i: (i, 0, 0, 0))],
        out_shape=[jax.ShapeDtypeStruct((nb * t, mix), BF16),
                   jax.ShapeDtypeStruct((nb, CONV_WIDTH - 1, 3 * mix), F32),
                   jax.ShapeDtypeStruct((nb, nheads, dh, dh), F32)],
        scratch_shapes=[pltpu.VMEM((bb, SUBLANE + t, 3 * mix), F32)]
                       + [pltpu.VMEM((c, mix), F32) for _ in range(6)]
                       + [pltpu.VMEM((nheads, c, c), F32), pltpu.VMEM((c, LANE), F32)],
        compiler_params=_params("parallel"),
        name="delta_sample",
    )(qkvz3, qkvz, uab, conv_state, s0, w_conv, alog, dtb, gdn)


def _s5_prep_body(lre_ref, lim_ref, ls_ref, bre_ref, bim_ref, wa_ref, abre_ref, abim_ref):
    lre = jnp.minimum(lre_ref[...], -1e-4)
    lim = lim_ref[...]
    dt = jnp.exp(ls_ref[...])
    mag = jnp.exp(lre * dt)
    ang = lim * dt
    abr = mag * jnp.cos(ang)
    abi = mag * jnp.sin(ang)
    nr = abr - 1.0
    ni = abi
    den = lre * lre + lim * lim
    cor = (nr * lre + ni * lim) / den
    coi = (ni * lre - nr * lim) / den
    bre = bre_ref[...]
    bim = bim_ref[...]
    shape2 = bre.shape[1:]
    row = lax.broadcasted_iota(jnp.int32, shape2, 0)
    col = lax.broadcasted_iota(jnp.int32, shape2, 1)
    diag = ((row // S5_GROUP) == (col // S5_STATE))[None]
    half = shape2[1]
    rows = shape2[0]
    wre = jnp.where(diag, cor * bre - coi * bim, 0.0)
    wim = jnp.where(diag, cor * bim + coi * bre, 0.0)
    wa_ref[:, 0:rows, 0:half] = wre.astype(BF16)
    wa_ref[:, 0:rows, half:2 * half] = wim.astype(BF16)
    wa_ref[:, rows:2 * rows, 0:half] = (abr * wre - abi * wim).astype(BF16)
    wa_ref[:, rows:2 * rows, half:2 * half] = (abr * wim + abi * wre).astype(BF16)
    abre_ref[...] = abr
    abim_ref[...] = abi


def _s5_prep(lam_re, lam_im, log_step, b_re, b_im):
    groups, states, gch = b_re.shape
    slabs = groups // S5_SLAB_GROUPS
    cols = S5_SLAB_STATES

    def row_param(p):
        return p.reshape(slabs, 1, cols)

    def tiled_b(b):
        bt = jnp.transpose(b, (0, 2, 1)).reshape(slabs, LANE, states)
        return jnp.tile(bt, (1, 1, S5_SLAB_GROUPS))

    ls = jnp.broadcast_to(log_step[:, None], (groups, states))
    wa, abre, abim = pl.pallas_call(
        _s5_prep_body,
        out_shape=[jax.ShapeDtypeStruct((slabs, 2 * LANE, 2 * cols), BF16),
                   jax.ShapeDtypeStruct((slabs, 1, cols), F32),
                   jax.ShapeDtypeStruct((slabs, 1, cols), F32)],
        compiler_params=pltpu.CompilerParams(vmem_limit_bytes=VMEM_LIMIT),
        name="s5prep",
    )(row_param(lam_re), row_param(lam_im), row_param(ls), tiled_b(b_re), tiled_b(b_im))
    nblk = groups * states // LANE
    return wa, abre.reshape(nblk, 1, LANE), abim.reshape(nblk, 1, LANE)


def _s5_out_weights(c):
    groups, gch, states = c.shape
    slabs = groups // S5_SLAB_GROUPS
    ct = jnp.transpose(c, (0, 2, 1)).reshape(slabs, S5_SLAB_STATES, gch)
    ct = jnp.tile(ct, (1, 1, S5_SLAB_GROUPS))
    row = jnp.arange(S5_SLAB_STATES)[:, None] // states
    col = jnp.arange(LANE)[None, :] // gch
    return jnp.where((row == col)[None], ct, 0.0).astype(BF16)


def _gelu_tanh(x):
    return x * (0.5 * (1.0 + jnp.tanh(math.sqrt(2.0 / math.pi) * (x + 0.044715 * (x * x * x)))))


S5_PIPELINE_PARTS = 4


def _time_major_perms(nb, lc):
    rows = nb * lc
    r = np.arange(rows)[:, None]
    c = np.arange(rows)[None, :]
    t_tm, b_tm = r // nb, r % nb
    to_tm = c == b_tm * lc + t_tm
    to_tm_prev = (c == b_tm * lc + t_tm - 1) & (t_tm % 2 == 1)
    to_bm = c == (r % lc) * nb + r // lc
    return tuple(jnp.asarray(m, BF16) for m in (to_tm, to_tm_prev, to_bm))


def _s5_body(u_ref, h0re_ref, h0im_ref, tm_ref, tmp_ref, bm_ref, wa_ref, abre_ref, abim_ref, wcre_ref, wcim_ref,
             dsk_ref, wglu_ref, gs5_ref, y_ref, hre_ref, him_ref, cre, cim, *xbufs, nb, lc):
    rows = nb * lc
    nblk = cre.shape[0]
    slabs = wa_ref.shape[0]
    per_slab = nblk // slabs
    half = per_slab * LANE
    steps_per_tile = SUBLANE // nb
    assert steps_per_tile in (1, 2) and lc % steps_per_tile == 0

    @pl.when(pl.program_id(1) == 0)
    def _():
        for j in range(nblk):
            if nb < SUBLANE:
                cre[j] = jnp.zeros((SUBLANE, LANE), F32)
                cim[j] = jnp.zeros((SUBLANE, LANE), F32)
            cre[j, SUBLANE - nb:SUBLANE, :] = h0re_ref[:, j * LANE:(j + 1) * LANE]
            cim[j, SUBLANE - nb:SUBLANE, :] = h0im_ref[:, j * LANE:(j + 1) * LANE]

    to_tm, to_bm = tm_ref[...], bm_ref[...]
    pieces = _split3(u_ref[...].reshape(rows, u_ref.shape[2]))[:2]
    u = sum(jnp.dot(to_tm, p, preferred_element_type=F32) for p in pieces)
    ub = u.astype(BF16)
    if steps_per_tile == 2:
        ub_prev = jnp.dot(tmp_ref[...], pieces[0], preferred_element_type=F32).astype(BF16)

    upper = lax.broadcasted_iota(jnp.int32, (SUBLANE, LANE), 0) >= nb
    parts = S5_PIPELINE_PARTS if rows % (S5_PIPELINE_PARTS * LANE) == 0 else 1
    prow = rows // parts

    def project_in(s, r):
        rs = slice(r * prow, (r + 1) * prow)
        sl = slice(s * LANE, (s + 1) * LANE)
        if steps_per_tile == 2:
            x = jnp.dot(jnp.concatenate([ub[rs, sl], ub_prev[rs, sl]], axis=1), wa_ref[s],
                        preferred_element_type=F32)
        else:
            x = jnp.dot(ub[rs, sl], wa_ref[s, 0:LANE, :], preferred_element_type=F32)
        for k in range(per_slab):
            xbufs[2 * s][k, rs, :] = x[:, k * LANE:(k + 1) * LANE]
            xbufs[2 * s + 1][k, rs, :] = x[:, half + k * LANE:half + (k + 1) * LANE]

    def recur(s, r, coef, carry):
        xre, xim = xbufs[2 * s], xbufs[2 * s + 1]
        for v in range(r * prow // SUBLANE, (r + 1) * prow // SUBLANE):
            at = slice(v * SUBLANE, (v + 1) * SUBLANE)
            for n in range(per_slab):
                hr, hi = carry[n]
                ar, ai = coef[n]
                if steps_per_tile == 2:
                    hr = jnp.where(upper, hr, pltpu.roll(hr, nb, 0))
                    hi = jnp.where(upper, hi, pltpu.roll(hi, nb, 0))
                nr = ar * hr - ai * hi + xre[n, at, :]
                ni = ar * hi + ai * hr + xim[n, at, :]
                xre[n, at, :] = nr
                xim[n, at, :] = ni
                carry[n] = (nr, ni)

    def project_out(s, r):
        rs = slice(r * prow, (r + 1) * prow)
        hre = jnp.concatenate([xbufs[2 * s][k, rs, :] for k in range(per_slab)], axis=1)
        him = jnp.concatenate([xbufs[2 * s + 1][k, rs, :] for k in range(per_slab)], axis=1)
        return _mm(hre, wcre_ref[s]) - _mm(him, wcim_ref[s])

    ychunks = [[None] * parts for _ in range(slabs)]
    for r in range(parts):
        project_in(0, r)
    for s in range(slabs):
        js = list(range(s * per_slab, (s + 1) * per_slab))
        coef, carry = [], []
        for j in js:
            ar = jnp.broadcast_to(abre_ref[j], (SUBLANE, LANE))
            ai = jnp.broadcast_to(abim_ref[j], (SUBLANE, LANE))
            if steps_per_tile == 2:
                coef.append((jnp.where(upper, ar * ar - ai * ai, ar), jnp.where(upper, 2.0 * ar * ai, ai)))
            else:
                coef.append((ar, ai))
            carry.append((cre[j], cim[j]))
        for r in range(parts):
            recur(s, r, coef, carry)
            if s + 1 < slabs:
                project_in(s + 1, r)
            if s >= 1:
                ychunks[s - 1][r] = project_out(s - 1, r)
        for n, j in enumerate(js):
            cre[j], cim[j] = carry[n]
    for r in range(parts):
        ychunks[slabs - 1][r] = project_out(slabs - 1, r)
    y = jnp.concatenate([jnp.concatenate(yc, axis=0) for yc in ychunks], axis=1) + dsk_ref[...] * u
    ab = _mm(_gelu_tanh(y), wglu_ref[...])
    n = y.shape[1]
    glu = ab[:, :n] * jax.nn.sigmoid(ab[:, n:])
    out = glu * lax.rsqrt(jnp.mean(glu * glu, axis=-1, keepdims=True) + EPS) * gs5_ref[...]
    out = jnp.dot(to_bm, out.astype(BF16), preferred_element_type=F32)
    y_ref[...] = out.reshape(y_ref.shape).astype(y_ref.dtype)

    @pl.when(pl.program_id(1) == pl.num_programs(1) - 1)
    def _():
        for j in range(nblk):
            hre_ref[:, j * LANE:(j + 1) * LANE] = cre[j, SUBLANE - nb:SUBLANE, :]
            him_ref[:, j * LANE:(j + 1) * LANE] = cim[j, SUBLANE - nb:SUBLANE, :]


def _s5(proj, u_col, h0re, h0im, wa, abre, abim, wcre, wcim, dskip, wglu, gs5, *, nb_total, t, nb, lc):
    mixb = dskip.shape[1]
    nstate = h0re.shape[1]
    nblk = nstate // LANE
    slabs = wa.shape[0]
    assert u_col % mixb == 0
    ublk = u_col // mixb
    uab3 = proj.reshape(nb_total, t, proj.shape[1])
    perms = _time_major_perms(nb, lc)
    full = lambda a: pl.BlockSpec(a.shape, lambda i, j: (0,) * a.ndim)
    y, hre, him = pl.pallas_call(
        functools.partial(_s5_body, nb=nb, lc=lc),
        grid=(nb_total // nb, t // lc),
        in_specs=[pl.BlockSpec((nb, lc, mixb), lambda i, j: (i, j, ublk)),
                  pl.BlockSpec((nb, nstate), lambda i, j: (i, 0)),
                  pl.BlockSpec((nb, nstate), lambda i, j: (i, 0))]
                 + [full(a) for a in perms]
                 + [full(wa), full(abre), full(abim), full(wcre), full(wcim), full(dskip), full(wglu), full(gs5)],
        out_specs=[pl.BlockSpec((nb, lc, mixb), lambda i, j: (i, j, 0)),
                   pl.BlockSpec((nb, nstate), lambda i, j: (i, 0)),
                   pl.BlockSpec((nb, nstate), lambda i, j: (i, 0))],
        out_shape=[jax.ShapeDtypeStruct((nb_total, t, mixb), F32),
                   jax.ShapeDtypeStruct((nb_total, nstate), F32),
                   jax.ShapeDtypeStruct((nb_total, nstate), F32)],
        scratch_shapes=[pltpu.VMEM((nblk, SUBLANE, LANE), F32),
                        pltpu.VMEM((nblk, SUBLANE, LANE), F32)]
                       + [pltpu.VMEM((nblk // slabs, nb * lc, LANE), F32) for _ in range(2 * slabs)],
        compiler_params=_params("parallel", "arbitrary"),
        name="s5",
    )(uab3, h0re, h0im, *perms, wa, abre, abim, wcre, wcim, dskip, wglu, gs5)
    return y.reshape(nb_total * t, mixb), hre, him


PART_ROWS = 256


def _row_parts(bb, tt):
    rows = bb * tt
    nparts = max(1, rows // PART_ROWS)
    if bb > 1:
        nparts = min(nparts, bb)
    prow = rows // nparts
    out = []
    for r in range(nparts):
        flat = slice(r * prow, (r + 1) * prow)
        if bb == 1:
            out.append(((slice(None), flat), slice(None), flat))
        else:
            bs = slice(r * bb // nparts, (r + 1) * bb // nparts)
            out.append(((bs, slice(None)), bs, flat))
    return out


def _outproj_body(o_ref, y_ref, wt_ref, wb_ref, x_ref, mod_ref, g_ref, out_ref, *, gate):
    mix = (jnp.dot(o_ref[...], wt_ref[...], preferred_element_type=F32)
           + _mm(y_ref[...], wb_ref[...]))
    n = mix * lax.rsqrt(jnp.mean(mix * mix, axis=-1, keepdims=True) + EPS) * g_ref[...]
    x = x_ref[...]
    out_ref[...] = x + mod_ref[:, gate:gate + 1, :] * n.reshape(x.shape)


def _outproj(o, y5, wtop, wbot, x3, mod, g, *, gate, rows):
    nb, t, d = x3.shape
    bb, tt = _token_blocks(nb, t, rows)
    nt = t // tt
    ka, kb = o.shape[1], y5.shape[1]
    mod, mod_shape, mod_index = _mod_block(mod, bb, nt)
    return pl.pallas_call(
        functools.partial(_outproj_body, gate=gate),
        grid=(nb * t // rows,),
        in_specs=[pl.BlockSpec((rows, ka), lambda i: (i, 0)),
                  pl.BlockSpec((rows, kb), lambda i: (i, 0)),
                  pl.BlockSpec((ka, d), lambda i: (0, 0)),
                  pl.BlockSpec((kb, d), lambda i: (0, 0)),
                  pl.BlockSpec((bb, tt, d), lambda i: (i // nt, i % nt, 0)),
                  pl.BlockSpec(mod_shape, mod_index),
                  pl.BlockSpec((1, d), lambda i: (0, 0))],
        out_specs=pl.BlockSpec((bb, tt, d), lambda i: (i // nt, i % nt, 0)),
        out_shape=jax.ShapeDtypeStruct((nb, t, d), F32),
        compiler_params=_params("parallel"),
        name="outproj",
    )(o, y5, wtop, wbot, x3, mod, g)


def _ffn_body(x_ref, mod_ref, gpre_ref, wg_ref, wu_ref, wd_ref, gpost_ref, out_ref, h_ref,
              *, shift, scale, gate):
    j = pl.program_id(1)
    last = pl.num_programs(1) - 1
    bb, tt, d = x_ref.shape
    parts = _row_parts(bb, tt)

    def swiglu(h):
        a = _silu(_mm(h, wg_ref[...])) * _mm(h, wu_ref[...])
        return _mm(a, wd_ref[...])

    @pl.when(j == 0)
    def _():
        for xi, mi, hr in parts:
            x = x_ref[xi]
            h = _modulated_norm(x, gpre_ref[...], mod_ref[mi, scale:scale + 1, :], mod_ref[mi, shift:shift + 1, :])
            h = h.reshape(hr.stop - hr.start, d).astype(BF16)
            h_ref[hr, :] = h
            out_ref[xi] = swiglu(h).reshape(x.shape)

    @pl.when((j > 0) & (j < last))
    def _():
        out_ref[...] += swiglu(h_ref[...]).reshape(out_ref.shape)

    @pl.when((j == last) & (j > 0))
    def _():
        for xi, mi, hr in parts:
            x = x_ref[xi]
            f = out_ref[xi] + swiglu(h_ref[hr, :]).reshape(x.shape)
            n = f * lax.rsqrt(jnp.mean(f * f, axis=-1, keepdims=True) + EPS) * gpost_ref[...]
            out_ref[xi] = x + mod_ref[mi, gate:gate + 1, :] * n


def _ffn(x3, mod, gpre, wg, wu, wd, gpost, *, shift, scale, gate, rows, tf):
    nb, t, d = x3.shape
    dff = wg.shape[1]
    assert dff // tf >= 2
    bb, tt = _token_blocks(nb, t, rows)
    nt = t // tt
    once = dict(pipeline_mode=pl.Buffered(1)) if bb > 1 else {}
    mod, mod_shape, mod_index = _mod_block(mod, bb, nt)
    return pl.pallas_call(
        functools.partial(_ffn_body, shift=shift, scale=scale, gate=gate),
        grid=(nb * t // rows, dff // tf),
        in_specs=[pl.BlockSpec((bb, tt, d), lambda i, j: (i // nt, i % nt, 0), **once),
                  pl.BlockSpec(mod_shape, lambda i, j: mod_index(i), **once),
                  pl.BlockSpec((1, d), lambda i, j: (0, 0)),
                  pl.BlockSpec((d, tf), lambda i, j: (0, j)),
                  pl.BlockSpec((d, tf), lambda i, j: (0, j)),
                  pl.BlockSpec((tf, d), lambda i, j: (j, 0)),
                  pl.BlockSpec((1, d), lambda i, j: (0, 0))],
        out_specs=pl.BlockSpec((bb, tt, d), lambda i, j: (i // nt, i % nt, 0)),
        out_shape=jax.ShapeDtypeStruct((nb, t, d), F32),
        scratch_shapes=[pltpu.VMEM((rows, d), BF16)],
        compiler_params=_params("parallel", "arbitrary"),
        name="ffn",
    )(x3, mod, gpre, wg, wu, wd, gpost)


ROWS = 512
FFN_ROWS = 1024
FFN_COLS = 256
PROMPT_CHUNK = 128
PROMPT_BASE = 16
SAMPLE_BATCH = 8
S5_PROMPT_STEPS = 128
S5_SAMPLE_BATCH = 8


def _layer_weights(w_ada, b_ada, g_pre_mix, g_post_mix, g_pre_ffn, g_post_ffn,
                   w_in, w_conv, a_log, dt_bias, g_dn_out,
                   lam_re, lam_im, log_step, b_re, b_im, c_re, c_im, d_skip,
                   w_glu, g_s5_out, w_out, w_gate, w_up, w_down):
    d = w_in.shape[0]
    nheads = a_log.shape[0]
    mix_a = nheads * DN_HEAD_DIM
    pad = lambda v: jnp.pad(v, (0, LANE - v.shape[0]))[None, :]
    wa, abre, abim = _s5_prep(lam_re, lam_im, log_step, b_re, b_im)
    return dict(
        w_ada=w_ada, b_ada=b_ada[None, :],
        g_pre_mix=g_pre_mix[None, :], g_post_mix=g_post_mix[None, :],
        g_pre_ffn=g_pre_ffn[None, :], g_post_ffn=g_post_ffn[None, :],
        w_proj=_win_prep(jnp.swapaxes(w_in, 0, 1), 4 * mix_a, 2 * nheads), u_col=4 * mix_a,
        w_conv=w_conv, alog=pad(a_log), dtb=pad(dt_bias), gdn=g_dn_out[None, :],
        wa=wa, abre=abre, abim=abim, wcre=_s5_out_weights(c_re), wcim=_s5_out_weights(c_im),
        dskip=d_skip.reshape(1, -1), wglu=w_glu.astype(BF16), gs5=g_s5_out[None, :],
        wtop=w_out[:mix_a].astype(BF16), wbot=w_out[mix_a:].astype(BF16),
        wg=w_gate, wu=w_up, wd=w_down,
    )


def _layer_group(x3, mod, conv_state, s0, h0re, h0im, lw, *, prompt):
    nb, t, d = x3.shape
    rows = min(ROWS, nb * t)
    proj = _nmm(x3, mod, lw["g_pre_mix"], lw["w_proj"], shift=0, scale=1, rows=rows if t >= rows else rows // 2)
    if prompt:
        o, cnew, snew = _delta_prompt(proj, proj, conv_state, s0, lw["w_conv"], lw["alog"], lw["dtb"], lw["gdn"],
                                      nb=nb, t=t, c=PROMPT_CHUNK, base=PROMPT_BASE)
        y5, hre, him = _s5(proj, lw["u_col"], h0re, h0im, lw["wa"], lw["abre"], lw["abim"], lw["wcre"], lw["wcim"],
                           lw["dskip"], lw["wglu"], lw["gs5"], nb_total=nb, t=t, nb=nb, lc=S5_PROMPT_STEPS)
    else:
        o, cnew, snew = _delta_sample(proj, proj, conv_state, s0, lw["w_conv"], lw["alog"], lw["dtb"], lw["gdn"],
                                      nb=nb, t=t, bb=SAMPLE_BATCH)
        y5, hre, him = _s5(proj, lw["u_col"], h0re, h0im, lw["wa"], lw["abre"], lw["abim"], lw["wcre"], lw["wcim"],
                           lw["dskip"], lw["wglu"], lw["gs5"], nb_total=nb, t=t, nb=S5_SAMPLE_BATCH, lc=t)
    x1 = _outproj(o, y5, lw["wtop"], lw["wbot"], x3, mod, lw["g_post_mix"], gate=2, rows=rows)
    x2 = _ffn(x1, mod, lw["g_pre_ffn"], lw["wg"], lw["wu"], lw["wd"], lw["g_post_ffn"],
              shift=3, scale=4, gate=5, rows=min(FFN_ROWS, nb * t), tf=FFN_COLS)
    ngroups = h0re.shape[1] // S5_STATE
    return x2, cnew, snew, hre.reshape(nb, ngroups, S5_STATE), him.reshape(nb, ngroups, S5_STATE)


def kernel(x_prompt, x_sample, c_prompt, c_sample, state_conv, state_delta, state_ssm_re, state_ssm_im, w_ada, b_ada, g_pre_mix, g_post_mix, g_pre_ffn, g_post_ffn, w_in, w_conv, a_log, dt_bias, g_dn_out, lam_re, lam_im, log_step, b_re, b_im, c_re, c_im, d_skip, w_glu, g_s5_out, w_out, w_gate, w_up, w_down):
    weights = (w_ada, b_ada, g_pre_mix, g_post_mix, g_pre_ffn, g_post_ffn,
               w_in, w_conv, a_log, dt_bias, g_dn_out,
               lam_re, lam_im, log_step, b_re, b_im, c_re, c_im, d_skip,
               w_glu, g_s5_out, w_out, w_gate, w_up, w_down)
    depth = w_ada.shape[0]
    bp, d = c_prompt.shape
    bs = c_sample.shape[0]
    dt_ = x_prompt.dtype
    nheads = state_delta.shape[2]
    nstate = state_ssm_re.shape[2] * state_ssm_re.shape[3]
    crows = -(-(bp + bs) // SUBLANE) * SUBLANE
    c_all = jnp.concatenate([c_sample, c_prompt, jnp.zeros((crows - bp - bs, d), dt_)], axis=0)
    yp, ys = x_prompt, x_sample
    outs = [[] for _ in range(8)]
    for l in range(depth):
        lw = _layer_weights(*[w[l] for w in weights])
        mod = _ada(c_all, lw["w_ada"], lw["b_ada"]).reshape(crows, 6, d)
        zc = jnp.zeros((bp, CONV_WIDTH - 1, state_conv.shape[3]), dt_)
        zd = jnp.zeros((bp,) + state_delta.shape[2:], dt_)
        zs = jnp.zeros((bp, nstate), dt_)
        yp, c1, d1, r1, i1 = _layer_group(yp, (mod, bs), zc, zd, zs, zs, lw, prompt=True)
        ys, c2, d2, r2, i2 = _layer_group(ys, (mod, 0), state_conv[l], state_delta[l],
                                          state_ssm_re[l].reshape(bs, nstate), state_ssm_im[l].reshape(bs, nstate),
                                          lw, prompt=False)
        for lst, v in zip(outs, (c1, d1, r1, i1, c2, d2, r2, i2)):
            lst.append(v)
    stacked = [jnp.stack(v) for v in outs]
    return (yp, ys, *stacked)
```

```python
import functools
import math

import jax
import jax.numpy as jnp
import numpy as np
from jax import lax
from jax.experimental import pallas as pl
from jax.experimental.pallas import tpu as pltpu

F32 = jnp.float32
BF16 = jnp.bfloat16
EPS = 1e-6
LANE = 128
SUBLANE = 8
V7X_VMEM_BYTES = 64 << 20
VMEM_LIMIT = V7X_VMEM_BYTES - (8 << 20)

DN_HEAD_DIM = 128
CONV_WIDTH = 4
S5_GROUP = 16
S5_STATE = 64
S5_SLAB_GROUPS = LANE // S5_GROUP
S5_SLAB_STATES = S5_SLAB_GROUPS * S5_STATE


def _params(*sem):
    return pltpu.CompilerParams(dimension_semantics=sem, vmem_limit_bytes=VMEM_LIMIT)


def _silu(x):
    return x * jax.nn.sigmoid(x)


def _mm(a, b):
    return jnp.dot(a.astype(BF16), b.astype(BF16), preferred_element_type=F32)


def _mm_nt(a, b):
    return lax.dot_general(a.astype(BF16), b.astype(BF16), (((1,), (1,)), ((), ())),
                           preferred_element_type=F32)


def _mm_tn(a, b):
    return lax.dot_general(a.astype(BF16), b.astype(BF16), (((0,), (0,)), ((), ())),
                           preferred_element_type=F32)


def _split3(x):
    x1 = x.astype(BF16)
    r1 = x - x1.astype(F32)
    x2 = r1.astype(BF16)
    x3 = (r1 - x2.astype(F32)).astype(BF16)
    return x1, x2, x3


def _ada_body(c_ref, w_ref, b_ref, o_ref):
    s = _silu(c_ref[...])
    o_ref[...] = _mm(s, w_ref[...]) + b_ref[...]


def _ada(c, w, b):
    rows, d = c.shape
    n = w.shape[1]
    tn = 1024
    return pl.pallas_call(
        _ada_body,
        grid=(n // tn,),
        in_specs=[pl.BlockSpec((rows, d), lambda j: (0, 0)),
                  pl.BlockSpec((d, tn), lambda j: (0, j)),
                  pl.BlockSpec((1, tn), lambda j: (0, j))],
        out_specs=pl.BlockSpec((rows, tn), lambda j: (0, j)),
        out_shape=jax.ShapeDtypeStruct((rows, n), F32),
        compiler_params=_params("arbitrary"),
        name="ada",
    )(c, w, b)


def _mod_block(mod, bb, nt):
    arr, first = mod
    assert first % bb == 0
    return arr, (bb, 6, arr.shape[2]), lambda i: (first // bb + i // nt, 0, 0)


def _token_blocks(nb, t, rows):
    if t >= rows:
        assert t % rows == 0
        return 1, rows
    assert rows % t == 0 and nb % (rows // t) == 0
    return rows // t, t


def _modulated_norm(x, g, scale, shift):
    y = x * lax.rsqrt(jnp.mean(x * x, axis=-1, keepdims=True) + EPS) * g
    return y * (1.0 + scale) + shift


def _win_prep_body(w_ref, o_ref, *, main, gates):
    n = w_ref.shape[0]
    rest = n - main - gates
    o_ref[0:main, :] = w_ref[0:main, :].astype(BF16)
    o_ref[main:main + rest, :] = w_ref[main + gates:n, :].astype(BF16)
    o_ref[main + rest:main + rest + gates, :] = w_ref[main:main + gates, :].astype(BF16)
    o_ref[main + rest + gates:, :] = jnp.zeros((LANE - gates, o_ref.shape[1]), BF16)


def _win_prep(w_in_t, main, gates):
    n, d = w_in_t.shape
    cb = min(512, d)
    nout = n - gates + LANE
    return pl.pallas_call(
        functools.partial(_win_prep_body, main=main, gates=gates),
        grid=(d // cb,),
        in_specs=[pl.BlockSpec((n, cb), lambda i: (0, i))],
        out_specs=pl.BlockSpec((nout, cb), lambda i: (0, i)),
        out_shape=jax.ShapeDtypeStruct((nout, d), BF16),
        compiler_params=_params("parallel"),
        name="winprep",
    )(w_in_t)


def _nmm_body(x_ref, mod_ref, g_ref, w_ref, o_ref, *, shift, scale):
    x = x_ref[...]
    bb, tt, d = x.shape
    h = _modulated_norm(x, g_ref[...], mod_ref[:, scale:scale + 1, :], mod_ref[:, shift:shift + 1, :])
    o_ref[...] = _mm_nt(h.reshape(bb * tt, d), w_ref[...])


def _nmm(x3, mod, g, w_t, *, shift, scale, rows):
    nb, t, d = x3.shape
    n = w_t.shape[0]
    bb, tt = _token_blocks(nb, t, rows)
    nt = t // tt
    mod, mod_shape, mod_index = _mod_block(mod, bb, nt)
    return pl.pallas_call(
        functools.partial(_nmm_body, shift=shift, scale=scale),
        grid=(nb * t // rows,),
        in_specs=[pl.BlockSpec((bb, tt, d), lambda i: (i // nt, i % nt, 0)),
                  pl.BlockSpec(mod_shape, mod_index),
                  pl.BlockSpec((1, d), lambda i: (0, 0)),
                  pl.BlockSpec((n, d), lambda i: (0, 0), pipeline_mode=pl.Buffered(1))],
        out_specs=pl.BlockSpec((rows, n), lambda i: (i, 0)),
        out_shape=jax.ShapeDtypeStruct((nb * t, n), F32),
        compiler_params=_params("parallel"),
        name="nmm",
    )(x3, mod, g, w_t)


def _seg_masks(c, seg):
    row = lax.broadcasted_iota(jnp.int32, (c, c), 0)
    col = lax.broadcasted_iota(jnp.int32, (c, c), 1)
    sh = int(math.log2(seg))
    same = (row >> sh) == (col >> sh)
    return row, col, same, same & (row >= col), same & (row > col)


def _tri_inverse_offdiag(lmats, row, col, base, seg):
    bsh = int(math.log2(base))
    blk = (row >> bsh) == (col >> bsh)
    l0s = [jnp.where(blk, l, 0.0) for l in lmats]
    pts = [-l for l in l0s]
    qs = [_mm(l, l) for l in l0s]
    k = 2
    while k < base:
        pts = [p + q + _mm(p, q) for p, q in zip(pts, qs)]
        k *= 2
        if k < base:
            qs = [_mm(q, q) for q in qs]
    m = base
    while m < seg:
        msh = int(math.log2(m))
        sel = ((row >> (msh + 1)) == (col >> (msh + 1))) & ((row >> msh) != (col >> msh))
        es = [jnp.where(sel, l, 0.0) for l in lmats]
        xs = [e + _mm(p, e) for p, e in zip(pts, es)]
        pts = [p - (x + _mm(x, p)) for p, x in zip(pts, xs)]
        m *= 2
    return pts


def _delta_gates(ab, alog, dtb, causal_bf, seg, nheads):
    lane = lax.broadcasted_iota(jnp.int32, ab.shape, 1)
    x = ab + dtb
    softplus = jnp.maximum(x, 0.0) + jnp.log1p(jnp.exp(-jnp.abs(x)))
    g = jnp.where(lane < nheads, -jnp.exp(alog) * softplus, 0.0)
    beta = jax.nn.sigmoid(ab)
    pieces = _split3(g)
    gcum = sum(jnp.dot(causal_bf, p, preferred_element_type=F32) for p in pieces)
    g3 = gcum.reshape(gcum.shape[0] // seg, seg, gcum.shape[1])
    glast = jnp.broadcast_to(g3[:, seg - 1:seg, :], g3.shape).reshape(gcum.shape)
    return gcum, beta, glast


def _decay_logits(gcum, nheads):
    c = gcum.shape[0]
    if c < LANE:
        gcum = jnp.concatenate([gcum, jnp.zeros((LANE - c, LANE), F32)], axis=0)
    return gcum.T


def _delta_intra(act, nheads, gates, dlog, masks, base, seg):
    gcum, beta, glast = gates
    row, col, _, causal, strict = masks
    c = act.shape[0]
    dh = DN_HEAD_DIM
    eg_all = jnp.exp(gcum)
    ed_all = jnp.exp(glast - gcum)
    ks, kqs_lhs, rhss, qgs, kdecs = [], [], [], [], []
    for h in range(nheads):
        q = act[:, h * dh:(h + 1) * dh]
        k = act[:, (nheads + h) * dh:(nheads + h + 1) * dh]
        v = act[:, (2 * nheads + h) * dh:(2 * nheads + h + 1) * dh]
        q = q * lax.rsqrt(jnp.sum(q * q, axis=-1, keepdims=True) + 1e-6) * (dh ** -0.5)
        k = k * lax.rsqrt(jnp.sum(k * k, axis=-1, keepdims=True) + 1e-6)
        b = beta[:, nheads + h:nheads + h + 1]
        eg = eg_all[:, h:h + 1]
        kb = k * b
        ks.append(k)
        kqs_lhs.append(jnp.concatenate([kb, q], axis=0))
        rhss.append(jnp.concatenate([v * b, kb * eg], axis=1))
        qgs.append(q * eg)
        kdecs.append(k * ed_all[:, h:h + 1])
    kqs = [_mm_nt(a, k) for a, k in zip(kqs_lhs, ks)]
    lmats, attns = [], []
    for h in range(nheads):
        dl = gcum[:, h:h + 1] - dlog[h:h + 1, 0:c]
        dec = jnp.where(causal, jnp.exp(jnp.where(causal, dl, 0.0)), 0.0)
        lmats.append(jnp.where(strict, kqs[h][:c] * dec, 0.0))
        attns.append(kqs[h][c:] * dec)
    tts = _tri_inverse_offdiag(lmats, row, col, base, seg)
    uws = [r + _mm(t, r) for t, r in zip(tts, rhss)]
    return [uw[:, :dh] for uw in uws], [uw[:, dh:] for uw in uws], qgs, kdecs, attns


def _delta_out(o, z, gdn):
    on = o * lax.rsqrt(jnp.mean(o * o, axis=-1, keepdims=True) + EPS) * gdn
    return on * _silu(z)


def _delta_prompt_body(qkv_ref, z_ref, ab_ref, cst_ref, s0_ref, wc_ref, alog_ref, dtb_ref, gdn_ref,
                       o_ref, cnew_ref, snew_ref, xc_ref, s_ref, *, c, nheads, base):
    dh = DN_HEAD_DIM
    halo = SUBLANE
    taps = CONV_WIDTH - 1

    @pl.when(pl.program_id(1) == 0)
    def _():
        xc_ref[0:halo, :] = jnp.zeros((halo, xc_ref.shape[1]), F32)
        xc_ref[halo - taps:halo, :] = cst_ref[0]
        s_ref[...] = s0_ref[0]

    xc_ref[halo:halo + c, :] = qkv_ref[...]
    conv = xc_ref[halo - taps:halo - taps + c, :] * wc_ref[0:1, :]
    for j in range(1, CONV_WIDTH):
        conv = conv + xc_ref[halo - taps + j:halo - taps + j + c, :] * wc_ref[j:j + 1, :]
    cnew_ref[0] = xc_ref[halo + c - taps:halo + c, :]
    xc_ref[0:halo, :] = xc_ref[c:c + halo, :]
    act = _silu(conv)

    masks = _seg_masks(c, c)
    causal_bf = jnp.where(masks[3], 1.0, 0.0).astype(BF16)
    gates = _delta_gates(ab_ref[...], alog_ref[...], dtb_ref[...], causal_bf, c, nheads)
    dlog = _decay_logits(gates[0], nheads)
    eglast = jnp.exp(gates[2][0:1, :])
    gdn = gdn_ref[...]
    us, ws_, qgs, kdecs, attns = _delta_intra(act, nheads, gates, dlog, masks, base, c)
    heads = range(nheads)
    ss = [s_ref[h] for h in heads]
    wss = [_mm(jnp.concatenate([ws_[h], qgs[h]], axis=0), ss[h]) for h in heads]
    vnews = [us[h] - wss[h][:c] for h in heads]
    os_ = [wss[h][c:] + _mm(attns[h], vnews[h]) for h in heads]
    snews = [ss[h] * eglast[:, h:h + 1] + _mm_tn(kdecs[h], vnews[h]) for h in heads]
    for h in heads:
        s_ref[h] = snews[h]
        o_ref[:, h * dh:(h + 1) * dh] = _delta_out(os_[h], z_ref[:, h * dh:(h + 1) * dh], gdn).astype(o_ref.dtype)

    @pl.when(pl.program_id(1) == pl.num_programs(1) - 1)
    def _():
        snew_ref[0] = s_ref[...]


def _delta_prompt(qkvz, uab, conv_state, s0, w_conv, alog, dtb, gdn, *, nb, t, c, base):
    nheads = s0.shape[1]
    dh = DN_HEAD_DIM
    mix = nheads * dh
    nc = t // c
    return pl.pallas_call(
        functools.partial(_delta_prompt_body, c=c, nheads=nheads, base=base),
        grid=(nb, nc),
        in_specs=[pl.BlockSpec((c, 3 * mix), lambda b, i: (b * nc + i, 0)),
                  pl.BlockSpec((c, mix), lambda b, i: (b * nc + i, 3)),
                  pl.BlockSpec((c, LANE), lambda b, i: (b * nc + i, uab.shape[1] // LANE - 1)),
                  pl.BlockSpec((1, CONV_WIDTH - 1, 3 * mix), lambda b, i: (b, 0, 0)),
                  pl.BlockSpec((1, nheads, dh, dh), lambda b, i: (b, 0, 0, 0)),
                  pl.BlockSpec((CONV_WIDTH, 3 * mix), lambda b, i: (0, 0)),
                  pl.BlockSpec((1, LANE), lambda b, i: (0, 0)),
                  pl.BlockSpec((1, LANE), lambda b, i: (0, 0)),
                  pl.BlockSpec((1, dh), lambda b, i: (0, 0))],
        out_specs=[pl.BlockSpec((c, mix), lambda b, i: (b * nc + i, 0)),
                   pl.BlockSpec((1, CONV_WIDTH - 1, 3 * mix), lambda b, i: (b, 0, 0)),
                   pl.BlockSpec((1, nheads, dh, dh), lambda b, i: (b, 0, 0, 0))],
        out_shape=[jax.ShapeDtypeStruct((nb * t, mix), BF16),
                   jax.ShapeDtypeStruct((nb, CONV_WIDTH - 1, 3 * mix), F32),
                   jax.ShapeDtypeStruct((nb, nheads, dh, dh), F32)],
        scratch_shapes=[pltpu.VMEM((c + SUBLANE, 3 * mix), F32),
                        pltpu.VMEM((nheads, dh, dh), F32)],
        compiler_params=_params("parallel", "arbitrary"),
        name="delta_prompt",
    )(qkvz, qkvz, uab, conv_state, s0, w_conv, alog, dtb, gdn)


def _delta_sample_body(qkv_ref, z_ref, ab_ref, cst_ref, s0_ref, wc_ref, alog_ref, dtb_ref, gdn_ref,
                       o_ref, cnew_ref, snew_ref,
                       xs_ref, w_s, qg_s, u_s, kd_s, vn_s, qs_s, attn_s, egl_s, *, bb, t, nheads):
    dh = DN_HEAD_DIM
    c = bb * t
    halo = SUBLANE
    taps = CONV_WIDTH - 1

    xs_ref[:, 0:halo, :] = jnp.zeros((bb, halo, xs_ref.shape[2]), F32)
    xs_ref[:, halo - taps:halo, :] = cst_ref[...]
    xs_ref[:, halo:halo + t, :] = qkv_ref[...]
    conv = xs_ref[:, halo - taps:halo - taps + t, :] * wc_ref[0:1, :]
    for j in range(1, CONV_WIDTH):
        conv = conv + xs_ref[:, halo - taps + j:halo - taps + j + t, :] * wc_ref[j:j + 1, :]
    cnew_ref[...] = xs_ref[:, halo + t - taps:halo + t, :]
    act = _silu(conv).reshape(c, conv.shape[2])

    masks = _seg_masks(c, t)
    causal_bf = jnp.where(masks[3], 1.0, 0.0).astype(BF16)
    gates = _delta_gates(ab_ref[...], alog_ref[...], dtb_ref[...], causal_bf, t, nheads)
    dlog = _decay_logits(gates[0], nheads)
    egl_s[...] = jnp.exp(gates[2])
    us, ws_, qgs, kdecs, attns = _delta_intra(act, nheads, gates, dlog, masks, t, t)
    for h in range(nheads):
        hs = slice(h * dh, (h + 1) * dh)
        u_s[:, hs] = us[h]
        w_s[:, hs] = ws_[h]
        qg_s[:, hs] = qgs[h]
        kd_s[:, hs] = kdecs[h]
        attn_s[h] = attns[h]

    def per_batch(b, carry):
        r = pl.multiple_of(b * t, t)
        rows = pl.ds(r, t)
        heads = range(nheads)
        hsl = [slice(h * dh, (h + 1) * dh) for h in heads]
        ss = [s0_ref[b, h] for h in heads]
        wss = [_mm(jnp.concatenate([w_s[rows, hsl[h]], qg_s[rows, hsl[h]]], axis=0), ss[h]) for h in heads]
        vnews = [u_s[rows, hsl[h]] - wss[h][:t] for h in heads]
        upds = [_mm_tn(kd_s[rows, hsl[h]], vnews[h]) for h in heads]
        egl = egl_s[pl.ds(r, 1), :]
        for h in heads:
            vn_s[rows, hsl[h]] = vnews[h]
            qs_s[rows, hsl[h]] = wss[h][t:]
            snew_ref[b, h] = ss[h] * egl[:, h:h + 1] + upds[h]
        return carry

    lax.fori_loop(0, bb, per_batch, 0, unroll=2)

    gdn = gdn_ref[...]
    for h in range(nheads):
        hs = slice(h * dh, (h + 1) * dh)
        o = qs_s[:, hs] + _mm(attn_s[h], vn_s[:, hs])
        o_ref[:, hs] = _delta_out(o, z_ref[:, hs], gdn).astype(o_ref.dtype)


def _delta_sample(qkvz, uab, conv_state, s0, w_conv, alog, dtb, gdn, *, nb, t, bb):
    nheads = s0.shape[1]
    dh = DN_HEAD_DIM
    mix = nheads * dh
    c = bb * t
    qkvz3 = qkvz.reshape(nb, t, qkvz.shape[1])
    return pl.pallas_call(
        functools.partial(_delta_sample_body, bb=bb, t=t, nheads=nheads),
        grid=(nb // bb,),
        in_specs=[pl.BlockSpec((bb, t, 3 * mix), lambda i: (i, 0, 0)),
                  pl.BlockSpec((c, mix), lambda i: (i, 3)),
                  pl.BlockSpec((c, LANE), lambda i: (i, uab.shape[1] // LANE - 1)),
                  pl.BlockSpec((bb, CONV_WIDTH - 1, 3 * mix), lambda i: (i, 0, 0)),
                  pl.BlockSpec((bb, nheads, dh, dh), lambda i: (i, 0, 0, 0)),
                  pl.BlockSpec((CONV_WIDTH, 3 * mix), lambda i: (0, 0)),
                  pl.BlockSpec((1, LANE), lambda i: (0, 0)),
                  pl.BlockSpec((1, LANE), lambda i: (0, 0)),
                  pl.BlockSpec((1, dh), lambda i: (0, 0))],
        out_specs=[pl.BlockSpec((c, mix), lambda i: (i, 0)),
                   pl.BlockSpec((bb, CONV_WIDTH - 1, 3 * mix), lambda i: (i, 0, 0)),
                   pl.BlockSpec((bb, nheads, dh, dh), lambda i: (i, 0, 0, 0))],
        out_shape=[jax.ShapeDtypeStruct((nb * t, mix), BF16),
                   jax.ShapeDtypeStruct((nb, CONV_WIDTH - 1, 3 * mix), F32),
                   jax.ShapeDtypeStruct((nb, nheads, dh, dh), F32)],
        scratch_shapes=[pltpu.VMEM((bb, SUBLANE + t, 3 * mix), F32)]
                       + [pltpu.VMEM((c, mix), F32) for _ in range(6)]
                       + [pltpu.VMEM((nheads, c, c), F32), pltpu.VMEM((c, LANE), F32)],
        compiler_params=_params("parallel"),
        name="delta_sample",
    )(qkvz3, qkvz, uab, conv_state, s0, w_conv, alog, dtb, gdn)


def _s5_prep_body(lre_ref, lim_ref, ls_ref, bre_ref, bim_ref, wa_ref, abre_ref, abim_ref):
    lre = jnp.minimum(lre_ref[...], -1e-4)
    lim = lim_ref[...]
    dt = jnp.exp(ls_ref[...])
    mag = jnp.exp(lre * dt)
    ang = lim * dt
    abr = mag * jnp.cos(ang)
    abi = mag * jnp.sin(ang)
    nr = abr - 1.0
    ni = abi
    den = lre * lre + lim * lim
    cor = (nr * lre + ni * lim) / den
    coi = (ni * lre - nr * lim) / den
    bre = bre_ref[...]
    bim = bim_ref[...]
    shape2 = bre.shape[1:]
    row = lax.broadcasted_iota(jnp.int32, shape2, 0)
    col = lax.broadcasted_iota(jnp.int32, shape2, 1)
    diag = ((row // S5_GROUP) == (col // S5_STATE))[None]
    half = shape2[1]
    rows = shape2[0]
    wre = jnp.where(diag, cor * bre - coi * bim, 0.0)
    wim = jnp.where(diag, cor * bim + coi * bre, 0.0)
    wa_ref[:, 0:rows, 0:half] = wre.astype(BF16)
    wa_ref[:, 0:rows, half:2 * half] = wim.astype(BF16)
    wa_ref[:, rows:2 * rows, 0:half] = (abr * wre - abi * wim).astype(BF16)
    wa_ref[:, rows:2 * rows, half:2 * half] = (abr * wim + abi * wre).astype(BF16)
    abre_ref[...] = abr
    abim_ref[...] = abi


def _s5_prep(lam_re, lam_im, log_step, b_re, b_im):
    groups, states, gch = b_re.shape
    slabs = groups // S5_SLAB_GROUPS
    cols = S5_SLAB_STATES

    def row_param(p):
        return p.reshape(slabs, 1, cols)

    def tiled_b(b):
        bt = jnp.transpose(b, (0, 2, 1)).reshape(slabs, LANE, states)
        return jnp.tile(bt, (1, 1, S5_SLAB_GROUPS))

    ls = jnp.broadcast_to(log_step[:, None], (groups, states))
    wa, abre, abim = pl.pallas_call(
        _s5_prep_body,
        out_shape=[jax.ShapeDtypeStruct((slabs, 2 * LANE, 2 * cols), BF16),
                   jax.ShapeDtypeStruct((slabs, 1, cols), F32),
                   jax.ShapeDtypeStruct((slabs, 1, cols), F32)],
        compiler_params=pltpu.CompilerParams(vmem_limit_bytes=VMEM_LIMIT),
        name="s5prep",
    )(row_param(lam_re), row_param(lam_im), row_param(ls), tiled_b(b_re), tiled_b(b_im))
    nblk = groups * states // LANE
    return wa, abre.reshape(nblk, 1, LANE), abim.reshape(nblk, 1, LANE)


def _s5_out_weights(c):
    groups, gch, states = c.shape
    slabs = groups // S5_SLAB_GROUPS
    ct = jnp.transpose(c, (0, 2, 1)).reshape(slabs, S5_SLAB_STATES, gch)
    ct = jnp.tile(ct, (1, 1, S5_SLAB_GROUPS))
    row = jnp.arange(S5_SLAB_STATES)[:, None] // states
    col = jnp.arange(LANE)[None, :] // gch
    return jnp.where((row == col)[None], ct, 0.0).astype(BF16)


def _gelu_tanh(x):
    return x * (0.5 * (1.0 + jnp.tanh(math.sqrt(2.0 / math.pi) * (x + 0.044715 * (x * x * x)))))


S5_PIPELINE_PARTS = 4


def _time_major_perms(nb, lc, nseg):
    seg_rows = nb * lc
    rows = nseg * seg_rows
    r = np.arange(rows)[:, None]
    c = np.arange(rows)[None, :]
    base, loc = r // seg_rows * seg_rows, r % seg_rows
    t_tm, b_tm = loc // nb, loc % nb
    to_tm = c == base + b_tm * lc + t_tm
    to_tm_prev = (c == base + b_tm * lc + t_tm - 1) & (t_tm % 2 == 1)
    to_bm = c == base + (loc % lc) * nb + loc // lc
    return tuple(jnp.asarray(m, BF16) for m in (to_tm, to_tm_prev, to_bm))


def _s5_body(u_ref, h0re_ref, h0im_ref, tm_ref, tmp_ref, bm_ref, wa_ref, abre_ref, abim_ref, wcre_ref, wcim_ref,
             dsk_ref, wglu_ref, gs5_ref, y_ref, hre_ref, him_ref, cre, cim, *xbufs, nb, lc, nseg):
    seg_rows = nb * lc
    rows = nseg * seg_rows
    nblk = cre.shape[0]
    slabs = wa_ref.shape[0]
    per_slab = nblk // slabs
    half = per_slab * LANE
    steps_per_tile = SUBLANE // nb
    assert steps_per_tile in (1, 2) and lc % steps_per_tile == 0
    assert nseg == 1 or nb == SUBLANE

    if nseg == 1:
        @pl.when(pl.program_id(1) == 0)
        def _():
            for j in range(nblk):
                if nb < SUBLANE:
                    cre[j] = jnp.zeros((SUBLANE, LANE), F32)
                    cim[j] = jnp.zeros((SUBLANE, LANE), F32)
                cre[j, SUBLANE - nb:SUBLANE, :] = h0re_ref[:, j * LANE:(j + 1) * LANE]
                cim[j, SUBLANE - nb:SUBLANE, :] = h0im_ref[:, j * LANE:(j + 1) * LANE]

    to_tm, to_bm = tm_ref[...], bm_ref[...]
    pieces = _split3(u_ref[...].reshape(rows, u_ref.shape[2]))[:2]
    u = sum(jnp.dot(to_tm, p, preferred_element_type=F32) for p in pieces)
    ub = u.astype(BF16)
    if steps_per_tile == 2:
        ub_prev = jnp.dot(tmp_ref[...], pieces[0], preferred_element_type=F32).astype(BF16)

    upper = lax.broadcasted_iota(jnp.int32, (SUBLANE, LANE), 0) >= nb
    parts = S5_PIPELINE_PARTS if rows % (S5_PIPELINE_PARTS * LANE) == 0 else 1
    prow = rows // parts

    def project_in(s, r):
        rs = slice(r * prow, (r + 1) * prow)
        sl = slice(s * LANE, (s + 1) * LANE)
        if steps_per_tile == 2:
            x = jnp.dot(jnp.concatenate([ub[rs, sl], ub_prev[rs, sl]], axis=1), wa_ref[s],
                        preferred_element_type=F32)
        else:
            x = jnp.dot(ub[rs, sl], wa_ref[s, 0:LANE, :], preferred_element_type=F32)
        for k in range(per_slab):
            xbufs[2 * s][k, rs, :] = x[:, k * LANE:(k + 1) * LANE]
            xbufs[2 * s + 1][k, rs, :] = x[:, half + k * LANE:half + (k + 1) * LANE]

    seg_tiles = seg_rows // SUBLANE

    def recur(s, r, coef, carry):
        xre, xim = xbufs[2 * s], xbufs[2 * s + 1]
        for v in range(r * prow // SUBLANE, (r + 1) * prow // SUBLANE):
            at = slice(v * SUBLANE, (v + 1) * SUBLANE)
            seg_rows_sl = slice(v // seg_tiles * nb, (v // seg_tiles + 1) * nb)
            if nseg > 1 and v % seg_tiles == 0:
                for n in range(per_slab):
                    cols = slice((s * per_slab + n) * LANE, (s * per_slab + n + 1) * LANE)
                    carry[n] = (h0re_ref[seg_rows_sl, cols], h0im_ref[seg_rows_sl, cols])
            for n in range(per_slab):
                hr, hi = carry[n]
                ar, ai = coef[n]
                if steps_per_tile == 2:
                    hr = jnp.where(upper, hr, pltpu.roll(hr, nb, 0))
                    hi = jnp.where(upper, hi, pltpu.roll(hi, nb, 0))
                nr = ar * hr - ai * hi + xre[n, at, :]
                ni = ar * hi + ai * hr + xim[n, at, :]
                xre[n, at, :] = nr
                xim[n, at, :] = ni
                carry[n] = (nr, ni)
            if nseg > 1 and v % seg_tiles == seg_tiles - 1:
                for n in range(per_slab):
                    cols = slice((s * per_slab + n) * LANE, (s * per_slab + n + 1) * LANE)
                    hre_ref[seg_rows_sl, cols], him_ref[seg_rows_sl, cols] = carry[n]

    def project_out(s, r):
        rs = slice(r * prow, (r + 1) * prow)
        hre = jnp.concatenate([xbufs[2 * s][k, rs, :] for k in range(per_slab)], axis=1)
        him = jnp.concatenate([xbufs[2 * s + 1][k, rs, :] for k in range(per_slab)], axis=1)
        return _mm(hre, wcre_ref[s]) - _mm(him, wcim_ref[s])

    ychunks = [[None] * parts for _ in range(slabs)]
    for r in range(parts):
        project_in(0, r)
    for s in range(slabs):
        js = list(range(s * per_slab, (s + 1) * per_slab))
        coef, carry = [], []
        for j in js:
            ar = jnp.broadcast_to(abre_ref[j], (SUBLANE, LANE))
            ai = jnp.broadcast_to(abim_ref[j], (SUBLANE, LANE))
            if steps_per_tile == 2:
                coef.append((jnp.where(upper, ar * ar - ai * ai, ar), jnp.where(upper, 2.0 * ar * ai, ai)))
            else:
                coef.append((ar, ai))
            carry.append((cre[j], cim[j]) if nseg == 1 else None)
        for r in range(parts):
            recur(s, r, coef, carry)
            if s + 1 < slabs:
                project_in(s + 1, r)
            if s >= 1:
                ychunks[s - 1][r] = project_out(s - 1, r)
        if nseg == 1:
            for n, j in enumerate(js):
                cre[j], cim[j] = carry[n]
    for r in range(parts):
        ychunks[slabs - 1][r] = project_out(slabs - 1, r)
    y = jnp.concatenate([jnp.concatenate(yc, axis=0) for yc in ychunks], axis=1) + dsk_ref[...] * u
    ab = _mm(_gelu_tanh(y), wglu_ref[...])
    n = y.shape[1]
    glu = ab[:, :n] * jax.nn.sigmoid(ab[:, n:])
    out = glu * lax.rsqrt(jnp.mean(glu * glu, axis=-1, keepdims=True) + EPS) * gs5_ref[...]
    out = jnp.dot(to_bm, out.astype(BF16), preferred_element_type=F32)
    y_ref[...] = out.reshape(y_ref.shape).astype(y_ref.dtype)

    if nseg == 1:
        @pl.when(pl.program_id(1) == pl.num_programs(1) - 1)
        def _():
            for j in range(nblk):
                hre_ref[:, j * LANE:(j + 1) * LANE] = cre[j, SUBLANE - nb:SUBLANE, :]
                him_ref[:, j * LANE:(j + 1) * LANE] = cim[j, SUBLANE - nb:SUBLANE, :]


def _s5(proj, u_col, h0re, h0im, wa, abre, abim, wcre, wcim, dskip, wglu, gs5, *, nb_total, t, nb, lc, nseg):
    mixb = dskip.shape[1]
    nstate = h0re.shape[1]
    nblk = nstate // LANE
    slabs = wa.shape[0]
    assert u_col % mixb == 0
    assert nseg == 1 or lc == t
    ublk = u_col // mixb
    nbs = nseg * nb
    uab3 = proj.reshape(nb_total, t, proj.shape[1])
    perms = _time_major_perms(nb, lc, nseg)
    full = lambda a: pl.BlockSpec(a.shape, lambda i, j: (0,) * a.ndim)
    y, hre, him = pl.pallas_call(
        functools.partial(_s5_body, nb=nb, lc=lc, nseg=nseg),
        grid=(nb_total // nbs, t // lc),
        in_specs=[pl.BlockSpec((nbs, lc, mixb), lambda i, j: (i, j, ublk)),
                  pl.BlockSpec((nbs, nstate), lambda i, j: (i, 0)),
                  pl.BlockSpec((nbs, nstate), lambda i, j: (i, 0))]
                 + [full(a) for a in perms]
                 + [full(wa), full(abre), full(abim), full(wcre), full(wcim), full(dskip), full(wglu), full(gs5)],
        out_specs=[pl.BlockSpec((nbs, lc, mixb), lambda i, j: (i, j, 0)),
                   pl.BlockSpec((nbs, nstate), lambda i, j: (i, 0)),
                   pl.BlockSpec((nbs, nstate), lambda i, j: (i, 0))],
        out_shape=[jax.ShapeDtypeStruct((nb_total, t, mixb), F32),
                   jax.ShapeDtypeStruct((nb_total, nstate), F32),
                   jax.ShapeDtypeStruct((nb_total, nstate), F32)],
        scratch_shapes=[pltpu.VMEM((nblk, SUBLANE, LANE), F32),
                        pltpu.VMEM((nblk, SUBLANE, LANE), F32)]
                       + [pltpu.VMEM((nblk // slabs, nbs * lc, LANE), F32) for _ in range(2 * slabs)],
        compiler_params=_params("parallel", "arbitrary"),
        name="s5",
    )(uab3, h0re, h0im, *perms, wa, abre, abim, wcre, wcim, dskip, wglu, gs5)
    return y.reshape(nb_total * t, mixb), hre, him


PART_ROWS = 256


def _row_parts(bb, tt):
    rows = bb * tt
    nparts = max(1, rows // PART_ROWS)
    if bb > 1:
        nparts = min(nparts, bb)
    prow = rows // nparts
    out = []
    for r in range(nparts):
        flat = slice(r * prow, (r + 1) * prow)
        if bb == 1:
            out.append(((slice(None), flat), slice(None), flat))
        else:
            bs = slice(r * bb // nparts, (r + 1) * bb // nparts)
            out.append(((bs, slice(None)), bs, flat))
    return out


def _outproj_body(o_ref, y_ref, wt_ref, wb_ref, x_ref, mod_ref, g_ref, out_ref, *, gate):
    mix = (jnp.dot(o_ref[...], wt_ref[...], preferred_element_type=F32)
           + _mm(y_ref[...], wb_ref[...]))
    n = mix * lax.rsqrt(jnp.mean(mix * mix, axis=-1, keepdims=True) + EPS) * g_ref[...]
    x = x_ref[...]
    out_ref[...] = x + mod_ref[:, gate:gate + 1, :] * n.reshape(x.shape)


def _outproj(o, y5, wtop, wbot, x3, mod, g, *, gate, rows):
    nb, t, d = x3.shape
    bb, tt = _token_blocks(nb, t, rows)
    nt = t // tt
    ka, kb = o.shape[1], y5.shape[1]
    mod, mod_shape, mod_index = _mod_block(mod, bb, nt)
    return pl.pallas_call(
        functools.partial(_outproj_body, gate=gate),
        grid=(nb * t // rows,),
        in_specs=[pl.BlockSpec((rows, ka), lambda i: (i, 0)),
                  pl.BlockSpec((rows, kb), lambda i: (i, 0)),
                  pl.BlockSpec((ka, d), lambda i: (0, 0)),
                  pl.BlockSpec((kb, d), lambda i: (0, 0)),
                  pl.BlockSpec((bb, tt, d), lambda i: (i // nt, i % nt, 0)),
                  pl.BlockSpec(mod_shape, mod_index),
                  pl.BlockSpec((1, d), lambda i: (0, 0))],
        out_specs=pl.BlockSpec((bb, tt, d), lambda i: (i // nt, i % nt, 0)),
        out_shape=jax.ShapeDtypeStruct((nb, t, d), F32),
        compiler_params=_params("parallel"),
        name="outproj",
    )(o, y5, wtop, wbot, x3, mod, g)


def _ffn_body(x_ref, mod_ref, gpre_ref, wg_ref, wu_ref, wd_ref, gpost_ref, out_ref, h_ref,
              *, shift, scale, gate):
    j = pl.program_id(1)
    last = pl.num_programs(1) - 1
    bb, tt, d = x_ref.shape
    parts = _row_parts(bb, tt)

    def swiglu(h):
        a = _silu(_mm(h, wg_ref[...])) * _mm(h, wu_ref[...])
        return _mm(a, wd_ref[...])

    @pl.when(j == 0)
    def _():
        for xi, mi, hr in parts:
            x = x_ref[xi]
            h = _modulated_norm(x, gpre_ref[...], mod_ref[mi, scale:scale + 1, :], mod_ref[mi, shift:shift + 1, :])
            h = h.reshape(hr.stop - hr.start, d).astype(BF16)
            h_ref[hr, :] = h
            out_ref[xi] = swiglu(h).reshape(x.shape)

    @pl.when((j > 0) & (j < last))
    def _():
        out_ref[...] += swiglu(h_ref[...]).reshape(out_ref.shape)

    @pl.when((j == last) & (j > 0))
    def _():
        for xi, mi, hr in parts:
            x = x_ref[xi]
            f = out_ref[xi] + swiglu(h_ref[hr, :]).reshape(x.shape)
            n = f * lax.rsqrt(jnp.mean(f * f, axis=-1, keepdims=True) + EPS) * gpost_ref[...]
            out_ref[xi] = x + mod_ref[mi, gate:gate + 1, :] * n


def _ffn(x3, mod, gpre, wg, wu, wd, gpost, *, shift, scale, gate, rows, tf):
    nb, t, d = x3.shape
    dff = wg.shape[1]
    assert dff // tf >= 2
    bb, tt = _token_blocks(nb, t, rows)
    nt = t // tt
    once = dict(pipeline_mode=pl.Buffered(1)) if bb > 1 else {}
    mod, mod_shape, mod_index = _mod_block(mod, bb, nt)
    return pl.pallas_call(
        functools.partial(_ffn_body, shift=shift, scale=scale, gate=gate),
        grid=(nb * t // rows, dff // tf),
        in_specs=[pl.BlockSpec((bb, tt, d), lambda i, j: (i // nt, i % nt, 0), **once),
                  pl.BlockSpec(mod_shape, lambda i, j: mod_index(i), **once),
                  pl.BlockSpec((1, d), lambda i, j: (0, 0)),
                  pl.BlockSpec((d, tf), lambda i, j: (0, j)),
                  pl.BlockSpec((d, tf), lambda i, j: (0, j)),
                  pl.BlockSpec((tf, d), lambda i, j: (j, 0)),
                  pl.BlockSpec((1, d), lambda i, j: (0, 0))],
        out_specs=pl.BlockSpec((bb, tt, d), lambda i, j: (i // nt, i % nt, 0)),
        out_shape=jax.ShapeDtypeStruct((nb, t, d), F32),
        scratch_shapes=[pltpu.VMEM((rows, d), BF16)],
        compiler_params=_params("parallel", "arbitrary"),
        name="ffn",
    )(x3, mod, gpre, wg, wu, wd, gpost)


ROWS = 512
FFN_ROWS = 1024
FFN_COLS = 256
PROMPT_CHUNK = 128
PROMPT_BASE = 16
SAMPLE_BATCH = 8
S5_PROMPT_STEPS = 128
S5_SAMPLE_SEGMENTS = 8


def _layer_weights(w_ada, b_ada, g_pre_mix, g_post_mix, g_pre_ffn, g_post_ffn,
                   w_in, w_conv, a_log, dt_bias, g_dn_out,
                   lam_re, lam_im, log_step, b_re, b_im, c_re, c_im, d_skip,
                   w_glu, g_s5_out, w_out, w_gate, w_up, w_down):
    d = w_in.shape[0]
    nheads = a_log.shape[0]
    mix_a = nheads * DN_HEAD_DIM
    pad = lambda v: jnp.pad(v, (0, LANE - v.shape[0]))[None, :]
    wa, abre, abim = _s5_prep(lam_re, lam_im, log_step, b_re, b_im)
    return dict(
        w_ada=w_ada, b_ada=b_ada[None, :],
        g_pre_mix=g_pre_mix[None, :], g_post_mix=g_post_mix[None, :],
        g_pre_ffn=g_pre_ffn[None, :], g_post_ffn=g_post_ffn[None, :],
        w_proj=_win_prep(jnp.swapaxes(w_in, 0, 1), 4 * mix_a, 2 * nheads), u_col=4 * mix_a,
        w_conv=w_conv, alog=pad(a_log), dtb=pad(dt_bias), gdn=g_dn_out[None, :],
        wa=wa, abre=abre, abim=abim, wcre=_s5_out_weights(c_re), wcim=_s5_out_weights(c_im),
        dskip=d_skip.reshape(1, -1), wglu=w_glu.astype(BF16), gs5=g_s5_out[None, :],
        wtop=w_out[:mix_a].astype(BF16), wbot=w_out[mix_a:].astype(BF16),
        wg=w_gate, wu=w_up, wd=w_down,
    )


def _layer_group(x3, mod, conv_state, s0, h0re, h0im, lw, *, prompt):
    nb, t, d = x3.shape
    rows = min(ROWS, nb * t)
    proj = _nmm(x3, mod, lw["g_pre_mix"], lw["w_proj"], shift=0, scale=1, rows=rows if t >= rows else rows // 2)
    if prompt:
        o, cnew, snew = _delta_prompt(proj, proj, conv_state, s0, lw["w_conv"], lw["alog"], lw["dtb"], lw["gdn"],
                                      nb=nb, t=t, c=PROMPT_CHUNK, base=PROMPT_BASE)
        y5, hre, him = _s5(proj, lw["u_col"], h0re, h0im, lw["wa"], lw["abre"], lw["abim"], lw["wcre"], lw["wcim"],
                           lw["dskip"], lw["wglu"], lw["gs5"], nb_total=nb, t=t, nb=nb, lc=S5_PROMPT_STEPS, nseg=1)
    else:
        o, cnew, snew = _delta_sample(proj, proj, conv_state, s0, lw["w_conv"], lw["alog"], lw["dtb"], lw["gdn"],
                                      nb=nb, t=t, bb=SAMPLE_BATCH)
        y5, hre, him = _s5(proj, lw["u_col"], h0re, h0im, lw["wa"], lw["abre"], lw["abim"], lw["wcre"], lw["wcim"],
                           lw["dskip"], lw["wglu"], lw["gs5"], nb_total=nb, t=t, nb=SUBLANE, lc=t,
                           nseg=min(S5_SAMPLE_SEGMENTS, nb // SUBLANE))
    x1 = _outproj(o, y5, lw["wtop"], lw["wbot"], x3, mod, lw["g_post_mix"], gate=2, rows=rows)
    x2 = _ffn(x1, mod, lw["g_pre_ffn"], lw["wg"], lw["wu"], lw["wd"], lw["g_post_ffn"],
              shift=3, scale=4, gate=5, rows=min(FFN_ROWS, nb * t), tf=FFN_COLS)
    ngroups = h0re.shape[1] // S5_STATE
    return x2, cnew, snew, hre.reshape(nb, ngroups, S5_STATE), him.reshape(nb, ngroups, S5_STATE)


def kernel(x_prompt, x_sample, c_prompt, c_sample, state_conv, state_delta, state_ssm_re, state_ssm_im, w_ada, b_ada, g_pre_mix, g_post_mix, g_pre_ffn, g_post_ffn, w_in, w_conv, a_log, dt_bias, g_dn_out, lam_re, lam_im, log_step, b_re, b_im, c_re, c_im, d_skip, w_glu, g_s5_out, w_out, w_gate, w_up, w_down):
    weights = (w_ada, b_ada, g_pre_mix, g_post_mix, g_pre_ffn, g_post_ffn,
               w_in, w_conv, a_log, dt_bias, g_dn_out,
               lam_re, lam_im, log_step, b_re, b_im, c_re, c_im, d_skip,
               w_glu, g_s5_out, w_out, w_gate, w_up, w_down)
    depth = w_ada.shape[0]
    bp, d = c_prompt.shape
    bs = c_sample.shape[0]
    dt_ = x_prompt.dtype
    nheads = state_delta.shape[2]
    nstate = state_ssm_re.shape[2] * state_ssm_re.shape[3]
    crows = -(-(bp + bs) // SUBLANE) * SUBLANE
    c_all = jnp.concatenate([c_sample, c_prompt, jnp.zeros((crows - bp - bs, d), dt_)], axis=0)
    yp, ys = x_prompt, x_sample
    outs = [[] for _ in range(8)]
    for l in range(depth):
        lw = _layer_weights(*[w[l] for w in weights])
        mod = _ada(c_all, lw["w_ada"], lw["b_ada"]).reshape(crows, 6, d)
        zc = jnp.zeros((bp, CONV_WIDTH - 1, state_conv.shape[3]), dt_)
        zd = jnp.zeros((bp,) + state_delta.shape[2:], dt_)
        zs = jnp.zeros((bp, nstate), dt_)
        yp, c1, d1, r1, i1 = _layer_group(yp, (mod, bs), zc, zd, zs, zs, lw, prompt=True)
        ys, c2, d2, r2, i2 = _layer_group(ys, (mod, 0), state_conv[l], state_delta[l],
                                          state_ssm_re[l].reshape(bs, nstate), state_ssm_im[l].reshape(bs, nstate),
                                          lw, prompt=False)
        for lst, v in zip(outs, (c1, d1, r1, i1, c2, d2, r2, i2)):
            lst.append(v)
    stacked = [jnp.stack(v) for v in outs]
    return (yp, ys, *stacked)
```

```python
import functools
import math

import jax
import jax.numpy as jnp
import numpy as np
from jax import lax
from jax.experimental import pallas as pl
from jax.experimental.pallas import tpu as pltpu

F32 = jnp.float32
BF16 = jnp.bfloat16
EPS = 1e-6
LANE = 128
SUBLANE = 8
V7X_VMEM_BYTES = 64 << 20
VMEM_LIMIT = V7X_VMEM_BYTES - (8 << 20)

DN_HEAD_DIM = 128
CONV_WIDTH = 4
S5_GROUP = 16
S5_STATE = 64
S5_SLAB_GROUPS = LANE // S5_GROUP
S5_SLAB_STATES = S5_SLAB_GROUPS * S5_STATE


def _params(*sem):
    return pltpu.CompilerParams(dimension_semantics=sem, vmem_limit_bytes=VMEM_LIMIT)


def _silu(x):
    return x * jax.nn.sigmoid(x)


def _mm(a, b):
    return jnp.dot(a.astype(BF16), b.astype(BF16), preferred_element_type=F32)


def _mm_nt(a, b):
    return lax.dot_general(a.astype(BF16), b.astype(BF16), (((1,), (1,)), ((), ())),
                           preferred_element_type=F32)


def _mm_tn(a, b):
    return lax.dot_general(a.astype(BF16), b.astype(BF16), (((0,), (0,)), ((), ())),
                           preferred_element_type=F32)


def _split3(x):
    x1 = x.astype(BF16)
    r1 = x - x1.astype(F32)
    x2 = r1.astype(BF16)
    x3 = (r1 - x2.astype(F32)).astype(BF16)
    return x1, x2, x3


def _ada_body(c_ref, w_ref, b_ref, o_ref):
    s = _silu(c_ref[...])
    o_ref[...] = _mm(s, w_ref[...]) + b_ref[...]


def _ada(c, w, b):
    rows, d = c.shape
    n = w.shape[1]
    tn = 1024
    return pl.pallas_call(
        _ada_body,
        grid=(n // tn,),
        in_specs=[pl.BlockSpec((rows, d), lambda j: (0, 0)),
                  pl.BlockSpec((d, tn), lambda j: (0, j)),
                  pl.BlockSpec((1, tn), lambda j: (0, j))],
        out_specs=pl.BlockSpec((rows, tn), lambda j: (0, j)),
        out_shape=jax.ShapeDtypeStruct((rows, n), F32),
        compiler_params=_params("arbitrary"),
        name="ada",
    )(c, w, b)


def _mod_block(mod, bb, nt):
    arr, first = mod
    assert first % bb == 0
    return arr, (bb, 6, arr.shape[2]), lambda i: (first // bb + i // nt, 0, 0)


def _token_blocks(nb, t, rows):
    if t >= rows:
        assert t % rows == 0
        return 1, rows
    assert rows % t == 0 and nb % (rows // t) == 0
    return rows // t, t


def _modulated_norm(x, g, scale, shift):
    y = x * lax.rsqrt(jnp.mean(x * x, axis=-1, keepdims=True) + EPS) * g
    return y * (1.0 + scale) + shift


def _win_prep_body(w_ref, o_ref, *, main, gates):
    n = w_ref.shape[0]
    rest = n - main - gates
    o_ref[0:main, :] = w_ref[0:main, :].astype(BF16)
    o_ref[main:main + rest, :] = w_ref[main + gates:n, :].astype(BF16)
    o_ref[main + rest:main + rest + gates, :] = w_ref[main:main + gates, :].astype(BF16)
    o_ref[main + rest + gates:, :] = jnp.zeros((LANE - gates, o_ref.shape[1]), BF16)


def _win_prep(w_in_t, main, gates):
    n, d = w_in_t.shape
    cb = min(512, d)
    nout = n - gates + LANE
    return pl.pallas_call(
        functools.partial(_win_prep_body, main=main, gates=gates),
        grid=(d // cb,),
        in_specs=[pl.BlockSpec((n, cb), lambda i: (0, i))],
        out_specs=pl.BlockSpec((nout, cb), lambda i: (0, i)),
        out_shape=jax.ShapeDtypeStruct((nout, d), BF16),
        compiler_params=_params("parallel"),
        name="winprep",
    )(w_in_t)


INPROJ_PART_ROWS = 256


def _inproj_body(x_ref, mod_ref, g_ref, w_ref, wc_ref, cst_ref, o_ref, cnew_ref, xc_ref,
                 *, shift, scale, nheads, nt):
    bb, tt, d = x_ref.shape
    rows = bb * tt
    dh = DN_HEAD_DIM
    mix = nheads * dh
    halo = SUBLANE
    taps = CONV_WIDTH - 1
    n = w_ref.shape[0]

    if bb == 1:
        @pl.when(pl.program_id(0) % nt == 0)
        def _():
            xc_ref[halo - taps:halo, :] = cst_ref[0]

    h = _modulated_norm(x_ref[...], g_ref[...], mod_ref[:, scale:scale + 1, :], mod_ref[:, shift:shift + 1, :])
    h = h.reshape(rows, d).astype(BF16)
    parts = _row_parts(bb, tt, INPROJ_PART_ROWS)
    cw = 2 * dh if mix % (2 * dh) == 0 else dh

    def project(flat, c0, c1):
        return lax.dot_general(h[flat, :], w_ref[c0:c1, :], (((1,), (1,)), ((), ())), preferred_element_type=F32)

    def conv_silu(p, c0, c1, raw):
        xi, _, flat = parts[p]
        cols = slice(c0, c1)
        if bb == 1:
            r0, r1 = flat.start, flat.stop
            xc_ref[halo + r0:halo + r1, cols] = raw
            conv = xc_ref[halo - taps + r0:halo - taps + r1, cols] * wc_ref[0:1, cols]
            for j in range(1, CONV_WIDTH):
                conv = conv + xc_ref[halo - taps + j + r0:halo - taps + j + r1, cols] * wc_ref[j:j + 1, cols]
            if p == len(parts) - 1:
                cnew_ref[0, :, cols] = xc_ref[halo + rows - taps:halo + rows, cols]
                xc_ref[0:halo, cols] = xc_ref[rows:rows + halo, cols]
        else:
            bs = xi[0]
            nseq = bs.stop - bs.start
            xc_ref[bs, halo - taps:halo, cols] = cst_ref[bs, :, cols]
            xc_ref[bs, halo:halo + tt, cols] = raw.reshape(nseq, tt, c1 - c0)
            conv = xc_ref[bs, halo - taps:halo - taps + tt, cols] * wc_ref[0:1, cols]
            for j in range(1, CONV_WIDTH):
                conv = conv + xc_ref[bs, halo - taps + j:halo - taps + j + tt, cols] * wc_ref[j:j + 1, cols]
            cnew_ref[bs, :, cols] = xc_ref[bs, halo + tt - taps:halo + tt, cols]
            conv = conv.reshape(nseq * tt, c1 - c0)
        return _silu(conv)

    def finish(p, c0, c1, raw):
        act = conv_silu(p, c0, c1, raw)
        flat = parts[p][2]
        if c0 >= 2 * mix:
            o_ref[flat, c0:c1] = act
            return
        gain = dh ** -0.5 if c0 < mix else 1.0
        for s in range(0, c1 - c0, dh):
            a = act[:, s:s + dh]
            o_ref[flat, c0 + s:c0 + s + dh] = a * (lax.rsqrt(jnp.sum(a * a, axis=-1, keepdims=True) + 1e-6) * gain)

    items = [(p, c0, min(c0 + cw, 3 * mix)) for p in range(len(parts)) for c0 in range(0, 3 * mix, cw)]
    rest = [(p, c0, min(c0 + cw, n)) for p in range(len(parts)) for c0 in range(3 * mix, n, cw)]
    pending = project(parts[items[0][0]][2], items[0][1], items[0][2])
    for k, (p, c0, c1) in enumerate(items):
        nxt = None
        if k + 1 < len(items):
            pn, n0, n1 = items[k + 1]
            nxt = project(parts[pn][2], n0, n1)
        if k < len(rest):
            pr, r0, r1 = rest[k]
            o_ref[parts[pr][2], r0:r1] = project(parts[pr][2], r0, r1)
        finish(p, c0, c1, pending)
        pending = nxt
    for pr, r0, r1 in rest[len(items):]:
        o_ref[parts[pr][2], r0:r1] = project(parts[pr][2], r0, r1)


def _inproj(x3, mod, g, w_t, w_conv, conv_state, *, shift, scale, rows, nheads):
    nb, t, d = x3.shape
    n = w_t.shape[0]
    qkv = 3 * nheads * DN_HEAD_DIM
    bb, tt = _token_blocks(nb, t, rows)
    nt = t // tt
    mod, mod_shape, mod_index = _mod_block(mod, bb, nt)
    if bb == 1:
        scratch = pltpu.VMEM((rows + SUBLANE, qkv), F32)
    else:
        scratch = pltpu.VMEM((bb, SUBLANE + tt, qkv), F32)
    return pl.pallas_call(
        functools.partial(_inproj_body, shift=shift, scale=scale, nheads=nheads, nt=nt),
        grid=(nb * t // rows,),
        in_specs=[pl.BlockSpec((bb, tt, d), lambda i: (i // nt, i % nt, 0), pipeline_mode=pl.Buffered(1)),
                  pl.BlockSpec(mod_shape, mod_index),
                  pl.BlockSpec((1, d), lambda i: (0, 0)),
                  pl.BlockSpec((n, d), lambda i: (0, 0), pipeline_mode=pl.Buffered(1)),
                  pl.BlockSpec((CONV_WIDTH, qkv), lambda i: (0, 0)),
                  pl.BlockSpec((bb, CONV_WIDTH - 1, qkv), lambda i: (i // nt, 0, 0))],
        out_specs=[pl.BlockSpec((rows, n), lambda i: (i, 0)),
                   pl.BlockSpec((bb, CONV_WIDTH - 1, qkv), lambda i: (i // nt, 0, 0))],
        out_shape=[jax.ShapeDtypeStruct((nb * t, n), F32),
                   jax.ShapeDtypeStruct((nb, CONV_WIDTH - 1, qkv), F32)],
        scratch_shapes=[scratch],
        compiler_params=_params("arbitrary"),
        name="inproj",
    )(x3, mod, g, w_t, w_conv, conv_state)


def _seg_masks(c, seg):
    row = lax.broadcasted_iota(jnp.int32, (c, c), 0)
    col = lax.broadcasted_iota(jnp.int32, (c, c), 1)
    sh = int(math.log2(seg))
    same = (row >> sh) == (col >> sh)
    return row, col, same, same & (row >= col), same & (row > col)


def _tri_inverse_offdiag(lmats, row, col, base, seg):
    bsh = int(math.log2(base))
    blk = (row >> bsh) == (col >> bsh)
    l0s = [jnp.where(blk, l, 0.0) for l in lmats]
    pts = [-l for l in l0s]
    qs = [_mm(l, l) for l in l0s]
    k = 2
    while k < base:
        pts = [p + q + _mm(p, q) for p, q in zip(pts, qs)]
        k *= 2
        if k < base:
            qs = [_mm(q, q) for q in qs]
    m = base
    while m < seg:
        msh = int(math.log2(m))
        sel = ((row >> (msh + 1)) == (col >> (msh + 1))) & ((row >> msh) != (col >> msh))
        es = [jnp.where(sel, l, 0.0) for l in lmats]
        xs = [e + _mm(p, e) for p, e in zip(pts, es)]
        pts = [p - (x + _mm(x, p)) for p, x in zip(pts, xs)]
        m *= 2
    return pts


def _delta_gates(ab, alog, dtb, causal_bf, seg, nheads):
    lane = lax.broadcasted_iota(jnp.int32, ab.shape, 1)
    x = ab + dtb
    softplus = jnp.maximum(x, 0.0) + jnp.log1p(jnp.exp(-jnp.abs(x)))
    g = jnp.where(lane < nheads, -jnp.exp(alog) * softplus, 0.0)
    beta = jax.nn.sigmoid(ab)
    pieces = _split3(g)
    gcum = sum(jnp.dot(causal_bf, p, preferred_element_type=F32) for p in pieces)
    g3 = gcum.reshape(gcum.shape[0] // seg, seg, gcum.shape[1])
    glast = jnp.broadcast_to(g3[:, seg - 1:seg, :], g3.shape).reshape(gcum.shape)
    return gcum, beta, glast


def _decay_logits(gcum, nheads):
    c = gcum.shape[0]
    if c < LANE:
        gcum = jnp.concatenate([gcum, jnp.zeros((LANE - c, LANE), F32)], axis=0)
    return gcum.T


def _delta_intra(act, nheads, gates, dlog, masks, base, seg):
    gcum, beta, glast = gates
    row, col, _, causal, strict = masks
    c = act.shape[0]
    dh = DN_HEAD_DIM
    eg_all = jnp.exp(gcum)
    ed_all = jnp.exp(glast - gcum)
    ks, kqs_lhs, rhss, qgs, kdecs = [], [], [], [], []
    for h in range(nheads):
        q = act[:, h * dh:(h + 1) * dh]
        k = act[:, (nheads + h) * dh:(nheads + h + 1) * dh]
        v = act[:, (2 * nheads + h) * dh:(2 * nheads + h + 1) * dh]
        b = beta[:, nheads + h:nheads + h + 1]
        eg = eg_all[:, h:h + 1]
        kb = k * b
        ks.append(k)
        kqs_lhs.append(jnp.concatenate([kb, q], axis=0))
        rhss.append(jnp.concatenate([v * b, kb * eg], axis=1))
        qgs.append(q * eg)
        kdecs.append(k * ed_all[:, h:h + 1])
    kqs = [_mm_nt(a, k) for a, k in zip(kqs_lhs, ks)]
    lmats, attns = [], []
    for h in range(nheads):
        dl = gcum[:, h:h + 1] - dlog[h:h + 1, 0:c]
        dec = jnp.where(causal, jnp.exp(jnp.where(causal, dl, 0.0)), 0.0)
        lmats.append(jnp.where(strict, kqs[h][:c] * dec, 0.0))
        attns.append(kqs[h][c:] * dec)
    tts = _tri_inverse_offdiag(lmats, row, col, base, seg)
    uws = [r + _mm(t, r) for t, r in zip(tts, rhss)]
    return [uw[:, :dh] for uw in uws], [uw[:, dh:] for uw in uws], qgs, kdecs, attns


def _delta_out(o, z, gdn):
    on = o * lax.rsqrt(jnp.mean(o * o, axis=-1, keepdims=True) + EPS) * gdn
    return on * _silu(z)


def _delta_prompt_body(qkv_ref, z_ref, ab_ref, s0_ref, alog_ref, dtb_ref, gdn_ref,
                       o_ref, snew_ref, s_ref, *, c, nheads, base):
    dh = DN_HEAD_DIM

    @pl.when(pl.program_id(1) == 0)
    def _():
        s_ref[...] = s0_ref[0]

    act = qkv_ref[...]
    masks = _seg_masks(c, c)
    causal_bf = jnp.where(masks[3], 1.0, 0.0).astype(BF16)
    gates = _delta_gates(ab_ref[...], alog_ref[...], dtb_ref[...], causal_bf, c, nheads)
    dlog = _decay_logits(gates[0], nheads)
    eglast = jnp.exp(gates[2][0:1, :])
    gdn = gdn_ref[...]
    us, ws_, qgs, kdecs, attns = _delta_intra(act, nheads, gates, dlog, masks, base, c)
    heads = range(nheads)
    ss = [s_ref[h] for h in heads]
    wss = [_mm(jnp.concatenate([ws_[h], qgs[h]], axis=0), ss[h]) for h in heads]
    vnews = [us[h] - wss[h][:c] for h in heads]
    os_ = [wss[h][c:] + _mm(attns[h], vnews[h]) for h in heads]
    snews = [ss[h] * eglast[:, h:h + 1] + _mm_tn(kdecs[h], vnews[h]) for h in heads]
    for h in heads:
        s_ref[h] = snews[h]
        o_ref[:, h * dh:(h + 1) * dh] = _delta_out(os_[h], z_ref[:, h * dh:(h + 1) * dh], gdn).astype(o_ref.dtype)

    @pl.when(pl.program_id(1) == pl.num_programs(1) - 1)
    def _():
        snew_ref[0] = s_ref[...]


def _delta_prompt(proj, s0, alog, dtb, gdn, *, nb, t, c, base):
    nheads = s0.shape[1]
    dh = DN_HEAD_DIM
    mix = nheads * dh
    nc = t // c
    return pl.pallas_call(
        functools.partial(_delta_prompt_body, c=c, nheads=nheads, base=base),
        grid=(nb, nc),
        in_specs=[pl.BlockSpec((c, 3 * mix), lambda b, i: (b * nc + i, 0)),
                  pl.BlockSpec((c, mix), lambda b, i: (b * nc + i, 3)),
                  pl.BlockSpec((c, LANE), lambda b, i: (b * nc + i, proj.shape[1] // LANE - 1)),
                  pl.BlockSpec((1, nheads, dh, dh), lambda b, i: (b, 0, 0, 0)),
                  pl.BlockSpec((1, LANE), lambda b, i: (0, 0)),
                  pl.BlockSpec((1, LANE), lambda b, i: (0, 0)),
                  pl.BlockSpec((1, dh), lambda b, i: (0, 0))],
        out_specs=[pl.BlockSpec((c, mix), lambda b, i: (b * nc + i, 0)),
                   pl.BlockSpec((1, nheads, dh, dh), lambda b, i: (b, 0, 0, 0))],
        out_shape=[jax.ShapeDtypeStruct((nb * t, mix), BF16),
                   jax.ShapeDtypeStruct((nb, nheads, dh, dh), F32)],
        scratch_shapes=[pltpu.VMEM((nheads, dh, dh), F32)],
        compiler_params=_params("parallel", "arbitrary"),
        name="delta_prompt",
    )(proj, proj, proj, s0, alog, dtb, gdn)


def _delta_sample_body(qkv_ref, z_ref, ab_ref, s0_ref, alog_ref, dtb_ref, gdn_ref,
                       o_ref, snew_ref,
                       w_s, qg_s, u_s, kd_s, vn_s, qs_s, attn_s, egl_s, *, bb, t, nheads):
    dh = DN_HEAD_DIM
    c = bb * t
    act = qkv_ref[...]
    masks = _seg_masks(c, t)
    causal_bf = jnp.where(masks[3], 1.0, 0.0).astype(BF16)
    gates = _delta_gates(ab_ref[...], alog_ref[...], dtb_ref[...], causal_bf, t, nheads)
    dlog = _decay_logits(gates[0], nheads)
    egl_s[...] = jnp.exp(gates[2])
    us, ws_, qgs, kdecs, attns = _delta_intra(act, nheads, gates, dlog, masks, t, t)
    for h in range(nheads):
        hs = slice(h * dh, (h + 1) * dh)
        u_s[:, hs] = us[h]
        w_s[:, hs] = ws_[h]
        qg_s[:, hs] = qgs[h]
        kd_s[:, hs] = kdecs[h]
        attn_s[h] = attns[h]

    def per_batch(b, carry):
        r = pl.multiple_of(b * t, t)
        rows = pl.ds(r, t)
        heads = range(nheads)
        hsl = [slice(h * dh, (h + 1) * dh) for h in heads]
        ss = [s0_ref[b, h] for h in heads]
        wss = [_mm(jnp.concatenate([w_s[rows, hsl[h]], qg_s[rows, hsl[h]]], axis=0), ss[h]) for h in heads]
        vnews = [u_s[rows, hsl[h]] - wss[h][:t] for h in heads]
        upds = [_mm_tn(kd_s[rows, hsl[h]], vnews[h]) for h in heads]
        egl = egl_s[pl.ds(r, 1), :]
        for h in heads:
            vn_s[rows, hsl[h]] = vnews[h]
            qs_s[rows, hsl[h]] = wss[h][t:]
            snew_ref[b, h] = ss[h] * egl[:, h:h + 1] + upds[h]
        return carry

    lax.fori_loop(0, bb, per_batch, 0, unroll=2)

    gdn = gdn_ref[...]
    for h in range(nheads):
        hs = slice(h * dh, (h + 1) * dh)
        o = qs_s[:, hs] + _mm(attn_s[h], vn_s[:, hs])
        o_ref[:, hs] = _delta_out(o, z_ref[:, hs], gdn).astype(o_ref.dtype)


def _delta_sample(proj, s0, alog, dtb, gdn, *, nb, t, bb):
    nheads = s0.shape[1]
    dh = DN_HEAD_DIM
    mix = nheads * dh
    c = bb * t
    return pl.pallas_call(
        functools.partial(_delta_sample_body, bb=bb, t=t, nheads=nheads),
        grid=(nb // bb,),
        in_specs=[pl.BlockSpec((c, 3 * mix), lambda i: (i, 0)),
                  pl.BlockSpec((c, mix), lambda i: (i, 3)),
                  pl.BlockSpec((c, LANE), lambda i: (i, proj.shape[1] // LANE - 1)),
                  pl.BlockSpec((bb, nheads, dh, dh), lambda i: (i, 0, 0, 0)),
                  pl.BlockSpec((1, LANE), lambda i: (0, 0)),
                  pl.BlockSpec((1, LANE), lambda i: (0, 0)),
                  pl.BlockSpec((1, dh), lambda i: (0, 0))],
        out_specs=[pl.BlockSpec((c, mix), lambda i: (i, 0)),
                   pl.BlockSpec((bb, nheads, dh, dh), lambda i: (i, 0, 0, 0))],
        out_shape=[jax.ShapeDtypeStruct((nb * t, mix), BF16),
                   jax.ShapeDtypeStruct((nb, nheads, dh, dh), F32)],
        scratch_shapes=[pltpu.VMEM((c, mix), F32) for _ in range(6)]
                       + [pltpu.VMEM((nheads, c, c), F32), pltpu.VMEM((c, LANE), F32)],
        compiler_params=_params("parallel"),
        name="delta_sample",
    )(proj, proj, proj, s0, alog, dtb, gdn)


def _s5_prep_body(lre_ref, lim_ref, ls_ref, bre_ref, bim_ref, wa_ref, abre_ref, abim_ref):
    lre = jnp.minimum(lre_ref[...], -1e-4)
    lim = lim_ref[...]
    dt = jnp.exp(ls_ref[...])
    mag = jnp.exp(lre * dt)
    ang = lim * dt
    abr = mag * jnp.cos(ang)
    abi = mag * jnp.sin(ang)
    nr = abr - 1.0
    ni = abi
    den = lre * lre + lim * lim
    cor = (nr * lre + ni * lim) / den
    coi = (ni * lre - nr * lim) / den
    bre = bre_ref[...]
    bim = bim_ref[...]
    shape2 = bre.shape[1:]
    row = lax.broadcasted_iota(jnp.int32, shape2, 0)
    col = lax.broadcasted_iota(jnp.int32, shape2, 1)
    diag = ((row // S5_GROUP) == (col // S5_STATE))[None]
    half = shape2[1]
    rows = shape2[0]
    wre = jnp.where(diag, cor * bre - coi * bim, 0.0)
    wim = jnp.where(diag, cor * bim + coi * bre, 0.0)
    wa_ref[:, 0:rows, 0:half] = wre.astype(BF16)
    wa_ref[:, 0:rows, half:2 * half] = wim.astype(BF16)
    wa_ref[:, rows:2 * rows, 0:half] = (abr * wre - abi * wim).astype(BF16)
    wa_ref[:, rows:2 * rows, half:2 * half] = (abr * wim + abi * wre).astype(BF16)
    abre_ref[...] = abr
    abim_ref[...] = abi


def _s5_prep(lam_re, lam_im, log_step, b_re, b_im):
    groups, states, gch = b_re.shape
    slabs = groups // S5_SLAB_GROUPS
    cols = S5_SLAB_STATES

    def row_param(p):
        return p.reshape(slabs, 1, cols)

    def tiled_b(b):
        bt = jnp.transpose(b, (0, 2, 1)).reshape(slabs, LANE, states)
        return jnp.tile(bt, (1, 1, S5_SLAB_GROUPS))

    ls = jnp.broadcast_to(log_step[:, None], (groups, states))
    wa, abre, abim = pl.pallas_call(
        _s5_prep_body,
        out_shape=[jax.ShapeDtypeStruct((slabs, 2 * LANE, 2 * cols), BF16),
                   jax.ShapeDtypeStruct((slabs, 1, cols), F32),
                   jax.ShapeDtypeStruct((slabs, 1, cols), F32)],
        compiler_params=pltpu.CompilerParams(vmem_limit_bytes=VMEM_LIMIT),
        name="s5prep",
    )(row_param(lam_re), row_param(lam_im), row_param(ls), tiled_b(b_re), tiled_b(b_im))
    nblk = groups * states // LANE
    return wa, abre.reshape(nblk, 1, LANE), abim.reshape(nblk, 1, LANE)


def _s5_out_weights(c):
    groups, gch, states = c.shape
    slabs = groups // S5_SLAB_GROUPS
    ct = jnp.transpose(c, (0, 2, 1)).reshape(slabs, S5_SLAB_STATES, gch)
    ct = jnp.tile(ct, (1, 1, S5_SLAB_GROUPS))
    row = jnp.arange(S5_SLAB_STATES)[:, None] // states
    col = jnp.arange(LANE)[None, :] // gch
    return jnp.where((row == col)[None], ct, 0.0).astype(BF16)


def _gelu_tanh(x):
    return x * (0.5 * (1.0 + jnp.tanh(math.sqrt(2.0 / math.pi) * (x + 0.044715 * (x * x * x)))))


S5_PIPELINE_PARTS = 4


def _time_major_perms(nb, lc, nseg):
    seg_rows = nb * lc
    rows = nseg * seg_rows
    r = np.arange(rows)[:, None]
    c = np.arange(rows)[None, :]
    base, loc = r // seg_rows * seg_rows, r % seg_rows
    t_tm, b_tm = loc // nb, loc % nb
    to_tm = c == base + b_tm * lc + t_tm
    to_tm_prev = (c == base + b_tm * lc + t_tm - 1) & (t_tm % 2 == 1)
    to_bm = c == base + (loc % lc) * nb + loc // lc
    return tuple(jnp.asarray(m, BF16) for m in (to_tm, to_tm_prev, to_bm))


def _s5_body(u_ref, h0re_ref, h0im_ref, tm_ref, tmp_ref, bm_ref, wa_ref, abre_ref, abim_ref, wcre_ref, wcim_ref,
             dsk_ref, wglu_ref, gs5_ref, y_ref, hre_ref, him_ref, cre, cim, *xbufs, nb, lc, nseg):
    seg_rows = nb * lc
    rows = nseg * seg_rows
    nblk = cre.shape[0]
    slabs = wa_ref.shape[0]
    per_slab = nblk // slabs
    half = per_slab * LANE
    steps_per_tile = SUBLANE // nb
    assert steps_per_tile in (1, 2) and lc % steps_per_tile == 0
    assert nseg == 1 or nb == SUBLANE

    if nseg == 1:
        @pl.when(pl.program_id(1) == 0)
        def _():
            for j in range(nblk):
                if nb < SUBLANE:
                    cre[j] = jnp.zeros((SUBLANE, LANE), F32)
                    cim[j] = jnp.zeros((SUBLANE, LANE), F32)
                cre[j, SUBLANE - nb:SUBLANE, :] = h0re_ref[:, j * LANE:(j + 1) * LANE]
                cim[j, SUBLANE - nb:SUBLANE, :] = h0im_ref[:, j * LANE:(j + 1) * LANE]

    to_tm, to_bm = tm_ref[...], bm_ref[...]
    pieces = _split3(u_ref[...].reshape(rows, u_ref.shape[2]))[:2]
    u = sum(jnp.dot(to_tm, p, preferred_element_type=F32) for p in pieces)
    ub = u.astype(BF16)
    if steps_per_tile == 2:
        ub_prev = jnp.dot(tmp_ref[...], pieces[0], preferred_element_type=F32).astype(BF16)

    upper = lax.broadcasted_iota(jnp.int32, (SUBLANE, LANE), 0) >= nb
    parts = S5_PIPELINE_PARTS if rows % (S5_PIPELINE_PARTS * LANE) == 0 else 1
    prow = rows // parts

    def project_in(s, r):
        rs = slice(r * prow, (r + 1) * prow)
        sl = slice(s * LANE, (s + 1) * LANE)
        if steps_per_tile == 2:
            x = jnp.dot(jnp.concatenate([ub[rs, sl], ub_prev[rs, sl]], axis=1), wa_ref[s],
                        preferred_element_type=F32)
        else:
            x = jnp.dot(ub[rs, sl], wa_ref[s, 0:LANE, :], preferred_element_type=F32)
        for k in range(per_slab):
            xbufs[2 * s][k, rs, :] = x[:, k * LANE:(k + 1) * LANE]
            xbufs[2 * s + 1][k, rs, :] = x[:, half + k * LANE:half + (k + 1) * LANE]

    seg_tiles = seg_rows // SUBLANE

    def recur(s, r, coef, carry):
        xre, xim = xbufs[2 * s], xbufs[2 * s + 1]
        for v in range(r * prow // SUBLANE, (r + 1) * prow // SUBLANE):
            at = slice(v * SUBLANE, (v + 1) * SUBLANE)
            seg_rows_sl = slice(v // seg_tiles * nb, (v // seg_tiles + 1) * nb)
            if nseg > 1 and v % seg_tiles == 0:
                for n in range(per_slab):
                    cols = slice((s * per_slab + n) * LANE, (s * per_slab + n + 1) * LANE)
                    carry[n] = (h0re_ref[seg_rows_sl, cols], h0im_ref[seg_rows_sl, cols])
            for n in range(per_slab):
                hr, hi = carry[n]
                ar, ai = coef[n]
                if steps_per_tile == 2:
                    hr = jnp.where(upper, hr, pltpu.roll(hr, nb, 0))
                    hi = jnp.where(upper, hi, pltpu.roll(hi, nb, 0))
                nr = ar * hr - ai * hi + xre[n, at, :]
                ni = ar * hi + ai * hr + xim[n, at, :]
                xre[n, at, :] = nr
                xim[n, at, :] = ni
                carry[n] = (nr, ni)
            if nseg > 1 and v % seg_tiles == seg_tiles - 1:
                for n in range(per_slab):
                    cols = slice((s * per_slab + n) * LANE, (s * per_slab + n + 1) * LANE)
                    hre_ref[seg_rows_sl, cols], him_ref[seg_rows_sl, cols] = carry[n]

    def project_out(s, r):
        rs = slice(r * prow, (r + 1) * prow)
        hre = jnp.concatenate([xbufs[2 * s][k, rs, :] for k in range(per_slab)], axis=1)
        him = jnp.concatenate([xbufs[2 * s + 1][k, rs, :] for k in range(per_slab)], axis=1)
        return _mm(hre, wcre_ref[s]) - _mm(him, wcim_ref[s])

    ychunks = [[None] * parts for _ in range(slabs)]
    for r in range(parts):
        project_in(0, r)
    for s in range(slabs):
        js = list(range(s * per_slab, (s + 1) * per_slab))
        coef, carry = [], []
        for j in js:
            ar = jnp.broadcast_to(abre_ref[j], (SUBLANE, LANE))
            ai = jnp.broadcast_to(abim_ref[j], (SUBLANE, LANE))
            if steps_per_tile == 2:
                coef.append((jnp.where(upper, ar * ar - ai * ai, ar), jnp.where(upper, 2.0 * ar * ai, ai)))
            else:
                coef.append((ar, ai))
            carry.append((cre[j], cim[j]) if nseg == 1 else None)
        for r in range(parts):
            recur(s, r, coef, carry)
            if s + 1 < slabs:
                project_in(s + 1, r)
            if s >= 1:
                ychunks[s - 1][r] = project_out(s - 1, r)
        if nseg == 1:
            for n, j in enumerate(js):
                cre[j], cim[j] = carry[n]
    for r in range(parts):
        ychunks[slabs - 1][r] = project_out(slabs - 1, r)
    y = jnp.concatenate([jnp.concatenate(yc, axis=0) for yc in ychunks], axis=1) + dsk_ref[...] * u
    ab = _mm(_gelu_tanh(y), wglu_ref[...])
    n = y.shape[1]
    glu = ab[:, :n] * jax.nn.sigmoid(ab[:, n:])
    out = glu * lax.rsqrt(jnp.mean(glu * glu, axis=-1, keepdims=True) + EPS) * gs5_ref[...]
    out = jnp.dot(to_bm, out.astype(BF16), preferred_element_type=F32)
    y_ref[...] = out.reshape(y_ref.shape).astype(y_ref.dtype)

    if nseg == 1:
        @pl.when(pl.program_id(1) == pl.num_programs(1) - 1)
        def _():
            for j in range(nblk):
                hre_ref[:, j * LANE:(j + 1) * LANE] = cre[j, SUBLANE - nb:SUBLANE, :]
                him_ref[:, j * LANE:(j + 1) * LANE] = cim[j, SUBLANE - nb:SUBLANE, :]


def _s5(proj, u_col, h0re, h0im, wa, abre, abim, wcre, wcim, dskip, wglu, gs5, *, nb_total, t, nb, lc, nseg):
    mixb = dskip.shape[1]
    nstate = h0re.shape[1]
    nblk = nstate // LANE
    slabs = wa.shape[0]
    assert u_col % mixb == 0
    assert nseg == 1 or lc == t
    ublk = u_col // mixb
    nbs = nseg * nb
    uab3 = proj.reshape(nb_total, t, proj.shape[1])
    perms = _time_major_perms(nb, lc, nseg)
    full = lambda a: pl.BlockSpec(a.shape, lambda i, j: (0,) * a.ndim)
    y, hre, him = pl.pallas_call(
        functools.partial(_s5_body, nb=nb, lc=lc, nseg=nseg),
        grid=(nb_total // nbs, t // lc),
        in_specs=[pl.BlockSpec((nbs, lc, mixb), lambda i, j: (i, j, ublk)),
                  pl.BlockSpec((nbs, nstate), lambda i, j: (i, 0)),
                  pl.BlockSpec((nbs, nstate), lambda i, j: (i, 0))]
                 + [full(a) for a in perms]
                 + [full(wa), full(abre), full(abim), full(wcre), full(wcim), full(dskip), full(wglu), full(gs5)],
        out_specs=[pl.BlockSpec((nbs, lc, mixb), lambda i, j: (i, j, 0)),
                   pl.BlockSpec((nbs, nstate), lambda i, j: (i, 0)),
                   pl.BlockSpec((nbs, nstate), lambda i, j: (i, 0))],
        out_shape=[jax.ShapeDtypeStruct((nb_total, t, mixb), F32),
                   jax.ShapeDtypeStruct((nb_total, nstate), F32),
                   jax.ShapeDtypeStruct((nb_total, nstate), F32)],
        scratch_shapes=[pltpu.VMEM((nblk, SUBLANE, LANE), F32),
                        pltpu.VMEM((nblk, SUBLANE, LANE), F32)]
                       + [pltpu.VMEM((nblk // slabs, nbs * lc, LANE), F32) for _ in range(2 * slabs)],
        compiler_params=_params("parallel", "arbitrary"),
        name="s5",
    )(uab3, h0re, h0im, *perms, wa, abre, abim, wcre, wcim, dskip, wglu, gs5)
    return y.reshape(nb_total * t, mixb), hre, him


PART_ROWS = 256


def _row_parts(bb, tt, part_rows=PART_ROWS):
    rows = bb * tt
    nparts = max(1, rows // part_rows)
    if bb > 1:
        nparts = min(nparts, bb)
    prow = rows // nparts
    out = []
    for r in range(nparts):
        flat = slice(r * prow, (r + 1) * prow)
        if bb == 1:
            out.append(((slice(None), flat), slice(None), flat))
        else:
            bs = slice(r * bb // nparts, (r + 1) * bb // nparts)
            out.append(((bs, slice(None)), bs, flat))
    return out


def _outproj_body(o_ref, y_ref, wt_ref, wb_ref, x_ref, mod_ref, g_ref, out_ref, *, gate):
    mix = (jnp.dot(o_ref[...], wt_ref[...], preferred_element_type=F32)
           + _mm(y_ref[...], wb_ref[...]))
    n = mix * lax.rsqrt(jnp.mean(mix * mix, axis=-1, keepdims=True) + EPS) * g_ref[...]
    x = x_ref[...]
    out_ref[...] = x + mod_ref[:, gate:gate + 1, :] * n.reshape(x.shape)


def _outproj(o, y5, wtop, wbot, x3, mod, g, *, gate, rows):
    nb, t, d = x3.shape
    bb, tt = _token_blocks(nb, t, rows)
    nt = t // tt
    ka, kb = o.shape[1], y5.shape[1]
    mod, mod_shape, mod_index = _mod_block(mod, bb, nt)
    return pl.pallas_call(
        functools.partial(_outproj_body, gate=gate),
        grid=(nb * t // rows,),
        in_specs=[pl.BlockSpec((rows, ka), lambda i: (i, 0)),
                  pl.BlockSpec((rows, kb), lambda i: (i, 0)),
                  pl.BlockSpec((ka, d), lambda i: (0, 0)),
                  pl.BlockSpec((kb, d), lambda i: (0, 0)),
                  pl.BlockSpec((bb, tt, d), lambda i: (i // nt, i % nt, 0)),
                  pl.BlockSpec(mod_shape, mod_index),
                  pl.BlockSpec((1, d), lambda i: (0, 0))],
        out_specs=pl.BlockSpec((bb, tt, d), lambda i: (i // nt, i % nt, 0)),
        out_shape=jax.ShapeDtypeStruct((nb, t, d), F32),
        compiler_params=_params("parallel"),
        name="outproj",
    )(o, y5, wtop, wbot, x3, mod, g)


def _ffn_body(x_ref, mod_ref, gpre_ref, wg_ref, wu_ref, wd_ref, gpost_ref, out_ref, h_ref,
              *, shift, scale, gate):
    j = pl.program_id(1)
    last = pl.num_programs(1) - 1
    bb, tt, d = x_ref.shape
    parts = _row_parts(bb, tt)

    def swiglu(h):
        a = _silu(_mm(h, wg_ref[...])) * _mm(h, wu_ref[...])
        return _mm(a, wd_ref[...])

    @pl.when(j == 0)
    def _():
        for xi, mi, hr in parts:
            x = x_ref[xi]
            h = _modulated_norm(x, gpre_ref[...], mod_ref[mi, scale:scale + 1, :], mod_ref[mi, shift:shift + 1, :])
            h = h.reshape(hr.stop - hr.start, d).astype(BF16)
            h_ref[hr, :] = h
            out_ref[xi] = swiglu(h).reshape(x.shape)

    @pl.when((j > 0) & (j < last))
    def _():
        out_ref[...] += swiglu(h_ref[...]).reshape(out_ref.shape)

    @pl.when((j == last) & (j > 0))
    def _():
        for xi, mi, hr in parts:
            x = x_ref[xi]
            f = out_ref[xi] + swiglu(h_ref[hr, :]).reshape(x.shape)
            n = f * lax.rsqrt(jnp.mean(f * f, axis=-1, keepdims=True) + EPS) * gpost_ref[...]
            out_ref[xi] = x + mod_ref[mi, gate:gate + 1, :] * n


def _ffn(x3, mod, gpre, wg, wu, wd, gpost, *, shift, scale, gate, rows, tf):
    nb, t, d = x3.shape
    dff = wg.shape[1]
    assert dff // tf >= 2
    bb, tt = _token_blocks(nb, t, rows)
    nt = t // tt
    once = dict(pipeline_mode=pl.Buffered(1)) if bb > 1 else {}
    mod, mod_shape, mod_index = _mod_block(mod, bb, nt)
    return pl.pallas_call(
        functools.partial(_ffn_body, shift=shift, scale=scale, gate=gate),
        grid=(nb * t // rows, dff // tf),
        in_specs=[pl.BlockSpec((bb, tt, d), lambda i, j: (i // nt, i % nt, 0), **once),
                  pl.BlockSpec(mod_shape, lambda i, j: mod_index(i), **once),
                  pl.BlockSpec((1, d), lambda i, j: (0, 0)),
                  pl.BlockSpec((d, tf), lambda i, j: (0, j)),
                  pl.BlockSpec((d, tf), lambda i, j: (0, j)),
                  pl.BlockSpec((tf, d), lambda i, j: (j, 0)),
                  pl.BlockSpec((1, d), lambda i, j: (0, 0))],
        out_specs=pl.BlockSpec((bb, tt, d), lambda i, j: (i // nt, i % nt, 0)),
        out_shape=jax.ShapeDtypeStruct((nb, t, d), F32),
        scratch_shapes=[pltpu.VMEM((rows, d), BF16)],
        compiler_params=_params("parallel", "arbitrary"),
        name="ffn",
    )(x3, mod, gpre, wg, wu, wd, gpost)


ROWS = 512
FFN_ROWS = 1024
FFN_COLS = 256
PROMPT_CHUNK = 128
PROMPT_BASE = 16
SAMPLE_BATCH = 8
S5_PROMPT_STEPS = 128
S5_SAMPLE_SEGMENTS = 8


def _layer_weights(w_ada, b_ada, g_pre_mix, g_post_mix, g_pre_ffn, g_post_ffn,
                   w_in, w_conv, a_log, dt_bias, g_dn_out,
                   lam_re, lam_im, log_step, b_re, b_im, c_re, c_im, d_skip,
                   w_glu, g_s5_out, w_out, w_gate, w_up, w_down):
    d = w_in.shape[0]
    nheads = a_log.shape[0]
    mix_a = nheads * DN_HEAD_DIM
    pad = lambda v: jnp.pad(v, (0, LANE - v.shape[0]))[None, :]
    wa, abre, abim = _s5_prep(lam_re, lam_im, log_step, b_re, b_im)
    return dict(
        w_ada=w_ada, b_ada=b_ada[None, :],
        g_pre_mix=g_pre_mix[None, :], g_post_mix=g_post_mix[None, :],
        g_pre_ffn=g_pre_ffn[None, :], g_post_ffn=g_post_ffn[None, :],
        w_proj=_win_prep(jnp.swapaxes(w_in, 0, 1), 4 * mix_a, 2 * nheads), u_col=4 * mix_a,
        w_conv=w_conv, alog=pad(a_log), dtb=pad(dt_bias), gdn=g_dn_out[None, :],
        wa=wa, abre=abre, abim=abim, wcre=_s5_out_weights(c_re), wcim=_s5_out_weights(c_im),
        dskip=d_skip.reshape(1, -1), wglu=w_glu.astype(BF16), gs5=g_s5_out[None, :],
        wtop=w_out[:mix_a].astype(BF16), wbot=w_out[mix_a:].astype(BF16),
        wg=w_gate, wu=w_up, wd=w_down,
    )


def _layer_group(x3, mod, conv_state, s0, h0re, h0im, lw, *, prompt):
    nb, t, d = x3.shape
    rows = min(ROWS, nb * t)
    nheads = s0.shape[1]
    proj, cnew = _inproj(x3, mod, lw["g_pre_mix"], lw["w_proj"], lw["w_conv"], conv_state, shift=0, scale=1,
                         rows=rows if t >= rows else rows // 2, nheads=nheads)
    if prompt:
        o, snew = _delta_prompt(proj, s0, lw["alog"], lw["dtb"], lw["gdn"],
                                nb=nb, t=t, c=PROMPT_CHUNK, base=PROMPT_BASE)
        y5, hre, him = _s5(proj, lw["u_col"], h0re, h0im, lw["wa"], lw["abre"], lw["abim"], lw["wcre"], lw["wcim"],
                           lw["dskip"], lw["wglu"], lw["gs5"], nb_total=nb, t=t, nb=nb, lc=S5_PROMPT_STEPS, nseg=1)
    else:
        o, snew = _delta_sample(proj, s0, lw["alog"], lw["dtb"], lw["gdn"], nb=nb, t=t, bb=SAMPLE_BATCH)
        y5, hre, him = _s5(proj, lw["u_col"], h0re, h0im, lw["wa"], lw["abre"], lw["abim"], lw["wcre"], lw["wcim"],
                           lw["dskip"], lw["wglu"], lw["gs5"], nb_total=nb, t=t, nb=SUBLANE, lc=t,
                           nseg=min(S5_SAMPLE_SEGMENTS, nb // SUBLANE))
    x1 = _outproj(o, y5, lw["wtop"], lw["wbot"], x3, mod, lw["g_post_mix"], gate=2, rows=rows)
    x2 = _ffn(x1, mod, lw["g_pre_ffn"], lw["wg"], lw["wu"], lw["wd"], lw["g_post_ffn"],
              shift=3, scale=4, gate=5, rows=min(FFN_ROWS, nb * t), tf=FFN_COLS)
    ngroups = h0re.shape[1] // S5_STATE
    return x2, cnew, snew, hre.reshape(nb, ngroups, S5_STATE), him.reshape(nb, ngroups, S5_STATE)


def kernel(x_prompt, x_sample, c_prompt, c_sample, state_conv, state_delta, state_ssm_re, state_ssm_im, w_ada, b_ada, g_pre_mix, g_post_mix, g_pre_ffn, g_post_ffn, w_in, w_conv, a_log, dt_bias, g_dn_out, lam_re, lam_im, log_step, b_re, b_im, c_re, c_im, d_skip, w_glu, g_s5_out, w_out, w_gate, w_up, w_down):
    weights = (w_ada, b_ada, g_pre_mix, g_post_mix, g_pre_ffn, g_post_ffn,
               w_in, w_conv, a_log, dt_bias, g_dn_out,
               lam_re, lam_im, log_step, b_re, b_im, c_re, c_im, d_skip,
               w_glu, g_s5_out, w_out, w_gate, w_up, w_down)
    depth = w_ada.shape[0]
    bp, d = c_prompt.shape
    bs = c_sample.shape[0]
    dt_ = x_prompt.dtype
    nheads = state_delta.shape[2]
    nstate = state_ssm_re.shape[2] * state_ssm_re.shape[3]
    crows = -(-(bp + bs) // SUBLANE) * SUBLANE
    c_all = jnp.concatenate([c_sample, c_prompt, jnp.zeros((crows - bp - bs, d), dt_)], axis=0)
    yp, ys = x_prompt, x_sample
    outs = [[] for _ in range(8)]
    for l in range(depth):
        lw = _layer_weights(*[w[l] for w in weights])
        mod = _ada(c_all, lw["w_ada"], lw["b_ada"]).reshape(crows, 6, d)
        zc = jnp.zeros((bp, CONV_WIDTH - 1, state_conv.shape[3]), dt_)
        zd = jnp.zeros((bp,) + state_delta.shape[2:], dt_)
        zs = jnp.zeros((bp, nstate), dt_)
        yp, c1, d1, r1, i1 = _layer_group(yp, (mod, bs), zc, zd, zs, zs, lw, prompt=True)
        ys, c2, d2, r2, i2 = _layer_group(ys, (mod, 0), state_conv[l], state_delta[l],
                                          state_ssm_re[l].reshape(bs, nstate), state_ssm_im[l].reshape(bs, nstate),
                                          lw, prompt=False)
        for lst, v in zip(outs, (c1, d1, r1, i1, c2, d2, r2, i2)):
            lst.append(v)
    stacked = [jnp.stack(v) for v in outs]
    return (yp, ys, *stacked)
```

```python
import functools
import math

import jax
import jax.numpy as jnp
import numpy as np
from jax import lax
from jax.experimental import pallas as pl
from jax.experimental.pallas import tpu as pltpu

F32 = jnp.float32
BF16 = jnp.bfloat16
EPS = 1e-6
LANE = 128
SUBLANE = 8
V7X_VMEM_BYTES = 64 << 20
VMEM_LIMIT = V7X_VMEM_BYTES - (8 << 20)

DN_HEAD_DIM = 128
CONV_WIDTH = 4
S5_GROUP = 16
S5_STATE = 64
S5_SLAB_GROUPS = LANE // S5_GROUP
S5_SLAB_STATES = S5_SLAB_GROUPS * S5_STATE


def _params(*sem):
    return pltpu.CompilerParams(dimension_semantics=sem, vmem_limit_bytes=VMEM_LIMIT)


def _silu(x):
    return x * jax.nn.sigmoid(x)


def _mm(a, b):
    return jnp.dot(a.astype(BF16), b.astype(BF16), preferred_element_type=F32)


def _mm_nt(a, b):
    return lax.dot_general(a.astype(BF16), b.astype(BF16), (((1,), (1,)), ((), ())),
                           preferred_element_type=F32)


def _mm_tn(a, b):
    return lax.dot_general(a.astype(BF16), b.astype(BF16), (((0,), (0,)), ((), ())),
                           preferred_element_type=F32)


def _split3(x):
    x1 = x.astype(BF16)
    r1 = x - x1.astype(F32)
    x2 = r1.astype(BF16)
    x3 = (r1 - x2.astype(F32)).astype(BF16)
    return x1, x2, x3


def _ada_body(c_ref, w_ref, b_ref, o_ref):
    s = _silu(c_ref[...])
    o_ref[...] = _mm(s, w_ref[...]) + b_ref[...]


def _ada(c, w, b):
    rows, d = c.shape
    n = w.shape[1]
    tn = 1024
    return pl.pallas_call(
        _ada_body,
        grid=(n // tn,),
        in_specs=[pl.BlockSpec((rows, d), lambda j: (0, 0)),
                  pl.BlockSpec((d, tn), lambda j: (0, j)),
                  pl.BlockSpec((1, tn), lambda j: (0, j))],
        out_specs=pl.BlockSpec((rows, tn), lambda j: (0, j)),
        out_shape=jax.ShapeDtypeStruct((rows, n), F32),
        compiler_params=_params("arbitrary"),
        name="ada",
    )(c, w, b)


def _mod_block(mod, bb, nt):
    arr, first = mod
    assert first % bb == 0
    return arr, (bb, 6, arr.shape[2]), lambda i: (first // bb + i // nt, 0, 0)


def _token_blocks(nb, t, rows):
    if t >= rows:
        assert t % rows == 0
        return 1, rows
    assert rows % t == 0 and nb % (rows // t) == 0
    return rows // t, t


def _modulated_norm(x, g, scale, shift):
    y = x * lax.rsqrt(jnp.mean(x * x, axis=-1, keepdims=True) + EPS) * g
    return y * (1.0 + scale) + shift


def _win_prep_body(w_ref, o_ref, *, main, gates):
    n = w_ref.shape[0]
    rest = n - main - gates
    o_ref[0:main, :] = w_ref[0:main, :].astype(BF16)
    o_ref[main:main + rest, :] = w_ref[main + gates:n, :].astype(BF16)
    o_ref[main + rest:main + rest + gates, :] = w_ref[main:main + gates, :].astype(BF16)
    o_ref[main + rest + gates:, :] = jnp.zeros((LANE - gates, o_ref.shape[1]), BF16)


def _win_prep(w_in_t, main, gates):
    n, d = w_in_t.shape
    cb = min(512, d)
    nout = n - gates + LANE
    return pl.pallas_call(
        functools.partial(_win_prep_body, main=main, gates=gates),
        grid=(d // cb,),
        in_specs=[pl.BlockSpec((n, cb), lambda i: (0, i))],
        out_specs=pl.BlockSpec((nout, cb), lambda i: (0, i)),
        out_shape=jax.ShapeDtypeStruct((nout, d), BF16),
        compiler_params=_params("parallel"),
        name="winprep",
    )(w_in_t)


INPROJ_PART_ROWS = 512


def _inproj_body(x_ref, mod_ref, g_ref, w_ref, wc_ref, cst_ref, o_ref, cnew_ref, xc_ref,
                 *, shift, scale, nheads, nt):
    bb, tt, d = x_ref.shape
    rows = bb * tt
    dh = DN_HEAD_DIM
    mix = nheads * dh
    halo = SUBLANE
    taps = CONV_WIDTH - 1
    n = w_ref.shape[0]

    if bb == 1:
        @pl.when(pl.program_id(0) % nt == 0)
        def _():
            xc_ref[halo - taps:halo, :] = cst_ref[0]

    h = _modulated_norm(x_ref[...], g_ref[...], mod_ref[:, scale:scale + 1, :], mod_ref[:, shift:shift + 1, :])
    h = h.reshape(rows, d).astype(BF16)
    parts = _row_parts(bb, tt, INPROJ_PART_ROWS)
    cw = mix

    def project(flat, c0, c1):
        return lax.dot_general(h[flat, :], w_ref[c0:c1, :], (((1,), (1,)), ((), ())), preferred_element_type=F32)

    def conv_silu(p, c0, c1, raw):
        xi, _, flat = parts[p]
        cols = slice(c0, c1)
        if bb == 1:
            r0, r1 = flat.start, flat.stop
            xc_ref[halo + r0:halo + r1, cols] = raw
            conv = xc_ref[halo - taps + r0:halo - taps + r1, cols] * wc_ref[0:1, cols]
            for j in range(1, CONV_WIDTH):
                conv = conv + xc_ref[halo - taps + j + r0:halo - taps + j + r1, cols] * wc_ref[j:j + 1, cols]
            if p == len(parts) - 1:
                cnew_ref[0, :, cols] = xc_ref[halo + rows - taps:halo + rows, cols]
                xc_ref[0:halo, cols] = xc_ref[rows:rows + halo, cols]
        else:
            bs = xi[0]
            nseq = bs.stop - bs.start
            xc_ref[bs, halo - taps:halo, cols] = cst_ref[bs, :, cols]
            xc_ref[bs, halo:halo + tt, cols] = raw.reshape(nseq, tt, c1 - c0)
            conv = xc_ref[bs, halo - taps:halo - taps + tt, cols] * wc_ref[0:1, cols]
            for j in range(1, CONV_WIDTH):
                conv = conv + xc_ref[bs, halo - taps + j:halo - taps + j + tt, cols] * wc_ref[j:j + 1, cols]
            cnew_ref[bs, :, cols] = xc_ref[bs, halo + tt - taps:halo + tt, cols]
            conv = conv.reshape(nseq * tt, c1 - c0)
        return _silu(conv)

    def finish(p, c0, c1, raw):
        act = conv_silu(p, c0, c1, raw)
        flat = parts[p][2]
        if c0 >= 2 * mix:
            o_ref[flat, c0:c1] = act
            return
        gain = dh ** -0.5 if c0 < mix else 1.0
        for s in range(0, c1 - c0, dh):
            a = act[:, s:s + dh]
            o_ref[flat, c0 + s:c0 + s + dh] = a * (lax.rsqrt(jnp.sum(a * a, axis=-1, keepdims=True) + 1e-6) * gain)

    items = [(p, c0, min(c0 + cw, 3 * mix)) for p in range(len(parts)) for c0 in range(0, 3 * mix, cw)]
    rest = [(p, 3 * mix, n) for p in range(len(parts))]
    pending = project(parts[items[0][0]][2], items[0][1], items[0][2])
    for k, (p, c0, c1) in enumerate(items):
        nxt = None
        if k + 1 < len(items):
            pn, n0, n1 = items[k + 1]
            nxt = project(parts[pn][2], n0, n1)
        if k < len(rest):
            pr, r0, r1 = rest[k]
            o_ref[parts[pr][2], r0:r1] = project(parts[pr][2], r0, r1)
        finish(p, c0, c1, pending)
        pending = nxt
    for pr, r0, r1 in rest[len(items):]:
        o_ref[parts[pr][2], r0:r1] = project(parts[pr][2], r0, r1)


def _inproj(x3, mod, g, w_t, w_conv, conv_state, *, shift, scale, rows, nheads):
    nb, t, d = x3.shape
    n = w_t.shape[0]
    qkv = 3 * nheads * DN_HEAD_DIM
    bb, tt = _token_blocks(nb, t, rows)
    nt = t // tt
    mod, mod_shape, mod_index = _mod_block(mod, bb, nt)
    if bb == 1:
        scratch = pltpu.VMEM((rows + SUBLANE, qkv), F32)
    else:
        scratch = pltpu.VMEM((bb, SUBLANE + tt, qkv), F32)
    return pl.pallas_call(
        functools.partial(_inproj_body, shift=shift, scale=scale, nheads=nheads, nt=nt),
        grid=(nb * t // rows,),
        in_specs=[pl.BlockSpec((bb, tt, d), lambda i: (i // nt, i % nt, 0), pipeline_mode=pl.Buffered(1)),
                  pl.BlockSpec(mod_shape, mod_index),
                  pl.BlockSpec((1, d), lambda i: (0, 0)),
                  pl.BlockSpec((n, d), lambda i: (0, 0), pipeline_mode=pl.Buffered(1)),
                  pl.BlockSpec((CONV_WIDTH, qkv), lambda i: (0, 0)),
                  pl.BlockSpec((bb, CONV_WIDTH - 1, qkv), lambda i: (i // nt, 0, 0))],
        out_specs=[pl.BlockSpec((rows, n), lambda i: (i, 0)),
                   pl.BlockSpec((bb, CONV_WIDTH - 1, qkv), lambda i: (i // nt, 0, 0))],
        out_shape=[jax.ShapeDtypeStruct((nb * t, n), F32),
                   jax.ShapeDtypeStruct((nb, CONV_WIDTH - 1, qkv), F32)],
        scratch_shapes=[scratch],
        compiler_params=_params("arbitrary"),
        name="inproj",
    )(x3, mod, g, w_t, w_conv, conv_state)


def _seg_masks(c, seg):
    row = lax.broadcasted_iota(jnp.int32, (c, c), 0)
    col = lax.broadcasted_iota(jnp.int32, (c, c), 1)
    sh = int(math.log2(seg))
    same = (row >> sh) == (col >> sh)
    return row, col, same, same & (row >= col), same & (row > col)


def _tri_inverse_offdiag(lmats, row, col, base, seg):
    bsh = int(math.log2(base))
    blk = (row >> bsh) == (col >> bsh)
    l0s = [jnp.where(blk, l, 0.0) for l in lmats]
    pts = [-l for l in l0s]
    qs = [_mm(l, l) for l in l0s]
    k = 2
    while k < base:
        pts = [p + q + _mm(p, q) for p, q in zip(pts, qs)]
        k *= 2
        if k < base:
            qs = [_mm(q, q) for q in qs]
    m = base
    while m < seg:
        msh = int(math.log2(m))
        sel = ((row >> (msh + 1)) == (col >> (msh + 1))) & ((row >> msh) != (col >> msh))
        es = [jnp.where(sel, l, 0.0) for l in lmats]
        xs = [e + _mm(p, e) for p, e in zip(pts, es)]
        pts = [p - (x + _mm(x, p)) for p, x in zip(pts, xs)]
        m *= 2
    return pts


def _delta_gates(ab, alog, dtb, causal_bf, seg, nheads):
    lane = lax.broadcasted_iota(jnp.int32, ab.shape, 1)
    x = ab + dtb
    softplus = jnp.maximum(x, 0.0) + jnp.log1p(jnp.exp(-jnp.abs(x)))
    g = jnp.where(lane < nheads, -jnp.exp(alog) * softplus, 0.0)
    beta = jax.nn.sigmoid(ab)
    pieces = _split3(g)
    gcum = sum(jnp.dot(causal_bf, p, preferred_element_type=F32) for p in pieces)
    g3 = gcum.reshape(gcum.shape[0] // seg, seg, gcum.shape[1])
    glast = jnp.broadcast_to(g3[:, seg - 1:seg, :], g3.shape).reshape(gcum.shape)
    return gcum, beta, glast


def _decay_logits(gcum, nheads):
    c = gcum.shape[0]
    if c < LANE:
        gcum = jnp.concatenate([gcum, jnp.zeros((LANE - c, LANE), F32)], axis=0)
    return gcum.T


def _delta_intra(act, nheads, gates, dlog, masks, base, seg):
    gcum, beta, glast = gates
    row, col, _, causal, strict = masks
    c = act.shape[0]
    dh = DN_HEAD_DIM
    eg_all = jnp.exp(gcum)
    ed_all = jnp.exp(glast - gcum)
    ks, kqs_lhs, rhss, qgs, kdecs = [], [], [], [], []
    for h in range(nheads):
        q = act[:, h * dh:(h + 1) * dh]
        k = act[:, (nheads + h) * dh:(nheads + h + 1) * dh]
        v = act[:, (2 * nheads + h) * dh:(2 * nheads + h + 1) * dh]
        b = beta[:, nheads + h:nheads + h + 1]
        eg = eg_all[:, h:h + 1]
        kb = k * b
        ks.append(k)
        kqs_lhs.append(jnp.concatenate([kb, q], axis=0))
        rhss.append(jnp.concatenate([v * b, kb * eg], axis=1))
        qgs.append(q * eg)
        kdecs.append(k * ed_all[:, h:h + 1])
    kqs = [_mm_nt(a, k) for a, k in zip(kqs_lhs, ks)]
    lmats, attns = [], []
    for h in range(nheads):
        dl = gcum[:, h:h + 1] - dlog[h:h + 1, 0:c]
        dec = jnp.where(causal, jnp.exp(jnp.where(causal, dl, 0.0)), 0.0)
        lmats.append(jnp.where(strict, kqs[h][:c] * dec, 0.0))
        attns.append(kqs[h][c:] * dec)
    tts = _tri_inverse_offdiag(lmats, row, col, base, seg)
    uws = [r + _mm(t, r) for t, r in zip(tts, rhss)]
    return [uw[:, :dh] for uw in uws], [uw[:, dh:] for uw in uws], qgs, kdecs, attns


def _delta_out(o, z, gdn):
    on = o * lax.rsqrt(jnp.mean(o * o, axis=-1, keepdims=True) + EPS) * gdn
    return on * _silu(z)


def _delta_prompt_body(qkv_ref, z_ref, ab_ref, s0_ref, alog_ref, dtb_ref, gdn_ref,
                       o_ref, snew_ref, s_ref, *, c, nheads, base):
    dh = DN_HEAD_DIM

    @pl.when(pl.program_id(1) == 0)
    def _():
        s_ref[...] = s0_ref[0]

    act = qkv_ref[...]
    masks = _seg_masks(c, c)
    causal_bf = jnp.where(masks[3], 1.0, 0.0).astype(BF16)
    gates = _delta_gates(ab_ref[...], alog_ref[...], dtb_ref[...], causal_bf, c, nheads)
    dlog = _decay_logits(gates[0], nheads)
    eglast = jnp.exp(gates[2][0:1, :])
    gdn = gdn_ref[...]
    us, ws_, qgs, kdecs, attns = _delta_intra(act, nheads, gates, dlog, masks, base, c)
    heads = range(nheads)
    ss = [s_ref[h] for h in heads]
    wss = [_mm(jnp.concatenate([ws_[h], qgs[h]], axis=0), ss[h]) for h in heads]
    vnews = [us[h] - wss[h][:c] for h in heads]
    os_ = [wss[h][c:] + _mm(attns[h], vnews[h]) for h in heads]
    snews = [ss[h] * eglast[:, h:h + 1] + _mm_tn(kdecs[h], vnews[h]) for h in heads]
    for h in heads:
        s_ref[h] = snews[h]
        o_ref[:, h * dh:(h + 1) * dh] = _delta_out(os_[h], z_ref[:, h * dh:(h + 1) * dh], gdn).astype(o_ref.dtype)

    @pl.when(pl.program_id(1) == pl.num_programs(1) - 1)
    def _():
        snew_ref[0] = s_ref[...]


def _delta_prompt(proj, s0, alog, dtb, gdn, *, nb, t, c, base):
    nheads = s0.shape[1]
    dh = DN_HEAD_DIM
    mix = nheads * dh
    nc = t // c
    return pl.pallas_call(
        functools.partial(_delta_prompt_body, c=c, nheads=nheads, base=base),
        grid=(nb, nc),
        in_specs=[pl.BlockSpec((c, 3 * mix), lambda b, i: (b * nc + i, 0)),
                  pl.BlockSpec((c, mix), lambda b, i: (b * nc + i, 3)),
                  pl.BlockSpec((c, LANE), lambda b, i: (b * nc + i, proj.shape[1] // LANE - 1)),
                  pl.BlockSpec((1, nheads, dh, dh), lambda b, i: (b, 0, 0, 0)),
                  pl.BlockSpec((1, LANE), lambda b, i: (0, 0)),
                  pl.BlockSpec((1, LANE), lambda b, i: (0, 0)),
                  pl.BlockSpec((1, dh), lambda b, i: (0, 0))],
        out_specs=[pl.BlockSpec((c, mix), lambda b, i: (b * nc + i, 0)),
                   pl.BlockSpec((1, nheads, dh, dh), lambda b, i: (b, 0, 0, 0))],
        out_shape=[jax.ShapeDtypeStruct((nb * t, mix), BF16),
                   jax.ShapeDtypeStruct((nb, nheads, dh, dh), F32)],
        scratch_shapes=[pltpu.VMEM((nheads, dh, dh), F32)],
        compiler_params=_params("parallel", "arbitrary"),
        name="delta_prompt",
    )(proj, proj, proj, s0, alog, dtb, gdn)


def _delta_sample_body(qkv_ref, z_ref, ab_ref, s0_ref, alog_ref, dtb_ref, gdn_ref,
                       o_ref, snew_ref,
                       w_s, qg_s, u_s, kd_s, vn_s, qs_s, attn_s, egl_s, *, bb, t, nheads):
    dh = DN_HEAD_DIM
    c = bb * t
    act = qkv_ref[...]
    masks = _seg_masks(c, t)
    causal_bf = jnp.where(masks[3], 1.0, 0.0).astype(BF16)
    gates = _delta_gates(ab_ref[...], alog_ref[...], dtb_ref[...], causal_bf, t, nheads)
    dlog = _decay_logits(gates[0], nheads)
    egl_s[...] = jnp.exp(gates[2])
    us, ws_, qgs, kdecs, attns = _delta_intra(act, nheads, gates, dlog, masks, t, t)
    for h in range(nheads):
        hs = slice(h * dh, (h + 1) * dh)
        u_s[:, hs] = us[h]
        w_s[:, hs] = ws_[h]
        qg_s[:, hs] = qgs[h]
        kd_s[:, hs] = kdecs[h]
        attn_s[h] = attns[h]

    def per_batch(b, carry):
        r = pl.multiple_of(b * t, t)
        rows = pl.ds(r, t)
        heads = range(nheads)
        hsl = [slice(h * dh, (h + 1) * dh) for h in heads]
        ss = [s0_ref[b, h] for h in heads]
        wss = [_mm(jnp.concatenate([w_s[rows, hsl[h]], qg_s[rows, hsl[h]]], axis=0), ss[h]) for h in heads]
        vnews = [u_s[rows, hsl[h]] - wss[h][:t] for h in heads]
        upds = [_mm_tn(kd_s[rows, hsl[h]], vnews[h]) for h in heads]
        egl = egl_s[pl.ds(r, 1), :]
        for h in heads:
            vn_s[rows, hsl[h]] = vnews[h]
            qs_s[rows, hsl[h]] = wss[h][t:]
            snew_ref[b, h] = ss[h] * egl[:, h:h + 1] + upds[h]
        return carry

    lax.fori_loop(0, bb, per_batch, 0, unroll=2)

    gdn = gdn_ref[...]
    for h in range(nheads):
        hs = slice(h * dh, (h + 1) * dh)
        o = qs_s[:, hs] + _mm(attn_s[h], vn_s[:, hs])
        o_ref[:, hs] = _delta_out(o, z_ref[:, hs], gdn).astype(o_ref.dtype)


def _delta_sample(proj, s0, alog, dtb, gdn, *, nb, t, bb):
    nheads = s0.shape[1]
    dh = DN_HEAD_DIM
    mix = nheads * dh
    c = bb * t
    return pl.pallas_call(
        functools.partial(_delta_sample_body, bb=bb, t=t, nheads=nheads),
        grid=(nb // bb,),
        in_specs=[pl.BlockSpec((c, 3 * mix), lambda i: (i, 0)),
                  pl.BlockSpec((c, mix), lambda i: (i, 3)),
                  pl.BlockSpec((c, LANE), lambda i: (i, proj.shape[1] // LANE - 1)),
                  pl.BlockSpec((bb, nheads, dh, dh), lambda i: (i, 0, 0, 0)),
                  pl.BlockSpec((1, LANE), lambda i: (0, 0)),
                  pl.BlockSpec((1, LANE), lambda i: (0, 0)),
                  pl.BlockSpec((1, dh), lambda i: (0, 0))],
        out_specs=[pl.BlockSpec((c, mix), lambda i: (i, 0)),
                   pl.BlockSpec((bb, nheads, dh, dh), lambda i: (i, 0, 0, 0))],
        out_shape=[jax.ShapeDtypeStruct((nb * t, mix), BF16),
                   jax.ShapeDtypeStruct((nb, nheads, dh, dh), F32)],
        scratch_shapes=[pltpu.VMEM((c, mix), F32) for _ in range(6)]
                       + [pltpu.VMEM((nheads, c, c), F32), pltpu.VMEM((c, LANE), F32)],
        compiler_params=_params("parallel"),
        name="delta_sample",
    )(proj, proj, proj, s0, alog, dtb, gdn)


def _s5_prep_body(lre_ref, lim_ref, ls_ref, bre_ref, bim_ref, wa_ref, abre_ref, abim_ref):
    lre = jnp.minimum(lre_ref[...], -1e-4)
    lim = lim_ref[...]
    dt = jnp.exp(ls_ref[...])
    mag = jnp.exp(lre * dt)
    ang = lim * dt
    abr = mag * jnp.cos(ang)
    abi = mag * jnp.sin(ang)
    nr = abr - 1.0
    ni = abi
    den = lre * lre + lim * lim
    cor = (nr * lre + ni * lim) / den
    coi = (ni * lre - nr * lim) / den
    bre = bre_ref[...]
    bim = bim_ref[...]
    shape2 = bre.shape[1:]
    row = lax.broadcasted_iota(jnp.int32, shape2, 0)
    col = lax.broadcasted_iota(jnp.int32, shape2, 1)
    diag = ((row // S5_GROUP) == (col // S5_STATE))[None]
    half = shape2[1]
    rows = shape2[0]
    wre = jnp.where(diag, cor * bre - coi * bim, 0.0)
    wim = jnp.where(diag, cor * bim + coi * bre, 0.0)
    wa_ref[:, 0:rows, 0:half] = wre.astype(BF16)
    wa_ref[:, 0:rows, half:2 * half] = wim.astype(BF16)
    wa_ref[:, rows:2 * rows, 0:half] = (abr * wre - abi * wim).astype(BF16)
    wa_ref[:, rows:2 * rows, half:2 * half] = (abr * wim + abi * wre).astype(BF16)
    abre_ref[...] = abr
    abim_ref[...] = abi


def _s5_prep(lam_re, lam_im, log_step, b_re, b_im):
    groups, states, gch = b_re.shape
    slabs = groups // S5_SLAB_GROUPS
    cols = S5_SLAB_STATES

    def row_param(p):
        return p.reshape(slabs, 1, cols)

    def tiled_b(b):
        bt = jnp.transpose(b, (0, 2, 1)).reshape(slabs, LANE, states)
        return jnp.tile(bt, (1, 1, S5_SLAB_GROUPS))

    ls = jnp.broadcast_to(log_step[:, None], (groups, states))
    wa, abre, abim = pl.pallas_call(
        _s5_prep_body,
        out_shape=[jax.ShapeDtypeStruct((slabs, 2 * LANE, 2 * cols), BF16),
                   jax.ShapeDtypeStruct((slabs, 1, cols), F32),
                   jax.ShapeDtypeStruct((slabs, 1, cols), F32)],
        compiler_params=pltpu.CompilerParams(vmem_limit_bytes=VMEM_LIMIT),
        name="s5prep",
    )(row_param(lam_re), row_param(lam_im), row_param(ls), tiled_b(b_re), tiled_b(b_im))
    nblk = groups * states // LANE
    return wa, abre.reshape(nblk, 1, LANE), abim.reshape(nblk, 1, LANE)


def _s5_out_weights(c):
    groups, gch, states = c.shape
    slabs = groups // S5_SLAB_GROUPS
    ct = jnp.transpose(c, (0, 2, 1)).reshape(slabs, S5_SLAB_STATES, gch)
    ct = jnp.tile(ct, (1, 1, S5_SLAB_GROUPS))
    row = jnp.arange(S5_SLAB_STATES)[:, None] // states
    col = jnp.arange(LANE)[None, :] // gch
    return jnp.where((row == col)[None], ct, 0.0).astype(BF16)


def _gelu_tanh(x):
    return x * (0.5 * (1.0 + jnp.tanh(math.sqrt(2.0 / math.pi) * (x + 0.044715 * (x * x * x)))))


S5_PIPELINE_PARTS = 4


def _time_major_perms(nb, lc, nseg):
    seg_rows = nb * lc
    rows = nseg * seg_rows
    r = np.arange(rows)[:, None]
    c = np.arange(rows)[None, :]
    base, loc = r // seg_rows * seg_rows, r % seg_rows
    t_tm, b_tm = loc // nb, loc % nb
    to_tm = c == base + b_tm * lc + t_tm
    to_tm_prev = (c == base + b_tm * lc + t_tm - 1) & (t_tm % 2 == 1)
    to_bm = c == base + (loc % lc) * nb + loc // lc
    return tuple(jnp.asarray(m, BF16) for m in (to_tm, to_tm_prev, to_bm))


def _s5_body(u_ref, h0re_ref, h0im_ref, tm_ref, tmp_ref, bm_ref, wa_ref, abre_ref, abim_ref, wcre_ref, wcim_ref,
             dsk_ref, wglu_ref, gs5_ref, y_ref, hre_ref, him_ref, cre, cim, *xbufs, nb, lc, nseg):
    seg_rows = nb * lc
    rows = nseg * seg_rows
    nblk = cre.shape[0]
    slabs = wa_ref.shape[0]
    per_slab = nblk // slabs
    half = per_slab * LANE
    steps_per_tile = SUBLANE // nb
    assert steps_per_tile in (1, 2) and lc % steps_per_tile == 0
    assert nseg == 1 or nb == SUBLANE

    if nseg == 1:
        @pl.when(pl.program_id(1) == 0)
        def _():
            for j in range(nblk):
                if nb < SUBLANE:
                    cre[j] = jnp.zeros((SUBLANE, LANE), F32)
                    cim[j] = jnp.zeros((SUBLANE, LANE), F32)
                cre[j, SUBLANE - nb:SUBLANE, :] = h0re_ref[:, j * LANE:(j + 1) * LANE]
                cim[j, SUBLANE - nb:SUBLANE, :] = h0im_ref[:, j * LANE:(j + 1) * LANE]

    to_tm, to_bm = tm_ref[...], bm_ref[...]
    pieces = _split3(u_ref[...].reshape(rows, u_ref.shape[2]))[:2]
    u = sum(jnp.dot(to_tm, p, preferred_element_type=F32) for p in pieces)
    ub = u.astype(BF16)
    if steps_per_tile == 2:
        ub_prev = jnp.dot(tmp_ref[...], pieces[0], preferred_element_type=F32).astype(BF16)

    upper = lax.broadcasted_iota(jnp.int32, (SUBLANE, LANE), 0) >= nb
    parts = S5_PIPELINE_PARTS if rows % (S5_PIPELINE_PARTS * LANE) == 0 else 1
    prow = rows // parts

    def project_in(s, r):
        rs = slice(r * prow, (r + 1) * prow)
        sl = slice(s * LANE, (s + 1) * LANE)
        if steps_per_tile == 2:
            x = jnp.dot(jnp.concatenate([ub[rs, sl], ub_prev[rs, sl]], axis=1), wa_ref[s],
                        preferred_element_type=F32)
        else:
            x = jnp.dot(ub[rs, sl], wa_ref[s, 0:LANE, :], preferred_element_type=F32)
        for k in range(per_slab):
            xbufs[2 * s][k, rs, :] = x[:, k * LANE:(k + 1) * LANE]
            xbufs[2 * s + 1][k, rs, :] = x[:, half + k * LANE:half + (k + 1) * LANE]

    seg_tiles = seg_rows // SUBLANE

    def recur(s, r, coef, carry):
        xre, xim = xbufs[2 * s], xbufs[2 * s + 1]
        for v in range(r * prow // SUBLANE, (r + 1) * prow // SUBLANE):
            at = slice(v * SUBLANE, (v + 1) * SUBLANE)
            seg_rows_sl = slice(v // seg_tiles * nb, (v // seg_tiles + 1) * nb)
            if nseg > 1 and v % seg_tiles == 0:
                for n in range(per_slab):
                    cols = slice((s * per_slab + n) * LANE, (s * per_slab + n + 1) * LANE)
                    carry[n] = (h0re_ref[seg_rows_sl, cols], h0im_ref[seg_rows_sl, cols])
            for n in range(per_slab):
                hr, hi = carry[n]
                ar, ai = coef[n]
                if steps_per_tile == 2:
                    hr = jnp.where(upper, hr, pltpu.roll(hr, nb, 0))
                    hi = jnp.where(upper, hi, pltpu.roll(hi, nb, 0))
                nr = ar * hr - ai * hi + xre[n, at, :]
                ni = ar * hi + ai * hr + xim[n, at, :]
                xre[n, at, :] = nr
                xim[n, at, :] = ni
                carry[n] = (nr, ni)
            if nseg > 1 and v % seg_tiles == seg_tiles - 1:
                for n in range(per_slab):
                    cols = slice((s * per_slab + n) * LANE, (s * per_slab + n + 1) * LANE)
                    hre_ref[seg_rows_sl, cols], him_ref[seg_rows_sl, cols] = carry[n]

    def project_out(s, r):
        rs = slice(r * prow, (r + 1) * prow)
        hre = jnp.concatenate([xbufs[2 * s][k, rs, :] for k in range(per_slab)], axis=1)
        him = jnp.concatenate([xbufs[2 * s + 1][k, rs, :] for k in range(per_slab)], axis=1)
        return _mm(hre, wcre_ref[s]) - _mm(him, wcim_ref[s])

    ychunks = [[None] * parts for _ in range(slabs)]
    for r in range(parts):
        project_in(0, r)
    for s in range(slabs):
        js = list(range(s * per_slab, (s + 1) * per_slab))
        coef, carry = [], []
        for j in js:
            ar = jnp.broadcast_to(abre_ref[j], (SUBLANE, LANE))
            ai = jnp.broadcast_to(abim_ref[j], (SUBLANE, LANE))
            if steps_per_tile == 2:
                coef.append((jnp.where(upper, ar * ar - ai * ai, ar), jnp.where(upper, 2.0 * ar * ai, ai)))
            else:
                coef.append((ar, ai))
            carry.append((cre[j], cim[j]) if nseg == 1 else None)
        for r in range(parts):
            recur(s, r, coef, carry)
            if s + 1 < slabs:
                project_in(s + 1, r)
            if s >= 1:
                ychunks[s - 1][r] = project_out(s - 1, r)
        if nseg == 1:
            for n, j in enumerate(js):
                cre[j], cim[j] = carry[n]
    for r in range(parts):
        ychunks[slabs - 1][r] = project_out(slabs - 1, r)
    y = jnp.concatenate([jnp.concatenate(yc, axis=0) for yc in ychunks], axis=1) + dsk_ref[...] * u
    ab = _mm(_gelu_tanh(y), wglu_ref[...])
    n = y.shape[1]
    glu = ab[:, :n] * jax.nn.sigmoid(ab[:, n:])
    out = glu * lax.rsqrt(jnp.mean(glu * glu, axis=-1, keepdims=True) + EPS) * gs5_ref[...]
    out = jnp.dot(to_bm, out.astype(BF16), preferred_element_type=F32)
    y_ref[...] = out.reshape(y_ref.shape).astype(y_ref.dtype)

    if nseg == 1:
        @pl.when(pl.program_id(1) == pl.num_programs(1) - 1)
        def _():
            for j in range(nblk):
                hre_ref[:, j * LANE:(j + 1) * LANE] = cre[j, SUBLANE - nb:SUBLANE, :]
                him_ref[:, j * LANE:(j + 1) * LANE] = cim[j, SUBLANE - nb:SUBLANE, :]


def _s5(proj, u_col, h0re, h0im, wa, abre, abim, wcre, wcim, dskip, wglu, gs5, *, nb_total, t, nb, lc, nseg):
    mixb = dskip.shape[1]
    nstate = h0re.shape[1]
    nblk = nstate // LANE
    slabs = wa.shape[0]
    assert u_col % mixb == 0
    assert nseg == 1 or lc == t
    ublk = u_col // mixb
    nbs = nseg * nb
    uab3 = proj.reshape(nb_total, t, proj.shape[1])
    perms = _time_major_perms(nb, lc, nseg)
    full = lambda a: pl.BlockSpec(a.shape, lambda i, j: (0,) * a.ndim)
    y, hre, him = pl.pallas_call(
        functools.partial(_s5_body, nb=nb, lc=lc, nseg=nseg),
        grid=(nb_total // nbs, t // lc),
        in_specs=[pl.BlockSpec((nbs, lc, mixb), lambda i, j: (i, j, ublk)),
                  pl.BlockSpec((nbs, nstate), lambda i, j: (i, 0)),
                  pl.BlockSpec((nbs, nstate), lambda i, j: (i, 0))]
                 + [full(a) for a in perms]
                 + [full(wa), full(abre), full(abim), full(wcre), full(wcim), full(dskip), full(wglu), full(gs5)],
        out_specs=[pl.BlockSpec((nbs, lc, mixb), lambda i, j: (i, j, 0)),
                   pl.BlockSpec((nbs, nstate), lambda i, j: (i, 0)),
                   pl.BlockSpec((nbs, nstate), lambda i, j: (i, 0))],
        out_shape=[jax.ShapeDtypeStruct((nb_total, t, mixb), F32),
                   jax.ShapeDtypeStruct((nb_total, nstate), F32),
                   jax.ShapeDtypeStruct((nb_total, nstate), F32)],
        scratch_shapes=[pltpu.VMEM((nblk, SUBLANE, LANE), F32),
                        pltpu.VMEM((nblk, SUBLANE, LANE), F32)]
                       + [pltpu.VMEM((nblk // slabs, nbs * lc, LANE), F32) for _ in range(2 * slabs)],
        compiler_params=_params("parallel", "arbitrary"),
        name="s5",
    )(uab3, h0re, h0im, *perms, wa, abre, abim, wcre, wcim, dskip, wglu, gs5)
    return y.reshape(nb_total * t, mixb), hre, him


PART_ROWS = 256


def _row_parts(bb, tt, part_rows=PART_ROWS):
    rows = bb * tt
    nparts = max(1, rows // part_rows)
    if bb > 1:
        nparts = min(nparts, bb)
    prow = rows // nparts
    out = []
    for r in range(nparts):
        flat = slice(r * prow, (r + 1) * prow)
        if bb == 1:
            out.append(((slice(None), flat), slice(None), flat))
        else:
            bs = slice(r * bb // nparts, (r + 1) * bb // nparts)
            out.append(((bs, slice(None)), bs, flat))
    return out


def _outproj_body(o_ref, y_ref, wt_ref, wb_ref, x_ref, mod_ref, g_ref, out_ref, *, gate):
    mix = (jnp.dot(o_ref[...], wt_ref[...], preferred_element_type=F32)
           + _mm(y_ref[...], wb_ref[...]))
    n = mix * lax.rsqrt(jnp.mean(mix * mix, axis=-1, keepdims=True) + EPS) * g_ref[...]
    x = x_ref[...]
    out_ref[...] = x + mod_ref[:, gate:gate + 1, :] * n.reshape(x.shape)


def _outproj(o, y5, wtop, wbot, x3, mod, g, *, gate, rows):
    nb, t, d = x3.shape
    bb, tt = _token_blocks(nb, t, rows)
    nt = t // tt
    ka, kb = o.shape[1], y5.shape[1]
    mod, mod_shape, mod_index = _mod_block(mod, bb, nt)
    return pl.pallas_call(
        functools.partial(_outproj_body, gate=gate),
        grid=(nb * t // rows,),
        in_specs=[pl.BlockSpec((rows, ka), lambda i: (i, 0)),
                  pl.BlockSpec((rows, kb), lambda i: (i, 0)),
                  pl.BlockSpec((ka, d), lambda i: (0, 0)),
                  pl.BlockSpec((kb, d), lambda i: (0, 0)),
                  pl.BlockSpec((bb, tt, d), lambda i: (i // nt, i % nt, 0)),
                  pl.BlockSpec(mod_shape, mod_index),
                  pl.BlockSpec((1, d), lambda i: (0, 0))],
        out_specs=pl.BlockSpec((bb, tt, d), lambda i: (i // nt, i % nt, 0)),
        out_shape=jax.ShapeDtypeStruct((nb, t, d), F32),
        compiler_params=_params("parallel"),
        name="outproj",
    )(o, y5, wtop, wbot, x3, mod, g)


def _ffn_body(x_ref, mod_ref, gpre_ref, wg_ref, wu_ref, wd_ref, gpost_ref, out_ref, h_ref,
              *, shift, scale, gate):
    j = pl.program_id(1)
    last = pl.num_programs(1) - 1
    bb, tt, d = x_ref.shape
    parts = _row_parts(bb, tt)

    def swiglu(h):
        a = _silu(_mm(h, wg_ref[...])) * _mm(h, wu_ref[...])
        return _mm(a, wd_ref[...])

    @pl.when(j == 0)
    def _():
        for xi, mi, hr in parts:
            x = x_ref[xi]
            h = _modulated_norm(x, gpre_ref[...], mod_ref[mi, scale:scale + 1, :], mod_ref[mi, shift:shift + 1, :])
            h = h.reshape(hr.stop - hr.start, d).astype(BF16)
            h_ref[hr, :] = h
            out_ref[xi] = swiglu(h).reshape(x.shape)

    @pl.when((j > 0) & (j < last))
    def _():
        out_ref[...] += swiglu(h_ref[...]).reshape(out_ref.shape)

    @pl.when((j == last) & (j > 0))
    def _():
        for xi, mi, hr in parts:
            x = x_ref[xi]
            f = out_ref[xi] + swiglu(h_ref[hr, :]).reshape(x.shape)
            n = f * lax.rsqrt(jnp.mean(f * f, axis=-1, keepdims=True) + EPS) * gpost_ref[...]
            out_ref[xi] = x + mod_ref[mi, gate:gate + 1, :] * n


def _ffn(x3, mod, gpre, wg, wu, wd, gpost, *, shift, scale, gate, rows, tf):
    nb, t, d = x3.shape
    dff = wg.shape[1]
    assert dff // tf >= 2
    bb, tt = _token_blocks(nb, t, rows)
    nt = t // tt
    once = dict(pipeline_mode=pl.Buffered(1)) if bb > 1 else {}
    mod, mod_shape, mod_index = _mod_block(mod, bb, nt)
    return pl.pallas_call(
        functools.partial(_ffn_body, shift=shift, scale=scale, gate=gate),
        grid=(nb * t // rows, dff // tf),
        in_specs=[pl.BlockSpec((bb, tt, d), lambda i, j: (i // nt, i % nt, 0), **once),
                  pl.BlockSpec(mod_shape, lambda i, j: mod_index(i), **once),
                  pl.BlockSpec((1, d), lambda i, j: (0, 0)),
                  pl.BlockSpec((d, tf), lambda i, j: (0, j)),
                  pl.BlockSpec((d, tf), lambda i, j: (0, j)),
                  pl.BlockSpec((tf, d), lambda i, j: (j, 0)),
                  pl.BlockSpec((1, d), lambda i, j: (0, 0))],
        out_specs=pl.BlockSpec((bb, tt, d), lambda i, j: (i // nt, i % nt, 0)),
        out_shape=jax.ShapeDtypeStruct((nb, t, d), F32),
        scratch_shapes=[pltpu.VMEM((rows, d), BF16)],
        compiler_params=_params("parallel", "arbitrary"),
        name="ffn",
    )(x3, mod, gpre, wg, wu, wd, gpost)


ROWS = 512
FFN_ROWS = 1024
FFN_COLS = 256
PROMPT_CHUNK = 128
PROMPT_BASE = 16
SAMPLE_BATCH = 8
S5_PROMPT_STEPS = 128
S5_SAMPLE_SEGMENTS = 8


def _layer_weights(w_ada, b_ada, g_pre_mix, g_post_mix, g_pre_ffn, g_post_ffn,
                   w_in, w_conv, a_log, dt_bias, g_dn_out,
                   lam_re, lam_im, log_step, b_re, b_im, c_re, c_im, d_skip,
                   w_glu, g_s5_out, w_out, w_gate, w_up, w_down):
    d = w_in.shape[0]
    nheads = a_log.shape[0]
    mix_a = nheads * DN_HEAD_DIM
    pad = lambda v: jnp.pad(v, (0, LANE - v.shape[0]))[None, :]
    wa, abre, abim = _s5_prep(lam_re, lam_im, log_step, b_re, b_im)
    return dict(
        w_ada=w_ada, b_ada=b_ada[None, :],
        g_pre_mix=g_pre_mix[None, :], g_post_mix=g_post_mix[None, :],
        g_pre_ffn=g_pre_ffn[None, :], g_post_ffn=g_post_ffn[None, :],
        w_proj=_win_prep(jnp.swapaxes(w_in, 0, 1), 4 * mix_a, 2 * nheads), u_col=4 * mix_a,
        w_conv=w_conv, alog=pad(a_log), dtb=pad(dt_bias), gdn=g_dn_out[None, :],
        wa=wa, abre=abre, abim=abim, wcre=_s5_out_weights(c_re), wcim=_s5_out_weights(c_im),
        dskip=d_skip.reshape(1, -1), wglu=w_glu.astype(BF16), gs5=g_s5_out[None, :],
        wtop=w_out[:mix_a].astype(BF16), wbot=w_out[mix_a:].astype(BF16),
        wg=w_gate, wu=w_up, wd=w_down,
    )


def _layer_group(x3, mod, conv_state, s0, h0re, h0im, lw, *, prompt):
    nb, t, d = x3.shape
    rows = min(ROWS, nb * t)
    nheads = s0.shape[1]
    proj, cnew = _inproj(x3, mod, lw["g_pre_mix"], lw["w_proj"], lw["w_conv"], conv_state, shift=0, scale=1,
                         rows=rows if t >= rows else rows // 2, nheads=nheads)
    if prompt:
        o, snew = _delta_prompt(proj, s0, lw["alog"], lw["dtb"], lw["gdn"],
                                nb=nb, t=t, c=PROMPT_CHUNK, base=PROMPT_BASE)
        y5, hre, him = _s5(proj, lw["u_col"], h0re, h0im, lw["wa"], lw["abre"], lw["abim"], lw["wcre"], lw["wcim"],
                           lw["dskip"], lw["wglu"], lw["gs5"], nb_total=nb, t=t, nb=nb, lc=S5_PROMPT_STEPS, nseg=1)
    else:
        o, snew = _delta_sample(proj, s0, lw["alog"], lw["dtb"], lw["gdn"], nb=nb, t=t, bb=SAMPLE_BATCH)
        y5, hre, him = _s5(proj, lw["u_col"], h0re, h0im, lw["wa"], lw["abre"], lw["abim"], lw["wcre"], lw["wcim"],
                           lw["dskip"], lw["wglu"], lw["gs5"], nb_total=nb, t=t, nb=SUBLANE, lc=t,
                           nseg=min(S5_SAMPLE_SEGMENTS, nb // SUBLANE))
    x1 = _outproj(o, y5, lw["wtop"], lw["wbot"], x3, mod, lw["g_post_mix"], gate=2, rows=rows)
    x2 = _ffn(x1, mod, lw["g_pre_ffn"], lw["wg"], lw["wu"], lw["wd"], lw["g_post_ffn"],
              shift=3, scale=4, gate=5, rows=min(FFN_ROWS, nb * t), tf=FFN_COLS)
    ngroups = h0re.shape[1] // S5_STATE
    return x2, cnew, snew, hre.reshape(nb, ngroups, S5_STATE), him.reshape(nb, ngroups, S5_STATE)


def kernel(x_prompt, x_sample, c_prompt, c_sample, state_conv, state_delta, state_ssm_re, state_ssm_im, w_ada, b_ada, g_pre_mix, g_post_mix, g_pre_ffn, g_post_ffn, w_in, w_conv, a_log, dt_bias, g_dn_out, lam_re, lam_im, log_step, b_re, b_im, c_re, c_im, d_skip, w_glu, g_s5_out, w_out, w_gate, w_up, w_down):
    weights = (w_ada, b_ada, g_pre_mix, g_post_mix, g_pre_ffn, g_post_ffn,
               w_in, w_conv, a_log, dt_bias, g_dn_out,
               lam_re, lam_im, log_step, b_re, b_im, c_re, c_im, d_skip,
               w_glu, g_s5_out, w_out, w_gate, w_up, w_down)
    depth = w_ada.shape[0]
    bp, d = c_prompt.shape
    bs = c_sample.shape[0]
    dt_ = x_prompt.dtype
    nheads = state_delta.shape[2]
    nstate = state_ssm_re.shape[2] * state_ssm_re.shape[3]
    crows = -(-(bp + bs) // SUBLANE) * SUBLANE
    c_all = jnp.concatenate([c_sample, c_prompt, jnp.zeros((crows - bp - bs, d), dt_)], axis=0)
    yp, ys = x_prompt, x_sample
    outs = [[] for _ in range(8)]
    for l in range(depth):
        lw = _layer_weights(*[w[l] for w in weights])
        mod = _ada(c_all, lw["w_ada"], lw["b_ada"]).reshape(crows, 6, d)
        zc = jnp.zeros((bp, CONV_WIDTH - 1, state_conv.shape[3]), dt_)
        zd = jnp.zeros((bp,) + state_delta.shape[2:], dt_)
        zs = jnp.zeros((bp, nstate), dt_)
        yp, c1, d1, r1, i1 = _layer_group(yp, (mod, bs), zc, zd, zs, zs, lw, prompt=True)
        ys, c2, d2, r2, i2 = _layer_group(ys, (mod, 0), state_conv[l], state_delta[l],
                                          state_ssm_re[l].reshape(bs, nstate), state_ssm_im[l].reshape(bs, nstate),
                                          lw, prompt=False)
        for lst, v in zip(outs, (c1, d1, r1, i1, c2, d2, r2, i2)):
            lst.append(v)
    stacked = [jnp.stack(v) for v in outs]
    return (yp, ys, *stacked)
```

```python
import functools
import math

import jax
import jax.numpy as jnp
import numpy as np
from jax import lax
from jax.experimental import pallas as pl
from jax.experimental.pallas import tpu as pltpu

F32 = jnp.float32
BF16 = jnp.bfloat16
EPS = 1e-6
LANE = 128
SUBLANE = 8
V7X_VMEM_BYTES = 64 << 20
VMEM_LIMIT = V7X_VMEM_BYTES - (8 << 20)

DN_HEAD_DIM = 128
CONV_WIDTH = 4
S5_GROUP = 16
S5_STATE = 64
S5_SLAB_GROUPS = LANE // S5_GROUP
S5_SLAB_STATES = S5_SLAB_GROUPS * S5_STATE


def _params(*sem):
    return pltpu.CompilerParams(dimension_semantics=sem, vmem_limit_bytes=VMEM_LIMIT)


def _silu(x):
    return x * jax.nn.sigmoid(x)


def _mm(a, b):
    return jnp.dot(a.astype(BF16), b.astype(BF16), preferred_element_type=F32)


def _mm_nt(a, b):
    return lax.dot_general(a.astype(BF16), b.astype(BF16), (((1,), (1,)), ((), ())),
                           preferred_element_type=F32)


def _mm_tn(a, b):
    return lax.dot_general(a.astype(BF16), b.astype(BF16), (((0,), (0,)), ((), ())),
                           preferred_element_type=F32)


def _split3(x):
    x1 = x.astype(BF16)
    r1 = x - x1.astype(F32)
    x2 = r1.astype(BF16)
    x3 = (r1 - x2.astype(F32)).astype(BF16)
    return x1, x2, x3


def _ada_body(c_ref, w_ref, b_ref, o_ref):
    s = _silu(c_ref[...])
    o_ref[...] = _mm(s, w_ref[...]) + b_ref[...]


def _ada(c, w, b):
    rows, d = c.shape
    n = w.shape[1]
    tn = 1024
    return pl.pallas_call(
        _ada_body,
        grid=(n // tn,),
        in_specs=[pl.BlockSpec((rows, d), lambda j: (0, 0)),
                  pl.BlockSpec((d, tn), lambda j: (0, j)),
                  pl.BlockSpec((1, tn), lambda j: (0, j))],
        out_specs=pl.BlockSpec((rows, tn), lambda j: (0, j)),
        out_shape=jax.ShapeDtypeStruct((rows, n), F32),
        compiler_params=_params("arbitrary"),
        name="ada",
    )(c, w, b)


def _mod_block(mod, bb, nt):
    arr, first = mod
    assert first % bb == 0
    return arr, (bb, 6, arr.shape[2]), lambda i: (first // bb + i // nt, 0, 0)


def _token_blocks(nb, t, rows):
    if t >= rows:
        assert t % rows == 0
        return 1, rows
    assert rows % t == 0 and nb % (rows // t) == 0
    return rows // t, t


def _modulated_norm(x, g, scale, shift):
    y = x * lax.rsqrt(jnp.mean(x * x, axis=-1, keepdims=True) + EPS) * g
    return y * (1.0 + scale) + shift


def _win_prep_body(w_ref, o_ref, *, main, gates):
    n = w_ref.shape[0]
    rest = n - main - gates
    o_ref[0:main, :] = w_ref[0:main, :].astype(BF16)
    o_ref[main:main + rest, :] = w_ref[main + gates:n, :].astype(BF16)
    o_ref[main + rest:main + rest + gates, :] = w_ref[main:main + gates, :].astype(BF16)
    o_ref[main + rest + gates:, :] = jnp.zeros((LANE - gates, o_ref.shape[1]), BF16)


def _win_prep(w_in_t, main, gates):
    n, d = w_in_t.shape
    cb = min(512, d)
    nout = n - gates + LANE
    return pl.pallas_call(
        functools.partial(_win_prep_body, main=main, gates=gates),
        grid=(d // cb,),
        in_specs=[pl.BlockSpec((n, cb), lambda i: (0, i))],
        out_specs=pl.BlockSpec((nout, cb), lambda i: (0, i)),
        out_shape=jax.ShapeDtypeStruct((nout, d), BF16),
        compiler_params=_params("parallel"),
        name="winprep",
    )(w_in_t)


def _nmm_body(x_ref, mod_ref, g_ref, w_ref, o_ref, *, shift, scale):
    x = x_ref[...]
    bb, tt, d = x.shape
    h = _modulated_norm(x, g_ref[...], mod_ref[:, scale:scale + 1, :], mod_ref[:, shift:shift + 1, :])
    o_ref[...] = _mm_nt(h.reshape(bb * tt, d), w_ref[...])


def _nmm(x3, mod, g, w_t, *, shift, scale, rows):
    nb, t, d = x3.shape
    n = w_t.shape[0]
    bb, tt = _token_blocks(nb, t, rows)
    nt = t // tt
    mod, mod_shape, mod_index = _mod_block(mod, bb, nt)
    return pl.pallas_call(
        functools.partial(_nmm_body, shift=shift, scale=scale),
        grid=(nb * t // rows,),
        in_specs=[pl.BlockSpec((bb, tt, d), lambda i: (i // nt, i % nt, 0)),
                  pl.BlockSpec(mod_shape, mod_index),
                  pl.BlockSpec((1, d), lambda i: (0, 0)),
                  pl.BlockSpec((n, d), lambda i: (0, 0), pipeline_mode=pl.Buffered(1))],
        out_specs=pl.BlockSpec((rows, n), lambda i: (i, 0)),
        out_shape=jax.ShapeDtypeStruct((nb * t, n), F32),
        compiler_params=_params("parallel"),
        name="nmm",
    )(x3, mod, g, w_t)


def _seg_masks(c, seg):
    row = lax.broadcasted_iota(jnp.int32, (c, c), 0)
    col = lax.broadcasted_iota(jnp.int32, (c, c), 1)
    sh = int(math.log2(seg))
    same = (row >> sh) == (col >> sh)
    return row, col, same, same & (row >= col), same & (row > col)


def _tri_inverse_offdiag(lmats, row, col, base, seg):
    bsh = int(math.log2(base))
    blk = (row >> bsh) == (col >> bsh)
    l0s = [jnp.where(blk, l, 0.0) for l in lmats]
    pts = [-l for l in l0s]
    qs = [_mm(l, l) for l in l0s]
    k = 2
    while k < base:
        pts = [p + q + _mm(p, q) for p, q in zip(pts, qs)]
        k *= 2
        if k < base:
            qs = [_mm(q, q) for q in qs]
    m = base
    while m < seg:
        msh = int(math.log2(m))
        sel = ((row >> (msh + 1)) == (col >> (msh + 1))) & ((row >> msh) != (col >> msh))
        es = [jnp.where(sel, l, 0.0) for l in lmats]
        xs = [e + _mm(p, e) for p, e in zip(pts, es)]
        pts = [p - (x + _mm(x, p)) for p, x in zip(pts, xs)]
        m *= 2
    return pts


def _delta_gates(ab, alog, dtb, causal_bf, seg, nheads):
    lane = lax.broadcasted_iota(jnp.int32, ab.shape, 1)
    x = ab + dtb
    softplus = jnp.maximum(x, 0.0) + jnp.log1p(jnp.exp(-jnp.abs(x)))
    g = jnp.where(lane < nheads, -jnp.exp(alog) * softplus, 0.0)
    beta = jax.nn.sigmoid(ab)
    pieces = _split3(g)
    gcum = sum(jnp.dot(causal_bf, p, preferred_element_type=F32) for p in pieces)
    g3 = gcum.reshape(gcum.shape[0] // seg, seg, gcum.shape[1])
    glast = jnp.broadcast_to(g3[:, seg - 1:seg, :], g3.shape).reshape(gcum.shape)
    return gcum, beta, glast


def _decay_logits(gcum, nheads):
    c = gcum.shape[0]
    if c < LANE:
        gcum = jnp.concatenate([gcum, jnp.zeros((LANE - c, LANE), F32)], axis=0)
    return gcum.T


def _delta_intra(act, nheads, gates, dlog, masks, base, seg):
    gcum, beta, glast = gates
    row, col, _, causal, strict = masks
    c = act.shape[0]
    dh = DN_HEAD_DIM
    eg_all = jnp.exp(gcum)
    ed_all = jnp.exp(glast - gcum)
    ks, kqs_lhs, rhss, qgs, kdecs = [], [], [], [], []
    for h in range(nheads):
        q = act[:, h * dh:(h + 1) * dh]
        k = act[:, (nheads + h) * dh:(nheads + h + 1) * dh]
        v = act[:, (2 * nheads + h) * dh:(2 * nheads + h + 1) * dh]
        q = q * lax.rsqrt(jnp.sum(q * q, axis=-1, keepdims=True) + 1e-6) * (dh ** -0.5)
        k = k * lax.rsqrt(jnp.sum(k * k, axis=-1, keepdims=True) + 1e-6)
        b = beta[:, nheads + h:nheads + h + 1]
        eg = eg_all[:, h:h + 1]
        kb = k * b
        ks.append(k)
        kqs_lhs.append(jnp.concatenate([kb, q], axis=0))
        rhss.append(jnp.concatenate([v * b, kb * eg], axis=1))
        qgs.append(q * eg)
        kdecs.append(k * ed_all[:, h:h + 1])
    kqs = [_mm_nt(a, k) for a, k in zip(kqs_lhs, ks)]
    lmats, attns = [], []
    for h in range(nheads):
        dl = gcum[:, h:h + 1] - dlog[h:h + 1, 0:c]
        dec = jnp.where(causal, jnp.exp(jnp.where(causal, dl, 0.0)), 0.0)
        lmats.append(jnp.where(strict, kqs[h][:c] * dec, 0.0))
        attns.append(kqs[h][c:] * dec)
    tts = _tri_inverse_offdiag(lmats, row, col, base, seg)
    uws = [r + _mm(t, r) for t, r in zip(tts, rhss)]
    return [uw[:, :dh] for uw in uws], [uw[:, dh:] for uw in uws], qgs, kdecs, attns


def _delta_out(o, z, gdn):
    on = o * lax.rsqrt(jnp.mean(o * o, axis=-1, keepdims=True) + EPS) * gdn
    return on * _silu(z)


def _delta_prompt_body(qkv_ref, z_ref, ab_ref, cst_ref, s0_ref, wc_ref, alog_ref, dtb_ref, gdn_ref,
                       o_ref, cnew_ref, snew_ref, xc_ref, s_ref, *, c, nheads, base):
    dh = DN_HEAD_DIM
    halo = SUBLANE
    taps = CONV_WIDTH - 1

    @pl.when(pl.program_id(1) == 0)
    def _():
        xc_ref[0:halo, :] = jnp.zeros((halo, xc_ref.shape[1]), F32)
        xc_ref[halo - taps:halo, :] = cst_ref[0]
        s_ref[...] = s0_ref[0]

    xc_ref[halo:halo + c, :] = qkv_ref[...]
    conv = xc_ref[halo - taps:halo - taps + c, :] * wc_ref[0:1, :]
    for j in range(1, CONV_WIDTH):
        conv = conv + xc_ref[halo - taps + j:halo - taps + j + c, :] * wc_ref[j:j + 1, :]
    cnew_ref[0] = xc_ref[halo + c - taps:halo + c, :]
    xc_ref[0:halo, :] = xc_ref[c:c + halo, :]
    act = _silu(conv)

    masks = _seg_masks(c, c)
    causal_bf = jnp.where(masks[3], 1.0, 0.0).astype(BF16)
    gates = _delta_gates(ab_ref[...], alog_ref[...], dtb_ref[...], causal_bf, c, nheads)
    dlog = _decay_logits(gates[0], nheads)
    eglast = jnp.exp(gates[2][0:1, :])
    gdn = gdn_ref[...]
    us, ws_, qgs, kdecs, attns = _delta_intra(act, nheads, gates, dlog, masks, base, c)
    heads = range(nheads)
    ss = [s_ref[h] for h in heads]
    wss = [_mm(jnp.concatenate([ws_[h], qgs[h]], axis=0), ss[h]) for h in heads]
    vnews = [us[h] - wss[h][:c] for h in heads]
    os_ = [wss[h][c:] + _mm(attns[h], vnews[h]) for h in heads]
    snews = [ss[h] * eglast[:, h:h + 1] + _mm_tn(kdecs[h], vnews[h]) for h in heads]
    for h in heads:
        s_ref[h] = snews[h]
        o_ref[:, h * dh:(h + 1) * dh] = _delta_out(os_[h], z_ref[:, h * dh:(h + 1) * dh], gdn).astype(o_ref.dtype)

    @pl.when(pl.program_id(1) == pl.num_programs(1) - 1)
    def _():
        snew_ref[0] = s_ref[...]


def _delta_prompt(qkvz, uab, conv_state, s0, w_conv, alog, dtb, gdn, *, nb, t, c, base):
    nheads = s0.shape[1]
    dh = DN_HEAD_DIM
    mix = nheads * dh
    nc = t // c
    return pl.pallas_call(
        functools.partial(_delta_prompt_body, c=c, nheads=nheads, base=base),
        grid=(nb, nc),
        in_specs=[pl.BlockSpec((c, 3 * mix), lambda b, i: (b * nc + i, 0)),
                  pl.BlockSpec((c, mix), lambda b, i: (b * nc + i, 3)),
                  pl.BlockSpec((c, LANE), lambda b, i: (b * nc + i, uab.shape[1] // LANE - 1)),
                  pl.BlockSpec((1, CONV_WIDTH - 1, 3 * mix), lambda b, i: (b, 0, 0)),
                  pl.BlockSpec((1, nheads, dh, dh), lambda b, i: (b, 0, 0, 0)),
                  pl.BlockSpec((CONV_WIDTH, 3 * mix), lambda b, i: (0, 0)),
                  pl.BlockSpec((1, LANE), lambda b, i: (0, 0)),
                  pl.BlockSpec((1, LANE), lambda b, i: (0, 0)),
                  pl.BlockSpec((1, dh), lambda b, i: (0, 0))],
        out_specs=[pl.BlockSpec((c, mix), lambda b, i: (b * nc + i, 0)),
                   pl.BlockSpec((1, CONV_WIDTH - 1, 3 * mix), lambda b, i: (b, 0, 0)),
                   pl.BlockSpec((1, nheads, dh, dh), lambda b, i: (b, 0, 0, 0))],
        out_shape=[jax.ShapeDtypeStruct((nb * t, mix), BF16),
                   jax.ShapeDtypeStruct((nb, CONV_WIDTH - 1, 3 * mix), F32),
                   jax.ShapeDtypeStruct((nb, nheads, dh, dh), F32)],
        scratch_shapes=[pltpu.VMEM((c + SUBLANE, 3 * mix), F32),
                        pltpu.VMEM((nheads, dh, dh), F32)],
        compiler_params=_params("parallel", "arbitrary"),
        name="delta_prompt",
    )(qkvz, qkvz, uab, conv_state, s0, w_conv, alog, dtb, gdn)


def _delta_sample_body(qkv_ref, z_ref, ab_ref, cst_ref, s0_ref, wc_ref, alog_ref, dtb_ref, gdn_ref,
                       o_ref, cnew_ref, snew_ref,
                       xs_ref, w_s, qg_s, u_s, kd_s, vn_s, qs_s, attn_s, egl_s, *, bb, t, nheads):
    dh = DN_HEAD_DIM
    c = bb * t
    halo = SUBLANE
    taps = CONV_WIDTH - 1

    xs_ref[:, 0:halo, :] = jnp.zeros((bb, halo, xs_ref.shape[2]), F32)
    xs_ref[:, halo - taps:halo, :] = cst_ref[...]
    xs_ref[:, halo:halo + t, :] = qkv_ref[...]
    conv = xs_ref[:, halo - taps:halo - taps + t, :] * wc_ref[0:1, :]
    for j in range(1, CONV_WIDTH):
        conv = conv + xs_ref[:, halo - taps + j:halo - taps + j + t, :] * wc_ref[j:j + 1, :]
    cnew_ref[...] = xs_ref[:, halo + t - taps:halo + t, :]
    act = _silu(conv).reshape(c, conv.shape[2])

    masks = _seg_masks(c, t)
    causal_bf = jnp.where(masks[3], 1.0, 0.0).astype(BF16)
    gates = _delta_gates(ab_ref[...], alog_ref[...], dtb_ref[...], causal_bf, t, nheads)
    dlog = _decay_logits(gates[0], nheads)
    egl_s[...] = jnp.exp(gates[2])
    us, ws_, qgs, kdecs, attns = _delta_intra(act, nheads, gates, dlog, masks, t, t)
    for h in range(nheads):
        hs = slice(h * dh, (h + 1) * dh)
        u_s[:, hs] = us[h]
        w_s[:, hs] = ws_[h]
        qg_s[:, hs] = qgs[h]
        kd_s[:, hs] = kdecs[h]
        attn_s[h] = attns[h]

    def per_batch(b, carry):
        r = pl.multiple_of(b * t, t)
        rows = pl.ds(r, t)
        heads = range(nheads)
        hsl = [slice(h * dh, (h + 1) * dh) for h in heads]
        ss = [s0_ref[b, h] for h in heads]
        wss = [_mm(jnp.concatenate([w_s[rows, hsl[h]], qg_s[rows, hsl[h]]], axis=0), ss[h]) for h in heads]
        vnews = [u_s[rows, hsl[h]] - wss[h][:t] for h in heads]
        upds = [_mm_tn(kd_s[rows, hsl[h]], vnews[h]) for h in heads]
        egl = egl_s[pl.ds(r, 1), :]
        for h in heads:
            vn_s[rows, hsl[h]] = vnews[h]
            qs_s[rows, hsl[h]] = wss[h][t:]
            snew_ref[b, h] = ss[h] * egl[:, h:h + 1] + upds[h]
        return carry

    lax.fori_loop(0, bb, per_batch, 0, unroll=2)

    gdn = gdn_ref[...]
    for h in range(nheads):
        hs = slice(h * dh, (h + 1) * dh)
        o = qs_s[:, hs] + _mm(attn_s[h], vn_s[:, hs])
        o_ref[:, hs] = _delta_out(o, z_ref[:, hs], gdn).astype(o_ref.dtype)


def _delta_sample(qkvz, uab, conv_state, s0, w_conv, alog, dtb, gdn, *, nb, t, bb):
    nheads = s0.shape[1]
    dh = DN_HEAD_DIM
    mix = nheads * dh
    c = bb * t
    qkvz3 = qkvz.reshape(nb, t, qkvz.shape[1])
    return pl.pallas_call(
        functools.partial(_delta_sample_body, bb=bb, t=t, nheads=nheads),
        grid=(nb // bb,),
        in_specs=[pl.BlockSpec((bb, t, 3 * mix), lambda i: (i, 0, 0)),
                  pl.BlockSpec((c, mix), lambda i: (i, 3)),
                  pl.BlockSpec((c, LANE), lambda i: (i, uab.shape[1] // LANE - 1)),
                  pl.BlockSpec((bb, CONV_WIDTH - 1, 3 * mix), lambda i: (i, 0, 0)),
                  pl.BlockSpec((bb, nheads, dh, dh), lambda i: (i, 0, 0, 0)),
                  pl.BlockSpec((CONV_WIDTH, 3 * mix), lambda i: (0, 0)),
                  pl.BlockSpec((1, LANE), lambda i: (0, 0)),
                  pl.BlockSpec((1, LANE), lambda i: (0, 0)),
                  pl.BlockSpec((1, dh), lambda i: (0, 0))],
        out_specs=[pl.BlockSpec((c, mix), lambda i: (i, 0)),
                   pl.BlockSpec((bb, CONV_WIDTH - 1, 3 * mix), lambda i: (i, 0, 0)),
                   pl.BlockSpec((bb, nheads, dh, dh), lambda i: (i, 0, 0, 0))],
        out_shape=[jax.ShapeDtypeStruct((nb * t, mix), BF16),
                   jax.ShapeDtypeStruct((nb, CONV_WIDTH - 1, 3 * mix), F32),
                   jax.ShapeDtypeStruct((nb, nheads, dh, dh), F32)],
        scratch_shapes=[pltpu.VMEM((bb, SUBLANE + t, 3 * mix), F32)]
                       + [pltpu.VMEM((c, mix), F32) for _ in range(6)]
                       + [pltpu.VMEM((nheads, c, c), F32), pltpu.VMEM((c, LANE), F32)],
        compiler_params=_params("parallel"),
        name="delta_sample",
    )(qkvz3, qkvz, uab, conv_state, s0, w_conv, alog, dtb, gdn)


def _s5_prep_body(lre_ref, lim_ref, ls_ref, bre_ref, bim_ref, wa_ref, abre_ref, abim_ref):
    lre = jnp.minimum(lre_ref[...], -1e-4)
    lim = lim_ref[...]
    dt = jnp.exp(ls_ref[...])
    mag = jnp.exp(lre * dt)
    ang = lim * dt
    abr = mag * jnp.cos(ang)
    abi = mag * jnp.sin(ang)
    nr = abr - 1.0
    ni = abi
    den = lre * lre + lim * lim
    cor = (nr * lre + ni * lim) / den
    coi = (ni * lre - nr * lim) / den
    bre = bre_ref[...]
    bim = bim_ref[...]
    shape2 = bre.shape[1:]
    row = lax.broadcasted_iota(jnp.int32, shape2, 0)
    col = lax.broadcasted_iota(jnp.int32, shape2, 1)
    diag = ((row // S5_GROUP) == (col // S5_STATE))[None]
    half = shape2[1]
    rows = shape2[0]
    wre = jnp.where(diag, cor * bre - coi * bim, 0.0)
    wim = jnp.where(diag, cor * bim + coi * bre, 0.0)
    wa_ref[:, 0:rows, 0:half] = wre.astype(BF16)
    wa_ref[:, 0:rows, half:2 * half] = wim.astype(BF16)
    wa_ref[:, rows:2 * rows, 0:half] = (abr * wre - abi * wim).astype(BF16)
    wa_ref[:, rows:2 * rows, half:2 * half] = (abr * wim + abi * wre).astype(BF16)
    abre_ref[...] = abr
    abim_ref[...] = abi


def _s5_prep(lam_re, lam_im, log_step, b_re, b_im):
    groups, states, gch = b_re.shape
    slabs = groups // S5_SLAB_GROUPS
    cols = S5_SLAB_STATES

    def row_param(p):
        return p.reshape(slabs, 1, cols)

    def tiled_b(b):
        bt = jnp.transpose(b, (0, 2, 1)).reshape(slabs, LANE, states)
        return jnp.tile(bt, (1, 1, S5_SLAB_GROUPS))

    ls = jnp.broadcast_to(log_step[:, None], (groups, states))
    wa, abre, abim = pl.pallas_call(
        _s5_prep_body,
        out_shape=[jax.ShapeDtypeStruct((slabs, 2 * LANE, 2 * cols), BF16),
                   jax.ShapeDtypeStruct((slabs, 1, cols), F32),
                   jax.ShapeDtypeStruct((slabs, 1, cols), F32)],
        compiler_params=pltpu.CompilerParams(vmem_limit_bytes=VMEM_LIMIT),
        name="s5prep",
    )(row_param(lam_re), row_param(lam_im), row_param(ls), tiled_b(b_re), tiled_b(b_im))
    nblk = groups * states // LANE
    return wa, abre.reshape(nblk, 1, LANE), abim.reshape(nblk, 1, LANE)


def _s5_out_weights(c):
    groups, gch, states = c.shape
    slabs = groups // S5_SLAB_GROUPS
    ct = jnp.transpose(c, (0, 2, 1)).reshape(slabs, S5_SLAB_STATES, gch)
    ct = jnp.tile(ct, (1, 1, S5_SLAB_GROUPS))
    row = jnp.arange(S5_SLAB_STATES)[:, None] // states
    col = jnp.arange(LANE)[None, :] // gch
    return jnp.where((row == col)[None], ct, 0.0).astype(BF16)


def _gelu_tanh(x):
    return x * (0.5 * (1.0 + jnp.tanh(math.sqrt(2.0 / math.pi) * (x + 0.044715 * (x * x * x)))))


S5_PIPELINE_PARTS = 4


def _time_major_perms(nb, lc, nseg):
    seg_rows = nb * lc
    rows = nseg * seg_rows
    r = np.arange(rows)[:, None]
    c = np.arange(rows)[None, :]
    base, loc = r // seg_rows * seg_rows, r % seg_rows
    t_tm, b_tm = loc // nb, loc % nb
    to_tm = c == base + b_tm * lc + t_tm
    to_tm_prev = (c == base + b_tm * lc + t_tm - 1) & (t_tm % 2 == 1)
    to_bm = c == base + (loc % lc) * nb + loc // lc
    return tuple(jnp.asarray(m, BF16) for m in (to_tm, to_tm_prev, to_bm))


def _s5_body(u_ref, h0re_ref, h0im_ref, tm_ref, tmp_ref, bm_ref, wa_ref, abre_ref, abim_ref, wcre_ref, wcim_ref,
             dsk_ref, wglu_ref, gs5_ref, y_ref, hre_ref, him_ref, cre, cim, *xbufs, nb, lc, nseg):
    seg_rows = nb * lc
    rows = nseg * seg_rows
    nblk = cre.shape[0]
    slabs = wa_ref.shape[0]
    per_slab = nblk // slabs
    half = per_slab * LANE
    steps_per_tile = SUBLANE // nb
    assert steps_per_tile in (1, 2) and lc % steps_per_tile == 0
    assert nseg == 1 or nb == SUBLANE

    if nseg == 1:
        @pl.when(pl.program_id(1) == 0)
        def _():
            for j in range(nblk):
                if nb < SUBLANE:
                    cre[j] = jnp.zeros((SUBLANE, LANE), F32)
                    cim[j] = jnp.zeros((SUBLANE, LANE), F32)
                cre[j, SUBLANE - nb:SUBLANE, :] = h0re_ref[:, j * LANE:(j + 1) * LANE]
                cim[j, SUBLANE - nb:SUBLANE, :] = h0im_ref[:, j * LANE:(j + 1) * LANE]

    to_tm, to_bm = tm_ref[...], bm_ref[...]
    pieces = _split3(u_ref[...].reshape(rows, u_ref.shape[2]))[:2]
    u = sum(jnp.dot(to_tm, p, preferred_element_type=F32) for p in pieces)
    ub = u.astype(BF16)
    if steps_per_tile == 2:
        ub_prev = jnp.dot(tmp_ref[...], pieces[0], preferred_element_type=F32).astype(BF16)

    upper = lax.broadcasted_iota(jnp.int32, (SUBLANE, LANE), 0) >= nb
    parts = S5_PIPELINE_PARTS if rows % (S5_PIPELINE_PARTS * LANE) == 0 else 1
    prow = rows // parts

    def project_in(s, r):
        rs = slice(r * prow, (r + 1) * prow)
        sl = slice(s * LANE, (s + 1) * LANE)
        if steps_per_tile == 2:
            x = jnp.dot(jnp.concatenate([ub[rs, sl], ub_prev[rs, sl]], axis=1), wa_ref[s],
                        preferred_element_type=F32)
        else:
            x = jnp.dot(ub[rs, sl], wa_ref[s, 0:LANE, :], preferred_element_type=F32)
        for k in range(per_slab):
            xbufs[2 * s][k, rs, :] = x[:, k * LANE:(k + 1) * LANE]
            xbufs[2 * s + 1][k, rs, :] = x[:, half + k * LANE:half + (k + 1) * LANE]

    seg_tiles = seg_rows // SUBLANE

    def recur(s, r, coef, carry):
        xre, xim = xbufs[2 * s], xbufs[2 * s + 1]
        for v in range(r * prow // SUBLANE, (r + 1) * prow // SUBLANE):
            at = slice(v * SUBLANE, (v + 1) * SUBLANE)
            seg_rows_sl = slice(v // seg_tiles * nb, (v // seg_tiles + 1) * nb)
            if nseg > 1 and v % seg_tiles == 0:
                for n in range(per_slab):
                    cols = slice((s * per_slab + n) * LANE, (s * per_slab + n + 1) * LANE)
                    carry[n] = (h0re_ref[seg_rows_sl, cols], h0im_ref[seg_rows_sl, cols])
            for n in range(per_slab):
                hr, hi = carry[n]
                ar, ai = coef[n]
                if steps_per_tile == 2:
                    hr = jnp.where(upper, hr, pltpu.roll(hr, nb, 0))
                    hi = jnp.where(upper, hi, pltpu.roll(hi, nb, 0))
                nr = ar * hr - ai * hi + xre[n, at, :]
                ni = ar * hi + ai * hr + xim[n, at, :]
                xre[n, at, :] = nr
                xim[n, at, :] = ni
                carry[n] = (nr, ni)
            if nseg > 1 and v % seg_tiles == seg_tiles - 1:
                for n in range(per_slab):
                    cols = slice((s * per_slab + n) * LANE, (s * per_slab + n + 1) * LANE)
                    hre_ref[seg_rows_sl, cols], him_ref[seg_rows_sl, cols] = carry[n]

    def project_out(s, r):
        rs = slice(r * prow, (r + 1) * prow)
        hre = jnp.concatenate([xbufs[2 * s][k, rs, :] for k in range(per_slab)], axis=1)
        him = jnp.concatenate([xbufs[2 * s + 1][k, rs, :] for k in range(per_slab)], axis=1)
        return _mm(hre, wcre_ref[s]) - _mm(him, wcim_ref[s])

    ychunks = [[None] * parts for _ in range(slabs)]
    for r in range(parts):
        project_in(0, r)
    for s in range(slabs):
        js = list(range(s * per_slab, (s + 1) * per_slab))
        coef, carry = [], []
        for j in js:
            ar = jnp.broadcast_to(abre_ref[j], (SUBLANE, LANE))
            ai = jnp.broadcast_to(abim_ref[j], (SUBLANE, LANE))
            if steps_per_tile == 2:
                coef.append((jnp.where(upper, ar * ar - ai * ai, ar), jnp.where(upper, 2.0 * ar * ai, ai)))
            else:
                coef.append((ar, ai))
            carry.append((cre[j], cim[j]) if nseg == 1 else None)
        for r in range(parts):
            recur(s, r, coef, carry)
            if s + 1 < slabs:
                project_in(s + 1, r)
            if s >= 1:
                ychunks[s - 1][r] = project_out(s - 1, r)
        if nseg == 1:
            for n, j in enumerate(js):
                cre[j], cim[j] = carry[n]
    for r in range(parts):
        ychunks[slabs - 1][r] = project_out(slabs - 1, r)
    y = jnp.concatenate([jnp.concatenate(yc, axis=0) for yc in ychunks], axis=1) + dsk_ref[...] * u
    ab = _mm(_gelu_tanh(y), wglu_ref[...])
    n = y.shape[1]
    glu = ab[:, :n] * jax.nn.sigmoid(ab[:, n:])
    out = glu * lax.rsqrt(jnp.mean(glu * glu, axis=-1, keepdims=True) + EPS) * gs5_ref[...]
    out = jnp.dot(to_bm, out.astype(BF16), preferred_element_type=F32)
    y_ref[...] = out.reshape(y_ref.shape).astype(y_ref.dtype)

    if nseg == 1:
        @pl.when(pl.program_id(1) == pl.num_programs(1) - 1)
        def _():
            for j in range(nblk):
                hre_ref[:, j * LANE:(j + 1) * LANE] = cre[j, SUBLANE - nb:SUBLANE, :]
                him_ref[:, j * LANE:(j + 1) * LANE] = cim[j, SUBLANE - nb:SUBLANE, :]


def _s5(proj, u_col, h0re, h0im, wa, abre, abim, wcre, wcim, dskip, wglu, gs5, *, nb_total, t, nb, lc, nseg):
    mixb = dskip.shape[1]
    nstate = h0re.shape[1]
    nblk = nstate // LANE
    slabs = wa.shape[0]
    assert u_col % mixb == 0
    assert nseg == 1 or lc == t
    ublk = u_col // mixb
    nbs = nseg * nb
    uab3 = proj.reshape(nb_total, t, proj.shape[1])
    perms = _time_major_perms(nb, lc, nseg)
    full = lambda a: pl.BlockSpec(a.shape, lambda i, j: (0,) * a.ndim)
    y, hre, him = pl.pallas_call(
        functools.partial(_s5_body, nb=nb, lc=lc, nseg=nseg),
        grid=(nb_total // nbs, t // lc),
        in_specs=[pl.BlockSpec((nbs, lc, mixb), lambda i, j: (i, j, ublk)),
                  pl.BlockSpec((nbs, nstate), lambda i, j: (i, 0)),
                  pl.BlockSpec((nbs, nstate), lambda i, j: (i, 0))]
                 + [full(a) for a in perms]
                 + [full(wa), full(abre), full(abim), full(wcre), full(wcim), full(dskip), full(wglu), full(gs5)],
        out_specs=[pl.BlockSpec((nbs, lc, mixb), lambda i, j: (i, j, 0)),
                   pl.BlockSpec((nbs, nstate), lambda i, j: (i, 0)),
                   pl.BlockSpec((nbs, nstate), lambda i, j: (i, 0))],
        out_shape=[jax.ShapeDtypeStruct((nb_total, t, mixb), F32),
                   jax.ShapeDtypeStruct((nb_total, nstate), F32),
                   jax.ShapeDtypeStruct((nb_total, nstate), F32)],
        scratch_shapes=[pltpu.VMEM((nblk, SUBLANE, LANE), F32),
                        pltpu.VMEM((nblk, SUBLANE, LANE), F32)]
                       + [pltpu.VMEM((nblk // slabs, nbs * lc, LANE), F32) for _ in range(2 * slabs)],
        compiler_params=_params("parallel", "arbitrary"),
        name="s5",
    )(uab3, h0re, h0im, *perms, wa, abre, abim, wcre, wcim, dskip, wglu, gs5)
    return y.reshape(nb_total * t, mixb), hre, him


PART_ROWS = 256


def _row_parts(bb, tt):
    rows = bb * tt
    nparts = max(1, rows // PART_ROWS)
    if bb > 1:
        nparts = min(nparts, bb)
    prow = rows // nparts
    out = []
    for r in range(nparts):
        flat = slice(r * prow, (r + 1) * prow)
        if bb == 1:
            out.append(((slice(None), flat), slice(None), flat))
        else:
            bs = slice(r * bb // nparts, (r + 1) * bb // nparts)
            out.append(((bs, slice(None)), bs, flat))
    return out


def _outproj_body(o_ref, y_ref, wt_ref, wb_ref, x_ref, mod_ref, g_ref, out_ref, *, gate):
    mix = (jnp.dot(o_ref[...], wt_ref[...], preferred_element_type=F32)
           + _mm(y_ref[...], wb_ref[...]))
    n = mix * lax.rsqrt(jnp.mean(mix * mix, axis=-1, keepdims=True) + EPS) * g_ref[...]
    x = x_ref[...]
    out_ref[...] = x + mod_ref[:, gate:gate + 1, :] * n.reshape(x.shape)


def _outproj(o, y5, wtop, wbot, x3, mod, g, *, gate, rows):
    nb, t, d = x3.shape
    bb, tt = _token_blocks(nb, t, rows)
    nt = t // tt
    ka, kb = o.shape[1], y5.shape[1]
    mod, mod_shape, mod_index = _mod_block(mod, bb, nt)
    return pl.pallas_call(
        functools.partial(_outproj_body, gate=gate),
        grid=(nb * t // rows,),
        in_specs=[pl.BlockSpec((rows, ka), lambda i: (i, 0)),
                  pl.BlockSpec((rows, kb), lambda i: (i, 0)),
                  pl.BlockSpec((ka, d), lambda i: (0, 0)),
                  pl.BlockSpec((kb, d), lambda i: (0, 0)),
                  pl.BlockSpec((bb, tt, d), lambda i: (i // nt, i % nt, 0)),
                  pl.BlockSpec(mod_shape, mod_index),
                  pl.BlockSpec((1, d), lambda i: (0, 0))],
        out_specs=pl.BlockSpec((bb, tt, d), lambda i: (i // nt, i % nt, 0)),
        out_shape=jax.ShapeDtypeStruct((nb, t, d), F32),
        compiler_params=_params("parallel"),
        name="outproj",
    )(o, y5, wtop, wbot, x3, mod, g)


def _ffn_body(x_ref, mod_ref, gpre_ref, wg_ref, wu_ref, wd_ref, gpost_ref, out_ref, h_ref,
              *, shift, scale, gate):
    j = pl.program_id(1)
    last = pl.num_programs(1) - 1
    bb, tt, d = x_ref.shape
    parts = _row_parts(bb, tt)

    def swiglu(h):
        a = _silu(_mm(h, wg_ref[...])) * _mm(h, wu_ref[...])
        return _mm(a, wd_ref[...])

    @pl.when(j == 0)
    def _():
        for xi, mi, hr in parts:
            x = x_ref[xi]
            h = _modulated_norm(x, gpre_ref[...], mod_ref[mi, scale:scale + 1, :], mod_ref[mi, shift:shift + 1, :])
            h = h.reshape(hr.stop - hr.start, d).astype(BF16)
            h_ref[hr, :] = h
            out_ref[xi] = swiglu(h).reshape(x.shape)

    @pl.when((j > 0) & (j < last))
    def _():
        out_ref[...] += swiglu(h_ref[...]).reshape(out_ref.shape)

    @pl.when((j == last) & (j > 0))
    def _():
        for xi, mi, hr in parts:
            x = x_ref[xi]
            f = out_ref[xi] + swiglu(h_ref[hr, :]).reshape(x.shape)
            n = f * lax.rsqrt(jnp.mean(f * f, axis=-1, keepdims=True) + EPS) * gpost_ref[...]
            out_ref[xi] = x + mod_ref[mi, gate:gate + 1, :] * n


def _ffn(x3, mod, gpre, wg, wu, wd, gpost, *, shift, scale, gate, rows, tf):
    nb, t, d = x3.shape
    dff = wg.shape[1]
    assert dff // tf >= 2
    bb, tt = _token_blocks(nb, t, rows)
    nt = t // tt
    once = dict(pipeline_mode=pl.Buffered(1)) if bb > 1 else {}
    mod, mod_shape, mod_index = _mod_block(mod, bb, nt)
    return pl.pallas_call(
        functools.partial(_ffn_body, shift=shift, scale=scale, gate=gate),
        grid=(nb * t // rows, dff // tf),
        in_specs=[pl.BlockSpec((bb, tt, d), lambda i, j: (i // nt, i % nt, 0), **once),
                  pl.BlockSpec(mod_shape, lambda i, j: mod_index(i), **once),
                  pl.BlockSpec((1, d), lambda i, j: (0, 0)),
                  pl.BlockSpec((d, tf), lambda i, j: (0, j)),
                  pl.BlockSpec((d, tf), lambda i, j: (0, j)),
                  pl.BlockSpec((tf, d), lambda i, j: (j, 0)),
                  pl.BlockSpec((1, d), lambda i, j: (0, 0))],
        out_specs=pl.BlockSpec((bb, tt, d), lambda i, j: (i // nt, i % nt, 0)),
        out_shape=jax.ShapeDtypeStruct((nb, t, d), F32),
        scratch_shapes=[pltpu.VMEM((rows, d), BF16)],
        compiler_params=_params("parallel", "arbitrary"),
        name="ffn",
    )(x3, mod, gpre, wg, wu, wd, gpost)


ROWS = 512
FFN_ROWS = 1024
FFN_COLS = 256
PROMPT_CHUNK = 128
PROMPT_BASE = 16
SAMPLE_BATCH = 8
S5_PROMPT_STEPS = 128
S5_SAMPLE_SEGMENTS = 8


def _layer_weights(w_ada, b_ada, g_pre_mix, g_post_mix, g_pre_ffn, g_post_ffn,
                   w_in, w_conv, a_log, dt_bias, g_dn_out,
                   lam_re, lam_im, log_step, b_re, b_im, c_re, c_im, d_skip,
                   w_glu, g_s5_out, w_out, w_gate, w_up, w_down):
    nheads = a_log.shape[0]
    mix_a = nheads * DN_HEAD_DIM
    pad = lambda v: jnp.pad(v, (0, LANE - v.shape[0]))[None, :]
    wa, abre, abim = _s5_prep(lam_re, lam_im, log_step, b_re, b_im)
    return dict(
        w_ada=w_ada, b_ada=b_ada[None, :],
        g_pre_mix=g_pre_mix[None, :], g_post_mix=g_post_mix[None, :],
        g_pre_ffn=g_pre_ffn[None, :], g_post_ffn=g_post_ffn[None, :],
        w_proj=_win_prep(jnp.swapaxes(w_in, 0, 1), 4 * mix_a, 2 * nheads), u_col=4 * mix_a,
        w_conv=w_conv, alog=pad(a_log), dtb=pad(dt_bias), gdn=g_dn_out[None, :],
        wa=wa, abre=abre, abim=abim, wcre=_s5_out_weights(c_re), wcim=_s5_out_weights(c_im),
        dskip=d_skip.reshape(1, -1), wglu=w_glu.astype(BF16), gs5=g_s5_out[None, :],
        wtop=w_out[:mix_a].astype(BF16), wbot=w_out[mix_a:].astype(BF16),
        wg=w_gate, wu=w_up, wd=w_down,
    )


def _layer_group(x3, mod, conv_state, s0, h0re, h0im, lw, *, prompt):
    nb, t, d = x3.shape
    rows = min(ROWS, nb * t)
    proj = _nmm(x3, mod, lw["g_pre_mix"], lw["w_proj"], shift=0, scale=1, rows=rows if t >= rows else rows // 2)
    if prompt:
        o, cnew, snew = _delta_prompt(proj, proj, conv_state, s0, lw["w_conv"], lw["alog"], lw["dtb"], lw["gdn"],
                                      nb=nb, t=t, c=PROMPT_CHUNK, base=PROMPT_BASE)
        y5, hre, him = _s5(proj, lw["u_col"], h0re, h0im, lw["wa"], lw["abre"], lw["abim"], lw["wcre"], lw["wcim"],
                           lw["dskip"], lw["wglu"], lw["gs5"], nb_total=nb, t=t, nb=nb, lc=S5_PROMPT_STEPS, nseg=1)
    else:
        o, cnew, snew = _delta_sample(proj, proj, conv_state, s0, lw["w_conv"], lw["alog"], lw["dtb"], lw["gdn"],
                                      nb=nb, t=t, bb=SAMPLE_BATCH)
        y5, hre, him = _s5(proj, lw["u_col"], h0re, h0im, lw["wa"], lw["abre"], lw["abim"], lw["wcre"], lw["wcim"],
                           lw["dskip"], lw["wglu"], lw["gs5"], nb_total=nb, t=t, nb=SUBLANE, lc=t,
                           nseg=min(S5_SAMPLE_SEGMENTS, nb // SUBLANE))
    x1 = _outproj(o, y5, lw["wtop"], lw["wbot"], x3, mod, lw["g_post_mix"], gate=2, rows=rows)
    x2 = _ffn(x1, mod, lw["g_pre_ffn"], lw["wg"], lw["wu"], lw["wd"], lw["g_post_ffn"],
              shift=3, scale=4, gate=5, rows=min(FFN_ROWS, nb * t), tf=FFN_COLS)
    ngroups = h0re.shape[1] // S5_STATE
    return x2, cnew, snew, hre.reshape(nb, ngroups, S5_STATE), him.reshape(nb, ngroups, S5_STATE)


def kernel(x_prompt, x_sample, c_prompt, c_sample, state_conv, state_delta, state_ssm_re, state_ssm_im, w_ada, b_ada, g_pre_mix, g_post_mix, g_pre_ffn, g_post_ffn, w_in, w_conv, a_log, dt_bias, g_dn_out, lam_re, lam_im, log_step, b_re, b_im, c_re, c_im, d_skip, w_glu, g_s5_out, w_out, w_gate, w_up, w_down):
    weights = (w_ada, b_ada, g_pre_mix, g_post_mix, g_pre_ffn, g_post_ffn,
               w_in, w_conv, a_log, dt_bias, g_dn_out,
               lam_re, lam_im, log_step, b_re, b_im, c_re, c_im, d_skip,
               w_glu, g_s5_out, w_out, w_gate, w_up, w_down)
    depth = w_ada.shape[0]
    bp, d = c_prompt.shape
    bs = c_sample.shape[0]
    dt_ = x_prompt.dtype
    nheads = state_delta.shape[2]
    nstate = state_ssm_re.shape[2] * state_ssm_re.shape[3]
    crows = -(-(bp + bs) // SUBLANE) * SUBLANE
    c_all = jnp.concatenate([c_sample, c_prompt, jnp.zeros((crows - bp - bs, d), dt_)], axis=0)
    yp, ys = x_prompt, x_sample
    outs = [[] for _ in range(8)]
    for l in range(depth):
        lw = _layer_weights(*[w[l] for w in weights])
        mod = _ada(c_all, lw["w_ada"], lw["b_ada"]).reshape(crows, 6, d)
        zc = jnp.zeros((bp, CONV_WIDTH - 1, state_conv.shape[3]), dt_)
        zd = jnp.zeros((bp,) + state_delta.shape[2:], dt_)
        zs = jnp.zeros((bp, nstate), dt_)
        yp, c1, d1, r1, i1 = _layer_group(yp, (mod, bs), zc, zd, zs, zs, lw, prompt=True)
        ys, c2, d2, r2, i2 = _layer_group(ys, (mod, 0), state_conv[l], state_delta[l],
                                          state_ssm_re[l].reshape(bs, nstate), state_ssm_im[l].reshape(bs, nstate),
                                          lw, prompt=False)
        for lst, v in zip(outs, (c1, d1, r1, i1, c2, d2, r2, i2)):
            lst.append(v)
    stacked = [jnp.stack(v) for v in outs]
    return (yp, ys, *stacked)
```

```python
import functools
import math

import jax
import jax.numpy as jnp
import numpy as np
from jax import lax
from jax.experimental import pallas as pl
from jax.experimental.pallas import tpu as pltpu

F32 = jnp.float32
BF16 = jnp.bfloat16
EPS = 1e-6
LANE = 128
SUBLANE = 8
V7X_VMEM_BYTES = 64 << 20
VMEM_LIMIT = V7X_VMEM_BYTES - (8 << 20)

DN_HEAD_DIM = 128
CONV_WIDTH = 4
S5_GROUP = 16
S5_STATE = 64
S5_SLAB_GROUPS = LANE // S5_GROUP
S5_SLAB_STATES = S5_SLAB_GROUPS * S5_STATE


def _params(*sem):
    return pltpu.CompilerParams(dimension_semantics=sem, vmem_limit_bytes=VMEM_LIMIT)


def _silu(x):
    return x * jax.nn.sigmoid(x)


def _mm(a, b):
    return jnp.dot(a.astype(BF16), b.astype(BF16), preferred_element_type=F32)


def _mm_nt(a, b):
    return lax.dot_general(a.astype(BF16), b.astype(BF16), (((1,), (1,)), ((), ())),
                           preferred_element_type=F32)


def _mm_tn(a, b):
    return lax.dot_general(a.astype(BF16), b.astype(BF16), (((0,), (0,)), ((), ())),
                           preferred_element_type=F32)


def _split3(x):
    x1 = x.astype(BF16)
    r1 = x - x1.astype(F32)
    x2 = r1.astype(BF16)
    x3 = (r1 - x2.astype(F32)).astype(BF16)
    return x1, x2, x3


def _ada_body(c_ref, w_ref, b_ref, o_ref):
    s = _silu(c_ref[...])
    o_ref[...] = _mm(s, w_ref[...]) + b_ref[...]


def _ada(c, w, b):
    rows, d = c.shape
    n = w.shape[1]
    tn = 1024
    return pl.pallas_call(
        _ada_body,
        grid=(n // tn,),
        in_specs=[pl.BlockSpec((rows, d), lambda j: (0, 0)),
                  pl.BlockSpec((d, tn), lambda j: (0, j)),
                  pl.BlockSpec((1, tn), lambda j: (0, j))],
        out_specs=pl.BlockSpec((rows, tn), lambda j: (0, j)),
        out_shape=jax.ShapeDtypeStruct((rows, n), F32),
        compiler_params=_params("arbitrary"),
        name="ada",
    )(c, w, b)


def _mod_block(mod, bb, nt):
    arr, first = mod
    assert first % bb == 0
    return arr, (bb, 6, arr.shape[2]), lambda i: (first // bb + i // nt, 0, 0)


def _token_blocks(nb, t, rows):
    if t >= rows:
        assert t % rows == 0
        return 1, rows
    assert rows % t == 0 and nb % (rows // t) == 0
    return rows // t, t


def _modulated_norm(x, g, scale, shift):
    y = x * lax.rsqrt(jnp.mean(x * x, axis=-1, keepdims=True) + EPS) * g
    return y * (1.0 + scale) + shift


def _win_prep_body(w_ref, o_ref, *, main, gates):
    n = w_ref.shape[0]
    rest = n - main - gates
    o_ref[0:main, :] = w_ref[0:main, :].astype(BF16)
    o_ref[main:main + rest, :] = w_ref[main + gates:n, :].astype(BF16)
    o_ref[main + rest:main + rest + gates, :] = w_ref[main:main + gates, :].astype(BF16)
    o_ref[main + rest + gates:, :] = jnp.zeros((LANE - gates, o_ref.shape[1]), BF16)


def _win_prep(w_in_t, main, gates):
    n, d = w_in_t.shape
    cb = min(512, d)
    nout = n - gates + LANE
    return pl.pallas_call(
        functools.partial(_win_prep_body, main=main, gates=gates),
        grid=(d // cb,),
        in_specs=[pl.BlockSpec((n, cb), lambda i: (0, i))],
        out_specs=pl.BlockSpec((nout, cb), lambda i: (0, i)),
        out_shape=jax.ShapeDtypeStruct((nout, d), BF16),
        compiler_params=_params("parallel"),
        name="winprep",
    )(w_in_t)


def _nmm_body(x_ref, mod_ref, g_ref, w_ref, o_ref, *, shift, scale):
    x = x_ref[...]
    bb, tt, d = x.shape
    h = _modulated_norm(x, g_ref[...], mod_ref[:, scale:scale + 1, :], mod_ref[:, shift:shift + 1, :])
    o_ref[...] = _mm_nt(h.reshape(bb * tt, d), w_ref[...])


def _nmm(x3, mod, g, w_t, *, shift, scale, rows):
    nb, t, d = x3.shape
    n = w_t.shape[0]
    bb, tt = _token_blocks(nb, t, rows)
    nt = t // tt
    mod, mod_shape, mod_index = _mod_block(mod, bb, nt)
    return pl.pallas_call(
        functools.partial(_nmm_body, shift=shift, scale=scale),
        grid=(nb * t // rows,),
        in_specs=[pl.BlockSpec((bb, tt, d), lambda i: (i // nt, i % nt, 0)),
                  pl.BlockSpec(mod_shape, mod_index),
                  pl.BlockSpec((1, d), lambda i: (0, 0)),
                  pl.BlockSpec((n, d), lambda i: (0, 0), pipeline_mode=pl.Buffered(1))],
        out_specs=pl.BlockSpec((rows, n), lambda i: (i, 0)),
        out_shape=jax.ShapeDtypeStruct((nb * t, n), F32),
        compiler_params=_params("parallel"),
        name="nmm",
    )(x3, mod, g, w_t)


def _seg_masks(c, seg):
    row = lax.broadcasted_iota(jnp.int32, (c, c), 0)
    col = lax.broadcasted_iota(jnp.int32, (c, c), 1)
    sh = int(math.log2(seg))
    same = (row >> sh) == (col >> sh)
    return row, col, same, same & (row >= col), same & (row > col)


def _tri_inverse_offdiag(lmats, row, col, base, seg):
    bsh = int(math.log2(base))
    blk = (row >> bsh) == (col >> bsh)
    l0s = [jnp.where(blk, l, 0.0) for l in lmats]
    pts = [-l for l in l0s]
    qs = [_mm(l, l) for l in l0s]
    k = 2
    while k < base:
        pts = [p + q + _mm(p, q) for p, q in zip(pts, qs)]
        k *= 2
        if k < base:
            qs = [_mm(q, q) for q in qs]
    m = base
    while m < seg:
        msh = int(math.log2(m))
        sel = ((row >> (msh + 1)) == (col >> (msh + 1))) & ((row >> msh) != (col >> msh))
        es = [jnp.where(sel, l, 0.0) for l in lmats]
        xs = [e + _mm(p, e) for p, e in zip(pts, es)]
        pts = [p - (x + _mm(x, p)) for p, x in zip(pts, xs)]
        m *= 2
    return pts


def _delta_gates(ab, alog, dtb, causal_bf, seg, nheads):
    lane = lax.broadcasted_iota(jnp.int32, ab.shape, 1)
    x = ab + dtb
    softplus = jnp.maximum(x, 0.0) + jnp.log1p(jnp.exp(-jnp.abs(x)))
    g = jnp.where(lane < nheads, -jnp.exp(alog) * softplus, 0.0)
    beta = jax.nn.sigmoid(ab)
    pieces = _split3(g)
    gcum = sum(jnp.dot(causal_bf, p, preferred_element_type=F32) for p in pieces)
    g3 = gcum.reshape(gcum.shape[0] // seg, seg, gcum.shape[1])
    glast = jnp.broadcast_to(g3[:, seg - 1:seg, :], g3.shape).reshape(gcum.shape)
    return gcum, beta, glast


def _decay_logits(gcum, nheads):
    c = gcum.shape[0]
    if c < LANE:
        gcum = jnp.concatenate([gcum, jnp.zeros((LANE - c, LANE), F32)], axis=0)
    return gcum.T


def _delta_intra(act, nheads, gates, dlog, masks, base, seg):
    gcum, beta, glast = gates
    row, col, _, causal, strict = masks
    c = act.shape[0]
    dh = DN_HEAD_DIM
    eg_all = jnp.exp(gcum)
    ed_all = jnp.exp(glast - gcum)
    ks, kqs_lhs, rhss, qgs, kdecs = [], [], [], [], []
    for h in range(nheads):
        q = act[:, h * dh:(h + 1) * dh]
        k = act[:, (nheads + h) * dh:(nheads + h + 1) * dh]
        v = act[:, (2 * nheads + h) * dh:(2 * nheads + h + 1) * dh]
        q = q * lax.rsqrt(jnp.sum(q * q, axis=-1, keepdims=True) + 1e-6) * (dh ** -0.5)
        k = k * lax.rsqrt(jnp.sum(k * k, axis=-1, keepdims=True) + 1e-6)
        b = beta[:, nheads + h:nheads + h + 1]
        eg = eg_all[:, h:h + 1]
        kb = k * b
        ks.append(k)
        kqs_lhs.append(jnp.concatenate([kb, q], axis=0))
        rhss.append(jnp.concatenate([v * b, kb * eg], axis=1))
        qgs.append(q * eg)
        kdecs.append(k * ed_all[:, h:h + 1])
    kqs = [_mm_nt(a, k) for a, k in zip(kqs_lhs, ks)]
    lmats, attns = [], []
    for h in range(nheads):
        dl = gcum[:, h:h + 1] - dlog[h:h + 1, 0:c]
        dec = jnp.where(causal, jnp.exp(jnp.where(causal, dl, 0.0)), 0.0)
        lmats.append(jnp.where(strict, kqs[h][:c] * dec, 0.0))
        attns.append(kqs[h][c:] * dec)
    tts = _tri_inverse_offdiag(lmats, row, col, base, seg)
    uws = [r + _mm(t, r) for t, r in zip(tts, rhss)]
    return [uw[:, :dh] for uw in uws], [uw[:, dh:] for uw in uws], qgs, kdecs, attns


def _delta_out(o, z, gdn):
    on = o * lax.rsqrt(jnp.mean(o * o, axis=-1, keepdims=True) + EPS) * gdn
    return on * _silu(z)


def _delta_prompt_body(qkv_ref, z_ref, ab_ref, cst_ref, s0_ref, wc_ref, alog_ref, dtb_ref, gdn_ref,
                       o_ref, cnew_ref, snew_ref, xc_ref, s_ref, *, c, nheads, base):
    dh = DN_HEAD_DIM
    halo = SUBLANE
    taps = CONV_WIDTH - 1

    @pl.when(pl.program_id(1) == 0)
    def _():
        xc_ref[0:halo, :] = jnp.zeros((halo, xc_ref.shape[1]), F32)
        xc_ref[halo - taps:halo, :] = cst_ref[0]
        s_ref[...] = s0_ref[0]

    xc_ref[halo:halo + c, :] = qkv_ref[...]
    conv = xc_ref[halo - taps:halo - taps + c, :] * wc_ref[0:1, :]
    for j in range(1, CONV_WIDTH):
        conv = conv + xc_ref[halo - taps + j:halo - taps + j + c, :] * wc_ref[j:j + 1, :]
    cnew_ref[0] = xc_ref[halo + c - taps:halo + c, :]
    xc_ref[0:halo, :] = xc_ref[c:c + halo, :]
    act = _silu(conv)

    masks = _seg_masks(c, c)
    causal_bf = jnp.where(masks[3], 1.0, 0.0).astype(BF16)
    gates = _delta_gates(ab_ref[...], alog_ref[...], dtb_ref[...], causal_bf, c, nheads)
    dlog = _decay_logits(gates[0], nheads)
    eglast = jnp.exp(gates[2][0:1, :])
    gdn = gdn_ref[...]
    us, ws_, qgs, kdecs, attns = _delta_intra(act, nheads, gates, dlog, masks, base, c)
    heads = range(nheads)
    ss = [s_ref[h] for h in heads]
    wss = [_mm(jnp.concatenate([ws_[h], qgs[h]], axis=0), ss[h]) for h in heads]
    vnews = [us[h] - wss[h][:c] for h in heads]
    os_ = [wss[h][c:] + _mm(attns[h], vnews[h]) for h in heads]
    snews = [ss[h] * eglast[:, h:h + 1] + _mm_tn(kdecs[h], vnews[h]) for h in heads]
    for h in heads:
        s_ref[h] = snews[h]
        o_ref[:, h * dh:(h + 1) * dh] = _delta_out(os_[h], z_ref[:, h * dh:(h + 1) * dh], gdn).astype(o_ref.dtype)

    @pl.when(pl.program_id(1) == pl.num_programs(1) - 1)
    def _():
        snew_ref[0] = s_ref[...]


def _delta_prompt(qkvz, uab, conv_state, s0, w_conv, alog, dtb, gdn, *, nb, t, c, base):
    nheads = s0.shape[1]
    dh = DN_HEAD_DIM
    mix = nheads * dh
    nc = t // c
    return pl.pallas_call(
        functools.partial(_delta_prompt_body, c=c, nheads=nheads, base=base),
        grid=(nb, nc),
        in_specs=[pl.BlockSpec((c, 3 * mix), lambda b, i: (b * nc + i, 0)),
                  pl.BlockSpec((c, mix), lambda b, i: (b * nc + i, 3)),
                  pl.BlockSpec((c, LANE), lambda b, i: (b * nc + i, uab.shape[1] // LANE - 1)),
                  pl.BlockSpec((1, CONV_WIDTH - 1, 3 * mix), lambda b, i: (b, 0, 0)),
                  pl.BlockSpec((1, nheads, dh, dh), lambda b, i: (b, 0, 0, 0)),
                  pl.BlockSpec((CONV_WIDTH, 3 * mix), lambda b, i: (0, 0)),
                  pl.BlockSpec((1, LANE), lambda b, i: (0, 0)),
                  pl.BlockSpec((1, LANE), lambda b, i: (0, 0)),
                  pl.BlockSpec((1, dh), lambda b, i: (0, 0))],
        out_specs=[pl.BlockSpec((c, mix), lambda b, i: (b * nc + i, 0)),
                   pl.BlockSpec((1, CONV_WIDTH - 1, 3 * mix), lambda b, i: (b, 0, 0)),
                   pl.BlockSpec((1, nheads, dh, dh), lambda b, i: (b, 0, 0, 0))],
        out_shape=[jax.ShapeDtypeStruct((nb * t, mix), BF16),
                   jax.ShapeDtypeStruct((nb, CONV_WIDTH - 1, 3 * mix), F32),
                   jax.ShapeDtypeStruct((nb, nheads, dh, dh), F32)],
        scratch_shapes=[pltpu.VMEM((c + SUBLANE, 3 * mix), F32),
                        pltpu.VMEM((nheads, dh, dh), F32)],
        compiler_params=_params("parallel", "arbitrary"),
        name="delta_prompt",
    )(qkvz, qkvz, uab, conv_state, s0, w_conv, alog, dtb, gdn)


def _delta_sample_body(qkv_ref, z_ref, ab_ref, cst_ref, s0_ref, wc_ref, alog_ref, dtb_ref, gdn_ref,
                       o_ref, cnew_ref, snew_ref,
                       xs_ref, w_s, qg_s, u_s, kd_s, vn_s, qs_s, attn_s, egl_s, *, bb, t, nheads):
    dh = DN_HEAD_DIM
    c = bb * t
    halo = SUBLANE
    taps = CONV_WIDTH - 1

    xs_ref[:, 0:halo, :] = jnp.zeros((bb, halo, xs_ref.shape[2]), F32)
    xs_ref[:, halo - taps:halo, :] = cst_ref[...]
    xs_ref[:, halo:halo + t, :] = qkv_ref[...]
    conv = xs_ref[:, halo - taps:halo - taps + t, :] * wc_ref[0:1, :]
    for j in range(1, CONV_WIDTH):
        conv = conv + xs_ref[:, halo - taps + j:halo - taps + j + t, :] * wc_ref[j:j + 1, :]
    cnew_ref[...] = xs_ref[:, halo + t - taps:halo + t, :]
    act = _silu(conv).reshape(c, conv.shape[2])

    masks = _seg_masks(c, t)
    causal_bf = jnp.where(masks[3], 1.0, 0.0).astype(BF16)
    gates = _delta_gates(ab_ref[...], alog_ref[...], dtb_ref[...], causal_bf, t, nheads)
    dlog = _decay_logits(gates[0], nheads)
    egl_s[...] = jnp.exp(gates[2])
    us, ws_, qgs, kdecs, attns = _delta_intra(act, nheads, gates, dlog, masks, t, t)
    for h in range(nheads):
        hs = slice(h * dh, (h + 1) * dh)
        u_s[:, hs] = us[h]
        w_s[:, hs] = ws_[h]
        qg_s[:, hs] = qgs[h]
        kd_s[:, hs] = kdecs[h]
        attn_s[h] = attns[h]

    def per_batch(b, carry):
        r = pl.multiple_of(b * t, t)
        rows = pl.ds(r, t)
        heads = range(nheads)
        hsl = [slice(h * dh, (h + 1) * dh) for h in heads]
        ss = [s0_ref[b, h] for h in heads]
        wss = [_mm(jnp.concatenate([w_s[rows, hsl[h]], qg_s[rows, hsl[h]]], axis=0), ss[h]) for h in heads]
        vnews = [u_s[rows, hsl[h]] - wss[h][:t] for h in heads]
        upds = [_mm_tn(kd_s[rows, hsl[h]], vnews[h]) for h in heads]
        egl = egl_s[pl.ds(r, 1), :]
        for h in heads:
            vn_s[rows, hsl[h]] = vnews[h]
            qs_s[rows, hsl[h]] = wss[h][t:]
            snew_ref[b, h] = ss[h] * egl[:, h:h + 1] + upds[h]
        return carry

    lax.fori_loop(0, bb, per_batch, 0, unroll=2)

    gdn = gdn_ref[...]
    for h in range(nheads):
        hs = slice(h * dh, (h + 1) * dh)
        o = qs_s[:, hs] + _mm(attn_s[h], vn_s[:, hs])
        o_ref[:, hs] = _delta_out(o, z_ref[:, hs], gdn).astype(o_ref.dtype)


def _delta_sample(qkvz, uab, conv_state, s0, w_conv, alog, dtb, gdn, *, nb, t, bb):
    nheads = s0.shape[1]
    dh = DN_HEAD_DIM
    mix = nheads * dh
    c = bb * t
    qkvz3 = qkvz.reshape(nb, t, qkvz.shape[1])
    return pl.pallas_call(
        functools.partial(_delta_sample_body, bb=bb, t=t, nheads=nheads),
        grid=(nb // bb,),
        in_specs=[pl.BlockSpec((bb, t, 3 * mix), lambda i: (i, 0, 0)),
                  pl.BlockSpec((c, mix), lambda i: (i, 3)),
                  pl.BlockSpec((c, LANE), lambda i: (i, uab.shape[1] // LANE - 1)),
                  pl.BlockSpec((bb, CONV_WIDTH - 1, 3 * mix), lambda i: (i, 0, 0)),
                  pl.BlockSpec((bb, nheads, dh, dh), lambda i: (i, 0, 0, 0)),
                  pl.BlockSpec((CONV_WIDTH, 3 * mix), lambda i: (0, 0)),
                  pl.BlockSpec((1, LANE), lambda i: (0, 0)),
                  pl.BlockSpec((1, LANE), lambda i: (0, 0)),
                  pl.BlockSpec((1, dh), lambda i: (0, 0))],
        out_specs=[pl.BlockSpec((c, mix), lambda i: (i, 0)),
                   pl.BlockSpec((bb, CONV_WIDTH - 1, 3 * mix), lambda i: (i, 0, 0)),
                   pl.BlockSpec((bb, nheads, dh, dh), lambda i: (i, 0, 0, 0))],
        out_shape=[jax.ShapeDtypeStruct((nb * t, mix), BF16),
                   jax.ShapeDtypeStruct((nb, CONV_WIDTH - 1, 3 * mix), F32),
                   jax.ShapeDtypeStruct((nb, nheads, dh, dh), F32)],
        scratch_shapes=[pltpu.VMEM((bb, SUBLANE + t, 3 * mix), F32)]
                       + [pltpu.VMEM((c, mix), F32) for _ in range(6)]
                       + [pltpu.VMEM((nheads, c, c), F32), pltpu.VMEM((c, LANE), F32)],
        compiler_params=_params("parallel"),
        name="delta_sample",
    )(qkvz3, qkvz, uab, conv_state, s0, w_conv, alog, dtb, gdn)


def _s5_prep_body(lre_ref, lim_ref, ls_ref, bre_ref, bim_ref, wa_ref, abre_ref, abim_ref):
    lre = jnp.minimum(lre_ref[...], -1e-4)
    lim = lim_ref[...]
    dt = jnp.exp(ls_ref[...])
    mag = jnp.exp(lre * dt)
    ang = lim * dt
    abr = mag * jnp.cos(ang)
    abi = mag * jnp.sin(ang)
    nr = abr - 1.0
    ni = abi
    den = lre * lre + lim * lim
    cor = (nr * lre + ni * lim) / den
    coi = (ni * lre - nr * lim) / den
    bre = bre_ref[...]
    bim = bim_ref[...]
    shape2 = bre.shape[1:]
    row = lax.broadcasted_iota(jnp.int32, shape2, 0)
    col = lax.broadcasted_iota(jnp.int32, shape2, 1)
    diag = ((row // S5_GROUP) == (col // S5_STATE))[None]
    half = shape2[1]
    rows = shape2[0]
    wre = jnp.where(diag, cor * bre - coi * bim, 0.0)
    wim = jnp.where(diag, cor * bim + coi * bre, 0.0)
    wa_ref[:, 0:rows, 0:half] = wre.astype(BF16)
    wa_ref[:, 0:rows, half:2 * half] = wim.astype(BF16)
    wa_ref[:, rows:2 * rows, 0:half] = (abr * wre - abi * wim).astype(BF16)
    wa_ref[:, rows:2 * rows, half:2 * half] = (abr * wim + abi * wre).astype(BF16)
    abre_ref[...] = abr
    abim_ref[...] = abi


def _s5_prep(lam_re, lam_im, log_step, b_re, b_im):
    groups, states, gch = b_re.shape
    slabs = groups // S5_SLAB_GROUPS
    cols = S5_SLAB_STATES

    def row_param(p):
        return p.reshape(slabs, 1, cols)

    def tiled_b(b):
        bt = jnp.transpose(b, (0, 2, 1)).reshape(slabs, LANE, states)
        return jnp.tile(bt, (1, 1, S5_SLAB_GROUPS))

    ls = jnp.broadcast_to(log_step[:, None], (groups, states))
    wa, abre, abim = pl.pallas_call(
        _s5_prep_body,
        out_shape=[jax.ShapeDtypeStruct((slabs, 2 * LANE, 2 * cols), BF16),
                   jax.ShapeDtypeStruct((slabs, 1, cols), F32),
                   jax.ShapeDtypeStruct((slabs, 1, cols), F32)],
        compiler_params=pltpu.CompilerParams(vmem_limit_bytes=VMEM_LIMIT),
        name="s5prep",
    )(row_param(lam_re), row_param(lam_im), row_param(ls), tiled_b(b_re), tiled_b(b_im))
    nblk = groups * states // LANE
    return wa, abre.reshape(nblk, 1, LANE), abim.reshape(nblk, 1, LANE)


def _s5_out_weights(c):
    groups, gch, states = c.shape
    slabs = groups // S5_SLAB_GROUPS
    ct = jnp.transpose(c, (0, 2, 1)).reshape(slabs, S5_SLAB_STATES, gch)
    ct = jnp.tile(ct, (1, 1, S5_SLAB_GROUPS))
    row = jnp.arange(S5_SLAB_STATES)[:, None] // states
    col = jnp.arange(LANE)[None, :] // gch
    return jnp.where((row == col)[None], ct, 0.0).astype(BF16)


def _gelu_tanh(x):
    return x * (0.5 * (1.0 + jnp.tanh(math.sqrt(2.0 / math.pi) * (x + 0.044715 * (x * x * x)))))


S5_PIPELINE_PARTS = 4


def _time_major_perms(nb, lc, nseg):
    seg_rows = nb * lc
    rows = nseg * seg_rows
    r = np.arange(rows)[:, None]
    c = np.arange(rows)[None, :]
    base, loc = r // seg_rows * seg_rows, r % seg_rows
    t_tm, b_tm = loc // nb, loc % nb
    to_tm = c == base + b_tm * lc + t_tm
    to_tm_prev = (c == base + b_tm * lc + t_tm - 1) & (t_tm % 2 == 1)
    to_bm = c == base + (loc % lc) * nb + loc // lc
    return tuple(jnp.asarray(m, BF16) for m in (to_tm, to_tm_prev, to_bm))


def _s5_body(u_ref, h0re_ref, h0im_ref, tm_ref, tmp_ref, bm_ref, wa_ref, abre_ref, abim_ref, wcre_ref, wcim_ref,
             dsk_ref, wglu_ref, gs5_ref, y_ref, hre_ref, him_ref, cre, cim, *xbufs, nb, lc, nseg):
    seg_rows = nb * lc
    rows = nseg * seg_rows
    nblk = cre.shape[0]
    slabs = wa_ref.shape[0]
    per_slab = nblk // slabs
    half = per_slab * LANE
    steps_per_tile = SUBLANE // nb
    assert steps_per_tile in (1, 2) and lc % steps_per_tile == 0
    assert nseg == 1 or nb == SUBLANE

    if nseg == 1:
        @pl.when(pl.program_id(1) == 0)
        def _():
            for j in range(nblk):
                if nb < SUBLANE:
                    cre[j] = jnp.zeros((SUBLANE, LANE), F32)
                    cim[j] = jnp.zeros((SUBLANE, LANE), F32)
                cre[j, SUBLANE - nb:SUBLANE, :] = h0re_ref[:, j * LANE:(j + 1) * LANE]
                cim[j, SUBLANE - nb:SUBLANE, :] = h0im_ref[:, j * LANE:(j + 1) * LANE]

    to_tm, to_bm = tm_ref[...], bm_ref[...]
    pieces = _split3(u_ref[...].reshape(rows, u_ref.shape[2]))[:2]
    u = sum(jnp.dot(to_tm, p, preferred_element_type=F32) for p in pieces)
    ub = u.astype(BF16)
    if steps_per_tile == 2:
        ub_prev = jnp.dot(tmp_ref[...], pieces[0], preferred_element_type=F32).astype(BF16)

    upper = lax.broadcasted_iota(jnp.int32, (SUBLANE, LANE), 0) >= nb
    parts = S5_PIPELINE_PARTS if rows % (S5_PIPELINE_PARTS * LANE) == 0 else 1
    prow = rows // parts

    def project_in(s, r):
        rs = slice(r * prow, (r + 1) * prow)
        sl = slice(s * LANE, (s + 1) * LANE)
        if steps_per_tile == 2:
            x = jnp.dot(jnp.concatenate([ub[rs, sl], ub_prev[rs, sl]], axis=1), wa_ref[s],
                        preferred_element_type=F32)
        else:
            x = jnp.dot(ub[rs, sl], wa_ref[s, 0:LANE, :], preferred_element_type=F32)
        for k in range(per_slab):
            xbufs[2 * s][k, rs, :] = x[:, k * LANE:(k + 1) * LANE]
            xbufs[2 * s + 1][k, rs, :] = x[:, half + k * LANE:half + (k + 1) * LANE]

    seg_tiles = seg_rows // SUBLANE

    def recur(s, r, coef, carry):
        xre, xim = xbufs[2 * s], xbufs[2 * s + 1]
        for v in range(r * prow // SUBLANE, (r + 1) * prow // SUBLANE):
            at = slice(v * SUBLANE, (v + 1) * SUBLANE)
            seg_rows_sl = slice(v // seg_tiles * nb, (v // seg_tiles + 1) * nb)
            if nseg > 1 and v % seg_tiles == 0:
                for n in range(per_slab):
                    cols = slice((s * per_slab + n) * LANE, (s * per_slab + n + 1) * LANE)
                    carry[n] = (h0re_ref[seg_rows_sl, cols], h0im_ref[seg_rows_sl, cols])
            for n in range(per_slab):
                hr, hi = carry[n]
                ar, ai = coef[n]
                if steps_per_tile == 2:
                    hr = jnp.where(upper, hr, pltpu.roll(hr, nb, 0))
                    hi = jnp.where(upper, hi, pltpu.roll(hi, nb, 0))
                nr = ar * hr - ai * hi + xre[n, at, :]
                ni = ar * hi + ai * hr + xim[n, at, :]
                xre[n, at, :] = nr
                xim[n, at, :] = ni
                carry[n] = (nr, ni)
            if nseg > 1 and v % seg_tiles == seg_tiles - 1:
                for n in range(per_slab):
                    cols = slice((s * per_slab + n) * LANE, (s * per_slab + n + 1) * LANE)
                    hre_ref[seg_rows_sl, cols], him_ref[seg_rows_sl, cols] = carry[n]

    def project_out(s, r):
        rs = slice(r * prow, (r + 1) * prow)
        hre = jnp.concatenate([xbufs[2 * s][k, rs, :] for k in range(per_slab)], axis=1)
        him = jnp.concatenate([xbufs[2 * s + 1][k, rs, :] for k in range(per_slab)], axis=1)
        return _mm(hre, wcre_ref[s]) - _mm(him, wcim_ref[s])

    ychunks = [[None] * parts for _ in range(slabs)]
    for r in range(parts):
        project_in(0, r)
    for s in range(slabs):
        js = list(range(s * per_slab, (s + 1) * per_slab))
        coef, carry = [], []
        for j in js:
            ar = jnp.broadcast_to(abre_ref[j], (SUBLANE, LANE))
            ai = jnp.broadcast_to(abim_ref[j], (SUBLANE, LANE))
            if steps_per_tile == 2:
                coef.append((jnp.where(upper, ar * ar - ai * ai, ar), jnp.where(upper, 2.0 * ar * ai, ai)))
            else:
                coef.append((ar, ai))
            carry.append((cre[j], cim[j]) if nseg == 1 else None)
        for r in range(parts):
            recur(s, r, coef, carry)
            if s + 1 < slabs:
                project_in(s + 1, r)
            if s >= 1:
                ychunks[s - 1][r] = project_out(s - 1, r)
        if nseg == 1:
            for n, j in enumerate(js):
                cre[j], cim[j] = carry[n]
    for r in range(parts):
        ychunks[slabs - 1][r] = project_out(slabs - 1, r)
    n = slabs * ychunks[0][0].shape[1]

    def act_part(r):
        rs = slice(r * prow, (r + 1) * prow)
        y = jnp.concatenate([ychunks[s][r] for s in range(slabs)], axis=1) + dsk_ref[...] * u[rs, :]
        return _gelu_tanh(y).astype(BF16)

    def gate_part(ab):
        glu = ab[:, :n] * jax.nn.sigmoid(ab[:, n:])
        return (glu * lax.rsqrt(jnp.mean(glu * glu, axis=-1, keepdims=True) + EPS) * gs5_ref[...]).astype(BF16)

    acts, abs_, outs = {}, {}, {}
    for step in range(parts + 2):
        if step < parts:
            acts[step] = act_part(step)
        if 0 <= step - 1 < parts:
            abs_[step - 1] = jnp.dot(acts.pop(step - 1), wglu_ref[...], preferred_element_type=F32)
        if 0 <= step - 2 < parts:
            outs[step - 2] = gate_part(abs_.pop(step - 2))
    out = jnp.concatenate([outs[r] for r in range(parts)], axis=0)
    out = jnp.dot(to_bm, out, preferred_element_type=F32)
    y_ref[...] = out.reshape(y_ref.shape).astype(y_ref.dtype)

    if nseg == 1:
        @pl.when(pl.program_id(1) == pl.num_programs(1) - 1)
        def _():
            for j in range(nblk):
                hre_ref[:, j * LANE:(j + 1) * LANE] = cre[j, SUBLANE - nb:SUBLANE, :]
                him_ref[:, j * LANE:(j + 1) * LANE] = cim[j, SUBLANE - nb:SUBLANE, :]


def _s5(proj, u_col, h0re, h0im, wa, abre, abim, wcre, wcim, dskip, wglu, gs5, *, nb_total, t, nb, lc, nseg):
    mixb = dskip.shape[1]
    nstate = h0re.shape[1]
    nblk = nstate // LANE
    slabs = wa.shape[0]
    assert u_col % mixb == 0
    assert nseg == 1 or lc == t
    ublk = u_col // mixb
    nbs = nseg * nb
    uab3 = proj.reshape(nb_total, t, proj.shape[1])
    perms = _time_major_perms(nb, lc, nseg)
    full = lambda a: pl.BlockSpec(a.shape, lambda i, j: (0,) * a.ndim)
    y, hre, him = pl.pallas_call(
        functools.partial(_s5_body, nb=nb, lc=lc, nseg=nseg),
        grid=(nb_total // nbs, t // lc),
        in_specs=[pl.BlockSpec((nbs, lc, mixb), lambda i, j: (i, j, ublk)),
                  pl.BlockSpec((nbs, nstate), lambda i, j: (i, 0)),
                  pl.BlockSpec((nbs, nstate), lambda i, j: (i, 0))]
                 + [full(a) for a in perms]
                 + [full(wa), full(abre), full(abim), full(wcre), full(wcim), full(dskip), full(wglu), full(gs5)],
        out_specs=[pl.BlockSpec((nbs, lc, mixb), lambda i, j: (i, j, 0)),
                   pl.BlockSpec((nbs, nstate), lambda i, j: (i, 0)),
                   pl.BlockSpec((nbs, nstate), lambda i, j: (i, 0))],
        out_shape=[jax.ShapeDtypeStruct((nb_total, t, mixb), F32),
                   jax.ShapeDtypeStruct((nb_total, nstate), F32),
                   jax.ShapeDtypeStruct((nb_total, nstate), F32)],
        scratch_shapes=[pltpu.VMEM((nblk, SUBLANE, LANE), F32),
                        pltpu.VMEM((nblk, SUBLANE, LANE), F32)]
                       + [pltpu.VMEM((nblk // slabs, nbs * lc, LANE), F32) for _ in range(2 * slabs)],
        compiler_params=_params("parallel", "arbitrary"),
        name="s5",
    )(uab3, h0re, h0im, *perms, wa, abre, abim, wcre, wcim, dskip, wglu, gs5)
    return y.reshape(nb_total * t, mixb), hre, him


PART_ROWS = 256


def _row_parts(bb, tt):
    rows = bb * tt
    nparts = max(1, rows // PART_ROWS)
    if bb > 1:
        nparts = min(nparts, bb)
    prow = rows // nparts
    out = []
    for r in range(nparts):
        flat = slice(r * prow, (r + 1) * prow)
        if bb == 1:
            out.append(((slice(None), flat), slice(None), flat))
        else:
            bs = slice(r * bb // nparts, (r + 1) * bb // nparts)
            out.append(((bs, slice(None)), bs, flat))
    return out


def _outproj_body(o_ref, y_ref, wt_ref, wb_ref, x_ref, mod_ref, g_ref, out_ref, *, gate):
    mix = (jnp.dot(o_ref[...], wt_ref[...], preferred_element_type=F32)
           + _mm(y_ref[...], wb_ref[...]))
    n = mix * lax.rsqrt(jnp.mean(mix * mix, axis=-1, keepdims=True) + EPS) * g_ref[...]
    x = x_ref[...]
    out_ref[...] = x + mod_ref[:, gate:gate + 1, :] * n.reshape(x.shape)


def _outproj(o, y5, wtop, wbot, x3, mod, g, *, gate, rows):
    nb, t, d = x3.shape
    bb, tt = _token_blocks(nb, t, rows)
    nt = t // tt
    ka, kb = o.shape[1], y5.shape[1]
    mod, mod_shape, mod_index = _mod_block(mod, bb, nt)
    return pl.pallas_call(
        functools.partial(_outproj_body, gate=gate),
        grid=(nb * t // rows,),
        in_specs=[pl.BlockSpec((rows, ka), lambda i: (i, 0)),
                  pl.BlockSpec((rows, kb), lambda i: (i, 0)),
                  pl.BlockSpec((ka, d), lambda i: (0, 0)),
                  pl.BlockSpec((kb, d), lambda i: (0, 0)),
                  pl.BlockSpec((bb, tt, d), lambda i: (i // nt, i % nt, 0)),
                  pl.BlockSpec(mod_shape, mod_index),
                  pl.BlockSpec((1, d), lambda i: (0, 0))],
        out_specs=pl.BlockSpec((bb, tt, d), lambda i: (i // nt, i % nt, 0)),
        out_shape=jax.ShapeDtypeStruct((nb, t, d), F32),
        compiler_params=_params("parallel"),
        name="outproj",
    )(o, y5, wtop, wbot, x3, mod, g)


def _ffn_body(x_ref, mod_ref, gpre_ref, wg_ref, wu_ref, wd_ref, gpost_ref, out_ref, h_ref,
              *, shift, scale, gate):
    j = pl.program_id(1)
    last = pl.num_programs(1) - 1
    bb, tt, d = x_ref.shape
    parts = _row_parts(bb, tt)

    def swiglu(h):
        a = _silu(_mm(h, wg_ref[...])) * _mm(h, wu_ref[...])
        return _mm(a, wd_ref[...])

    @pl.when(j == 0)
    def _():
        for xi, mi, hr in parts:
            x = x_ref[xi]
            h = _modulated_norm(x, gpre_ref[...], mod_ref[mi, scale:scale + 1, :], mod_ref[mi, shift:shift + 1, :])
            h = h.reshape(hr.stop - hr.start, d).astype(BF16)
            h_ref[hr, :] = h
            out_ref[xi] = swiglu(h).reshape(x.shape)

    @pl.when((j > 0) & (j < last))
    def _():
        out_ref[...] += swiglu(h_ref[...]).reshape(out_ref.shape)

    @pl.when((j == last) & (j > 0))
    def _():
        for xi, mi, hr in parts:
            x = x_ref[xi]
            f = out_ref[xi] + swiglu(h_ref[hr, :]).reshape(x.shape)
            n = f * lax.rsqrt(jnp.mean(f * f, axis=-1, keepdims=True) + EPS) * gpost_ref[...]
            out_ref[xi] = x + mod_ref[mi, gate:gate + 1, :] * n


def _ffn(x3, mod, gpre, wg, wu, wd, gpost, *, shift, scale, gate, rows, tf):
    nb, t, d = x3.shape
    dff = wg.shape[1]
    assert dff // tf >= 2
    bb, tt = _token_blocks(nb, t, rows)
    nt = t // tt
    once = dict(pipeline_mode=pl.Buffered(1)) if bb > 1 else {}
    mod, mod_shape, mod_index = _mod_block(mod, bb, nt)
    return pl.pallas_call(
        functools.partial(_ffn_body, shift=shift, scale=scale, gate=gate),
        grid=(nb * t // rows, dff // tf),
        in_specs=[pl.BlockSpec((bb, tt, d), lambda i, j: (i // nt, i % nt, 0), **once),
                  pl.BlockSpec(mod_shape, lambda i, j: mod_index(i), **once),
                  pl.BlockSpec((1, d), lambda i, j: (0, 0)),
                  pl.BlockSpec((d, tf), lambda i, j: (0, j)),
                  pl.BlockSpec((d, tf), lambda i, j: (0, j)),
                  pl.BlockSpec((tf, d), lambda i, j: (j, 0)),
                  pl.BlockSpec((1, d), lambda i, j: (0, 0))],
        out_specs=pl.BlockSpec((bb, tt, d), lambda i, j: (i // nt, i % nt, 0)),
        out_shape=jax.ShapeDtypeStruct((nb, t, d), F32),
        scratch_shapes=[pltpu.VMEM((rows, d), BF16)],
        compiler_params=_params("parallel", "arbitrary"),
        name="ffn",
    )(x3, mod, gpre, wg, wu, wd, gpost)


ROWS = 512
FFN_ROWS = 1024
FFN_COLS = 256
PROMPT_CHUNK = 128
PROMPT_BASE = 16
SAMPLE_BATCH = 8
S5_PROMPT_STEPS = 128
S5_SAMPLE_SEGMENTS = 8


def _layer_weights(w_ada, b_ada, g_pre_mix, g_post_mix, g_pre_ffn, g_post_ffn,
                   w_in, w_conv, a_log, dt_bias, g_dn_out,
                   lam_re, lam_im, log_step, b_re, b_im, c_re, c_im, d_skip,
                   w_glu, g_s5_out, w_out, w_gate, w_up, w_down):
    nheads = a_log.shape[0]
    mix_a = nheads * DN_HEAD_DIM
    pad = lambda v: jnp.pad(v, (0, LANE - v.shape[0]))[None, :]
    wa, abre, abim = _s5_prep(lam_re, lam_im, log_step, b_re, b_im)
    return dict(
        w_ada=w_ada, b_ada=b_ada[None, :],
        g_pre_mix=g_pre_mix[None, :], g_post_mix=g_post_mix[None, :],
        g_pre_ffn=g_pre_ffn[None, :], g_post_ffn=g_post_ffn[None, :],
        w_proj=_win_prep(jnp.swapaxes(w_in, 0, 1), 4 * mix_a, 2 * nheads), u_col=4 * mix_a,
        w_conv=w_conv, alog=pad(a_log), dtb=pad(dt_bias), gdn=g_dn_out[None, :],
        wa=wa, abre=abre, abim=abim, wcre=_s5_out_weights(c_re), wcim=_s5_out_weights(c_im),
        dskip=d_skip.reshape(1, -1), wglu=w_glu.astype(BF16), gs5=g_s5_out[None, :],
        wtop=w_out[:mix_a].astype(BF16), wbot=w_out[mix_a:].astype(BF16),
        wg=w_gate, wu=w_up, wd=w_down,
    )


def _layer_group(x3, mod, conv_state, s0, h0re, h0im, lw, *, prompt):
    nb, t, d = x3.shape
    rows = min(ROWS, nb * t)
    proj = _nmm(x3, mod, lw["g_pre_mix"], lw["w_proj"], shift=0, scale=1, rows=rows if t >= rows else rows // 2)
    if prompt:
        o, cnew, snew = _delta_prompt(proj, proj, conv_state, s0, lw["w_conv"], lw["alog"], lw["dtb"], lw["gdn"],
                                      nb=nb, t=t, c=PROMPT_CHUNK, base=PROMPT_BASE)
        y5, hre, him = _s5(proj, lw["u_col"], h0re, h0im, lw["wa"], lw["abre"], lw["abim"], lw["wcre"], lw["wcim"],
                           lw["dskip"], lw["wglu"], lw["gs5"], nb_total=nb, t=t, nb=nb, lc=S5_PROMPT_STEPS, nseg=1)
    else:
        o, cnew, snew = _delta_sample(proj, proj, conv_state, s0, lw["w_conv"], lw["alog"], lw["dtb"], lw["gdn"],
                                      nb=nb, t=t, bb=SAMPLE_BATCH)
        y5, hre, him = _s5(proj, lw["u_col"], h0re, h0im, lw["wa"], lw["abre"], lw["abim"], lw["wcre"], lw["wcim"],
                           lw["dskip"], lw["wglu"], lw["gs5"], nb_total=nb, t=t, nb=SUBLANE, lc=t,
                           nseg=min(S5_SAMPLE_SEGMENTS, nb // SUBLANE))
    x1 = _outproj(o, y5, lw["wtop"], lw["wbot"], x3, mod, lw["g_post_mix"], gate=2, rows=rows)
    x2 = _ffn(x1, mod, lw["g_pre_ffn"], lw["wg"], lw["wu"], lw["wd"], lw["g_post_ffn"],
              shift=3, scale=4, gate=5, rows=min(FFN_ROWS, nb * t), tf=FFN_COLS)
    ngroups = h0re.shape[1] // S5_STATE
    return x2, cnew, snew, hre.reshape(nb, ngroups, S5_STATE), him.reshape(nb, ngroups, S5_STATE)


def kernel(x_prompt, x_sample, c_prompt, c_sample, state_conv, state_delta, state_ssm_re, state_ssm_im, w_ada, b_ada, g_pre_mix, g_post_mix, g_pre_ffn, g_post_ffn, w_in, w_conv, a_log, dt_bias, g_dn_out, lam_re, lam_im, log_step, b_re, b_im, c_re, c_im, d_skip, w_glu, g_s5_out, w_out, w_gate, w_up, w_down):
    weights = (w_ada, b_ada, g_pre_mix, g_post_mix, g_pre_ffn, g_post_ffn,
               w_in, w_conv, a_log, dt_bias, g_dn_out,
               lam_re, lam_im, log_step, b_re, b_im, c_re, c_im, d_skip,
               w_glu, g_s5_out, w_out, w_gate, w_up, w_down)
    depth = w_ada.shape[0]
    bp, d = c_prompt.shape
    bs = c_sample.shape[0]
    dt_ = x_prompt.dtype
    nheads = state_delta.shape[2]
    nstate = state_ssm_re.shape[2] * state_ssm_re.shape[3]
    crows = -(-(bp + bs) // SUBLANE) * SUBLANE
    c_all = jnp.concatenate([c_sample, c_prompt, jnp.zeros((crows - bp - bs, d), dt_)], axis=0)
    yp, ys = x_prompt, x_sample
    outs = [[] for _ in range(8)]
    for l in range(depth):
        lw = _layer_weights(*[w[l] for w in weights])
        mod = _ada(c_all, lw["w_ada"], lw["b_ada"]).reshape(crows, 6, d)
        zc = jnp.zeros((bp, CONV_WIDTH - 1, state_conv.shape[3]), dt_)
        zd = jnp.zeros((bp,) + state_delta.shape[2:], dt_)
        zs = jnp.zeros((bp, nstate), dt_)
        yp, c1, d1, r1, i1 = _layer_group(yp, (mod, bs), zc, zd, zs, zs, lw, prompt=True)
        ys, c2, d2, r2, i2 = _layer_group(ys, (mod, 0), state_conv[l], state_delta[l],
                                          state_ssm_re[l].reshape(bs, nstate), state_ssm_im[l].reshape(bs, nstate),
                                          lw, prompt=False)
        for lst, v in zip(outs, (c1, d1, r1, i1, c2, d2, r2, i2)):
            lst.append(v)
    stacked = [jnp.stack(v) for v in outs]
    return (yp, ys, *stacked)
```

```python
import functools
import math

import jax
import jax.numpy as jnp
import numpy as np
from jax import lax
from jax.experimental import pallas as pl
from jax.experimental.pallas import tpu as pltpu

F32 = jnp.float32
BF16 = jnp.bfloat16
EPS = 1e-6
LANE = 128
SUBLANE = 8
V7X_VMEM_BYTES = 64 << 20
VMEM_LIMIT = V7X_VMEM_BYTES - (8 << 20)

DN_HEAD_DIM = 128
CONV_WIDTH = 4
S5_GROUP = 16
S5_STATE = 64
S5_SLAB_GROUPS = LANE // S5_GROUP
S5_SLAB_STATES = S5_SLAB_GROUPS * S5_STATE


def _params(*sem):
    return pltpu.CompilerParams(dimension_semantics=sem, vmem_limit_bytes=VMEM_LIMIT)


def _silu(x):
    return x * jax.nn.sigmoid(x)


def _mm(a, b):
    return jnp.dot(a.astype(BF16), b.astype(BF16), preferred_element_type=F32)


def _mm_nt(a, b):
    return lax.dot_general(a.astype(BF16), b.astype(BF16), (((1,), (1,)), ((), ())),
                           preferred_element_type=F32)


def _mm_tn(a, b):
    return lax.dot_general(a.astype(BF16), b.astype(BF16), (((0,), (0,)), ((), ())),
                           preferred_element_type=F32)


def _split3(x):
    x1 = x.astype(BF16)
    r1 = x - x1.astype(F32)
    x2 = r1.astype(BF16)
    x3 = (r1 - x2.astype(F32)).astype(BF16)
    return x1, x2, x3


def _ada_body(c_ref, w_ref, b_ref, o_ref):
    s = _silu(c_ref[...])
    o_ref[...] = _mm(s, w_ref[...]) + b_ref[...]


def _ada(c, w, b):
    rows, d = c.shape
    n = w.shape[1]
    tn = 1024
    return pl.pallas_call(
        _ada_body,
        grid=(n // tn,),
        in_specs=[pl.BlockSpec((rows, d), lambda j: (0, 0)),
                  pl.BlockSpec((d, tn), lambda j: (0, j)),
                  pl.BlockSpec((1, tn), lambda j: (0, j))],
        out_specs=pl.BlockSpec((rows, tn), lambda j: (0, j)),
        out_shape=jax.ShapeDtypeStruct((rows, n), F32),
        compiler_params=_params("arbitrary"),
        name="ada",
    )(c, w, b)


def _mod_block(mod, bb, nt):
    arr, first = mod
    assert first % bb == 0
    return arr, (bb, 6, arr.shape[2]), lambda i: (first // bb + i // nt, 0, 0)


def _token_blocks(nb, t, rows):
    if t >= rows:
        assert t % rows == 0
        return 1, rows
    assert rows % t == 0 and nb % (rows // t) == 0
    return rows // t, t


def _modulated_norm(x, g, scale, shift):
    y = x * lax.rsqrt(jnp.mean(x * x, axis=-1, keepdims=True) + EPS) * g
    return y * (1.0 + scale) + shift


def _win_prep_body(w_ref, o_ref, *, main, gates):
    n = w_ref.shape[0]
    rest = n - main - gates
    o_ref[0:main, :] = w_ref[0:main, :].astype(BF16)
    o_ref[main:main + rest, :] = w_ref[main + gates:n, :].astype(BF16)
    o_ref[main + rest:main + rest + gates, :] = w_ref[main:main + gates, :].astype(BF16)
    o_ref[main + rest + gates:, :] = jnp.zeros((LANE - gates, o_ref.shape[1]), BF16)


def _win_prep(w_in_t, main, gates):
    n, d = w_in_t.shape
    cb = min(512, d)
    nout = n - gates + LANE
    return pl.pallas_call(
        functools.partial(_win_prep_body, main=main, gates=gates),
        grid=(d // cb,),
        in_specs=[pl.BlockSpec((n, cb), lambda i: (0, i))],
        out_specs=pl.BlockSpec((nout, cb), lambda i: (0, i)),
        out_shape=jax.ShapeDtypeStruct((nout, d), BF16),
        compiler_params=_params("parallel"),
        name="winprep",
    )(w_in_t)


INPROJ_ROWS = 256


def _inproj_body(x_ref, mod_ref, g_ref, w_ref, wc_ref, cst_ref, o_ref, cnew_ref, xc_ref,
                 *, shift, scale, nheads, nt):
    bb, tt, d = x_ref.shape
    rows = bb * tt
    dh = DN_HEAD_DIM
    mix = nheads * dh
    halo = SUBLANE
    taps = CONV_WIDTH - 1
    n = w_ref.shape[0]

    if bb == 1:
        @pl.when(pl.program_id(0) % nt == 0)
        def _():
            xc_ref[halo - taps:halo, :] = cst_ref[0]

    h = _modulated_norm(x_ref[...], g_ref[...], mod_ref[:, scale:scale + 1, :], mod_ref[:, shift:shift + 1, :])
    h = h.reshape(rows, d).astype(BF16)

    def project(c0, c1):
        return lax.dot_general(h, w_ref[c0:c1, :], (((1,), (1,)), ((), ())), preferred_element_type=F32)

    def conv_silu(c0, c1, raw):
        cols = slice(c0, c1)
        if bb == 1:
            xc_ref[halo:halo + rows, cols] = raw
            conv = xc_ref[halo - taps:halo - taps + rows, cols] * wc_ref[0:1, cols]
            for j in range(1, CONV_WIDTH):
                conv = conv + xc_ref[halo - taps + j:halo - taps + j + rows, cols] * wc_ref[j:j + 1, cols]
            cnew_ref[0, :, cols] = xc_ref[halo + rows - taps:halo + rows, cols]
            xc_ref[0:halo, cols] = xc_ref[rows:rows + halo, cols]
        else:
            xc_ref[:, halo - taps:halo, cols] = cst_ref[:, :, cols]
            xc_ref[:, halo:halo + tt, cols] = raw.reshape(bb, tt, c1 - c0)
            conv = xc_ref[:, halo - taps:halo - taps + tt, cols] * wc_ref[0:1, cols]
            for j in range(1, CONV_WIDTH):
                conv = conv + xc_ref[:, halo - taps + j:halo - taps + j + tt, cols] * wc_ref[j:j + 1, cols]
            cnew_ref[:, :, cols] = xc_ref[:, halo + tt - taps:halo + tt, cols]
            conv = conv.reshape(rows, c1 - c0)
        return _silu(conv)

    def store_l2(c0, act, gain):
        for hd in range(nheads):
            a = act[:, hd * dh:(hd + 1) * dh]
            o_ref[:, c0 + hd * dh:c0 + (hd + 1) * dh] = (
                a * (lax.rsqrt(jnp.sum(a * a, axis=-1, keepdims=True) + 1e-6) * gain))

    q_raw = project(0, mix)
    k_raw = project(mix, 2 * mix)
    store_l2(0, conv_silu(0, mix, q_raw), dh ** -0.5)
    v_raw = project(2 * mix, 3 * mix)
    store_l2(mix, conv_silu(mix, 2 * mix, k_raw), 1.0)
    o_ref[:, 3 * mix:n] = project(3 * mix, n)
    o_ref[:, 2 * mix:3 * mix] = conv_silu(2 * mix, 3 * mix, v_raw)


def _inproj(x3, mod, g, w_t, w_conv, conv_state, *, shift, scale, rows, nheads):
    nb, t, d = x3.shape
    n = w_t.shape[0]
    qkv = 3 * nheads * DN_HEAD_DIM
    bb, tt = _token_blocks(nb, t, rows)
    nt = t // tt
    mod, mod_shape, mod_index = _mod_block(mod, bb, nt)
    if bb == 1:
        scratch = pltpu.VMEM((rows + SUBLANE, qkv), F32)
    else:
        scratch = pltpu.VMEM((bb, SUBLANE + tt, qkv), F32)
    return pl.pallas_call(
        functools.partial(_inproj_body, shift=shift, scale=scale, nheads=nheads, nt=nt),
        grid=(nb * t // rows,),
        in_specs=[pl.BlockSpec((bb, tt, d), lambda i: (i // nt, i % nt, 0)),
                  pl.BlockSpec(mod_shape, mod_index),
                  pl.BlockSpec((1, d), lambda i: (0, 0)),
                  pl.BlockSpec((n, d), lambda i: (0, 0), pipeline_mode=pl.Buffered(1)),
                  pl.BlockSpec((CONV_WIDTH, qkv), lambda i: (0, 0)),
                  pl.BlockSpec((bb, CONV_WIDTH - 1, qkv), lambda i: (i // nt, 0, 0))],
        out_specs=[pl.BlockSpec((rows, n), lambda i: (i, 0)),
                   pl.BlockSpec((bb, CONV_WIDTH - 1, qkv), lambda i: (i // nt, 0, 0))],
        out_shape=[jax.ShapeDtypeStruct((nb * t, n), F32),
                   jax.ShapeDtypeStruct((nb, CONV_WIDTH - 1, qkv), F32)],
        scratch_shapes=[scratch],
        compiler_params=_params("arbitrary"),
        name="inproj",
    )(x3, mod, g, w_t, w_conv, conv_state)


def _seg_masks(c, seg):
    row = lax.broadcasted_iota(jnp.int32, (c, c), 0)
    col = lax.broadcasted_iota(jnp.int32, (c, c), 1)
    sh = int(math.log2(seg))
    same = (row >> sh) == (col >> sh)
    return row, col, same, same & (row >= col), same & (row > col)


def _tri_inverse_offdiag(lmats, row, col, base, seg):
    bsh = int(math.log2(base))
    blk = (row >> bsh) == (col >> bsh)
    l0s = [jnp.where(blk, l, 0.0) for l in lmats]
    pts = [-l for l in l0s]
    qs = [_mm(l, l) for l in l0s]
    k = 2
    while k < base:
        pts = [p + q + _mm(p, q) for p, q in zip(pts, qs)]
        k *= 2
        if k < base:
            qs = [_mm(q, q) for q in qs]
    m = base
    while m < seg:
        msh = int(math.log2(m))
        sel = ((row >> (msh + 1)) == (col >> (msh + 1))) & ((row >> msh) != (col >> msh))
        es = [jnp.where(sel, l, 0.0) for l in lmats]
        xs = [e + _mm(p, e) for p, e in zip(pts, es)]
        pts = [p - (x + _mm(x, p)) for p, x in zip(pts, xs)]
        m *= 2
    return pts


def _delta_gates(ab, alog, dtb, causal_bf, seg, nheads):
    lane = lax.broadcasted_iota(jnp.int32, ab.shape, 1)
    x = ab + dtb
    softplus = jnp.maximum(x, 0.0) + jnp.log1p(jnp.exp(-jnp.abs(x)))
    g = jnp.where(lane < nheads, -jnp.exp(alog) * softplus, 0.0)
    beta = jax.nn.sigmoid(ab)
    pieces = _split3(g)
    gcum = sum(jnp.dot(causal_bf, p, preferred_element_type=F32) for p in pieces)
    g3 = gcum.reshape(gcum.shape[0] // seg, seg, gcum.shape[1])
    glast = jnp.broadcast_to(g3[:, seg - 1:seg, :], g3.shape).reshape(gcum.shape)
    return gcum, beta, glast


def _decay_logits(gcum, nheads):
    c = gcum.shape[0]
    if c < LANE:
        gcum = jnp.concatenate([gcum, jnp.zeros((LANE - c, LANE), F32)], axis=0)
    return gcum.T


def _delta_intra(act, nheads, gates, dlog, masks, base, seg, heads=None):
    gcum, beta, glast = gates
    row, col, _, causal, strict = masks
    c = act.shape[0]
    dh = DN_HEAD_DIM
    heads = list(range(nheads)) if heads is None else list(heads)
    eg_all = jnp.exp(gcum)
    ed_all = jnp.exp(glast - gcum)
    ks, kqs_lhs, rhss, qgs, kdecs = [], [], [], [], []
    for h in heads:
        q = act[:, h * dh:(h + 1) * dh]
        k = act[:, (nheads + h) * dh:(nheads + h + 1) * dh]
        v = act[:, (2 * nheads + h) * dh:(2 * nheads + h + 1) * dh]
        b = beta[:, nheads + h:nheads + h + 1]
        eg = eg_all[:, h:h + 1]
        kb = k * b
        ks.append(k)
        kqs_lhs.append(jnp.concatenate([kb, q], axis=0))
        rhss.append(jnp.concatenate([v * b, kb * eg], axis=1))
        qgs.append(q * eg)
        kdecs.append(k * ed_all[:, h:h + 1])
    kqs = [_mm_nt(a, k) for a, k in zip(kqs_lhs, ks)]
    lmats, attns = [], []
    for n, h in enumerate(heads):
        dl = gcum[:, h:h + 1] - dlog[h:h + 1, 0:c]
        dec = jnp.where(causal, jnp.exp(jnp.where(causal, dl, 0.0)), 0.0)
        lmats.append(jnp.where(strict, kqs[n][:c] * dec, 0.0))
        attns.append(kqs[n][c:] * dec)
    tts = _tri_inverse_offdiag(lmats, row, col, base, seg)
    uws = [r + _mm(t, r) for t, r in zip(tts, rhss)]
    return [uw[:, :dh] for uw in uws], [uw[:, dh:] for uw in uws], qgs, kdecs, attns


DELTA_HEAD_GROUP = 8


def _delta_out(o, z, gdn):
    on = o * lax.rsqrt(jnp.mean(o * o, axis=-1, keepdims=True) + EPS) * gdn
    return on * _silu(z)


def _delta_prompt_body(qkv_ref, z_ref, ab_ref, s0_ref, alog_ref, dtb_ref, gdn_ref,
                       o_ref, snew_ref, s_ref, *, c, nheads, base):
    dh = DN_HEAD_DIM

    @pl.when(pl.program_id(1) == 0)
    def _():
        s_ref[...] = s0_ref[0]

    act = qkv_ref[...]
    masks = _seg_masks(c, c)
    causal_bf = jnp.where(masks[3], 1.0, 0.0).astype(BF16)
    gates = _delta_gates(ab_ref[...], alog_ref[...], dtb_ref[...], causal_bf, c, nheads)
    dlog = _decay_logits(gates[0], nheads)
    eglast = jnp.exp(gates[2][0:1, :])
    gdn = gdn_ref[...]
    for g0 in range(0, nheads, DELTA_HEAD_GROUP):
        heads = list(range(g0, min(g0 + DELTA_HEAD_GROUP, nheads)))
        us, ws_, qgs, kdecs, attns = _delta_intra(act, nheads, gates, dlog, masks, base, c, heads)
        idx = range(len(heads))
        ss = [s_ref[h] for h in heads]
        wss = [_mm(jnp.concatenate([ws_[n], qgs[n]], axis=0), ss[n]) for n in idx]
        vnews = [us[n] - wss[n][:c] for n in idx]
        os_ = [wss[n][c:] + _mm(attns[n], vnews[n]) for n in idx]
        snews = [ss[n] * eglast[:, h:h + 1] + _mm_tn(kdecs[n], vnews[n]) for n, h in enumerate(heads)]
        for n, h in enumerate(heads):
            s_ref[h] = snews[n]
            o_ref[:, h * dh:(h + 1) * dh] = _delta_out(os_[n], z_ref[:, h * dh:(h + 1) * dh], gdn).astype(o_ref.dtype)

    @pl.when(pl.program_id(1) == pl.num_programs(1) - 1)
    def _():
        snew_ref[0] = s_ref[...]


def _delta_prompt(proj, s0, alog, dtb, gdn, *, nb, t, c, base):
    nheads = s0.shape[1]
    dh = DN_HEAD_DIM
    mix = nheads * dh
    nc = t // c
    return pl.pallas_call(
        functools.partial(_delta_prompt_body, c=c, nheads=nheads, base=base),
        grid=(nb, nc),
        in_specs=[pl.BlockSpec((c, 3 * mix), lambda b, i: (b * nc + i, 0)),
                  pl.BlockSpec((c, mix), lambda b, i: (b * nc + i, 3)),
                  pl.BlockSpec((c, LANE), lambda b, i: (b * nc + i, proj.shape[1] // LANE - 1)),
                  pl.BlockSpec((1, nheads, dh, dh), lambda b, i: (b, 0, 0, 0)),
                  pl.BlockSpec((1, LANE), lambda b, i: (0, 0)),
                  pl.BlockSpec((1, LANE), lambda b, i: (0, 0)),
                  pl.BlockSpec((1, dh), lambda b, i: (0, 0))],
        out_specs=[pl.BlockSpec((c, mix), lambda b, i: (b * nc + i, 0)),
                   pl.BlockSpec((1, nheads, dh, dh), lambda b, i: (b, 0, 0, 0))],
        out_shape=[jax.ShapeDtypeStruct((nb * t, mix), BF16),
                   jax.ShapeDtypeStruct((nb, nheads, dh, dh), F32)],
        scratch_shapes=[pltpu.VMEM((nheads, dh, dh), F32)],
        compiler_params=_params("parallel", "arbitrary"),
        name="delta_prompt",
    )(proj, proj, proj, s0, alog, dtb, gdn)


def _delta_sample_body(qkv_ref, z_ref, ab_ref, s0_ref, alog_ref, dtb_ref, gdn_ref,
                       o_ref, snew_ref,
                       w_s, qg_s, u_s, kd_s, vn_s, qs_s, attn_s, egl_s, *, bb, t, nheads):
    dh = DN_HEAD_DIM
    c = bb * t
    act = qkv_ref[...]
    masks = _seg_masks(c, t)
    causal_bf = jnp.where(masks[3], 1.0, 0.0).astype(BF16)
    gates = _delta_gates(ab_ref[...], alog_ref[...], dtb_ref[...], causal_bf, t, nheads)
    dlog = _decay_logits(gates[0], nheads)
    egl_s[...] = jnp.exp(gates[2])
    us, ws_, qgs, kdecs, attns = _delta_intra(act, nheads, gates, dlog, masks, t, t)
    for h in range(nheads):
        hs = slice(h * dh, (h + 1) * dh)
        u_s[:, hs] = us[h]
        w_s[:, hs] = ws_[h]
        qg_s[:, hs] = qgs[h]
        kd_s[:, hs] = kdecs[h]
        attn_s[h] = attns[h]

    def per_batch(b, carry):
        r = pl.multiple_of(b * t, t)
        rows = pl.ds(r, t)
        heads = range(nheads)
        hsl = [slice(h * dh, (h + 1) * dh) for h in heads]
        ss = [s0_ref[b, h] for h in heads]
        wss = [_mm(jnp.concatenate([w_s[rows, hsl[h]], qg_s[rows, hsl[h]]], axis=0), ss[h]) for h in heads]
        vnews = [u_s[rows, hsl[h]] - wss[h][:t] for h in heads]
        upds = [_mm_tn(kd_s[rows, hsl[h]], vnews[h]) for h in heads]
        egl = egl_s[pl.ds(r, 1), :]
        for h in heads:
            vn_s[rows, hsl[h]] = vnews[h]
            qs_s[rows, hsl[h]] = wss[h][t:]
            snew_ref[b, h] = ss[h] * egl[:, h:h + 1] + upds[h]
        return carry

    lax.fori_loop(0, bb, per_batch, 0, unroll=2)

    gdn = gdn_ref[...]
    for h in range(nheads):
        hs = slice(h * dh, (h + 1) * dh)
        o = qs_s[:, hs] + _mm(attn_s[h], vn_s[:, hs])
        o_ref[:, hs] = _delta_out(o, z_ref[:, hs], gdn).astype(o_ref.dtype)


def _delta_sample(proj, s0, alog, dtb, gdn, *, nb, t, bb):
    nheads = s0.shape[1]
    dh = DN_HEAD_DIM
    mix = nheads * dh
    c = bb * t
    return pl.pallas_call(
        functools.partial(_delta_sample_body, bb=bb, t=t, nheads=nheads),
        grid=(nb // bb,),
        in_specs=[pl.BlockSpec((c, 3 * mix), lambda i: (i, 0)),
                  pl.BlockSpec((c, mix), lambda i: (i, 3)),
                  pl.BlockSpec((c, LANE), lambda i: (i, proj.shape[1] // LANE - 1)),
                  pl.BlockSpec((bb, nheads, dh, dh), lambda i: (i, 0, 0, 0)),
                  pl.BlockSpec((1, LANE), lambda i: (0, 0)),
                  pl.BlockSpec((1, LANE), lambda i: (0, 0)),
                  pl.BlockSpec((1, dh), lambda i: (0, 0))],
        out_specs=[pl.BlockSpec((c, mix), lambda i: (i, 0)),
                   pl.BlockSpec((bb, nheads, dh, dh), lambda i: (i, 0, 0, 0))],
        out_shape=[jax.ShapeDtypeStruct((nb * t, mix), BF16),
                   jax.ShapeDtypeStruct((nb, nheads, dh, dh), F32)],
        scratch_shapes=[pltpu.VMEM((c, mix), F32) for _ in range(6)]
                       + [pltpu.VMEM((nheads, c, c), F32), pltpu.VMEM((c, LANE), F32)],
        compiler_params=_params("parallel"),
        name="delta_sample",
    )(proj, proj, proj, s0, alog, dtb, gdn)


def _s5_prep_body(lre_ref, lim_ref, ls_ref, bre_ref, bim_ref, wa_ref, abre_ref, abim_ref):
    lre = jnp.minimum(lre_ref[...], -1e-4)
    lim = lim_ref[...]
    dt = jnp.exp(ls_ref[...])
    mag = jnp.exp(lre * dt)
    ang = lim * dt
    abr = mag * jnp.cos(ang)
    abi = mag * jnp.sin(ang)
    nr = abr - 1.0
    ni = abi
    den = lre * lre + lim * lim
    cor = (nr * lre + ni * lim) / den
    coi = (ni * lre - nr * lim) / den
    bre = bre_ref[...]
    bim = bim_ref[...]
    shape2 = bre.shape[1:]
    row = lax.broadcasted_iota(jnp.int32, shape2, 0)
    col = lax.broadcasted_iota(jnp.int32, shape2, 1)
    diag = ((row // S5_GROUP) == (col // S5_STATE))[None]
    half = shape2[1]
    rows = shape2[0]
    wre = jnp.where(diag, cor * bre - coi * bim, 0.0)
    wim = jnp.where(diag, cor * bim + coi * bre, 0.0)
    wa_ref[:, 0:rows, 0:half] = wre.astype(BF16)
    wa_ref[:, 0:rows, half:2 * half] = wim.astype(BF16)
    wa_ref[:, rows:2 * rows, 0:half] = (abr * wre - abi * wim).astype(BF16)
    wa_ref[:, rows:2 * rows, half:2 * half] = (abr * wim + abi * wre).astype(BF16)
    abre_ref[...] = abr
    abim_ref[...] = abi


def _s5_prep(lam_re, lam_im, log_step, b_re, b_im):
    groups, states, gch = b_re.shape
    slabs = groups // S5_SLAB_GROUPS
    cols = S5_SLAB_STATES

    def row_param(p):
        return p.reshape(slabs, 1, cols)

    def tiled_b(b):
        bt = jnp.transpose(b, (0, 2, 1)).reshape(slabs, LANE, states)
        return jnp.tile(bt, (1, 1, S5_SLAB_GROUPS))

    ls = jnp.broadcast_to(log_step[:, None], (groups, states))
    wa, abre, abim = pl.pallas_call(
        _s5_prep_body,
        out_shape=[jax.ShapeDtypeStruct((slabs, 2 * LANE, 2 * cols), BF16),
                   jax.ShapeDtypeStruct((slabs, 1, cols), F32),
                   jax.ShapeDtypeStruct((slabs, 1, cols), F32)],
        compiler_params=pltpu.CompilerParams(vmem_limit_bytes=VMEM_LIMIT),
        name="s5prep",
    )(row_param(lam_re), row_param(lam_im), row_param(ls), tiled_b(b_re), tiled_b(b_im))
    nblk = groups * states // LANE
    return wa, abre.reshape(nblk, 1, LANE), abim.reshape(nblk, 1, LANE)


def _s5_out_weights(c):
    groups, gch, states = c.shape
    slabs = groups // S5_SLAB_GROUPS
    ct = jnp.transpose(c, (0, 2, 1)).reshape(slabs, S5_SLAB_STATES, gch)
    ct = jnp.tile(ct, (1, 1, S5_SLAB_GROUPS))
    row = jnp.arange(S5_SLAB_STATES)[:, None] // states
    col = jnp.arange(LANE)[None, :] // gch
    return jnp.where((row == col)[None], ct, 0.0).astype(BF16)


def _gelu_tanh(x):
    return x * (0.5 * (1.0 + jnp.tanh(math.sqrt(2.0 / math.pi) * (x + 0.044715 * (x * x * x)))))


S5_PIPELINE_PARTS = 4


def _time_major_perms(nb, lc, nseg):
    seg_rows = nb * lc
    rows = nseg * seg_rows
    r = np.arange(rows)[:, None]
    c = np.arange(rows)[None, :]
    base, loc = r // seg_rows * seg_rows, r % seg_rows
    t_tm, b_tm = loc // nb, loc % nb
    to_tm = c == base + b_tm * lc + t_tm
    to_tm_prev = (c == base + b_tm * lc + t_tm - 1) & (t_tm % 2 == 1)
    to_bm = c == base + (loc % lc) * nb + loc // lc
    return tuple(jnp.asarray(m, BF16) for m in (to_tm, to_tm_prev, to_bm))


def _s5_body(u_ref, h0re_ref, h0im_ref, tm_ref, tmp_ref, bm_ref, wa_ref, abre_ref, abim_ref, wcre_ref, wcim_ref,
             dsk_ref, wglu_ref, gs5_ref, y_ref, hre_ref, him_ref, cre, cim, *xbufs, nb, lc, nseg):
    seg_rows = nb * lc
    rows = nseg * seg_rows
    nblk = cre.shape[0]
    slabs = wa_ref.shape[0]
    per_slab = nblk // slabs
    half = per_slab * LANE
    steps_per_tile = SUBLANE // nb
    assert steps_per_tile in (1, 2) and lc % steps_per_tile == 0
    assert nseg == 1 or nb == SUBLANE

    if nseg == 1:
        @pl.when(pl.program_id(1) == 0)
        def _():
            for j in range(nblk):
                if nb < SUBLANE:
                    cre[j] = jnp.zeros((SUBLANE, LANE), F32)
                    cim[j] = jnp.zeros((SUBLANE, LANE), F32)
                cre[j, SUBLANE - nb:SUBLANE, :] = h0re_ref[:, j * LANE:(j + 1) * LANE]
                cim[j, SUBLANE - nb:SUBLANE, :] = h0im_ref[:, j * LANE:(j + 1) * LANE]

    to_tm, to_bm = tm_ref[...], bm_ref[...]
    pieces = _split3(u_ref[...].reshape(rows, u_ref.shape[2]))[:2]
    u = sum(jnp.dot(to_tm, p, preferred_element_type=F32) for p in pieces)
    ub = u.astype(BF16)
    if steps_per_tile == 2:
        ub_prev = jnp.dot(tmp_ref[...], pieces[0], preferred_element_type=F32).astype(BF16)

    upper = lax.broadcasted_iota(jnp.int32, (SUBLANE, LANE), 0) >= nb
    parts = S5_PIPELINE_PARTS if rows % (S5_PIPELINE_PARTS * LANE) == 0 else 1
    prow = rows // parts

    def project_in(s, r):
        rs = slice(r * prow, (r + 1) * prow)
        sl = slice(s * LANE, (s + 1) * LANE)
        if steps_per_tile == 2:
            x = jnp.dot(jnp.concatenate([ub[rs, sl], ub_prev[rs, sl]], axis=1), wa_ref[s],
                        preferred_element_type=F32)
        else:
            x = jnp.dot(ub[rs, sl], wa_ref[s, 0:LANE, :], preferred_element_type=F32)
        for k in range(per_slab):
            xbufs[2 * s][k, rs, :] = x[:, k * LANE:(k + 1) * LANE]
            xbufs[2 * s + 1][k, rs, :] = x[:, half + k * LANE:half + (k + 1) * LANE]

    seg_tiles = seg_rows // SUBLANE

    def recur(s, r, coef, carry):
        xre, xim = xbufs[2 * s], xbufs[2 * s + 1]
        for v in range(r * prow // SUBLANE, (r + 1) * prow // SUBLANE):
            at = slice(v * SUBLANE, (v + 1) * SUBLANE)
            seg_rows_sl = slice(v // seg_tiles * nb, (v // seg_tiles + 1) * nb)
            if nseg > 1 and v % seg_tiles == 0:
                for n in range(per_slab):
                    cols = slice((s * per_slab + n) * LANE, (s * per_slab + n + 1) * LANE)
                    carry[n] = (h0re_ref[seg_rows_sl, cols], h0im_ref[seg_rows_sl, cols])
            for n in range(per_slab):
                hr, hi = carry[n]
                ar, ai = coef[n]
                if steps_per_tile == 2:
                    hr = jnp.where(upper, hr, pltpu.roll(hr, nb, 0))
                    hi = jnp.where(upper, hi, pltpu.roll(hi, nb, 0))
                nr = ar * hr - ai * hi + xre[n, at, :]
                ni = ar * hi + ai * hr + xim[n, at, :]
                xre[n, at, :] = nr
                xim[n, at, :] = ni
                carry[n] = (nr, ni)
            if nseg > 1 and v % seg_tiles == seg_tiles - 1:
                for n in range(per_slab):
                    cols = slice((s * per_slab + n) * LANE, (s * per_slab + n + 1) * LANE)
                    hre_ref[seg_rows_sl, cols], him_ref[seg_rows_sl, cols] = carry[n]

    def project_out(s, r):
        rs = slice(r * prow, (r + 1) * prow)
        hre = jnp.concatenate([xbufs[2 * s][k, rs, :] for k in range(per_slab)], axis=1)
        him = jnp.concatenate([xbufs[2 * s + 1][k, rs, :] for k in range(per_slab)], axis=1)
        return _mm(hre, wcre_ref[s]) - _mm(him, wcim_ref[s])

    ychunks = [[None] * parts for _ in range(slabs)]
    for r in range(parts):
        project_in(0, r)
    for s in range(slabs):
        js = list(range(s * per_slab, (s + 1) * per_slab))
        coef, carry = [], []
        for j in js:
            ar = jnp.broadcast_to(abre_ref[j], (SUBLANE, LANE))
            ai = jnp.broadcast_to(abim_ref[j], (SUBLANE, LANE))
            if steps_per_tile == 2:
                coef.append((jnp.where(upper, ar * ar - ai * ai, ar), jnp.where(upper, 2.0 * ar * ai, ai)))
            else:
                coef.append((ar, ai))
            carry.append((cre[j], cim[j]) if nseg == 1 else None)
        for r in range(parts):
            recur(s, r, coef, carry)
            if s + 1 < slabs:
                project_in(s + 1, r)
            if s >= 1:
                ychunks[s - 1][r] = project_out(s - 1, r)
        if nseg == 1:
            for n, j in enumerate(js):
                cre[j], cim[j] = carry[n]
    for r in range(parts):
        ychunks[slabs - 1][r] = project_out(slabs - 1, r)
    y = jnp.concatenate([jnp.concatenate(yc, axis=0) for yc in ychunks], axis=1) + dsk_ref[...] * u
    ab = _mm(_gelu_tanh(y), wglu_ref[...])
    n = y.shape[1]
    glu = ab[:, :n] * jax.nn.sigmoid(ab[:, n:])
    out = glu * lax.rsqrt(jnp.mean(glu * glu, axis=-1, keepdims=True) + EPS) * gs5_ref[...]
    out = jnp.dot(to_bm, out.astype(BF16), preferred_element_type=F32)
    y_ref[...] = out.reshape(y_ref.shape).astype(y_ref.dtype)

    if nseg == 1:
        @pl.when(pl.program_id(1) == pl.num_programs(1) - 1)
        def _():
            for j in range(nblk):
                hre_ref[:, j * LANE:(j + 1) * LANE] = cre[j, SUBLANE - nb:SUBLANE, :]
                him_ref[:, j * LANE:(j + 1) * LANE] = cim[j, SUBLANE - nb:SUBLANE, :]


def _s5(proj, u_col, h0re, h0im, wa, abre, abim, wcre, wcim, dskip, wglu, gs5, *, nb_total, t, nb, lc, nseg):
    mixb = dskip.shape[1]
    nstate = h0re.shape[1]
    nblk = nstate // LANE
    slabs = wa.shape[0]
    assert u_col % mixb == 0
    assert nseg == 1 or lc == t
    ublk = u_col // mixb
    nbs = nseg * nb
    uab3 = proj.reshape(nb_total, t, proj.shape[1])
    perms = _time_major_perms(nb, lc, nseg)
    full = lambda a: pl.BlockSpec(a.shape, lambda i, j: (0,) * a.ndim)
    y, hre, him = pl.pallas_call(
        functools.partial(_s5_body, nb=nb, lc=lc, nseg=nseg),
        grid=(nb_total // nbs, t // lc),
        in_specs=[pl.BlockSpec((nbs, lc, mixb), lambda i, j: (i, j, ublk)),
                  pl.BlockSpec((nbs, nstate), lambda i, j: (i, 0)),
                  pl.BlockSpec((nbs, nstate), lambda i, j: (i, 0))]
                 + [full(a) for a in perms]
                 + [full(wa), full(abre), full(abim), full(wcre), full(wcim), full(dskip), full(wglu), full(gs5)],
        out_specs=[pl.BlockSpec((nbs, lc, mixb), lambda i, j: (i, j, 0)),
                   pl.BlockSpec((nbs, nstate), lambda i, j: (i, 0)),
                   pl.BlockSpec((nbs, nstate), lambda i, j: (i, 0))],
        out_shape=[jax.ShapeDtypeStruct((nb_total, t, mixb), F32),
                   jax.ShapeDtypeStruct((nb_total, nstate), F32),
                   jax.ShapeDtypeStruct((nb_total, nstate), F32)],
        scratch_shapes=[pltpu.VMEM((nblk, SUBLANE, LANE), F32),
                        pltpu.VMEM((nblk, SUBLANE, LANE), F32)]
                       + [pltpu.VMEM((nblk // slabs, nbs * lc, LANE), F32) for _ in range(2 * slabs)],
        compiler_params=_params("parallel", "arbitrary"),
        name="s5",
    )(uab3, h0re, h0im, *perms, wa, abre, abim, wcre, wcim, dskip, wglu, gs5)
    return y.reshape(nb_total * t, mixb), hre, him


PART_ROWS = 256


def _row_parts(bb, tt):
    rows = bb * tt
    nparts = max(1, rows // PART_ROWS)
    if bb > 1:
        nparts = min(nparts, bb)
    prow = rows // nparts
    out = []
    for r in range(nparts):
        flat = slice(r * prow, (r + 1) * prow)
        if bb == 1:
            out.append(((slice(None), flat), slice(None), flat))
        else:
            bs = slice(r * bb // nparts, (r + 1) * bb // nparts)
            out.append(((bs, slice(None)), bs, flat))
    return out


def _outproj_body(o_ref, y_ref, wt_ref, wb_ref, x_ref, mod_ref, g_ref, out_ref, *, gate):
    mix = (jnp.dot(o_ref[...], wt_ref[...], preferred_element_type=F32)
           + _mm(y_ref[...], wb_ref[...]))
    n = mix * lax.rsqrt(jnp.mean(mix * mix, axis=-1, keepdims=True) + EPS) * g_ref[...]
    x = x_ref[...]
    out_ref[...] = x + mod_ref[:, gate:gate + 1, :] * n.reshape(x.shape)


def _outproj(o, y5, wtop, wbot, x3, mod, g, *, gate, rows):
    nb, t, d = x3.shape
    bb, tt = _token_blocks(nb, t, rows)
    nt = t // tt
    ka, kb = o.shape[1], y5.shape[1]
    mod, mod_shape, mod_index = _mod_block(mod, bb, nt)
    return pl.pallas_call(
        functools.partial(_outproj_body, gate=gate),
        grid=(nb * t // rows,),
        in_specs=[pl.BlockSpec((rows, ka), lambda i: (i, 0)),
                  pl.BlockSpec((rows, kb), lambda i: (i, 0)),
                  pl.BlockSpec((ka, d), lambda i: (0, 0)),
                  pl.BlockSpec((kb, d), lambda i: (0, 0)),
                  pl.BlockSpec((bb, tt, d), lambda i: (i // nt, i % nt, 0)),
                  pl.BlockSpec(mod_shape, mod_index),
                  pl.BlockSpec((1, d), lambda i: (0, 0))],
        out_specs=pl.BlockSpec((bb, tt, d), lambda i: (i // nt, i % nt, 0)),
        out_shape=jax.ShapeDtypeStruct((nb, t, d), F32),
        compiler_params=_params("parallel"),
        name="outproj",
    )(o, y5, wtop, wbot, x3, mod, g)


def _ffn_body(x_ref, mod_ref, gpre_ref, wg_ref, wu_ref, wd_ref, gpost_ref, out_ref, h_ref,
              *, shift, scale, gate):
    j = pl.program_id(1)
    last = pl.num_programs(1) - 1
    bb, tt, d = x_ref.shape
    parts = _row_parts(bb, tt)

    def swiglu(h):
        a = _silu(_mm(h, wg_ref[...])) * _mm(h, wu_ref[...])
        return _mm(a, wd_ref[...])

    @pl.when(j == 0)
    def _():
        for xi, mi, hr in parts:
            x = x_ref[xi]
            h = _modulated_norm(x, gpre_ref[...], mod_ref[mi, scale:scale + 1, :], mod_ref[mi, shift:shift + 1, :])
            h = h.reshape(hr.stop - hr.start, d).astype(BF16)
            h_ref[hr, :] = h
            out_ref[xi] = swiglu(h).reshape(x.shape)

    @pl.when((j > 0) & (j < last))
    def _():
        out_ref[...] += swiglu(h_ref[...]).reshape(out_ref.shape)

    @pl.when((j == last) & (j > 0))
    def _():
        for xi, mi, hr in parts:
            x = x_ref[xi]
            f = out_ref[xi] + swiglu(h_ref[hr, :]).reshape(x.shape)
            n = f * lax.rsqrt(jnp.mean(f * f, axis=-1, keepdims=True) + EPS) * gpost_ref[...]
            out_ref[xi] = x + mod_ref[mi, gate:gate + 1, :] * n


def _ffn(x3, mod, gpre, wg, wu, wd, gpost, *, shift, scale, gate, rows, tf):
    nb, t, d = x3.shape
    dff = wg.shape[1]
    assert dff // tf >= 2
    bb, tt = _token_blocks(nb, t, rows)
    nt = t // tt
    once = dict(pipeline_mode=pl.Buffered(1)) if bb > 1 else {}
    mod, mod_shape, mod_index = _mod_block(mod, bb, nt)
    return pl.pallas_call(
        functools.partial(_ffn_body, shift=shift, scale=scale, gate=gate),
        grid=(nb * t // rows, dff // tf),
        in_specs=[pl.BlockSpec((bb, tt, d), lambda i, j: (i // nt, i % nt, 0), **once),
                  pl.BlockSpec(mod_shape, lambda i, j: mod_index(i), **once),
                  pl.BlockSpec((1, d), lambda i, j: (0, 0)),
                  pl.BlockSpec((d, tf), lambda i, j: (0, j)),
                  pl.BlockSpec((d, tf), lambda i, j: (0, j)),
                  pl.BlockSpec((tf, d), lambda i, j: (j, 0)),
                  pl.BlockSpec((1, d), lambda i, j: (0, 0))],
        out_specs=pl.BlockSpec((bb, tt, d), lambda i, j: (i // nt, i % nt, 0)),
        out_shape=jax.ShapeDtypeStruct((nb, t, d), F32),
        scratch_shapes=[pltpu.VMEM((rows, d), BF16)],
        compiler_params=_params("parallel", "arbitrary"),
        name="ffn",
    )(x3, mod, gpre, wg, wu, wd, gpost)


ROWS = 512
FFN_ROWS = 1024
FFN_COLS = 256
PROMPT_CHUNK = 128
PROMPT_BASE = 16
SAMPLE_BATCH = 8
S5_PROMPT_STEPS = 128
S5_SAMPLE_SEGMENTS = 8


def _layer_weights(w_ada, b_ada, g_pre_mix, g_post_mix, g_pre_ffn, g_post_ffn,
                   w_in, w_conv, a_log, dt_bias, g_dn_out,
                   lam_re, lam_im, log_step, b_re, b_im, c_re, c_im, d_skip,
                   w_glu, g_s5_out, w_out, w_gate, w_up, w_down):
    nheads = a_log.shape[0]
    mix_a = nheads * DN_HEAD_DIM
    pad = lambda v: jnp.pad(v, (0, LANE - v.shape[0]))[None, :]
    wa, abre, abim = _s5_prep(lam_re, lam_im, log_step, b_re, b_im)
    return dict(
        w_ada=w_ada, b_ada=b_ada[None, :],
        g_pre_mix=g_pre_mix[None, :], g_post_mix=g_post_mix[None, :],
        g_pre_ffn=g_pre_ffn[None, :], g_post_ffn=g_post_ffn[None, :],
        w_proj=_win_prep(jnp.swapaxes(w_in, 0, 1), 4 * mix_a, 2 * nheads), u_col=4 * mix_a,
        w_conv=w_conv, alog=pad(a_log), dtb=pad(dt_bias), gdn=g_dn_out[None, :],
        wa=wa, abre=abre, abim=abim, wcre=_s5_out_weights(c_re), wcim=_s5_out_weights(c_im),
        dskip=d_skip.reshape(1, -1), wglu=w_glu.astype(BF16), gs5=g_s5_out[None, :],
        wtop=w_out[:mix_a].astype(BF16), wbot=w_out[mix_a:].astype(BF16),
        wg=w_gate, wu=w_up, wd=w_down,
    )


def _layer_group(x3, mod, conv_state, s0, h0re, h0im, lw, *, prompt):
    nb, t, d = x3.shape
    rows = min(ROWS, nb * t)
    nheads = s0.shape[1]
    proj, cnew = _inproj(x3, mod, lw["g_pre_mix"], lw["w_proj"], lw["w_conv"], conv_state, shift=0, scale=1,
                         rows=min(INPROJ_ROWS, nb * t), nheads=nheads)
    if prompt:
        o, snew = _delta_prompt(proj, s0, lw["alog"], lw["dtb"], lw["gdn"],
                                nb=nb, t=t, c=PROMPT_CHUNK, base=PROMPT_BASE)
        y5, hre, him = _s5(proj, lw["u_col"], h0re, h0im, lw["wa"], lw["abre"], lw["abim"], lw["wcre"], lw["wcim"],
                           lw["dskip"], lw["wglu"], lw["gs5"], nb_total=nb, t=t, nb=nb, lc=S5_PROMPT_STEPS, nseg=1)
    else:
        o, snew = _delta_sample(proj, s0, lw["alog"], lw["dtb"], lw["gdn"], nb=nb, t=t, bb=SAMPLE_BATCH)
        y5, hre, him = _s5(proj, lw["u_col"], h0re, h0im, lw["wa"], lw["abre"], lw["abim"], lw["wcre"], lw["wcim"],
                           lw["dskip"], lw["wglu"], lw["gs5"], nb_total=nb, t=t, nb=SUBLANE, lc=t,
                           nseg=min(S5_SAMPLE_SEGMENTS, nb // SUBLANE))
    x1 = _outproj(o, y5, lw["wtop"], lw["wbot"], x3, mod, lw["g_post_mix"], gate=2, rows=rows)
    x2 = _ffn(x1, mod, lw["g_pre_ffn"], lw["wg"], lw["wu"], lw["wd"], lw["g_post_ffn"],
              shift=3, scale=4, gate=5, rows=min(FFN_ROWS, nb * t), tf=FFN_COLS)
    ngroups = h0re.shape[1] // S5_STATE
    return x2, cnew, snew, hre.reshape(nb, ngroups, S5_STATE), him.reshape(nb, ngroups, S5_STATE)


def kernel(x_prompt, x_sample, c_prompt, c_sample, state_conv, state_delta, state_ssm_re, state_ssm_im, w_ada, b_ada, g_pre_mix, g_post_mix, g_pre_ffn, g_post_ffn, w_in, w_conv, a_log, dt_bias, g_dn_out, lam_re, lam_im, log_step, b_re, b_im, c_re, c_im, d_skip, w_glu, g_s5_out, w_out, w_gate, w_up, w_down):
    weights = (w_ada, b_ada, g_pre_mix, g_post_mix, g_pre_ffn, g_post_ffn,
               w_in, w_conv, a_log, dt_bias, g_dn_out,
               lam_re, lam_im, log_step, b_re, b_im, c_re, c_im, d_skip,
               w_glu, g_s5_out, w_out, w_gate, w_up, w_down)
    depth = w_ada.shape[0]
    bp, d = c_prompt.shape
    bs = c_sample.shape[0]
    dt_ = x_prompt.dtype
    nheads = state_delta.shape[2]
    nstate = state_ssm_re.shape[2] * state_ssm_re.shape[3]
    crows = -(-(bp + bs) // SUBLANE) * SUBLANE
    c_all = jnp.concatenate([c_sample, c_prompt, jnp.zeros((crows - bp - bs, d), dt_)], axis=0)
    yp, ys = x_prompt, x_sample
    outs = [[] for _ in range(8)]
    for l in range(depth):
        lw = _layer_weights(*[w[l] for w in weights])
        mod = _ada(c_all, lw["w_ada"], lw["b_ada"]).reshape(crows, 6, d)
        zc = jnp.zeros((bp, CONV_WIDTH - 1, state_conv.shape[3]), dt_)
        zd = jnp.zeros((bp,) + state_delta.shape[2:], dt_)
        zs = jnp.zeros((bp, nstate), dt_)
        yp, c1, d1, r1, i1 = _layer_group(yp, (mod, bs), zc, zd, zs, zs, lw, prompt=True)
        ys, c2, d2, r2, i2 = _layer_group(ys, (mod, 0), state_conv[l], state_delta[l],
                                          state_ssm_re[l].reshape(bs, nstate), state_ssm_im[l].reshape(bs, nstate),
                                          lw, prompt=False)
        for lst, v in zip(outs, (c1, d1, r1, i1, c2, d2, r2, i2)):
            lst.append(v)
    stacked = [jnp.stack(v) for v in outs]
    return (yp, ys, *stacked)
```

```python
import functools
import math

import jax
import jax.numpy as jnp
import numpy as np
from jax import lax
from jax.experimental import pallas as pl
from jax.experimental.pallas import tpu as pltpu

F32 = jnp.float32
BF16 = jnp.bfloat16
EPS = 1e-6
LANE = 128
SUBLANE = 8
V7X_VMEM_BYTES = 64 << 20
VMEM_LIMIT = V7X_VMEM_BYTES - (8 << 20)

DN_HEAD_DIM = 128
CONV_WIDTH = 4
S5_GROUP = 16
S5_STATE = 64
S5_SLAB_GROUPS = LANE // S5_GROUP
S5_SLAB_STATES = S5_SLAB_GROUPS * S5_STATE


def _params(*sem):
    return pltpu.CompilerParams(dimension_semantics=sem, vmem_limit_bytes=VMEM_LIMIT)


def _silu(x):
    return x * jax.nn.sigmoid(x)


def _mm(a, b):
    return jnp.dot(a.astype(BF16), b.astype(BF16), preferred_element_type=F32)


def _mm_nt(a, b):
    return lax.dot_general(a.astype(BF16), b.astype(BF16), (((1,), (1,)), ((), ())),
                           preferred_element_type=F32)


def _mm_tn(a, b):
    return lax.dot_general(a.astype(BF16), b.astype(BF16), (((0,), (0,)), ((), ())),
                           preferred_element_type=F32)


def _split3(x):
    x1 = x.astype(BF16)
    r1 = x - x1.astype(F32)
    x2 = r1.astype(BF16)
    x3 = (r1 - x2.astype(F32)).astype(BF16)
    return x1, x2, x3


def _ada_body(c_ref, w_ref, b_ref, o_ref):
    s = _silu(c_ref[...])
    o_ref[...] = _mm(s, w_ref[...]) + b_ref[...]


def _ada(c, w, b):
    rows, d = c.shape
    n = w.shape[1]
    tn = 2048 if n % 2048 == 0 else 1024
    return pl.pallas_call(
        _ada_body,
        grid=(n // tn,),
        in_specs=[pl.BlockSpec((rows, d), lambda j: (0, 0)),
                  pl.BlockSpec((d, tn), lambda j: (0, j)),
                  pl.BlockSpec((1, tn), lambda j: (0, j))],
        out_specs=pl.BlockSpec((rows, tn), lambda j: (0, j)),
        out_shape=jax.ShapeDtypeStruct((rows, n), F32),
        compiler_params=_params("arbitrary"),
        name="ada",
    )(c, w, b)


def _mod_block(mod, bb, nt):
    arr, first = mod
    assert first % bb == 0
    return arr, (bb, 6, arr.shape[2]), lambda i: (first // bb + i // nt, 0, 0)


def _token_blocks(nb, t, rows):
    if t >= rows:
        assert t % rows == 0
        return 1, rows
    assert rows % t == 0 and nb % (rows // t) == 0
    return rows // t, t


def _modulated_norm(x, g, scale, shift):
    y = x * lax.rsqrt(jnp.mean(x * x, axis=-1, keepdims=True) + EPS) * g
    return y * (1.0 + scale) + shift


def _win_prep_body(w_ref, o_ref, *, main, gates):
    n = w_ref.shape[0]
    rest = n - main - gates
    o_ref[0:main, :] = w_ref[0:main, :].astype(BF16)
    o_ref[main:main + rest, :] = w_ref[main + gates:n, :].astype(BF16)
    o_ref[main + rest:main + rest + gates, :] = w_ref[main:main + gates, :].astype(BF16)
    o_ref[main + rest + gates:, :] = jnp.zeros((LANE - gates, o_ref.shape[1]), BF16)


def _win_prep(w_in_t, main, gates):
    n, d = w_in_t.shape
    cb = min(512, d)
    nout = n - gates + LANE
    return pl.pallas_call(
        functools.partial(_win_prep_body, main=main, gates=gates),
        grid=(d // cb,),
        in_specs=[pl.BlockSpec((n, cb), lambda i: (0, i))],
        out_specs=pl.BlockSpec((nout, cb), lambda i: (0, i)),
        out_shape=jax.ShapeDtypeStruct((nout, d), BF16),
        compiler_params=_params("parallel"),
        name="winprep",
    )(w_in_t)


INPROJ_ROWS = 256


def _inproj_body(x_ref, mod_ref, g_ref, w_ref, wc_ref, cst_ref, o_ref, cnew_ref, xc_ref,
                 *, shift, scale, nheads, nt):
    bb, tt, d = x_ref.shape
    rows = bb * tt
    dh = DN_HEAD_DIM
    mix = nheads * dh
    halo = SUBLANE
    taps = CONV_WIDTH - 1
    n = w_ref.shape[0]

    if bb == 1:
        @pl.when(pl.program_id(0) % nt == 0)
        def _():
            xc_ref[halo - taps:halo, :] = cst_ref[0]

    h = _modulated_norm(x_ref[...], g_ref[...], mod_ref[:, scale:scale + 1, :], mod_ref[:, shift:shift + 1, :])
    h = h.reshape(rows, d).astype(BF16)

    def project(c0, c1):
        return lax.dot_general(h, w_ref[c0:c1, :], (((1,), (1,)), ((), ())), preferred_element_type=F32)

    def conv_silu(c0, c1, raw):
        cols = slice(c0, c1)
        if bb == 1:
            xc_ref[halo:halo + rows, cols] = raw
            conv = xc_ref[halo - taps:halo - taps + rows, cols] * wc_ref[0:1, cols]
            for j in range(1, CONV_WIDTH):
                conv = conv + xc_ref[halo - taps + j:halo - taps + j + rows, cols] * wc_ref[j:j + 1, cols]
            cnew_ref[0, :, cols] = xc_ref[halo + rows - taps:halo + rows, cols]
            xc_ref[0:halo, cols] = xc_ref[rows:rows + halo, cols]
        else:
            xc_ref[:, halo - taps:halo, cols] = cst_ref[:, :, cols]
            xc_ref[:, halo:halo + tt, cols] = raw.reshape(bb, tt, c1 - c0)
            conv = xc_ref[:, halo - taps:halo - taps + tt, cols] * wc_ref[0:1, cols]
            for j in range(1, CONV_WIDTH):
                conv = conv + xc_ref[:, halo - taps + j:halo - taps + j + tt, cols] * wc_ref[j:j + 1, cols]
            cnew_ref[:, :, cols] = xc_ref[:, halo + tt - taps:halo + tt, cols]
            conv = conv.reshape(rows, c1 - c0)
        return _silu(conv)

    def store_l2(c0, act, gain):
        for hd in range(nheads):
            a = act[:, hd * dh:(hd + 1) * dh]
            o_ref[:, c0 + hd * dh:c0 + (hd + 1) * dh] = (
                a * (lax.rsqrt(jnp.sum(a * a, axis=-1, keepdims=True) + 1e-6) * gain))

    q_raw = project(0, mix)
    k_raw = project(mix, 2 * mix)
    store_l2(0, conv_silu(0, mix, q_raw), dh ** -0.5)
    v_raw = project(2 * mix, 3 * mix)
    store_l2(mix, conv_silu(mix, 2 * mix, k_raw), 1.0)
    o_ref[:, 3 * mix:n] = project(3 * mix, n)
    o_ref[:, 2 * mix:3 * mix] = conv_silu(2 * mix, 3 * mix, v_raw)


def _inproj(x3, mod, g, w_t, w_conv, conv_state, *, shift, scale, rows, nheads):
    nb, t, d = x3.shape
    n = w_t.shape[0]
    qkv = 3 * nheads * DN_HEAD_DIM
    bb, tt = _token_blocks(nb, t, rows)
    nt = t // tt
    mod, mod_shape, mod_index = _mod_block(mod, bb, nt)
    if bb == 1:
        scratch = pltpu.VMEM((rows + SUBLANE, qkv), F32)
    else:
        scratch = pltpu.VMEM((bb, SUBLANE + tt, qkv), F32)
    return pl.pallas_call(
        functools.partial(_inproj_body, shift=shift, scale=scale, nheads=nheads, nt=nt),
        grid=(nb * t // rows,),
        in_specs=[pl.BlockSpec((bb, tt, d), lambda i: (i // nt, i % nt, 0)),
                  pl.BlockSpec(mod_shape, mod_index),
                  pl.BlockSpec((1, d), lambda i: (0, 0)),
                  pl.BlockSpec((n, d), lambda i: (0, 0), pipeline_mode=pl.Buffered(1)),
                  pl.BlockSpec((CONV_WIDTH, qkv), lambda i: (0, 0)),
                  pl.BlockSpec((bb, CONV_WIDTH - 1, qkv), lambda i: (i // nt, 0, 0))],
        out_specs=[pl.BlockSpec((rows, n), lambda i: (i, 0)),
                   pl.BlockSpec((bb, CONV_WIDTH - 1, qkv), lambda i: (i // nt, 0, 0))],
        out_shape=[jax.ShapeDtypeStruct((nb * t, n), F32),
                   jax.ShapeDtypeStruct((nb, CONV_WIDTH - 1, qkv), F32)],
        scratch_shapes=[scratch],
        compiler_params=_params("arbitrary"),
        name="inproj",
    )(x3, mod, g, w_t, w_conv, conv_state)


def _seg_masks(c, seg):
    row = lax.broadcasted_iota(jnp.int32, (c, c), 0)
    col = lax.broadcasted_iota(jnp.int32, (c, c), 1)
    sh = int(math.log2(seg))
    same = (row >> sh) == (col >> sh)
    return row, col, same, same & (row >= col), same & (row > col)


def _tri_inverse_offdiag(lmats, row, col, base, seg):
    bsh = int(math.log2(base))
    blk = (row >> bsh) == (col >> bsh)
    l0s = [jnp.where(blk, l, 0.0) for l in lmats]
    pts = [-l for l in l0s]
    qs = [_mm(l, l) for l in l0s]
    k = 2
    while k < base:
        pts = [p + q + _mm(p, q) for p, q in zip(pts, qs)]
        k *= 2
        if k < base:
            qs = [_mm(q, q) for q in qs]
    m = base
    while m < seg:
        msh = int(math.log2(m))
        sel = ((row >> (msh + 1)) == (col >> (msh + 1))) & ((row >> msh) != (col >> msh))
        es = [jnp.where(sel, l, 0.0) for l in lmats]
        xs = [e + _mm(p, e) for p, e in zip(pts, es)]
        pts = [p - (x + _mm(x, p)) for p, x in zip(pts, xs)]
        m *= 2
    return pts


def _delta_gates(ab, alog, dtb, causal_bf, seg, nheads):
    lane = lax.broadcasted_iota(jnp.int32, ab.shape, 1)
    x = ab + dtb
    softplus = jnp.maximum(x, 0.0) + jnp.log1p(jnp.exp(-jnp.abs(x)))
    g = jnp.where(lane < nheads, -jnp.exp(alog) * softplus, 0.0)
    beta = jax.nn.sigmoid(ab)
    pieces = _split3(g)
    gcum = sum(jnp.dot(causal_bf, p, preferred_element_type=F32) for p in pieces)
    g3 = gcum.reshape(gcum.shape[0] // seg, seg, gcum.shape[1])
    glast = jnp.broadcast_to(g3[:, seg - 1:seg, :], g3.shape).reshape(gcum.shape)
    return gcum, beta, glast


def _decay_logits(gcum, nheads):
    c = gcum.shape[0]
    if c < LANE:
        gcum = jnp.concatenate([gcum, jnp.zeros((LANE - c, LANE), F32)], axis=0)
    return gcum.T


def _delta_intra(act, nheads, gates, dlog, masks, base, seg, heads=None):
    gcum, beta, glast = gates
    row, col, _, causal, strict = masks
    c = act.shape[0]
    dh = DN_HEAD_DIM
    heads = list(range(nheads)) if heads is None else list(heads)
    eg_all = jnp.exp(gcum)
    ed_all = jnp.exp(glast - gcum)
    ks, kqs_lhs, rhss, qgs, kdecs = [], [], [], [], []
    for h in heads:
        q = act[:, h * dh:(h + 1) * dh]
        k = act[:, (nheads + h) * dh:(nheads + h + 1) * dh]
        v = act[:, (2 * nheads + h) * dh:(2 * nheads + h + 1) * dh]
        b = beta[:, nheads + h:nheads + h + 1]
        eg = eg_all[:, h:h + 1]
        kb = k * b
        ks.append(k)
        kqs_lhs.append(jnp.concatenate([kb, q], axis=0))
        rhss.append(jnp.concatenate([v * b, kb * eg], axis=1))
        qgs.append(q * eg)
        kdecs.append(k * ed_all[:, h:h + 1])
    kqs = [_mm_nt(a, k) for a, k in zip(kqs_lhs, ks)]
    lmats, attns = [], []
    for n, h in enumerate(heads):
        dl = gcum[:, h:h + 1] - dlog[h:h + 1, 0:c]
        dec = jnp.where(causal, jnp.exp(jnp.where(causal, dl, 0.0)), 0.0)
        lmats.append(jnp.where(strict, kqs[n][:c] * dec, 0.0))
        attns.append(kqs[n][c:] * dec)
    tts = _tri_inverse_offdiag(lmats, row, col, base, seg)
    uws = [r + _mm(t, r) for t, r in zip(tts, rhss)]
    return [uw[:, :dh] for uw in uws], [uw[:, dh:] for uw in uws], qgs, kdecs, attns


DELTA_HEAD_GROUP = 8


def _delta_out(o, z, gdn):
    on = o * lax.rsqrt(jnp.mean(o * o, axis=-1, keepdims=True) + EPS) * gdn
    return on * _silu(z)


def _delta_prompt_body(qkv_ref, z_ref, ab_ref, s0_ref, alog_ref, dtb_ref, gdn_ref,
                       o_ref, snew_ref, s_ref, *, c, nheads, base):
    dh = DN_HEAD_DIM

    @pl.when(pl.program_id(1) == 0)
    def _():
        s_ref[...] = s0_ref[0]

    act = qkv_ref[...]
    masks = _seg_masks(c, c)
    causal_bf = jnp.where(masks[3], 1.0, 0.0).astype(BF16)
    gates = _delta_gates(ab_ref[...], alog_ref[...], dtb_ref[...], causal_bf, c, nheads)
    dlog = _decay_logits(gates[0], nheads)
    eglast = jnp.exp(gates[2][0:1, :])
    gdn = gdn_ref[...]
    for g0 in range(0, nheads, DELTA_HEAD_GROUP):
        heads = list(range(g0, min(g0 + DELTA_HEAD_GROUP, nheads)))
        us, ws_, qgs, kdecs, attns = _delta_intra(act, nheads, gates, dlog, masks, base, c, heads)
        idx = range(len(heads))
        ss = [s_ref[h] for h in heads]
        wss = [_mm(jnp.concatenate([ws_[n], qgs[n]], axis=0), ss[n]) for n in idx]
        vnews = [us[n] - wss[n][:c] for n in idx]
        os_ = [wss[n][c:] + _mm(attns[n], vnews[n]) for n in idx]
        snews = [ss[n] * eglast[:, h:h + 1] + _mm_tn(kdecs[n], vnews[n]) for n, h in enumerate(heads)]
        for n, h in enumerate(heads):
            s_ref[h] = snews[n]
            o_ref[:, h * dh:(h + 1) * dh] = _delta_out(os_[n], z_ref[:, h * dh:(h + 1) * dh], gdn).astype(o_ref.dtype)

    @pl.when(pl.program_id(1) == pl.num_programs(1) - 1)
    def _():
        snew_ref[0] = s_ref[...]


def _delta_prompt(proj, s0, alog, dtb, gdn, *, nb, t, c, base):
    nheads = s0.shape[1]
    dh = DN_HEAD_DIM
    mix = nheads * dh
    nc = t // c
    return pl.pallas_call(
        functools.partial(_delta_prompt_body, c=c, nheads=nheads, base=base),
        grid=(nb, nc),
        in_specs=[pl.BlockSpec((c, 3 * mix), lambda b, i: (b * nc + i, 0)),
                  pl.BlockSpec((c, mix), lambda b, i: (b * nc + i, 3)),
                  pl.BlockSpec((c, LANE), lambda b, i: (b * nc + i, proj.shape[1] // LANE - 1)),
                  pl.BlockSpec((1, nheads, dh, dh), lambda b, i: (b, 0, 0, 0)),
                  pl.BlockSpec((1, LANE), lambda b, i: (0, 0)),
                  pl.BlockSpec((1, LANE), lambda b, i: (0, 0)),
                  pl.BlockSpec((1, dh), lambda b, i: (0, 0))],
        out_specs=[pl.BlockSpec((c, mix), lambda b, i: (b * nc + i, 0)),
                   pl.BlockSpec((1, nheads, dh, dh), lambda b, i: (b, 0, 0, 0))],
        out_shape=[jax.ShapeDtypeStruct((nb * t, mix), BF16),
                   jax.ShapeDtypeStruct((nb, nheads, dh, dh), F32)],
        scratch_shapes=[pltpu.VMEM((nheads, dh, dh), F32)],
        compiler_params=_params("parallel", "arbitrary"),
        name="delta_prompt",
    )(proj, proj, proj, s0, alog, dtb, gdn)


def _delta_sample_body(qkv_ref, z_ref, ab_ref, s0_ref, alog_ref, dtb_ref, gdn_ref,
                       o_ref, snew_ref,
                       w_s, qg_s, u_s, kd_s, vn_s, qs_s, attn_s, egl_s, *, bb, t, nheads):
    dh = DN_HEAD_DIM
    c = bb * t
    act = qkv_ref[...]
    masks = _seg_masks(c, t)
    causal_bf = jnp.where(masks[3], 1.0, 0.0).astype(BF16)
    gates = _delta_gates(ab_ref[...], alog_ref[...], dtb_ref[...], causal_bf, t, nheads)
    dlog = _decay_logits(gates[0], nheads)
    egl_s[...] = jnp.exp(gates[2])
    us, ws_, qgs, kdecs, attns = _delta_intra(act, nheads, gates, dlog, masks, t, t)
    for h in range(nheads):
        hs = slice(h * dh, (h + 1) * dh)
        u_s[:, hs] = us[h]
        w_s[:, hs] = ws_[h]
        qg_s[:, hs] = qgs[h]
        kd_s[:, hs] = kdecs[h]
        attn_s[h] = attns[h]

    def per_batch(b, carry):
        r = pl.multiple_of(b * t, t)
        rows = pl.ds(r, t)
        heads = range(nheads)
        hsl = [slice(h * dh, (h + 1) * dh) for h in heads]
        ss = [s0_ref[b, h] for h in heads]
        wss = [_mm(jnp.concatenate([w_s[rows, hsl[h]], qg_s[rows, hsl[h]]], axis=0), ss[h]) for h in heads]
        vnews = [u_s[rows, hsl[h]] - wss[h][:t] for h in heads]
        upds = [_mm_tn(kd_s[rows, hsl[h]], vnews[h]) for h in heads]
        egl = egl_s[pl.ds(r, 1), :]
        for h in heads:
            vn_s[rows, hsl[h]] = vnews[h]
            qs_s[rows, hsl[h]] = wss[h][t:]
            snew_ref[b, h] = ss[h] * egl[:, h:h + 1] + upds[h]
        return carry

    lax.fori_loop(0, bb, per_batch, 0, unroll=2)

    gdn = gdn_ref[...]
    for h in range(nheads):
        hs = slice(h * dh, (h + 1) * dh)
        o = qs_s[:, hs] + _mm(attn_s[h], vn_s[:, hs])
        o_ref[:, hs] = _delta_out(o, z_ref[:, hs], gdn).astype(o_ref.dtype)


def _delta_sample(proj, s0, alog, dtb, gdn, *, nb, t, bb):
    nheads = s0.shape[1]
    dh = DN_HEAD_DIM
    mix = nheads * dh
    c = bb * t
    return pl.pallas_call(
        functools.partial(_delta_sample_body, bb=bb, t=t, nheads=nheads),
        grid=(nb // bb,),
        in_specs=[pl.BlockSpec((c, 3 * mix), lambda i: (i, 0)),
                  pl.BlockSpec((c, mix), lambda i: (i, 3)),
                  pl.BlockSpec((c, LANE), lambda i: (i, proj.shape[1] // LANE - 1)),
                  pl.BlockSpec((bb, nheads, dh, dh), lambda i: (i, 0, 0, 0)),
                  pl.BlockSpec((1, LANE), lambda i: (0, 0)),
                  pl.BlockSpec((1, LANE), lambda i: (0, 0)),
                  pl.BlockSpec((1, dh), lambda i: (0, 0))],
        out_specs=[pl.BlockSpec((c, mix), lambda i: (i, 0)),
                   pl.BlockSpec((bb, nheads, dh, dh), lambda i: (i, 0, 0, 0))],
        out_shape=[jax.ShapeDtypeStruct((nb * t, mix), BF16),
                   jax.ShapeDtypeStruct((nb, nheads, dh, dh), F32)],
        scratch_shapes=[pltpu.VMEM((c, mix), F32) for _ in range(6)]
                       + [pltpu.VMEM((nheads, c, c), F32), pltpu.VMEM((c, LANE), F32)],
        compiler_params=_params("parallel"),
        name="delta_sample",
    )(proj, proj, proj, s0, alog, dtb, gdn)


def _s5_prep_body(lre_ref, lim_ref, ls_ref, bre_ref, bim_ref, wa_ref, abre_ref, abim_ref):
    lre = jnp.minimum(lre_ref[...], -1e-4)
    lim = lim_ref[...]
    dt = jnp.exp(ls_ref[...])
    mag = jnp.exp(lre * dt)
    ang = lim * dt
    abr = mag * jnp.cos(ang)
    abi = mag * jnp.sin(ang)
    nr = abr - 1.0
    ni = abi
    den = lre * lre + lim * lim
    cor = (nr * lre + ni * lim) / den
    coi = (ni * lre - nr * lim) / den
    bre = bre_ref[...]
    bim = bim_ref[...]
    shape2 = bre.shape[1:]
    row = lax.broadcasted_iota(jnp.int32, shape2, 0)
    col = lax.broadcasted_iota(jnp.int32, shape2, 1)
    diag = ((row // S5_GROUP) == (col // S5_STATE))[None]
    half = shape2[1]
    rows = shape2[0]
    wre = jnp.where(diag, cor * bre - coi * bim, 0.0)
    wim = jnp.where(diag, cor * bim + coi * bre, 0.0)
    wa_ref[:, 0:rows, 0:half] = wre.astype(BF16)
    wa_ref[:, 0:rows, half:2 * half] = wim.astype(BF16)
    wa_ref[:, rows:2 * rows, 0:half] = (abr * wre - abi * wim).astype(BF16)
    wa_ref[:, rows:2 * rows, half:2 * half] = (abr * wim + abi * wre).astype(BF16)
    abre_ref[...] = abr
    abim_ref[...] = abi


def _s5_prep(lam_re, lam_im, log_step, b_re, b_im):
    groups, states, gch = b_re.shape
    slabs = groups // S5_SLAB_GROUPS
    cols = S5_SLAB_STATES

    def row_param(p):
        return p.reshape(slabs, 1, cols)

    def tiled_b(b):
        bt = jnp.transpose(b, (0, 2, 1)).reshape(slabs, LANE, states)
        return jnp.tile(bt, (1, 1, S5_SLAB_GROUPS))

    ls = jnp.broadcast_to(log_step[:, None], (groups, states))
    wa, abre, abim = pl.pallas_call(
        _s5_prep_body,
        out_shape=[jax.ShapeDtypeStruct((slabs, 2 * LANE, 2 * cols), BF16),
                   jax.ShapeDtypeStruct((slabs, 1, cols), F32),
                   jax.ShapeDtypeStruct((slabs, 1, cols), F32)],
        compiler_params=pltpu.CompilerParams(vmem_limit_bytes=VMEM_LIMIT),
        name="s5prep",
    )(row_param(lam_re), row_param(lam_im), row_param(ls), tiled_b(b_re), tiled_b(b_im))
    nblk = groups * states // LANE
    return wa, abre.reshape(nblk, 1, LANE), abim.reshape(nblk, 1, LANE)


def _s5_out_weights(c):
    groups, gch, states = c.shape
    slabs = groups // S5_SLAB_GROUPS
    ct = jnp.transpose(c, (0, 2, 1)).reshape(slabs, S5_SLAB_STATES, gch)
    ct = jnp.tile(ct, (1, 1, S5_SLAB_GROUPS))
    row = jnp.arange(S5_SLAB_STATES)[:, None] // states
    col = jnp.arange(LANE)[None, :] // gch
    return jnp.where((row == col)[None], ct, 0.0).astype(BF16)


def _gelu_tanh(x):
    return x * (0.5 * (1.0 + jnp.tanh(math.sqrt(2.0 / math.pi) * (x + 0.044715 * (x * x * x)))))


S5_PIPELINE_PARTS = 4


def _time_major_perms(nb, lc, nseg):
    seg_rows = nb * lc
    rows = nseg * seg_rows
    r = np.arange(rows)[:, None]
    c = np.arange(rows)[None, :]
    base, loc = r // seg_rows * seg_rows, r % seg_rows
    t_tm, b_tm = loc // nb, loc % nb
    to_tm = c == base + b_tm * lc + t_tm
    to_tm_prev = (c == base + b_tm * lc + t_tm - 1) & (t_tm % 2 == 1)
    to_bm = c == base + (loc % lc) * nb + loc // lc
    return tuple(jnp.asarray(m, BF16) for m in (to_tm, to_tm_prev, to_bm))


def _s5_body(u_ref, h0re_ref, h0im_ref, tm_ref, tmp_ref, bm_ref, wa_ref, abre_ref, abim_ref, wcre_ref, wcim_ref,
             dsk_ref, wglu_ref, gs5_ref, y_ref, hre_ref, him_ref, cre, cim, *xbufs, nb, lc, nseg):
    seg_rows = nb * lc
    rows = nseg * seg_rows
    nblk = cre.shape[0]
    slabs = wa_ref.shape[0]
    per_slab = nblk // slabs
    half = per_slab * LANE
    steps_per_tile = SUBLANE // nb
    assert steps_per_tile in (1, 2) and lc % steps_per_tile == 0
    assert nseg == 1 or nb == SUBLANE

    if nseg == 1:
        @pl.when(pl.program_id(1) == 0)
        def _():
            for j in range(nblk):
                if nb < SUBLANE:
                    cre[j] = jnp.zeros((SUBLANE, LANE), F32)
                    cim[j] = jnp.zeros((SUBLANE, LANE), F32)
                cre[j, SUBLANE - nb:SUBLANE, :] = h0re_ref[:, j * LANE:(j + 1) * LANE]
                cim[j, SUBLANE - nb:SUBLANE, :] = h0im_ref[:, j * LANE:(j + 1) * LANE]

    to_tm, to_bm = tm_ref[...], bm_ref[...]
    pieces = _split3(u_ref[...].reshape(rows, u_ref.shape[2]))[:2]
    u = sum(jnp.dot(to_tm, p, preferred_element_type=F32) for p in pieces)
    ub = u.astype(BF16)
    if steps_per_tile == 2:
        ub_prev = jnp.dot(tmp_ref[...], pieces[0], preferred_element_type=F32).astype(BF16)

    upper = lax.broadcasted_iota(jnp.int32, (SUBLANE, LANE), 0) >= nb
    parts = S5_PIPELINE_PARTS if rows % (S5_PIPELINE_PARTS * LANE) == 0 else 1
    prow = rows // parts

    def project_in(s, r):
        rs = slice(r * prow, (r + 1) * prow)
        sl = slice(s * LANE, (s + 1) * LANE)
        if steps_per_tile == 2:
            x = jnp.dot(jnp.concatenate([ub[rs, sl], ub_prev[rs, sl]], axis=1), wa_ref[s],
                        preferred_element_type=F32)
        else:
            x = jnp.dot(ub[rs, sl], wa_ref[s, 0:LANE, :], preferred_element_type=F32)
        for k in range(per_slab):
            xbufs[2 * s][k, rs, :] = x[:, k * LANE:(k + 1) * LANE]
            xbufs[2 * s + 1][k, rs, :] = x[:, half + k * LANE:half + (k + 1) * LANE]

    seg_tiles = seg_rows // SUBLANE

    def recur(s, r, coef, carry):
        xre, xim = xbufs[2 * s], xbufs[2 * s + 1]
        for v in range(r * prow // SUBLANE, (r + 1) * prow // SUBLANE):
            at = slice(v * SUBLANE, (v + 1) * SUBLANE)
            seg_rows_sl = slice(v // seg_tiles * nb, (v // seg_tiles + 1) * nb)
            if nseg > 1 and v % seg_tiles == 0:
                for n in range(per_slab):
                    cols = slice((s * per_slab + n) * LANE, (s * per_slab + n + 1) * LANE)
                    carry[n] = (h0re_ref[seg_rows_sl, cols], h0im_ref[seg_rows_sl, cols])
            for n in range(per_slab):
                hr, hi = carry[n]
                ar, ai = coef[n]
                if steps_per_tile == 2:
                    hr = jnp.where(upper, hr, pltpu.roll(hr, nb, 0))
                    hi = jnp.where(upper, hi, pltpu.roll(hi, nb, 0))
                nr = ar * hr - ai * hi + xre[n, at, :]
                ni = ar * hi + ai * hr + xim[n, at, :]
                xre[n, at, :] = nr
                xim[n, at, :] = ni
                carry[n] = (nr, ni)
            if nseg > 1 and v % seg_tiles == seg_tiles - 1:
                for n in range(per_slab):
                    cols = slice((s * per_slab + n) * LANE, (s * per_slab + n + 1) * LANE)
                    hre_ref[seg_rows_sl, cols], him_ref[seg_rows_sl, cols] = carry[n]

    def project_out(s, r):
        rs = slice(r * prow, (r + 1) * prow)
        hre = jnp.concatenate([xbufs[2 * s][k, rs, :] for k in range(per_slab)], axis=1)
        him = jnp.concatenate([xbufs[2 * s + 1][k, rs, :] for k in range(per_slab)], axis=1)
        return _mm(hre, wcre_ref[s]) - _mm(him, wcim_ref[s])

    ychunks = [[None] * parts for _ in range(slabs)]
    for r in range(parts):
        project_in(0, r)
    for s in range(slabs):
        js = list(range(s * per_slab, (s + 1) * per_slab))
        coef, carry = [], []
        for j in js:
            ar = jnp.broadcast_to(abre_ref[j], (SUBLANE, LANE))
            ai = jnp.broadcast_to(abim_ref[j], (SUBLANE, LANE))
            if steps_per_tile == 2:
                coef.append((jnp.where(upper, ar * ar - ai * ai, ar), jnp.where(upper, 2.0 * ar * ai, ai)))
            else:
                coef.append((ar, ai))
            carry.append((cre[j], cim[j]) if nseg == 1 else None)
        for r in range(parts):
            recur(s, r, coef, carry)
            if s + 1 < slabs:
                project_in(s + 1, r)
            if s >= 1:
                ychunks[s - 1][r] = project_out(s - 1, r)
        if nseg == 1:
            for n, j in enumerate(js):
                cre[j], cim[j] = carry[n]
    for r in range(parts):
        ychunks[slabs - 1][r] = project_out(slabs - 1, r)
    y = jnp.concatenate([jnp.concatenate(yc, axis=0) for yc in ychunks], axis=1) + dsk_ref[...] * u
    ab = _mm(_gelu_tanh(y), wglu_ref[...])
    n = y.shape[1]
    glu = ab[:, :n] * jax.nn.sigmoid(ab[:, n:])
    out = glu * lax.rsqrt(jnp.mean(glu * glu, axis=-1, keepdims=True) + EPS) * gs5_ref[...]
    out = jnp.dot(to_bm, out.astype(BF16), preferred_element_type=F32)
    y_ref[...] = out.reshape(y_ref.shape).astype(y_ref.dtype)

    if nseg == 1:
        @pl.when(pl.program_id(1) == pl.num_programs(1) - 1)
        def _():
            for j in range(nblk):
                hre_ref[:, j * LANE:(j + 1) * LANE] = cre[j, SUBLANE - nb:SUBLANE, :]
                him_ref[:, j * LANE:(j + 1) * LANE] = cim[j, SUBLANE - nb:SUBLANE, :]


def _s5(proj, u_col, h0re, h0im, wa, abre, abim, wcre, wcim, dskip, wglu, gs5, *, nb_total, t, nb, lc, nseg):
    mixb = dskip.shape[1]
    nstate = h0re.shape[1]
    nblk = nstate // LANE
    slabs = wa.shape[0]
    assert u_col % mixb == 0
    assert nseg == 1 or lc == t
    ublk = u_col // mixb
    nbs = nseg * nb
    uab3 = proj.reshape(nb_total, t, proj.shape[1])
    perms = _time_major_perms(nb, lc, nseg)
    full = lambda a: pl.BlockSpec(a.shape, lambda i, j: (0,) * a.ndim)
    y, hre, him = pl.pallas_call(
        functools.partial(_s5_body, nb=nb, lc=lc, nseg=nseg),
        grid=(nb_total // nbs, t // lc),
        in_specs=[pl.BlockSpec((nbs, lc, mixb), lambda i, j: (i, j, ublk)),
                  pl.BlockSpec((nbs, nstate), lambda i, j: (i, 0)),
                  pl.BlockSpec((nbs, nstate), lambda i, j: (i, 0))]
                 + [full(a) for a in perms]
                 + [full(wa), full(abre), full(abim), full(wcre), full(wcim), full(dskip), full(wglu), full(gs5)],
        out_specs=[pl.BlockSpec((nbs, lc, mixb), lambda i, j: (i, j, 0)),
                   pl.BlockSpec((nbs, nstate), lambda i, j: (i, 0)),
                   pl.BlockSpec((nbs, nstate), lambda i, j: (i, 0))],
        out_shape=[jax.ShapeDtypeStruct((nb_total, t, mixb), F32),
                   jax.ShapeDtypeStruct((nb_total, nstate), F32),
                   jax.ShapeDtypeStruct((nb_total, nstate), F32)],
        scratch_shapes=[pltpu.VMEM((nblk, SUBLANE, LANE), F32),
                        pltpu.VMEM((nblk, SUBLANE, LANE), F32)]
                       + [pltpu.VMEM((nblk // slabs, nbs * lc, LANE), F32) for _ in range(2 * slabs)],
        compiler_params=_params("parallel", "arbitrary"),
        name="s5",
    )(uab3, h0re, h0im, *perms, wa, abre, abim, wcre, wcim, dskip, wglu, gs5)
    return y.reshape(nb_total * t, mixb), hre, him


PART_ROWS = 256


def _row_parts(bb, tt):
    rows = bb * tt
    nparts = max(1, rows // PART_ROWS)
    if bb > 1:
        nparts = min(nparts, bb)
    prow = rows // nparts
    out = []
    for r in range(nparts):
        flat = slice(r * prow, (r + 1) * prow)
        if bb == 1:
            out.append(((slice(None), flat), slice(None), flat))
        else:
            bs = slice(r * bb // nparts, (r + 1) * bb // nparts)
            out.append(((bs, slice(None)), bs, flat))
    return out


def _outproj_body(o_ref, y_ref, wt_ref, wb_ref, x_ref, mod_ref, g_ref, out_ref, *, gate):
    mix = (jnp.dot(o_ref[...], wt_ref[...], preferred_element_type=F32)
           + _mm(y_ref[...], wb_ref[...]))
    n = mix * lax.rsqrt(jnp.mean(mix * mix, axis=-1, keepdims=True) + EPS) * g_ref[...]
    x = x_ref[...]
    out_ref[...] = x + mod_ref[:, gate:gate + 1, :] * n.reshape(x.shape)


def _outproj(o, y5, wtop, wbot, x3, mod, g, *, gate, rows):
    nb, t, d = x3.shape
    bb, tt = _token_blocks(nb, t, rows)
    nt = t // tt
    ka, kb = o.shape[1], y5.shape[1]
    mod, mod_shape, mod_index = _mod_block(mod, bb, nt)
    return pl.pallas_call(
        functools.partial(_outproj_body, gate=gate),
        grid=(nb * t // rows,),
        in_specs=[pl.BlockSpec((rows, ka), lambda i: (i, 0)),
                  pl.BlockSpec((rows, kb), lambda i: (i, 0)),
                  pl.BlockSpec((ka, d), lambda i: (0, 0)),
                  pl.BlockSpec((kb, d), lambda i: (0, 0)),
                  pl.BlockSpec((bb, tt, d), lambda i: (i // nt, i % nt, 0)),
                  pl.BlockSpec(mod_shape, mod_index),
                  pl.BlockSpec((1, d), lambda i: (0, 0))],
        out_specs=pl.BlockSpec((bb, tt, d), lambda i: (i // nt, i % nt, 0)),
        out_shape=jax.ShapeDtypeStruct((nb, t, d), F32),
        compiler_params=_params("parallel"),
        name="outproj",
    )(o, y5, wtop, wbot, x3, mod, g)


def _ffn_body(x_ref, mod_ref, gpre_ref, wg_ref, wu_ref, wd_ref, gpost_ref, out_ref, h_ref,
              *, shift, scale, gate):
    j = pl.program_id(1)
    last = pl.num_programs(1) - 1
    bb, tt, d = x_ref.shape
    parts = _row_parts(bb, tt)

    def swiglu(h):
        a = _silu(_mm(h, wg_ref[...])) * _mm(h, wu_ref[...])
        return _mm(a, wd_ref[...])

    @pl.when(j == 0)
    def _():
        for xi, mi, hr in parts:
            x = x_ref[xi]
            h = _modulated_norm(x, gpre_ref[...], mod_ref[mi, scale:scale + 1, :], mod_ref[mi, shift:shift + 1, :])
            h = h.reshape(hr.stop - hr.start, d).astype(BF16)
            h_ref[hr, :] = h
            out_ref[xi] = swiglu(h).reshape(x.shape)

    @pl.when((j > 0) & (j < last))
    def _():
        out_ref[...] += swiglu(h_ref[...]).reshape(out_ref.shape)

    @pl.when((j == last) & (j > 0))
    def _():
        for xi, mi, hr in parts:
            x = x_ref[xi]
            f = out_ref[xi] + swiglu(h_ref[hr, :]).reshape(x.shape)
            n = f * lax.rsqrt(jnp.mean(f * f, axis=-1, keepdims=True) + EPS) * gpost_ref[...]
            out_ref[xi] = x + mod_ref[mi, gate:gate + 1, :] * n


def _ffn(x3, mod, gpre, wg, wu, wd, gpost, *, shift, scale, gate, rows, tf):
    nb, t, d = x3.shape
    dff = wg.shape[1]
    assert dff // tf >= 2
    bb, tt = _token_blocks(nb, t, rows)
    nt = t // tt
    once = dict(pipeline_mode=pl.Buffered(1)) if bb > 1 else {}
    mod, mod_shape, mod_index = _mod_block(mod, bb, nt)
    return pl.pallas_call(
        functools.partial(_ffn_body, shift=shift, scale=scale, gate=gate),
        grid=(nb * t // rows, dff // tf),
        in_specs=[pl.BlockSpec((bb, tt, d), lambda i, j: (i // nt, i % nt, 0), **once),
                  pl.BlockSpec(mod_shape, lambda i, j: mod_index(i), **once),
                  pl.BlockSpec((1, d), lambda i, j: (0, 0)),
                  pl.BlockSpec((d, tf), lambda i, j: (0, j)),
                  pl.BlockSpec((d, tf), lambda i, j: (0, j)),
                  pl.BlockSpec((tf, d), lambda i, j: (j, 0)),
                  pl.BlockSpec((1, d), lambda i, j: (0, 0))],
        out_specs=pl.BlockSpec((bb, tt, d), lambda i, j: (i // nt, i % nt, 0)),
        out_shape=jax.ShapeDtypeStruct((nb, t, d), F32),
        scratch_shapes=[pltpu.VMEM((rows, d), BF16)],
        compiler_params=_params("parallel", "arbitrary"),
        name="ffn",
    )(x3, mod, gpre, wg, wu, wd, gpost)


ROWS = 512
FFN_ROWS = 1024
FFN_COLS = 256
PROMPT_CHUNK = 128
PROMPT_BASE = 16
SAMPLE_BATCH = 8
S5_PROMPT_STEPS = 128
S5_SAMPLE_SEGMENTS = 8


def _layer_weights(w_ada, b_ada, g_pre_mix, g_post_mix, g_pre_ffn, g_post_ffn,
                   w_in, w_conv, a_log, dt_bias, g_dn_out,
                   lam_re, lam_im, log_step, b_re, b_im, c_re, c_im, d_skip,
                   w_glu, g_s5_out, w_out, w_gate, w_up, w_down):
    nheads = a_log.shape[0]
    mix_a = nheads * DN_HEAD_DIM
    pad = lambda v: jnp.pad(v, (0, LANE - v.shape[0]))[None, :]
    wa, abre, abim = _s5_prep(lam_re, lam_im, log_step, b_re, b_im)
    return dict(
        w_ada=w_ada, b_ada=b_ada[None, :],
        g_pre_mix=g_pre_mix[None, :], g_post_mix=g_post_mix[None, :],
        g_pre_ffn=g_pre_ffn[None, :], g_post_ffn=g_post_ffn[None, :],
        w_proj=_win_prep(jnp.swapaxes(w_in, 0, 1), 4 * mix_a, 2 * nheads), u_col=4 * mix_a,
        w_conv=w_conv, alog=pad(a_log), dtb=pad(dt_bias), gdn=g_dn_out[None, :],
        wa=wa, abre=abre, abim=abim, wcre=_s5_out_weights(c_re), wcim=_s5_out_weights(c_im),
        dskip=d_skip.reshape(1, -1), wglu=w_glu.astype(BF16), gs5=g_s5_out[None, :],
        wtop=w_out[:mix_a].astype(BF16), wbot=w_out[mix_a:].astype(BF16),
        wg=w_gate, wu=w_up, wd=w_down,
    )


def _layer_group(x3, mod, conv_state, s0, h0re, h0im, lw, *, prompt):
    nb, t, d = x3.shape
    rows = min(ROWS, nb * t)
    nheads = s0.shape[1]
    proj, cnew = _inproj(x3, mod, lw["g_pre_mix"], lw["w_proj"], lw["w_conv"], conv_state, shift=0, scale=1,
                         rows=min(INPROJ_ROWS, nb * t), nheads=nheads)
    if prompt:
        o, snew = _delta_prompt(proj, s0, lw["alog"], lw["dtb"], lw["gdn"],
                                nb=nb, t=t, c=PROMPT_CHUNK, base=PROMPT_BASE)
        y5, hre, him = _s5(proj, lw["u_col"], h0re, h0im, lw["wa"], lw["abre"], lw["abim"], lw["wcre"], lw["wcim"],
                           lw["dskip"], lw["wglu"], lw["gs5"], nb_total=nb, t=t, nb=nb, lc=S5_PROMPT_STEPS, nseg=1)
    else:
        o, snew = _delta_sample(proj, s0, lw["alog"], lw["dtb"], lw["gdn"], nb=nb, t=t, bb=SAMPLE_BATCH)
        y5, hre, him = _s5(proj, lw["u_col"], h0re, h0im, lw["wa"], lw["abre"], lw["abim"], lw["wcre"], lw["wcim"],
                           lw["dskip"], lw["wglu"], lw["gs5"], nb_total=nb, t=t, nb=SUBLANE, lc=t,
                           nseg=min(S5_SAMPLE_SEGMENTS, nb // SUBLANE))
    x1 = _outproj(o, y5, lw["wtop"], lw["wbot"], x3, mod, lw["g_post_mix"], gate=2, rows=rows)
    x2 = _ffn(x1, mod, lw["g_pre_ffn"], lw["wg"], lw["wu"], lw["wd"], lw["g_post_ffn"],
              shift=3, scale=4, gate=5, rows=min(FFN_ROWS, nb * t), tf=FFN_COLS)
    ngroups = h0re.shape[1] // S5_STATE
    return x2, cnew, snew, hre.reshape(nb, ngroups, S5_STATE), him.reshape(nb, ngroups, S5_STATE)


def kernel(x_prompt, x_sample, c_prompt, c_sample, state_conv, state_delta, state_ssm_re, state_ssm_im, w_ada, b_ada, g_pre_mix, g_post_mix, g_pre_ffn, g_post_ffn, w_in, w_conv, a_log, dt_bias, g_dn_out, lam_re, lam_im, log_step, b_re, b_im, c_re, c_im, d_skip, w_glu, g_s5_out, w_out, w_gate, w_up, w_down):
    weights = (w_ada, b_ada, g_pre_mix, g_post_mix, g_pre_ffn, g_post_ffn,
               w_in, w_conv, a_log, dt_bias, g_dn_out,
               lam_re, lam_im, log_step, b_re, b_im, c_re, c_im, d_skip,
               w_glu, g_s5_out, w_out, w_gate, w_up, w_down)
    depth = w_ada.shape[0]
    bp, d = c_prompt.shape
    bs = c_sample.shape[0]
    dt_ = x_prompt.dtype
    nheads = state_delta.shape[2]
    nstate = state_ssm_re.shape[2] * state_ssm_re.shape[3]
    crows = -(-(bp + bs) // SUBLANE) * SUBLANE
    c_all = jnp.concatenate([c_sample, c_prompt, jnp.zeros((crows - bp - bs, d), dt_)], axis=0)
    yp, ys = x_prompt, x_sample
    outs = [[] for _ in range(8)]
    for l in range(depth):
        lw = _layer_weights(*[w[l] for w in weights])
        mod = _ada(c_all, lw["w_ada"], lw["b_ada"]).reshape(crows, 6, d)
        zc = jnp.zeros((bp, CONV_WIDTH - 1, state_conv.shape[3]), dt_)
        zd = jnp.zeros((bp,) + state_delta.shape[2:], dt_)
        zs = jnp.zeros((bp, nstate), dt_)
        yp, c1, d1, r1, i1 = _layer_group(yp, (mod, bs), zc, zd, zs, zs, lw, prompt=True)
        ys, c2, d2, r2, i2 = _layer_group(ys, (mod, 0), state_conv[l], state_delta[l],
                                          state_ssm_re[l].reshape(bs, nstate), state_ssm_im[l].reshape(bs, nstate),
                                          lw, prompt=False)
        for lst, v in zip(outs, (c1, d1, r1, i1, c2, d2, r2, i2)):
            lst.append(v)
    stacked = [jnp.stack(v) for v in outs]
    return (yp, ys, *stacked)
```
